```python
import jax, jax.numpy as jnp
from jax import lax
import numpy as np

D_MODEL = 1024
BATCH = 8
SEQ = 2048
DEPTH = 2
DEC_BATCH = 128
DEC_SEQ = 8
PAST_LEN = 2048
PAGE_SIZE = 128

N_A = DEPTH // 2
N_B = DEPTH - N_A
MIX_WIDTH = D_MODEL
TOK_WIDTH = MIX_WIDTH // 2
MEM_WIDTH = MIX_WIDTH - TOK_WIDTH
POOL_WINDOWS = (2, 4, 8, 16)
N_POOL_GROUPS = len(POOL_WINDOWS)
POOL_GROUP = TOK_WIDTH // N_POOL_GROUPS
POOL_STATE = max(POOL_WINDOWS) - 1
FOX_HEAD_DIM = 64
FOX_HEADS = TOK_WIDTH // FOX_HEAD_DIM
MEM_HEADS = 4
MEM_HEAD_DIM = MEM_WIDTH // MEM_HEADS
N_MEM = 256
D_FF = -(-8 * D_MODEL // (3 * 256)) * 256
QBLOCK = 128
EPS = 1e-6
FGATE_BIAS = 3.0

kernel_name = "yoco_pool_fox_memory_decoder_step"


def rmsnorm(x, g):
    xf = x.astype(jnp.float32)
    y = xf * lax.rsqrt(jnp.mean(xf * xf, axis=-1, keepdims=True) + EPS)
    return (y * g.astype(jnp.float32)).astype(x.dtype)


def swiglu(x, w_gu, w_down):
    g, u = jnp.split(x @ w_gu, 2, axis=-1)
    return (jax.nn.silu(g) * u) @ w_down


def pool_mix(u_prev, u_new, w_pool, scale):
    B, S, _ = u_new.shape
    P = u_prev.shape[1]
    u_ext = jnp.concatenate([u_prev, u_new], axis=1)
    c = jnp.cumsum(u_ext.astype(jnp.float32), axis=1)
    c = jnp.pad(c, ((0, 0), (1, 0), (0, 0))).reshape(B, P + S + 1, N_POOL_GROUPS, POOL_GROUP)
    hi = P + 1 + jnp.arange(S)
    win = jnp.array(POOL_WINDOWS, dtype=jnp.int32)
    lo = jnp.maximum(hi[:, None] - win[None, :], 0)
    c_hi = c[:, hi]
    c_lo = c[:, lo, jnp.arange(N_POOL_GROUPS)[None, :]]
    count = (hi[:, None] - lo).astype(jnp.float32)[..., None]
    y = (c_hi - c_lo) / count - u_new.reshape(B, S, N_POOL_GROUPS, POOL_GROUP).astype(jnp.float32)
    z = jnp.einsum('bsgc,gcd->bsgd', y.astype(u_new.dtype), w_pool).reshape(B, S, TOK_WIDTH)
    return z * scale, u_ext[:, -POOL_STATE:]


def fox_attention(q, k, v, f_q, f_k, q_pos, k_pos):
    B, Sq, H, Dh = q.shape
    blk = QBLOCK if Sq % QBLOCK == 0 else Sq
    nb = Sq // blk
    scale = Dh ** -0.5
    qb = q.reshape(B, nb, blk, H, Dh).transpose(1, 0, 2, 3, 4)
    fqb = f_q.reshape(B, nb, blk, H).transpose(1, 0, 3, 2)
    pb = q_pos.reshape(nb, blk)
    fk_t = f_k.transpose(0, 2, 1)

    def one_block(args):
        qi, fqi, pi = args
        s = jnp.einsum('bqhd,bkhd->bhqk', qi, k).astype(jnp.float32) * scale
        s = s + (fqi[..., :, None] - fk_t[..., None, :])
        mask = k_pos[None, :] <= pi[:, None]
        p = jax.nn.softmax(jnp.where(mask, s, -jnp.inf), axis=-1)
        return jnp.einsum('bhqk,bkhd->bqhd', p.astype(v.dtype), v)

    out = lax.map(one_block, (qb, fqb, pb))
    return out.transpose(1, 0, 2, 3, 4).reshape(B, Sq, H, Dh)


def mem_attention(q, mk, mv):
    s = jnp.einsum('bqhd,bmhd->bhqm', q, mk).astype(jnp.float32) * (q.shape[-1] ** -0.5)
    p = jax.nn.softmax(s, axis=-1)
    return jnp.einsum('bhqm,bmhd->bqhd', p.astype(mv.dtype), mv)


def memory_kv(mem, g_mem, w_mem_kv, g_mem_k):
    B, M, _ = mem.shape
    k, v = jnp.split(rmsnorm(mem, g_mem) @ w_mem_kv, 2, axis=-1)
    k = rmsnorm(k.reshape(B, M, MEM_HEADS, MEM_HEAD_DIM), g_mem_k)
    return k, v.reshape(B, M, MEM_HEADS, MEM_HEAD_DIM)


def shared_kv(h, g_kv, w_kv, g_fox_k, w_fg, b_fg):
    B, S, _ = h.shape
    hn = rmsnorm(h, g_kv)
    k, v = jnp.split(hn @ w_kv, 2, axis=-1)
    k = rmsnorm(k.reshape(B, S, FOX_HEADS, FOX_HEAD_DIM), g_fox_k)
    v = v.reshape(B, S, FOX_HEADS, FOX_HEAD_DIM)
    logf = jax.nn.log_sigmoid((hn @ w_fg + b_fg).astype(jnp.float32))
    return k, v, logf


def decoder(x, q_pos, pool_prev, mem_k, mem_v, past, g_mix, w_in, w_out, g_ffn, w_gu, w_down,
            g_mem_q, w_pool, pool_scale, g_kv, w_kv, g_fox_k, w_fg, b_fg, g_fox_q):
    B, S, _ = x.shape
    h = x
    pool_states = []
    for l in range(DEPTH):
        z = rmsnorm(h, g_mix[l]) @ w_in[l]
        z_tok, z_mem = z[..., :TOK_WIDTH], z[..., TOK_WIDTH:]
        if l < N_A:
            tok, st = pool_mix(pool_prev[l], z_tok, w_pool[l], pool_scale[l])
            pool_states.append(st)
        else:
            q = rmsnorm(z_tok.reshape(B, S, FOX_HEADS, FOX_HEAD_DIM), g_fox_q[l - N_A])
            tok = fox_attention(q, k_all, v_all, f_q, f_all, q_pos, k_pos).reshape(B, S, TOK_WIDTH)
        qm = rmsnorm(z_mem.reshape(B, S, MEM_HEADS, MEM_HEAD_DIM), g_mem_q[l])
        mo = mem_attention(qm, mem_k[l], mem_v[l]).reshape(B, S, MEM_WIDTH)
        h = h + jnp.concatenate([tok, mo], axis=-1) @ w_out[l]
        h = h + swiglu(rmsnorm(h, g_ffn[l]), w_gu[l], w_down[l])
        if l == N_A - 1:
            k_new, v_new, logf_new = shared_kv(h, g_kv, w_kv, g_fox_k, w_fg, b_fg)
            if past is None:
                k_all, v_all, logf_all = k_new, v_new, logf_new
                k_pos = q_pos
            else:
                k_past, v_past, logf_past = past
                k_all = jnp.concatenate([k_past.astype(k_new.dtype), k_new], axis=1)
                v_all = jnp.concatenate([v_past.astype(v_new.dtype), v_new], axis=1)
                logf_all = jnp.concatenate([logf_past.astype(jnp.float32), logf_new], axis=1)
                k_pos = jnp.arange(k_all.shape[1])
            f_all = jnp.cumsum(logf_all, axis=1)
            f_q = f_all[:, -S:]
    return h, jnp.stack(pool_states), k_new, v_new, logf_new.astype(x.dtype)


def setup_inputs(seed: int = 0) -> dict:
    key = jax.random.key(seed)
    ks = iter(jax.random.split(key, 40))
    nrm = lambda shape, s=1.0: jax.random.normal(next(ks), shape, jnp.float32) * s
    gain = lambda shape: 1.0 + 0.1 * jax.random.normal(next(ks), shape, jnp.float32)
    n_pages = PAST_LEN // PAGE_SIZE
    n_phys = -(-5 * DEC_BATCH * n_pages // 4)
    page_table = jax.random.permutation(next(ks), n_phys)[:DEC_BATCH * n_pages]
    page_table = page_table.reshape(DEC_BATCH, n_pages).astype(jnp.int32)
    return {
        "x_prompt": nrm((BATCH, SEQ, D_MODEL)),
        "x_sample": nrm((DEC_BATCH, DEC_SEQ, D_MODEL)),
        "cache_mem_k": nrm((DEPTH, DEC_BATCH, N_MEM, MEM_HEADS, MEM_HEAD_DIM)),
        "cache_mem_v": nrm((DEPTH, DEC_BATCH, N_MEM, MEM_HEADS, MEM_HEAD_DIM)),
        "state_pool": nrm((N_A, DEC_BATCH, POOL_STATE, TOK_WIDTH)),
        "cache_k": nrm((n_phys, PAGE_SIZE, FOX_HEADS, FOX_HEAD_DIM)),
        "cache_v": nrm((n_phys, PAGE_SIZE, FOX_HEADS, FOX_HEAD_DIM)),
        "cache_logf": jax.nn.log_sigmoid(FGATE_BIAS + nrm((n_phys, PAGE_SIZE, FOX_HEADS))),
        "page_table": page_table,
        "mem_prompt": nrm((BATCH, N_MEM, D_MODEL)),
        "g_mix": gain((DEPTH, D_MODEL)),
        "w_in": nrm((DEPTH, D_MODEL, MIX_WIDTH), D_MODEL ** -0.5),
        "w_out": nrm((DEPTH, MIX_WIDTH, D_MODEL), MIX_WIDTH ** -0.5),
        "g_ffn": gain((DEPTH, D_MODEL)),
        "w_gu": nrm((DEPTH, D_MODEL, 2 * D_FF), D_MODEL ** -0.5),
        "w_down": nrm((DEPTH, D_FF, D_MODEL), D_FF ** -0.5),
        "g_mem": gain((DEPTH, D_MODEL)),
        "w_mem_kv": nrm((DEPTH, D_MODEL, 2 * MEM_WIDTH), D_MODEL ** -0.5),
        "g_mem_q": gain((DEPTH, MEM_HEAD_DIM)),
        "g_mem_k": gain((DEPTH, MEM_HEAD_DIM)),
        "w_pool": nrm((N_A, N_POOL_GROUPS, POOL_GROUP, POOL_GROUP), POOL_GROUP ** -0.5),
        "pool_scale": gain((N_A, TOK_WIDTH)),
        "g_kv": gain((D_MODEL,)),
        "w_kv": nrm((D_MODEL, 2 * TOK_WIDTH), D_MODEL ** -0.5),
        "g_fox_k": gain((FOX_HEAD_DIM,)),
        "w_fg": nrm((D_MODEL, FOX_HEADS), D_MODEL ** -0.5),
        "b_fg": FGATE_BIAS + nrm((FOX_HEADS,), 0.1),
        "g_fox_q": gain((N_B, FOX_HEAD_DIM)),
    }


def reference(x_prompt, x_sample, cache_mem_k, cache_mem_v, state_pool, cache_k, cache_v, cache_logf,
              page_table, mem_prompt, g_mix, w_in, w_out, g_ffn, w_gu, w_down, g_mem, w_mem_kv,
              g_mem_q, g_mem_k, w_pool, pool_scale, g_kv, w_kv, g_fox_k, w_fg, b_fg, g_fox_q):
    weights = (g_mix, w_in, w_out, g_ffn, w_gu, w_down, g_mem_q, w_pool, pool_scale,
               g_kv, w_kv, g_fox_k, w_fg, b_fg, g_fox_q)
    B, S, _ = x_prompt.shape
    mem_kv = [memory_kv(mem_prompt, g_mem[l], w_mem_kv[l], g_mem_k[l]) for l in range(DEPTH)]
    mem_k_prompt = jnp.stack([kv[0] for kv in mem_kv])
    mem_v_prompt = jnp.stack([kv[1] for kv in mem_kv])
    pool_empty = jnp.zeros((N_A, B, 0, TOK_WIDTH), x_prompt.dtype)
    y_prompt, pool_state_prompt, k_p, v_p, logf_p = decoder(
        x_prompt, jnp.arange(S), pool_empty, mem_k_prompt, mem_v_prompt, None, *weights)
    n_pp = S // PAGE_SIZE
    k_rows_prompt = k_p.reshape(B, n_pp, PAGE_SIZE, FOX_HEADS, FOX_HEAD_DIM)
    v_rows_prompt = v_p.reshape(B, n_pp, PAGE_SIZE, FOX_HEADS, FOX_HEAD_DIM)
    logf_rows_prompt = logf_p.reshape(B, n_pp, PAGE_SIZE, FOX_HEADS)
    DB, DS, _ = x_sample.shape
    n_pages = page_table.shape[1]
    past_len = n_pages * PAGE_SIZE

    def gather(c):
        g = c[page_table]
        return g.reshape((DB, past_len) + c.shape[2:])

    past = (gather(cache_k), gather(cache_v), gather(cache_logf))
    y_sample, pool_state_sample, k_rows_sample, v_rows_sample, logf_rows_sample = decoder(
        x_sample, past_len + jnp.arange(DS), state_pool, cache_mem_k, cache_mem_v, past, *weights)
    return (y_prompt, y_sample, k_rows_prompt, v_rows_prompt, logf_rows_prompt, mem_k_prompt,
            mem_v_prompt, pool_state_prompt, k_rows_sample, v_rows_sample, logf_rows_sample,
            pool_state_sample)
```

```python
import functools

import jax
import jax.numpy as jnp
from jax import lax
from jax.experimental import pallas as pl
from jax.experimental.pallas import tpu as pltpu

D_MODEL = 1024
TOK_WIDTH = 512
MEM_WIDTH = 512
POOL_WINDOWS = (2, 4, 8, 16)
POOL_GROUP = 128
POOL_STATE = 15
POOL_HIST = 16
FOX_HEADS = 8
FOX_HEAD_DIM = 64
MEM_HEADS = 4
MEM_HEAD_DIM = 128
N_MEM = 256
D_FF = 2816
PAGE_SIZE = 128
EPS = 1e-6
LANES = 128
VMEM_LIMIT = 56 * 1024 * 1024

F32 = jnp.float32
BF16 = jnp.bfloat16
NT_DIMS = (((1,), (1,)), ((), ()))


def _rmsnorm(x, g):
    return x * lax.rsqrt(jnp.mean(x * x, axis=-1, keepdims=True) + EPS) * g


def _headnorm128(z, g):
    outs = []
    for h in range(z.shape[-1] // LANES):
        zh = z[:, h * LANES:(h + 1) * LANES]
        r = lax.rsqrt(jnp.mean(zh * zh, axis=-1, keepdims=True) + EPS)
        outs.append(zh * r * g[:, h * LANES:(h + 1) * LANES])
    return jnp.concatenate(outs, axis=-1)


def _headnorm64(z, g):
    outs = []
    for p in range(z.shape[-1] // LANES):
        zp = z[:, p * LANES:(p + 1) * LANES]
        sq = zp * zp
        lane = lax.broadcasted_iota(jnp.int32, zp.shape, 1)
        lo = lane < FOX_HEAD_DIM
        s_lo = jnp.sum(jnp.where(lo, sq, 0.0), axis=-1, keepdims=True)
        s_hi = jnp.sum(jnp.where(lo, 0.0, sq), axis=-1, keepdims=True)
        r = jnp.where(lo, lax.rsqrt(s_lo / FOX_HEAD_DIM + EPS), lax.rsqrt(s_hi / FOX_HEAD_DIM + EPS))
        outs.append(zp * r * g[:, p * LANES:(p + 1) * LANES])
    return jnp.concatenate(outs, axis=-1)


def _proj_in_kernel(x_ref, g_ref, w_ref, gtok_ref, gmem_ref, tok_ref, qm_ref, *, fox):
    xn = _rmsnorm(x_ref[...], g_ref[...]).astype(BF16)
    z = jnp.dot(xn, w_ref[...], preferred_element_type=F32)
    z_tok = z[:, :TOK_WIDTH]
    if fox:
        tok_ref[...] = (_headnorm64(z_tok, gtok_ref[...]) * (FOX_HEAD_DIM ** -0.5)).astype(tok_ref.dtype)
    else:
        tok_ref[...] = z_tok
    qm_ref[...] = _headnorm128(z[:, TOK_WIDTH:], gmem_ref[...]).astype(qm_ref.dtype)


def _proj_in(x, g, w, g_tok, g_mem, *, fox, tm, act):
    t = x.shape[0]
    row = lambda i: (i, 0)
    fixed = lambda i: (0, 0)
    return pl.pallas_call(
        functools.partial(_proj_in_kernel, fox=fox),
        out_shape=(jax.ShapeDtypeStruct((t, TOK_WIDTH), act if fox else F32),
                   jax.ShapeDtypeStruct((t, MEM_WIDTH), act)),
        grid=(t // tm,),
        in_specs=[pl.BlockSpec((tm, D_MODEL), row),
                  pl.BlockSpec((1, D_MODEL), fixed),
                  pl.BlockSpec((D_MODEL, D_MODEL), fixed),
                  pl.BlockSpec((1, TOK_WIDTH), fixed),
                  pl.BlockSpec((1, MEM_WIDTH), fixed)],
        out_specs=(pl.BlockSpec((tm, TOK_WIDTH), row), pl.BlockSpec((tm, MEM_WIDTH), row)),
        compiler_params=pltpu.CompilerParams(dimension_semantics=("arbitrary",),
                                             vmem_limit_bytes=VMEM_LIMIT),
        name="proj_in_fox" if fox else "proj_in_pool",
    )(x, g, w, g_tok, g_mem)


def _log_sigmoid(x):
    return jnp.minimum(x, 0.0) - jnp.log(1.0 + jnp.exp(-jnp.abs(x)))


def _proj_kv_kernel(x_ref, g_ref, w_ref, gk_ref, wfg_ref, bfg_ref,
                    k_ref, v_ref, kb_ref, vb_ref, logf_ref):
    xn = _rmsnorm(x_ref[...], g_ref[...]).astype(BF16)
    z = jnp.dot(xn, w_ref[...], preferred_element_type=F32)
    k = _headnorm64(z[:, :TOK_WIDTH], gk_ref[...])
    v = z[:, TOK_WIDTH:]
    k_ref[...] = k
    v_ref[...] = v
    kb_ref[...] = k.astype(BF16)
    vb_ref[...] = v.astype(BF16)
    gate = lax.dot_general(wfg_ref[...], xn, NT_DIMS, preferred_element_type=F32) + bfg_ref[...]
    logf_ref[...] = _log_sigmoid(gate)


def _proj_kv(x, g, w, g_k, wfg_t, bfg, *, tm):
    t = x.shape[0]
    row = lambda i: (i, 0)
    fixed = lambda i: (0, 0)
    return pl.pallas_call(
        _proj_kv_kernel,
        out_shape=(jax.ShapeDtypeStruct((t, TOK_WIDTH), F32),
                   jax.ShapeDtypeStruct((t, TOK_WIDTH), F32),
                   jax.ShapeDtypeStruct((t, TOK_WIDTH), BF16),
                   jax.ShapeDtypeStruct((t, TOK_WIDTH), BF16),
                   jax.ShapeDtypeStruct((FOX_HEADS, t), F32)),
        grid=(t // tm,),
        in_specs=[pl.BlockSpec((tm, D_MODEL), row),
                  pl.BlockSpec((1, D_MODEL), fixed),
                  pl.BlockSpec((D_MODEL, D_MODEL), fixed),
                  pl.BlockSpec((1, TOK_WIDTH), fixed),
                  pl.BlockSpec((FOX_HEADS, D_MODEL), fixed),
                  pl.BlockSpec((FOX_HEADS, 1), fixed)],
        out_specs=(pl.BlockSpec((tm, TOK_WIDTH), row), pl.BlockSpec((tm, TOK_WIDTH), row),
                   pl.BlockSpec((tm, TOK_WIDTH), row), pl.BlockSpec((tm, TOK_WIDTH), row),
                   pl.BlockSpec((FOX_HEADS, tm), lambda i: (0, i))),
        compiler_params=pltpu.CompilerParams(dimension_semantics=("arbitrary",),
                                             vmem_limit_bytes=VMEM_LIMIT),
        name="proj_kv",
    )(x, g, w, g_k, wfg_t, bfg)


def _proj_memkv_kernel(x_ref, g_ref, w_ref, gk_ref, k_ref, v_ref):
    xn = _rmsnorm(x_ref[...], g_ref[0]).astype(BF16)
    z = jnp.dot(xn, w_ref[0], preferred_element_type=F32)
    k_ref[0] = _headnorm128(z[:, :MEM_WIDTH], gk_ref[0])
    v_ref[0] = z[:, MEM_WIDTH:]


def _proj_memkv(x, g, w, g_k, *, tm):
    t = x.shape[0]
    n_layers = w.shape[0]
    return pl.pallas_call(
        _proj_memkv_kernel,
        out_shape=(jax.ShapeDtypeStruct((n_layers, t, MEM_WIDTH), F32),
                   jax.ShapeDtypeStruct((n_layers, t, MEM_WIDTH), F32)),
        grid=(n_layers, t // tm),
        in_specs=[pl.BlockSpec((tm, D_MODEL), lambda l, i: (i, 0)),
                  pl.BlockSpec((1, 1, D_MODEL), lambda l, i: (l, 0, 0)),
                  pl.BlockSpec((1, D_MODEL, D_MODEL), lambda l, i: (l, 0, 0)),
                  pl.BlockSpec((1, 1, MEM_WIDTH), lambda l, i: (l, 0, 0))],
        out_specs=(pl.BlockSpec((1, tm, MEM_WIDTH), lambda l, i: (l, i, 0)),
                   pl.BlockSpec((1, tm, MEM_WIDTH), lambda l, i: (l, i, 0))),
        compiler_params=pltpu.CompilerParams(dimension_semantics=("arbitrary", "arbitrary"),
                                             vmem_limit_bytes=VMEM_LIMIT),
        name="proj_memkv",
    )(x, g, w, g_k)


def _pool_kernel(u_ref, prev_ref, w_ref, scale_ref, o_ref, ext_ref, *, n_prev, chunk):
    bb, s_len, _ = u_ref.shape
    ext_ref[:, 0:POOL_HIST, :] = prev_ref[...]
    ext_ref[:, POOL_HIST:, :] = u_ref[...]
    for c in range(s_len // chunk):
        r0 = POOL_HIST + c * chunk
        pos = c * chunk + lax.broadcasted_iota(jnp.int32, (1, chunk, 1), 1)
        for gi, win in enumerate(POOL_WINDOWS):
            lanes = slice(gi * POOL_GROUP, (gi + 1) * POOL_GROUP)
            u_new = ext_ref[:, r0:r0 + chunk, lanes]
            acc = u_new
            for k in range(1, win):
                acc = acc + ext_ref[:, r0 - k:r0 - k + chunk, lanes]
            count = jnp.minimum(win, n_prev + pos + 1).astype(F32)
            y = acc / count - u_new
            z = jnp.dot(y.reshape(bb * chunk, POOL_GROUP).astype(BF16), w_ref[gi],
                        preferred_element_type=F32)
            z = z.reshape(bb, chunk, POOL_GROUP) * scale_ref[:, lanes]
            o_ref[:, c * chunk:(c + 1) * chunk, lanes] = z.astype(o_ref.dtype)


def _pool_mix(u, prev, w_pool, scale, *, n_prev, bb, chunk, act):
    b, s_len, _ = u.shape
    return pl.pallas_call(
        functools.partial(_pool_kernel, n_prev=n_prev, chunk=chunk),
        out_shape=jax.ShapeDtypeStruct((b, s_len, TOK_WIDTH), act),
        grid=(b // bb,),
        in_specs=[pl.BlockSpec((bb, s_len, TOK_WIDTH), lambda i: (i, 0, 0)),
                  pl.BlockSpec((bb, POOL_HIST, TOK_WIDTH), lambda i: (i, 0, 0)),
                  pl.BlockSpec((len(POOL_WINDOWS), POOL_GROUP, POOL_GROUP), lambda i: (0, 0, 0)),
                  pl.BlockSpec((1, TOK_WIDTH), lambda i: (0, 0))],
        out_specs=pl.BlockSpec((bb, s_len, TOK_WIDTH), lambda i: (i, 0, 0)),
        scratch_shapes=[pltpu.VMEM((bb, POOL_HIST + s_len, TOK_WIDTH), F32)],
        compiler_params=pltpu.CompilerParams(dimension_semantics=("arbitrary",),
                                             vmem_limit_bytes=VMEM_LIMIT),
        name="pool_mix",
    )(u, prev, w_pool, scale)


def _mem_attn_kernel(q_ref, k_ref, v_ref, o_ref):
    for h in range(MEM_HEADS):
        lanes = slice(h * MEM_HEAD_DIM, (h + 1) * MEM_HEAD_DIM)
        q = q_ref[:, :, lanes].astype(BF16)
        k = k_ref[:, :, lanes].astype(BF16)
        v = v_ref[:, :, lanes].astype(BF16)
        s = jnp.einsum("bqd,bkd->bqk", q, k, preferred_element_type=F32) * (MEM_HEAD_DIM ** -0.5)
        m = jnp.max(s, axis=-1, keepdims=True)
        p = jnp.exp(s - m)
        l = jnp.sum(p, axis=-1, keepdims=True)
        o = jnp.einsum("bqk,bkd->bqd", p.astype(BF16), v, preferred_element_type=F32)
        o_ref[:, :, lanes] = (o / l).astype(o_ref.dtype)


def _mem_attn(q, mk, mv, *, bb, ts):
    b, s_len, _ = q.shape
    return pl.pallas_call(
        _mem_attn_kernel,
        out_shape=jax.ShapeDtypeStruct((b, s_len, MEM_WIDTH), q.dtype),
        grid=(b // bb, s_len // ts),
        in_specs=[pl.BlockSpec((bb, ts, MEM_WIDTH), lambda i, j: (i, j, 0)),
                  pl.BlockSpec((bb, N_MEM, MEM_WIDTH), lambda i, j: (i, 0, 0)),
                  pl.BlockSpec((bb, N_MEM, MEM_WIDTH), lambda i, j: (i, 0, 0))],
        out_specs=pl.BlockSpec((bb, ts, MEM_WIDTH), lambda i, j: (i, j, 0)),
        compiler_params=pltpu.CompilerParams(dimension_semantics=("arbitrary", "arbitrary"),
                                             vmem_limit_bytes=VMEM_LIMIT),
        name="mem_attn",
    )(q, mk, mv)


def _out_ffn_kernel(h_ref, tok_ref, mo_ref, wout_ref, g_ref, wg_ref, wu_ref, wd_ref,
                    o_ref, xn_ref, acc_ref):
    kf = pl.program_id(1)

    @pl.when(kf == 0)
    def _():
        h1 = (h_ref[...]
              + jnp.dot(tok_ref[...].astype(BF16), wout_ref[:TOK_WIDTH, :], preferred_element_type=F32)
              + jnp.dot(mo_ref[...].astype(BF16), wout_ref[TOK_WIDTH:, :], preferred_element_type=F32))
        acc_ref[...] = h1
        xn_ref[...] = _rmsnorm(h1, g_ref[...]).astype(BF16)

    x = xn_ref[...]
    gate = jnp.dot(x, wg_ref[...], preferred_element_type=F32)
    up = jnp.dot(x, wu_ref[...], preferred_element_type=F32)
    act = (gate * (1.0 / (1.0 + jnp.exp(-gate))) * up).astype(BF16)
    acc_ref[...] += jnp.dot(act, wd_ref[...], preferred_element_type=F32)

    @pl.when(kf == pl.num_programs(1) - 1)
    def _():
        o_ref[...] = acc_ref[...]


def _out_ffn(h, tok, mo, w_out, g, w_gu, w_down, *, tm, tf):
    t = h.shape[0]
    nf = D_FF // tf
    row = lambda i, k: (i, 0)
    fixed = lambda i, k: (0, 0)
    return pl.pallas_call(
        _out_ffn_kernel,
        out_shape=jax.ShapeDtypeStruct((t, D_MODEL), F32),
        grid=(t // tm, nf),
        in_specs=[pl.BlockSpec((tm, D_MODEL), row),
                  pl.BlockSpec((tm, TOK_WIDTH), row),
                  pl.BlockSpec((tm, MEM_WIDTH), row),
                  pl.BlockSpec((D_MODEL, D_MODEL), fixed),
                  pl.BlockSpec((1, D_MODEL), fixed),
                  pl.BlockSpec((D_MODEL, tf), lambda i, k: (0, k)),
                  pl.BlockSpec((D_MODEL, tf), lambda i, k: (0, nf + k)),
                  pl.BlockSpec((tf, D_MODEL), lambda i, k: (k, 0))],
        out_specs=pl.BlockSpec((tm, D_MODEL), row),
        scratch_shapes=[pltpu.VMEM((tm, D_MODEL), BF16), pltpu.VMEM((tm, D_MODEL), F32)],
        compiler_params=pltpu.CompilerParams(dimension_semantics=("arbitrary", "arbitrary"),
                                             vmem_limit_bytes=VMEM_LIMIT),
        name="out_ffn",
    )(h, tok, mo, w_out, g, w_gu, w_gu, w_down)


def _lane_cumsum(x):
    n = x.shape[-1]
    lane = lax.broadcasted_iota(jnp.int32, x.shape, 1)
    k = 1
    while k < n:
        x = x + jnp.where(lane >= k, pltpu.roll(x, k, axis=1), 0.0)
        k *= 2
    return x


def _cumsum_kernel(x_ref, o_ref):
    o_ref[...] = _lane_cumsum(x_ref[...])


def _cumsum_rows(x, *, seg):
    rows, t = x.shape
    return pl.pallas_call(
        _cumsum_kernel,
        out_shape=jax.ShapeDtypeStruct((rows, t), F32),
        grid=(t // seg,),
        in_specs=[pl.BlockSpec((rows, seg), lambda i: (0, i))],
        out_specs=pl.BlockSpec((rows, seg), lambda i: (0, i)),
        compiler_params=pltpu.CompilerParams(dimension_semantics=("arbitrary",)),
        name="logf_cumsum",
    )(x)


def _fox_prompt_kernel(q_ref, k_ref, v_ref, frow_ref, fcol_ref, o_ref, m_ref, l_ref, acc_ref, *, blk):
    qi = pl.program_id(2)
    q = q_ref[0]
    lane = lax.broadcasted_iota(jnp.int32, q.shape, 1)
    row = lax.broadcasted_iota(jnp.int32, (blk, blk), 0)
    col = lax.broadcasted_iota(jnp.int32, (blk, blk), 1)

    for j in range(2):
        in_head = (lane < FOX_HEAD_DIM) if j == 0 else (lane >= FOX_HEAD_DIM)
        qj = jnp.where(in_head, q, jnp.zeros_like(q))
        fq = fcol_ref[0, :, j:j + 1]
        m_ref[j] = jnp.full((blk, 1), -jnp.inf, F32)
        l_ref[j] = jnp.zeros((blk, 1), F32)
        acc_ref[j] = jnp.zeros((blk, LANES), F32)

        def step(ki, masked, qj=qj, fq=fq, j=j):
            ks = pl.multiple_of(ki * blk, blk)
            kt = k_ref[0, pl.ds(ks, blk), :]
            vt = v_ref[0, pl.ds(ks, blk), :]
            s = lax.dot_general(qj, kt, NT_DIMS, preferred_element_type=F32)
            s = s + (fq - frow_ref[0, j:j + 1, pl.ds(ks, blk)])
            if masked:
                s = jnp.where(col <= row, s, -jnp.inf)
            m_prev = m_ref[j]
            m_new = jnp.maximum(m_prev, jnp.max(s, axis=-1, keepdims=True))
            alpha = jnp.exp(m_prev - m_new)
            p = jnp.exp(s - m_new)
            l_ref[j] = alpha * l_ref[j] + jnp.sum(p, axis=-1, keepdims=True)
            acc_ref[j] = alpha * acc_ref[j] + jnp.dot(p.astype(BF16), vt, preferred_element_type=F32)
            m_ref[j] = m_new

        def body(ki, carry):
            step(ki, False)
            return carry

        lax.fori_loop(0, qi, body, 0)
        step(qi, True)

    o = jnp.where(lane < FOX_HEAD_DIM, acc_ref[0] / l_ref[0], acc_ref[1] / l_ref[1])
    o_ref[0] = o.astype(o_ref.dtype)


def _fox_prompt(q, k, v, frow, fcol, *, blk):
    b, s_len, _ = q.shape
    n_pairs = TOK_WIDTH // LANES
    nq = s_len // blk
    return pl.pallas_call(
        functools.partial(_fox_prompt_kernel, blk=blk),
        out_shape=jax.ShapeDtypeStruct((b, s_len, TOK_WIDTH), BF16),
        grid=(b, n_pairs, nq),
        in_specs=[pl.BlockSpec((1, blk, LANES), lambda bi, hp, qi: (bi, qi, hp)),
                  pl.BlockSpec((1, s_len, LANES), lambda bi, hp, qi: (bi, 0, hp)),
                  pl.BlockSpec((1, s_len, LANES), lambda bi, hp, qi: (bi, 0, hp)),
                  pl.BlockSpec((1, 2, s_len), lambda bi, hp, qi: (hp, 0, bi)),
                  pl.BlockSpec((1, blk, 2), lambda bi, hp, qi: (hp, bi * nq + qi, 0))],
        out_specs=pl.BlockSpec((1, blk, LANES), lambda bi, hp, qi: (bi, qi, hp)),
        scratch_shapes=[pltpu.VMEM((2, blk, 1), F32), pltpu.VMEM((2, blk, 1), F32),
                        pltpu.VMEM((2, blk, LANES), F32)],
        compiler_params=pltpu.CompilerParams(
            dimension_semantics=("arbitrary", "arbitrary", "arbitrary"),
            vmem_limit_bytes=VMEM_LIMIT),
        name="fox_prompt",
    )(q, k, v, frow, fcol)


def _lane_suffix_sum(x):
    n = x.shape[-1]
    lane = lax.broadcasted_iota(jnp.int32, x.shape, 1)
    k = 1
    while k < n:
        x = x + jnp.where(lane + k < n, pltpu.roll(x, n - k, axis=1), 0.0)
        k *= 2
    return x


def _fox_sample_kernel(pt_ref, q_ref, kn_ref, vn_ref, lfn_ref, kp_ref, vp_ref, lfp_ref, o_ref,
                       qblk_ref, pad_k_ref, pad_v_ref, cq_ref, run_ref, m_ref, l_ref, acc_ref):
    del pt_ref
    j = pl.program_id(1)
    n_q = q_ref.shape[1]
    n_rows = n_q * FOX_HEADS
    head = lax.broadcasted_iota(jnp.int32, (FOX_HEADS, TOK_WIDTH), 0)
    lane_head = jnp.right_shift(lax.broadcasted_iota(jnp.int32, (FOX_HEADS, TOK_WIDTH), 1), 6)
    own_head = head == lane_head

    def accumulate(s, bias8, v_bf16, visible=None):
        s = s + (jnp.concatenate([bias8] * n_q, axis=0) + cq_ref[...])
        if visible is not None:
            s = jnp.where(visible, s, -jnp.inf)
        m_prev = m_ref[...]
        m_new = jnp.maximum(m_prev, jnp.max(s, axis=-1, keepdims=True))
        alpha = jnp.exp(m_prev - m_new)
        p = jnp.exp(s - m_new)
        l_ref[...] = alpha * l_ref[...] + jnp.sum(p, axis=-1, keepdims=True)
        acc_ref[...] = alpha * acc_ref[...] + jnp.dot(p.astype(BF16), v_bf16,
                                                      preferred_element_type=F32)
        m_ref[...] = m_new

    @pl.when(j == 0)
    def _():
        q = q_ref[0].astype(F32)
        blocks = [jnp.where(own_head, jnp.broadcast_to(q[t:t + 1, :], (FOX_HEADS, TOK_WIDTH)), 0.0)
                  for t in range(n_q)]
        qblk_ref[...] = jnp.concatenate(blocks, axis=0).astype(BF16)
        pad_k_ref[...] = jnp.zeros_like(pad_k_ref)
        pad_v_ref[...] = jnp.zeros_like(pad_v_ref)
        pad_k_ref[0:n_q, :] = kn_ref[0]
        pad_v_ref[0:n_q, :] = vn_ref[0]
        c_new = _lane_cumsum(lfn_ref[0])
        cq_ref[...] = jnp.concatenate([c_new[:, t:t + 1] for t in range(n_q)], axis=0)
        run_ref[...] = jnp.zeros_like(run_ref)
        m_ref[...] = jnp.full(m_ref.shape, -jnp.inf, F32)
        l_ref[...] = jnp.zeros_like(l_ref)
        acc_ref[...] = jnp.zeros_like(acc_ref)
        s = lax.dot_general(qblk_ref[...], pad_k_ref[...].astype(BF16), NT_DIMS,
                            preferred_element_type=F32)
        t_of_row = jnp.right_shift(lax.broadcasted_iota(jnp.int32, (n_rows, PAGE_SIZE), 0), 3)
        key = lax.broadcasted_iota(jnp.int32, (n_rows, PAGE_SIZE), 1)
        accumulate(s, -c_new, pad_v_ref[...].astype(BF16), visible=key <= t_of_row)

    lf = lfp_ref[0]
    suffix = _lane_suffix_sum(lf)
    s = lax.dot_general(qblk_ref[...], kp_ref[0].astype(BF16), NT_DIMS, preferred_element_type=F32)
    accumulate(s, run_ref[...] + (suffix - lf), vp_ref[0].astype(BF16))
    run_ref[...] = run_ref[...] + suffix[:, 0:1]

    @pl.when(j == pl.num_programs(1) - 1)
    def _():
        o = acc_ref[...] / l_ref[...]
        for t in range(n_q):
            o_t = jnp.where(own_head, o[t * FOX_HEADS:(t + 1) * FOX_HEADS, :], 0.0)
            o_ref[0, t:t + 1, :] = jnp.sum(o_t, axis=0, keepdims=True)


def _fox_sample(page_table, q, k_new, v_new, logf_new_t, cache_k, cache_v, cache_logf_t):
    b, n_q, _ = q.shape
    n_pages = page_table.shape[0] // b
    n_rows = n_q * FOX_HEADS
    cur = lambda bi, j, pt: (bi, 0, 0)
    page = lambda bi, j, pt: (pt[bi * n_pages + (n_pages - 1 - j)], 0, 0)
    grid_spec = pltpu.PrefetchScalarGridSpec(
        num_scalar_prefetch=1,
        grid=(b, n_pages),
        in_specs=[pl.BlockSpec((1, n_q, TOK_WIDTH), cur),
                  pl.BlockSpec((1, n_q, TOK_WIDTH), cur),
                  pl.BlockSpec((1, n_q, TOK_WIDTH), cur),
                  pl.BlockSpec((1, FOX_HEADS, PAGE_SIZE), cur),
                  pl.BlockSpec((1, PAGE_SIZE, TOK_WIDTH), page),
                  pl.BlockSpec((1, PAGE_SIZE, TOK_WIDTH), page),
                  pl.BlockSpec((1, FOX_HEADS, PAGE_SIZE), page)],
        out_specs=pl.BlockSpec((1, n_q, TOK_WIDTH), cur),
        scratch_shapes=[pltpu.VMEM((n_rows, TOK_WIDTH), BF16),
                        pltpu.VMEM((PAGE_SIZE, TOK_WIDTH), F32),
                        pltpu.VMEM((PAGE_SIZE, TOK_WIDTH), F32),
                        pltpu.VMEM((n_rows, 1), F32),
                        pltpu.VMEM((FOX_HEADS, 1), F32),
                        pltpu.VMEM((n_rows, 1), F32),
                        pltpu.VMEM((n_rows, 1), F32),
                        pltpu.VMEM((n_rows, TOK_WIDTH), F32)])
    return pl.pallas_call(
        _fox_sample_kernel,
        out_shape=jax.ShapeDtypeStruct((b, n_q, TOK_WIDTH), F32),
        grid_spec=grid_spec,
        compiler_params=pltpu.CompilerParams(dimension_semantics=("arbitrary", "arbitrary"),
                                             vmem_limit_bytes=VMEM_LIMIT),
        name="fox_sample",
    )(page_table, q, k_new, v_new, logf_new_t, cache_k, cache_v, cache_logf_t)


def _decoder(x, pool_prev, n_prev, mem_k, mem_v, past, wts, *, tm, pool_bb, pool_chunk, mem_bb, mem_ts,
             act):
    b, s_len, _ = x.shape
    t = b * s_len
    h = x.reshape(t, D_MODEL)

    z_tok, qm = _proj_in(h, wts["g_mix"][0], wts["w_in"][0], wts["g_fox_q_t"], wts["g_mem_q_t"][0],
                         fox=False, tm=tm, act=act)
    z_tok = z_tok.reshape(b, s_len, TOK_WIDTH)
    tok = _pool_mix(z_tok, pool_prev, wts["w_pool"], wts["pool_scale"],
                    n_prev=n_prev, bb=pool_bb, chunk=pool_chunk, act=act)
    mo = _mem_attn(qm.reshape(b, s_len, MEM_WIDTH), mem_k[0], mem_v[0], bb=mem_bb, ts=mem_ts)
    h = _out_ffn(h, tok.reshape(t, TOK_WIDTH), mo.reshape(t, MEM_WIDTH), wts["w_out"][0],
                 wts["g_ffn"][0], wts["w_gu"][0], wts["w_down"][0], tm=tm, tf=D_FF // 2)

    k_new, v_new, k_bf, v_bf, logf_t = _proj_kv(h, wts["g_kv"], wts["w_kv"], wts["g_fox_k_t"],
                                                wts["w_fg_t"], wts["b_fg"], tm=tm)

    q, qm = _proj_in(h, wts["g_mix"][1], wts["w_in"][1], wts["g_fox_q_t"], wts["g_mem_q_t"][1],
                     fox=True, tm=tm, act=act)
    if past is None:
        f_row = _cumsum_rows(logf_t, seg=s_len)
        n_pairs = FOX_HEADS // 2
        f_row = f_row.reshape(n_pairs, 2, t)
        f_col = f_row.transpose(0, 2, 1)
        tok = _fox_prompt(q.reshape(b, s_len, TOK_WIDTH), k_bf.reshape(b, s_len, TOK_WIDTH),
                          v_bf.reshape(b, s_len, TOK_WIDTH), f_row, f_col, blk=512)
    else:
        page_table, cache_k, cache_v, cache_logf_t = past
        lfn = logf_t.reshape(FOX_HEADS, b, s_len).transpose(1, 0, 2)
        lfn = jnp.pad(lfn, ((0, 0), (0, 0), (0, PAGE_SIZE - s_len)))
        tok = _fox_sample(page_table, q.reshape(b, s_len, TOK_WIDTH),
                          k_new.reshape(b, s_len, TOK_WIDTH), v_new.reshape(b, s_len, TOK_WIDTH),
                          lfn, cache_k, cache_v, cache_logf_t)
    mo = _mem_attn(qm.reshape(b, s_len, MEM_WIDTH), mem_k[1], mem_v[1], bb=mem_bb, ts=mem_ts)
    h = _out_ffn(h, tok.reshape(t, TOK_WIDTH), mo.reshape(t, MEM_WIDTH), wts["w_out"][1],
                 wts["g_ffn"][1], wts["w_gu"][1], wts["w_down"][1], tm=tm, tf=D_FF // 2)
    return h.reshape(b, s_len, D_MODEL), z_tok, k_new, v_new, logf_t.T


def kernel(x_prompt, x_sample, cache_mem_k, cache_mem_v, state_pool, cache_k, cache_v, cache_logf,
           page_table, mem_prompt, g_mix, w_in, w_out, g_ffn, w_gu, w_down, g_mem, w_mem_kv,
           g_mem_q, g_mem_k, w_pool, pool_scale, g_kv, w_kv, g_fox_k, w_fg, b_fg, g_fox_q):
    depth = w_in.shape[0]
    b, s_len, _ = x_prompt.shape
    db, ds, _ = x_sample.shape
    wts = {
        "g_mix": g_mix.reshape(depth, 1, D_MODEL),
        "w_in": w_in.astype(BF16),
        "w_out": w_out.astype(BF16),
        "g_ffn": g_ffn.reshape(depth, 1, D_MODEL),
        "w_gu": w_gu.astype(BF16),
        "w_down": w_down.astype(BF16),
        "g_mem_q_t": jnp.tile(g_mem_q, (1, MEM_HEADS)).reshape(depth, 1, MEM_WIDTH),
        "w_pool": w_pool[0].astype(BF16),
        "pool_scale": pool_scale[0].reshape(1, TOK_WIDTH),
        "g_kv": g_kv.reshape(1, D_MODEL),
        "w_kv": w_kv.astype(BF16),
        "g_fox_k_t": jnp.tile(g_fox_k, FOX_HEADS).reshape(1, TOK_WIDTH),
        "w_fg_t": w_fg.T.astype(BF16),
        "b_fg": b_fg.reshape(FOX_HEADS, 1),
        "g_fox_q_t": jnp.tile(g_fox_q[0], FOX_HEADS).reshape(1, TOK_WIDTH),
    }

    mem_k_p, mem_v_p = _proj_memkv(mem_prompt.reshape(b * N_MEM, D_MODEL),
                                   g_mem.reshape(depth, 1, D_MODEL), w_mem_kv.astype(BF16),
                                   jnp.tile(g_mem_k, (1, MEM_HEADS)).reshape(depth, 1, MEM_WIDTH), tm=512)
    mem_k_p = mem_k_p.reshape(depth, b, N_MEM, MEM_WIDTH)
    mem_v_p = mem_v_p.reshape(depth, b, N_MEM, MEM_WIDTH)
    y_p, ztok_p, k_p, v_p, logf_p = _decoder(
        x_prompt, jnp.zeros((b, POOL_HIST, TOK_WIDTH), F32), 0, mem_k_p, mem_v_p, None, wts,
        tm=512, pool_bb=1, pool_chunk=256, mem_bb=1, mem_ts=512, act=BF16)
    n_pp = s_len // PAGE_SIZE
    head_shape = (FOX_HEADS, FOX_HEAD_DIM)

    n_phys = cache_k.shape[0]
    past = (page_table.reshape(-1),
            cache_k.reshape(n_phys, PAGE_SIZE, TOK_WIDTH),
            cache_v.reshape(n_phys, PAGE_SIZE, TOK_WIDTH),
            cache_logf.transpose(0, 2, 1))
    prev = jnp.pad(state_pool[0], ((0, 0), (POOL_HIST - POOL_STATE, 0), (0, 0)))
    y_s, ztok_s, k_s, v_s, logf_s = _decoder(
        x_sample, prev, POOL_STATE, cache_mem_k.reshape(depth, db, N_MEM, MEM_WIDTH),
        cache_mem_v.reshape(depth, db, N_MEM, MEM_WIDTH), past, wts,
        tm=512, pool_bb=32, pool_chunk=ds, mem_bb=8, mem_ts=ds, act=F32)

    pool_state_p = ztok_p[:, s_len - POOL_STATE:][None]
    pool_state_s = jnp.concatenate([state_pool[0], ztok_s], axis=1)[:, -POOL_STATE:][None]
    return (y_p, y_s,
            k_p.reshape((b, n_pp, PAGE_SIZE) + head_shape),
            v_p.reshape((b, n_pp, PAGE_SIZE) + head_shape),
            logf_p.reshape(b, n_pp, PAGE_SIZE, FOX_HEADS),
            mem_k_p.reshape(depth, b, N_MEM, MEM_HEADS, MEM_HEAD_DIM),
            mem_v_p.reshape(depth, b, N_MEM, MEM_HEADS, MEM_HEAD_DIM),
            pool_state_p,
            k_s.reshape((db, ds) + head_shape),
            v_s.reshape((db, ds) + head_shape),
            logf_s.reshape(db, ds, FOX_HEADS),
            pool_state_s)
```

```python
import functools

import jax
import jax.numpy as jnp
from jax import lax
from jax.experimental import pallas as pl
from jax.experimental.pallas import tpu as pltpu

D_MODEL = 1024
TOK_WIDTH = 512
MEM_WIDTH = 512
POOL_WINDOWS = (2, 4, 8, 16)
POOL_GROUP = 128
POOL_STATE = 15
POOL_HIST = 16
FOX_HEADS = 8
FOX_HEAD_DIM = 64
MEM_HEADS = 4
MEM_HEAD_DIM = 128
N_MEM = 256
D_FF = 2816
PAGE_SIZE = 128
EPS = 1e-6
LANES = 128
VMEM_LIMIT = 56 * 1024 * 1024
FOX_ROW_SPLIT = 2
PAGES_PER_STEP = 4

F32 = jnp.float32
BF16 = jnp.bfloat16
NT_DIMS = (((1,), (1,)), ((), ()))


def _rmsnorm(x, g):
    return x * lax.rsqrt(jnp.mean(x * x, axis=-1, keepdims=True) + EPS) * g


def _headnorm128(z, g):
    outs = []
    for h in range(z.shape[-1] // LANES):
        zh = z[:, h * LANES:(h + 1) * LANES]
        r = lax.rsqrt(jnp.mean(zh * zh, axis=-1, keepdims=True) + EPS)
        outs.append(zh * r * g[:, h * LANES:(h + 1) * LANES])
    return jnp.concatenate(outs, axis=-1)


def _headnorm64(z, g):
    outs = []
    for p in range(z.shape[-1] // LANES):
        zp = z[:, p * LANES:(p + 1) * LANES]
        sq = zp * zp
        lane = lax.broadcasted_iota(jnp.int32, zp.shape, 1)
        lo = lane < FOX_HEAD_DIM
        s_lo = jnp.sum(jnp.where(lo, sq, 0.0), axis=-1, keepdims=True)
        s_hi = jnp.sum(jnp.where(lo, 0.0, sq), axis=-1, keepdims=True)
        r = jnp.where(lo, lax.rsqrt(s_lo / FOX_HEAD_DIM + EPS), lax.rsqrt(s_hi / FOX_HEAD_DIM + EPS))
        outs.append(zp * r * g[:, p * LANES:(p + 1) * LANES])
    return jnp.concatenate(outs, axis=-1)


def _proj_in_kernel(x_ref, g_ref, w_ref, gtok_ref, gmem_ref, tok_ref, qm_ref, *, fox):
    xn = _rmsnorm(x_ref[...], g_ref[...]).astype(BF16)
    z = jnp.dot(xn, w_ref[...], preferred_element_type=F32)
    z_tok = z[:, :TOK_WIDTH]
    if fox:
        tok_ref[...] = (_headnorm64(z_tok, gtok_ref[...]) * (FOX_HEAD_DIM ** -0.5)).astype(tok_ref.dtype)
    else:
        tok_ref[...] = z_tok
    qm_ref[...] = _headnorm128(z[:, TOK_WIDTH:], gmem_ref[...]).astype(qm_ref.dtype)


def _proj_in(x, g, w, g_tok, g_mem, *, fox, tm, act):
    t = x.shape[0]
    row = lambda i: (i, 0)
    fixed = lambda i: (0, 0)
    return pl.pallas_call(
        functools.partial(_proj_in_kernel, fox=fox),
        out_shape=(jax.ShapeDtypeStruct((t, TOK_WIDTH), act if fox else F32),
                   jax.ShapeDtypeStruct((t, MEM_WIDTH), act)),
        grid=(t // tm,),
        in_specs=[pl.BlockSpec((tm, D_MODEL), row),
                  pl.BlockSpec((1, D_MODEL), fixed),
                  pl.BlockSpec((D_MODEL, D_MODEL), fixed),
                  pl.BlockSpec((1, TOK_WIDTH), fixed),
                  pl.BlockSpec((1, MEM_WIDTH), fixed)],
        out_specs=(pl.BlockSpec((tm, TOK_WIDTH), row), pl.BlockSpec((tm, MEM_WIDTH), row)),
        compiler_params=pltpu.CompilerParams(dimension_semantics=("arbitrary",),
                                             vmem_limit_bytes=VMEM_LIMIT),
        name="proj_in_fox" if fox else "proj_in_pool",
    )(x, g, w, g_tok, g_mem)


def _log_sigmoid(x):
    return jnp.minimum(x, 0.0) - jnp.log(1.0 + jnp.exp(-jnp.abs(x)))


def _proj_kv_kernel(x_ref, g_ref, w_ref, gk_ref, wfg_ref, bfg_ref,
                    k_ref, v_ref, kb_ref, vb_ref, logf_ref):
    xn = _rmsnorm(x_ref[...], g_ref[...]).astype(BF16)
    z = jnp.dot(xn, w_ref[...], preferred_element_type=F32)
    k = _headnorm64(z[:, :TOK_WIDTH], gk_ref[...])
    v = z[:, TOK_WIDTH:]
    k_ref[...] = k
    v_ref[...] = v
    kb_ref[...] = k.astype(BF16)
    vb_ref[...] = v.astype(BF16)
    gate = lax.dot_general(wfg_ref[...], xn, NT_DIMS, preferred_element_type=F32) + bfg_ref[...]
    logf_ref[...] = _log_sigmoid(gate)


def _proj_kv(x, g, w, g_k, wfg_t, bfg, *, tm):
    t = x.shape[0]
    row = lambda i: (i, 0)
    fixed = lambda i: (0, 0)
    return pl.pallas_call(
        _proj_kv_kernel,
        out_shape=(jax.ShapeDtypeStruct((t, TOK_WIDTH), F32),
                   jax.ShapeDtypeStruct((t, TOK_WIDTH), F32),
                   jax.ShapeDtypeStruct((t, TOK_WIDTH), BF16),
                   jax.ShapeDtypeStruct((t, TOK_WIDTH), BF16),
                   jax.ShapeDtypeStruct((FOX_HEADS, t), F32)),
        grid=(t // tm,),
        in_specs=[pl.BlockSpec((tm, D_MODEL), row),
                  pl.BlockSpec((1, D_MODEL), fixed),
                  pl.BlockSpec((D_MODEL, D_MODEL), fixed),
                  pl.BlockSpec((1, TOK_WIDTH), fixed),
                  pl.BlockSpec((FOX_HEADS, D_MODEL), fixed),
                  pl.BlockSpec((FOX_HEADS, 1), fixed)],
        out_specs=(pl.BlockSpec((tm, TOK_WIDTH), row), pl.BlockSpec((tm, TOK_WIDTH), row),
                   pl.BlockSpec((tm, TOK_WIDTH), row), pl.BlockSpec((tm, TOK_WIDTH), row),
                   pl.BlockSpec((FOX_HEADS, tm), lambda i: (0, i))),
        compiler_params=pltpu.CompilerParams(dimension_semantics=("arbitrary",),
                                             vmem_limit_bytes=VMEM_LIMIT),
        name="proj_kv",
    )(x, g, w, g_k, wfg_t, bfg)


def _proj_memkv_kernel(x_ref, g_ref, w_ref, gk_ref, k_ref, v_ref):
    xn = _rmsnorm(x_ref[...], g_ref[0]).astype(BF16)
    z = jnp.dot(xn, w_ref[0], preferred_element_type=F32)
    k = _headnorm128(z[:, :MEM_WIDTH], gk_ref[0])
    for h in range(MEM_HEADS):
        k_ref[0, :, h, :] = k[:, h * MEM_HEAD_DIM:(h + 1) * MEM_HEAD_DIM]
        v_ref[0, :, h, :] = z[:, MEM_WIDTH + h * MEM_HEAD_DIM:MEM_WIDTH + (h + 1) * MEM_HEAD_DIM]


def _proj_memkv(x, g, w, g_k, *, tm):
    t = x.shape[0]
    n_layers = w.shape[0]
    out_block = pl.BlockSpec((1, tm, MEM_HEADS, MEM_HEAD_DIM), lambda l, i: (l, i, 0, 0))
    return pl.pallas_call(
        _proj_memkv_kernel,
        out_shape=(jax.ShapeDtypeStruct((n_layers, t, MEM_HEADS, MEM_HEAD_DIM), F32),
                   jax.ShapeDtypeStruct((n_layers, t, MEM_HEADS, MEM_HEAD_DIM), F32)),
        grid=(n_layers, t // tm),
        in_specs=[pl.BlockSpec((tm, D_MODEL), lambda l, i: (i, 0)),
                  pl.BlockSpec((1, 1, D_MODEL), lambda l, i: (l, 0, 0)),
                  pl.BlockSpec((1, D_MODEL, D_MODEL), lambda l, i: (l, 0, 0)),
                  pl.BlockSpec((1, 1, MEM_WIDTH), lambda l, i: (l, 0, 0))],
        out_specs=(out_block, out_block),
        compiler_params=pltpu.CompilerParams(dimension_semantics=("arbitrary", "arbitrary"),
                                             vmem_limit_bytes=VMEM_LIMIT),
        name="proj_memkv",
    )(x, g, w, g_k)


def _pool_kernel(u_ref, prev_ref, w_ref, scale_ref, o_ref, ext_ref, *, n_prev, chunk):
    bb, s_len, _ = u_ref.shape
    ext_ref[:, 0:POOL_HIST, :] = prev_ref[...]
    ext_ref[:, POOL_HIST:, :] = u_ref[...]
    for c in range(s_len // chunk):
        r0 = POOL_HIST + c * chunk
        pos = c * chunk + lax.broadcasted_iota(jnp.int32, (1, chunk, 1), 1)
        for gi, win in enumerate(POOL_WINDOWS):
            lanes = slice(gi * POOL_GROUP, (gi + 1) * POOL_GROUP)
            u_new = ext_ref[:, r0:r0 + chunk, lanes]
            acc = u_new
            for k in range(1, win):
                acc = acc + ext_ref[:, r0 - k:r0 - k + chunk, lanes]
            count = jnp.minimum(win, n_prev + pos + 1).astype(F32)
            y = acc / count - u_new
            z = jnp.dot(y.reshape(bb * chunk, POOL_GROUP).astype(BF16), w_ref[gi],
                        preferred_element_type=F32)
            z = z.reshape(bb, chunk, POOL_GROUP) * scale_ref[:, lanes]
            o_ref[:, c * chunk:(c + 1) * chunk, lanes] = z.astype(o_ref.dtype)


def _pool_mix(u, prev, w_pool, scale, *, n_prev, bb, chunk, act):
    b, s_len, _ = u.shape
    return pl.pallas_call(
        functools.partial(_pool_kernel, n_prev=n_prev, chunk=chunk),
        out_shape=jax.ShapeDtypeStruct((b, s_len, TOK_WIDTH), act),
        grid=(b // bb,),
        in_specs=[pl.BlockSpec((bb, s_len, TOK_WIDTH), lambda i: (i, 0, 0)),
                  pl.BlockSpec((bb, POOL_HIST, TOK_WIDTH), lambda i: (i, 0, 0)),
                  pl.BlockSpec((len(POOL_WINDOWS), POOL_GROUP, POOL_GROUP), lambda i: (0, 0, 0)),
                  pl.BlockSpec((1, TOK_WIDTH), lambda i: (0, 0))],
        out_specs=pl.BlockSpec((bb, s_len, TOK_WIDTH), lambda i: (i, 0, 0)),
        scratch_shapes=[pltpu.VMEM((bb, POOL_HIST + s_len, TOK_WIDTH), F32)],
        compiler_params=pltpu.CompilerParams(dimension_semantics=("arbitrary",),
                                             vmem_limit_bytes=VMEM_LIMIT),
        name="pool_mix",
    )(u, prev, w_pool, scale)


def _mem_attn_kernel(q_ref, k_ref, v_ref, o_ref):
    for h in range(MEM_HEADS):
        lanes = slice(h * MEM_HEAD_DIM, (h + 1) * MEM_HEAD_DIM)
        q = q_ref[:, :, lanes].astype(BF16)
        k = k_ref[:, :, h, :].astype(BF16)
        v = v_ref[:, :, h, :].astype(BF16)
        s = jnp.einsum("bqd,bkd->bqk", q, k, preferred_element_type=F32) * (MEM_HEAD_DIM ** -0.5)
        m = jnp.max(s, axis=-1, keepdims=True)
        p = jnp.exp(s - m)
        l = jnp.sum(p, axis=-1, keepdims=True)
        o = jnp.einsum("bqk,bkd->bqd", p.astype(BF16), v, preferred_element_type=F32)
        o_ref[:, :, lanes] = (o / l).astype(o_ref.dtype)


def _mem_attn(q, mk, mv, layer, *, bb, ts):
    b, s_len, _ = q.shape
    mem_block = pl.BlockSpec((None, bb, N_MEM, MEM_HEADS, MEM_HEAD_DIM),
                             lambda i, j: (layer, i, 0, 0, 0))
    return pl.pallas_call(
        _mem_attn_kernel,
        out_shape=jax.ShapeDtypeStruct((b, s_len, MEM_WIDTH), q.dtype),
        grid=(b // bb, s_len // ts),
        in_specs=[pl.BlockSpec((bb, ts, MEM_WIDTH), lambda i, j: (i, j, 0)), mem_block, mem_block],
        out_specs=pl.BlockSpec((bb, ts, MEM_WIDTH), lambda i, j: (i, j, 0)),
        compiler_params=pltpu.CompilerParams(dimension_semantics=("arbitrary", "arbitrary"),
                                             vmem_limit_bytes=VMEM_LIMIT),
        name="mem_attn",
    )(q, mk, mv)


def _out_ffn_kernel(h_ref, tok_ref, mo_ref, wout_ref, g_ref, wg_ref, wu_ref, wd_ref,
                    o_ref, xn_ref, acc_ref):
    kf = pl.program_id(1)

    @pl.when(kf == 0)
    def _():
        h1 = (h_ref[...]
              + jnp.dot(tok_ref[...].astype(BF16), wout_ref[:TOK_WIDTH, :], preferred_element_type=F32)
              + jnp.dot(mo_ref[...].astype(BF16), wout_ref[TOK_WIDTH:, :], preferred_element_type=F32))
        acc_ref[...] = h1
        xn_ref[...] = _rmsnorm(h1, g_ref[...]).astype(BF16)

    x = xn_ref[...]
    gate = jnp.dot(x, wg_ref[...], preferred_element_type=F32)
    up = jnp.dot(x, wu_ref[...], preferred_element_type=F32)
    act = (gate * (1.0 / (1.0 + jnp.exp(-gate))) * up).astype(BF16)
    acc_ref[...] += jnp.dot(act, wd_ref[...], preferred_element_type=F32)

    @pl.when(kf == pl.num_programs(1) - 1)
    def _():
        o_ref[...] = acc_ref[...]


def _out_ffn(h, tok, mo, w_out, g, w_gu, w_down, *, tm, tf):
    t = h.shape[0]
    nf = D_FF // tf
    row = lambda i, k: (i, 0)
    fixed = lambda i, k: (0, 0)
    return pl.pallas_call(
        _out_ffn_kernel,
        out_shape=jax.ShapeDtypeStruct((t, D_MODEL), F32),
        grid=(t // tm, nf),
        in_specs=[pl.BlockSpec((tm, D_MODEL), row),
                  pl.BlockSpec((tm, TOK_WIDTH), row),
                  pl.BlockSpec((tm, MEM_WIDTH), row),
                  pl.BlockSpec((D_MODEL, D_MODEL), fixed),
                  pl.BlockSpec((1, D_MODEL), fixed),
                  pl.BlockSpec((D_MODEL, tf), lambda i, k: (0, k)),
                  pl.BlockSpec((D_MODEL, tf), lambda i, k: (0, nf + k)),
                  pl.BlockSpec((tf, D_MODEL), lambda i, k: (k, 0))],
        out_specs=pl.BlockSpec((tm, D_MODEL), row),
        scratch_shapes=[pltpu.VMEM((tm, D_MODEL), BF16), pltpu.VMEM((tm, D_MODEL), F32)],
        compiler_params=pltpu.CompilerParams(dimension_semantics=("arbitrary", "arbitrary"),
                                             vmem_limit_bytes=VMEM_LIMIT),
        name="out_ffn",
    )(h, tok, mo, w_out, g, w_gu, w_gu, w_down)


def _lane_cumsum(x):
    n = x.shape[-1]
    lane = lax.broadcasted_iota(jnp.int32, x.shape, 1)
    k = 1
    while k < n:
        x = x + jnp.where(lane >= k, pltpu.roll(x, k, axis=1), 0.0)
        k *= 2
    return x


def _cumsum_kernel(x_ref, o_ref):
    o_ref[...] = _lane_cumsum(x_ref[...])


def _cumsum_rows(x, *, seg):
    rows, t = x.shape
    return pl.pallas_call(
        _cumsum_kernel,
        out_shape=jax.ShapeDtypeStruct((rows, t), F32),
        grid=(t // seg,),
        in_specs=[pl.BlockSpec((rows, seg), lambda i: (0, i))],
        out_specs=pl.BlockSpec((rows, seg), lambda i: (0, i)),
        compiler_params=pltpu.CompilerParams(dimension_semantics=("arbitrary",)),
        name="logf_cumsum",
    )(x)


def _fox_prompt_kernel(q_ref, k_ref, v_ref, frow_ref, o_ref, m_ref, l_ref, acc_ref, *, blk):
    qi = pl.program_id(2)
    q = q_ref[0]
    lane = lax.broadcasted_iota(jnp.int32, q.shape, 1)
    sub = blk // FOX_ROW_SPLIT
    n_rep = blk // LANES
    row = lax.broadcasted_iota(jnp.int32, (sub, blk), 0)
    col = lax.broadcasted_iota(jnp.int32, (sub, blk), 1)
    q_head = [jnp.where(lane < FOX_HEAD_DIM, q, jnp.zeros_like(q)),
              jnp.where(lane >= FOX_HEAD_DIM, q, jnp.zeros_like(q))]
    m_ref[...] = jnp.full(m_ref.shape, -jnp.inf, F32)
    l_ref[...] = jnp.zeros_like(l_ref)
    acc_ref[...] = jnp.zeros_like(acc_ref)
    last = pl.multiple_of((qi + 1) * blk - LANES, LANES)
    f_end = [frow_ref[0, j:j + 1, pl.ds(last, LANES)][:, LANES - 1:LANES] for j in range(2)]

    def tile(ki, masked):
        ks = pl.multiple_of(ki * blk, blk)
        kt = k_ref[0, pl.ds(ks, blk), :]
        vt = v_ref[0, pl.ds(ks, blk), :]
        for j in range(2):
            bias = f_end[j] - frow_ref[0, j:j + 1, pl.ds(ks, blk)]
            for r in range(FOX_ROW_SPLIT):
                rows = slice(r * sub, (r + 1) * sub)
                s = lax.dot_general(q_head[j][rows], kt, NT_DIMS, preferred_element_type=F32) + bias
                if masked:
                    s = jnp.where(col <= row + r * sub, s, -jnp.inf)
                m_prev = m_ref[j, rows]
                m_new = jnp.maximum(m_prev, jnp.max(s, axis=-1, keepdims=True))
                alpha = jnp.exp(m_prev - m_new)
                p = jnp.exp(s - jnp.concatenate([m_new] * n_rep, axis=1))
                p_lanes = p[:, :LANES]
                for c in range(1, n_rep):
                    p_lanes = p_lanes + p[:, c * LANES:(c + 1) * LANES]
                l_ref[j, rows] = alpha * l_ref[j, rows] + p_lanes
                acc_ref[j, rows] = alpha * acc_ref[j, rows] + jnp.dot(p.astype(BF16), vt,
                                                                      preferred_element_type=F32)
                m_ref[j, rows] = m_new

    def body(ki, carry):
        tile(ki, False)
        return carry

    lax.fori_loop(0, qi, body, 0)
    tile(qi, True)

    l0 = jnp.sum(l_ref[0], axis=-1, keepdims=True)
    l1 = jnp.sum(l_ref[1], axis=-1, keepdims=True)
    o = jnp.where(lane < FOX_HEAD_DIM, acc_ref[0] / l0, acc_ref[1] / l1)
    o_ref[0] = o.astype(o_ref.dtype)


def _fox_prompt(q, k, v, frow, *, blk):
    b, s_len, _ = q.shape
    n_pairs = TOK_WIDTH // LANES
    nq = s_len // blk
    return pl.pallas_call(
        functools.partial(_fox_prompt_kernel, blk=blk),
        out_shape=jax.ShapeDtypeStruct((b, s_len, TOK_WIDTH), BF16),
        grid=(b, n_pairs, nq),
        in_specs=[pl.BlockSpec((1, blk, LANES), lambda bi, hp, qi: (bi, qi, hp)),
                  pl.BlockSpec((1, s_len, LANES), lambda bi, hp, qi: (bi, 0, hp)),
                  pl.BlockSpec((1, s_len, LANES), lambda bi, hp, qi: (bi, 0, hp)),
                  pl.BlockSpec((1, 2, s_len), lambda bi, hp, qi: (hp, 0, bi))],
        out_specs=pl.BlockSpec((1, blk, LANES), lambda bi, hp, qi: (bi, qi, hp)),
        scratch_shapes=[pltpu.VMEM((2, blk, LANES), F32), pltpu.VMEM((2, blk, LANES), F32),
                        pltpu.VMEM((2, blk, LANES), F32)],
        compiler_params=pltpu.CompilerParams(
            dimension_semantics=("arbitrary", "arbitrary", "arbitrary"),
            vmem_limit_bytes=VMEM_LIMIT),
        name="fox_prompt",
    )(q, k, v, frow)


def _page_bias_kernel(pt_ref, logf_ref, o_ref, x_ref, *, n_pages):
    b = pl.program_id(0)
    n = x_ref.shape[-1]
    for j in range(n_pages):
        x_ref[j:j + 1, :] = logf_ref[pl.ds(pt_ref[b * n_pages + j], 1), :]
    x = x_ref[...]
    lane = lax.broadcasted_iota(jnp.int32, x.shape, 1)
    incl = x
    k = FOX_HEADS
    while k < n:
        incl = incl + jnp.where(lane + k < n, pltpu.roll(incl, n - k, axis=1), 0.0)
        k *= 2
    total = jnp.where(lane < FOX_HEADS, incl, 0.0)
    k = FOX_HEADS
    while k < n:
        total = total + pltpu.roll(total, k, axis=1)
        k *= 2
    excl = incl - x
    later_pages = jnp.zeros((1, n), F32)
    for j in reversed(range(n_pages)):
        o_ref[0, j:j + 1, :] = later_pages + excl[j:j + 1, :]
        later_pages = later_pages + total[j:j + 1, :]


def _page_bias(page_table, cache_logf2d, *, n_batch):
    n_pages = page_table.shape[0] // n_batch
    n_phys, width = cache_logf2d.shape
    grid_spec = pltpu.PrefetchScalarGridSpec(
        num_scalar_prefetch=1,
        grid=(n_batch,),
        in_specs=[pl.BlockSpec((n_phys, width), lambda bi, pt: (0, 0))],
        out_specs=pl.BlockSpec((1, n_pages, width), lambda bi, pt: (bi, 0, 0)),
        scratch_shapes=[pltpu.VMEM((n_pages, width), F32)])
    return pl.pallas_call(
        functools.partial(_page_bias_kernel, n_pages=n_pages),
        out_shape=jax.ShapeDtypeStruct((n_batch, n_pages, width), F32),
        grid_spec=grid_spec,
        compiler_params=pltpu.CompilerParams(dimension_semantics=("arbitrary",),
                                             vmem_limit_bytes=VMEM_LIMIT),
        name="page_bias",
    )(page_table, cache_logf2d)


def _fox_sample_kernel(pt_ref, q_ref, kn_ref, vn_ref, lfrow_ref, lfcol_ref, bias_ref, k_hbm, v_hbm,
                       o_ref, kbuf, vbuf, sem, mb_ref, m_ref, l_ref, acc_ref, *, n_pages):
    b = pl.program_id(0)
    slot = lax.rem(b, 2)
    n_rows = q_ref.shape[1]
    n_q = n_rows // FOX_HEADS
    page_rows = PAGE_SIZE * FOX_HEADS

    def page_copies(batch, slot_):
        copies = []
        for j in range(n_pages):
            page = pt_ref[batch * n_pages + j]
            copies.append(pltpu.make_async_copy(k_hbm.at[page], kbuf.at[slot_, j], sem.at[0, slot_]))
            copies.append(pltpu.make_async_copy(v_hbm.at[page], vbuf.at[slot_, j], sem.at[1, slot_]))
        return copies

    @pl.when(b == 0)
    def _():
        for c in page_copies(0, 0):
            c.start()

    @pl.when(b + 1 < pl.num_programs(0))
    def _():
        for c in page_copies(b + 1, 1 - slot):
            c.start()

    x_col = lfcol_ref[0]
    pieces = [x_col[0:FOX_HEADS]]
    for t in range(1, n_q):
        pieces.append(pieces[-1] + x_col[t * FOX_HEADS:(t + 1) * FOX_HEADS])
    cq = jnp.concatenate(pieces, axis=0)
    c_row = jnp.broadcast_to(lfrow_ref[0], (FOX_HEADS, LANES))
    lane = lax.broadcasted_iota(jnp.int32, c_row.shape, 1)
    k = FOX_HEADS
    while k < n_rows:
        c_row = c_row + jnp.where(lane >= k, pltpu.roll(c_row, k, axis=1), 0.0)
        k *= 2
    c_keys = jnp.concatenate([c_row[:, :n_rows]] * n_q, axis=0)

    q2 = q_ref[0].astype(BF16)
    row = lax.broadcasted_iota(jnp.int32, (n_rows, n_rows), 0)
    col = lax.broadcasted_iota(jnp.int32, (n_rows, n_rows), 1)
    same_head = jnp.bitwise_and(row, FOX_HEADS - 1) == jnp.bitwise_and(col, FOX_HEADS - 1)
    causal = jnp.right_shift(col, 3) <= jnp.right_shift(row, 3)
    s = lax.dot_general(q2, kn_ref[0].astype(BF16), NT_DIMS, preferred_element_type=F32)
    s = jnp.where(same_head, jnp.where(causal, s + (cq - c_keys), -jnp.inf), -jnp.inf)
    m0 = jnp.max(s, axis=-1, keepdims=True)
    p = jnp.exp(s - m0)
    m_ref[...] = m0
    l_ref[...] = jnp.sum(p, axis=-1, keepdims=True)
    acc_ref[...] = jnp.dot(p.astype(BF16), vn_ref[0].astype(BF16), preferred_element_type=F32)

    row = lax.broadcasted_iota(jnp.int32, (n_rows, page_rows), 0)
    col = lax.broadcasted_iota(jnp.int32, (n_rows, page_rows), 1)
    same_head = jnp.bitwise_and(row, FOX_HEADS - 1) == jnp.bitwise_and(col, FOX_HEADS - 1)
    mb_ref[...] = jnp.where(same_head, jnp.broadcast_to(cq, (n_rows, page_rows)), -jnp.inf)

    for c in page_copies(b, slot):
        c.wait()

    def chunk_step(i, carry):
        scores = []
        for pg in range(PAGES_PER_STEP):
            j = i * PAGES_PER_STEP + pg
            k2 = kbuf[slot, j].reshape(page_rows, FOX_HEAD_DIM).astype(BF16)
            s = lax.dot_general(q2, k2, NT_DIMS, preferred_element_type=F32)
            scores.append(s + (mb_ref[...] + bias_ref[0, pl.ds(j, 1), :]))
        m_prev = m_ref[...]
        m_new = m_prev
        for s in scores:
            m_new = jnp.maximum(m_new, jnp.max(s, axis=-1, keepdims=True))
        alpha = jnp.exp(m_prev - m_new)
        l_new = alpha * l_ref[...]
        acc = alpha * acc_ref[...]
        for pg, s in enumerate(scores):
            j = i * PAGES_PER_STEP + pg
            v2 = vbuf[slot, j].reshape(page_rows, FOX_HEAD_DIM).astype(BF16)
            p = jnp.exp(s - m_new)
            l_new = l_new + jnp.sum(p, axis=-1, keepdims=True)
            acc = acc + jnp.dot(p.astype(BF16), v2, preferred_element_type=F32)
        m_ref[...] = m_new
        l_ref[...] = l_new
        acc_ref[...] = acc
        return carry

    lax.fori_loop(0, n_pages // PAGES_PER_STEP, chunk_step, 0)
    o_ref[0] = acc_ref[...] / l_ref[...]


def _fox_sample(page_table, q2, kn2, vn2, lf_row, lf_col, bias, cache_k, cache_v):
    b, n_rows, _ = q2.shape
    n_pages = page_table.shape[0] // b
    cur = lambda bi, pt: (bi, 0, 0)
    page_shape = cache_k.shape[1:]
    grid_spec = pltpu.PrefetchScalarGridSpec(
        num_scalar_prefetch=1,
        grid=(b,),
        in_specs=[pl.BlockSpec((1, n_rows, FOX_HEAD_DIM), cur),
                  pl.BlockSpec((1, n_rows, FOX_HEAD_DIM), cur),
                  pl.BlockSpec((1, n_rows, FOX_HEAD_DIM), cur),
                  pl.BlockSpec((1, 1, LANES), cur),
                  pl.BlockSpec((1, n_rows, 1), cur),
                  pl.BlockSpec((1, n_pages, PAGE_SIZE * FOX_HEADS), cur),
                  pl.BlockSpec(memory_space=pl.ANY),
                  pl.BlockSpec(memory_space=pl.ANY)],
        out_specs=pl.BlockSpec((1, n_rows, FOX_HEAD_DIM), cur),
        scratch_shapes=[pltpu.VMEM((2, n_pages) + page_shape, F32),
                        pltpu.VMEM((2, n_pages) + page_shape, F32),
                        pltpu.SemaphoreType.DMA((2, 2)),
                        pltpu.VMEM((n_rows, PAGE_SIZE * FOX_HEADS), F32),
                        pltpu.VMEM((n_rows, 1), F32),
                        pltpu.VMEM((n_rows, 1), F32),
                        pltpu.VMEM((n_rows, FOX_HEAD_DIM), F32)])
    return pl.pallas_call(
        functools.partial(_fox_sample_kernel, n_pages=n_pages),
        out_shape=jax.ShapeDtypeStruct((b, n_rows, FOX_HEAD_DIM), F32),
        grid_spec=grid_spec,
        compiler_params=pltpu.CompilerParams(dimension_semantics=("arbitrary",),
                                             vmem_limit_bytes=VMEM_LIMIT),
        name="fox_sample",
    )(page_table, q2, kn2, vn2, lf_row, lf_col, bias, cache_k, cache_v)


def _decoder(x, pool_prev, n_prev, mem_k, mem_v, past, wts, *, tm, pool_bb, pool_chunk, mem_bb, mem_ts,
             act):
    b, s_len, _ = x.shape
    t = b * s_len
    h = x.reshape(t, D_MODEL)

    z_tok, qm = _proj_in(h, wts["g_mix"][0], wts["w_in"][0], wts["g_fox_q_t"], wts["g_mem_q_t"][0],
                         fox=False, tm=tm, act=act)
    z_tok = z_tok.reshape(b, s_len, TOK_WIDTH)
    tok = _pool_mix(z_tok, pool_prev, wts["w_pool"], wts["pool_scale"],
                    n_prev=n_prev, bb=pool_bb, chunk=pool_chunk, act=act)
    mo = _mem_attn(qm.reshape(b, s_len, MEM_WIDTH), mem_k, mem_v, 0, bb=mem_bb, ts=mem_ts)
    h = _out_ffn(h, tok.reshape(t, TOK_WIDTH), mo.reshape(t, MEM_WIDTH), wts["w_out"][0],
                 wts["g_ffn"][0], wts["w_gu"][0], wts["w_down"][0], tm=tm, tf=D_FF // 2)

    k_new, v_new, k_bf, v_bf, logf_t = _proj_kv(h, wts["g_kv"], wts["w_kv"], wts["g_fox_k_t"],
                                                wts["w_fg_t"], wts["b_fg"], tm=tm)

    q, qm = _proj_in(h, wts["g_mix"][1], wts["w_in"][1], wts["g_fox_q_t"], wts["g_mem_q_t"][1],
                     fox=True, tm=tm, act=act)
    if past is None:
        f_row = _cumsum_rows(logf_t, seg=s_len)
        n_pairs = FOX_HEADS // 2
        f_row = f_row.reshape(n_pairs, 2, t)
        tok = _fox_prompt(q.reshape(b, s_len, TOK_WIDTH), k_bf.reshape(b, s_len, TOK_WIDTH),
                          v_bf.reshape(b, s_len, TOK_WIDTH), f_row, blk=512)
    else:
        page_table, cache_k, cache_v, cache_logf2d = past
        n_rows = s_len * FOX_HEADS
        to_rows = lambda a: a.reshape(b, n_rows, FOX_HEAD_DIM)
        lf = logf_t.T.reshape(b, n_rows)
        lf_row = jnp.pad(lf, ((0, 0), (0, LANES - n_rows))).reshape(b, 1, LANES)
        bias = _page_bias(page_table, cache_logf2d, n_batch=b)
        tok = _fox_sample(page_table, to_rows(q), to_rows(k_new), to_rows(v_new), lf_row,
                          lf.reshape(b, n_rows, 1), bias, cache_k, cache_v)
    mo = _mem_attn(qm.reshape(b, s_len, MEM_WIDTH), mem_k, mem_v, 1, bb=mem_bb, ts=mem_ts)
    h = _out_ffn(h, tok.reshape(t, TOK_WIDTH), mo.reshape(t, MEM_WIDTH), wts["w_out"][1],
                 wts["g_ffn"][1], wts["w_gu"][1], wts["w_down"][1], tm=tm, tf=D_FF // 2)
    return h.reshape(b, s_len, D_MODEL), z_tok, k_new, v_new, logf_t.T


def kernel(x_prompt, x_sample, cache_mem_k, cache_mem_v, state_pool, cache_k, cache_v, cache_logf,
           page_table, mem_prompt, g_mix, w_in, w_out, g_ffn, w_gu, w_down, g_mem, w_mem_kv,
           g_mem_q, g_mem_k, w_pool, pool_scale, g_kv, w_kv, g_fox_k, w_fg, b_fg, g_fox_q):
    depth = w_in.shape[0]
    b, s_len, _ = x_prompt.shape
    db, ds, _ = x_sample.shape
    wts = {
        "g_mix": g_mix.reshape(depth, 1, D_MODEL),
        "w_in": w_in.astype(BF16),
        "w_out": w_out.astype(BF16),
        "g_ffn": g_ffn.reshape(depth, 1, D_MODEL),
        "w_gu": w_gu.astype(BF16),
        "w_down": w_down.astype(BF16),
        "g_mem_q_t": jnp.tile(g_mem_q, (1, MEM_HEADS)).reshape(depth, 1, MEM_WIDTH),
        "w_pool": w_pool[0].astype(BF16),
        "pool_scale": pool_scale[0].reshape(1, TOK_WIDTH),
        "g_kv": g_kv.reshape(1, D_MODEL),
        "w_kv": w_kv.astype(BF16),
        "g_fox_k_t": jnp.tile(g_fox_k, FOX_HEADS).reshape(1, TOK_WIDTH),
        "w_fg_t": w_fg.T.astype(BF16),
        "b_fg": b_fg.reshape(FOX_HEADS, 1),
        "g_fox_q_t": jnp.tile(g_fox_q[0], FOX_HEADS).reshape(1, TOK_WIDTH),
    }

    mem_k_p, mem_v_p = _proj_memkv(mem_prompt.reshape(b * N_MEM, D_MODEL),
                                   g_mem.reshape(depth, 1, D_MODEL), w_mem_kv.astype(BF16),
                                   jnp.tile(g_mem_k, (1, MEM_HEADS)).reshape(depth, 1, MEM_WIDTH), tm=512)
    mem_k_p = mem_k_p.reshape(depth, b, N_MEM, MEM_HEADS, MEM_HEAD_DIM)
    mem_v_p = mem_v_p.reshape(depth, b, N_MEM, MEM_HEADS, MEM_HEAD_DIM)
    y_p, ztok_p, k_p, v_p, logf_p = _decoder(
        x_prompt, jnp.zeros((b, POOL_HIST, TOK_WIDTH), F32), 0, mem_k_p, mem_v_p, None, wts,
        tm=512, pool_bb=1, pool_chunk=256, mem_bb=1, mem_ts=512, act=BF16)
    n_pp = s_len // PAGE_SIZE
    head_shape = (FOX_HEADS, FOX_HEAD_DIM)

    n_phys = cache_k.shape[0]
    past = (page_table.reshape(-1), cache_k, cache_v,
            cache_logf.reshape(n_phys, PAGE_SIZE * FOX_HEADS))
    prev = jnp.pad(state_pool[0], ((0, 0), (POOL_HIST - POOL_STATE, 0), (0, 0)))
    y_s, ztok_s, k_s, v_s, logf_s = _decoder(
        x_sample, prev, POOL_STATE, cache_mem_k, cache_mem_v, past, wts,
        tm=512, pool_bb=32, pool_chunk=ds, mem_bb=8, mem_ts=ds, act=F32)

    pool_state_p = ztok_p[:, s_len - POOL_STATE:][None]
    pool_state_s = jnp.concatenate([state_pool[0], ztok_s], axis=1)[:, -POOL_STATE:][None]
    return (y_p, y_s,
            k_p.reshape((b, n_pp, PAGE_SIZE) + head_shape),
            v_p.reshape((b, n_pp, PAGE_SIZE) + head_shape),
            logf_p.reshape(b, n_pp, PAGE_SIZE, FOX_HEADS),
            mem_k_p, mem_v_p, pool_state_p,
            k_s.reshape((db, ds) + head_shape),
            v_s.reshape((db, ds) + head_shape),
            logf_s.reshape(db, ds, FOX_HEADS),
            pool_state_s)
```

```python
import functools

import jax
import jax.numpy as jnp
from jax import lax
from jax.experimental import pallas as pl
from jax.experimental.pallas import tpu as pltpu

D_MODEL = 1024
TOK_WIDTH = 512
MEM_WIDTH = 512
POOL_WINDOWS = (2, 4, 8, 16)
POOL_GROUP = 128
POOL_STATE = 15
POOL_HIST = 16
FOX_HEADS = 8
FOX_HEAD_DIM = 64
MEM_HEADS = 4
MEM_HEAD_DIM = 128
N_MEM = 256
D_FF = 2816
PAGE_SIZE = 128
EPS = 1e-6
LANES = 128
VMEM_LIMIT = 56 * 1024 * 1024
FOX_ROW_SPLIT = 2
PAGES_PER_STEP = 4

F32 = jnp.float32
BF16 = jnp.bfloat16
NT_DIMS = (((1,), (1,)), ((), ()))


def _rmsnorm(x, g):
    return x * lax.rsqrt(jnp.mean(x * x, axis=-1, keepdims=True) + EPS) * g


def _headnorm128(z, g):
    outs = []
    for h in range(z.shape[-1] // LANES):
        zh = z[:, h * LANES:(h + 1) * LANES]
        r = lax.rsqrt(jnp.mean(zh * zh, axis=-1, keepdims=True) + EPS)
        outs.append(zh * r * g[:, h * LANES:(h + 1) * LANES])
    return jnp.concatenate(outs, axis=-1)


def _headnorm64(z, g):
    outs = []
    for p in range(z.shape[-1] // LANES):
        zp = z[:, p * LANES:(p + 1) * LANES]
        sq = zp * zp
        lane = lax.broadcasted_iota(jnp.int32, zp.shape, 1)
        lo = lane < FOX_HEAD_DIM
        s_lo = jnp.sum(jnp.where(lo, sq, 0.0), axis=-1, keepdims=True)
        s_hi = jnp.sum(jnp.where(lo, 0.0, sq), axis=-1, keepdims=True)
        r = jnp.where(lo, lax.rsqrt(s_lo / FOX_HEAD_DIM + EPS), lax.rsqrt(s_hi / FOX_HEAD_DIM + EPS))
        outs.append(zp * r * g[:, p * LANES:(p + 1) * LANES])
    return jnp.concatenate(outs, axis=-1)


def _proj_in_kernel(x_ref, g_ref, w_ref, gtok_ref, gmem_ref, tok_ref, qm_ref, *, fox):
    xn = _rmsnorm(x_ref[...], g_ref[...]).astype(BF16)
    z = jnp.dot(xn, w_ref[...], preferred_element_type=F32)
    z_tok = z[:, :TOK_WIDTH]
    if fox:
        tok_ref[...] = (_headnorm64(z_tok, gtok_ref[...]) * (FOX_HEAD_DIM ** -0.5)).astype(tok_ref.dtype)
    else:
        tok_ref[...] = z_tok
    qm_ref[...] = _headnorm128(z[:, TOK_WIDTH:], gmem_ref[...]).astype(qm_ref.dtype)


def _proj_in(x, g, w, g_tok, g_mem, *, fox, tm, act):
    t = x.shape[0]
    row = lambda i: (i, 0)
    fixed = lambda i: (0, 0)
    return pl.pallas_call(
        functools.partial(_proj_in_kernel, fox=fox),
        out_shape=(jax.ShapeDtypeStruct((t, TOK_WIDTH), act if fox else F32),
                   jax.ShapeDtypeStruct((t, MEM_WIDTH), act)),
        grid=(t // tm,),
        in_specs=[pl.BlockSpec((tm, D_MODEL), row),
                  pl.BlockSpec((1, D_MODEL), fixed),
                  pl.BlockSpec((D_MODEL, D_MODEL), fixed),
                  pl.BlockSpec((1, TOK_WIDTH), fixed),
                  pl.BlockSpec((1, MEM_WIDTH), fixed)],
        out_specs=(pl.BlockSpec((tm, TOK_WIDTH), row), pl.BlockSpec((tm, MEM_WIDTH), row)),
        compiler_params=pltpu.CompilerParams(dimension_semantics=("arbitrary",),
                                             vmem_limit_bytes=VMEM_LIMIT),
        name="proj_in_fox" if fox else "proj_in_pool",
    )(x, g, w, g_tok, g_mem)


def _log_sigmoid(x):
    return jnp.minimum(x, 0.0) - jnp.log(1.0 + jnp.exp(-jnp.abs(x)))


def _proj_kv_kernel(x_ref, g_ref, w_ref, gk_ref, wfg_ref, bfg_ref,
                    k_ref, v_ref, kb_ref, vb_ref, logf_ref):
    xn = _rmsnorm(x_ref[...], g_ref[...]).astype(BF16)
    z = jnp.dot(xn, w_ref[...], preferred_element_type=F32)
    k = _headnorm64(z[:, :TOK_WIDTH], gk_ref[...])
    v = z[:, TOK_WIDTH:]
    k_ref[...] = k
    v_ref[...] = v
    kb_ref[...] = k.astype(BF16)
    vb_ref[...] = v.astype(BF16)
    gate = lax.dot_general(wfg_ref[...], xn, NT_DIMS, preferred_element_type=F32) + bfg_ref[...]
    logf_ref[...] = _log_sigmoid(gate)


def _proj_kv(x, g, w, g_k, wfg_t, bfg, *, tm):
    t = x.shape[0]
    row = lambda i: (i, 0)
    fixed = lambda i: (0, 0)
    return pl.pallas_call(
        _proj_kv_kernel,
        out_shape=(jax.ShapeDtypeStruct((t, TOK_WIDTH), F32),
                   jax.ShapeDtypeStruct((t, TOK_WIDTH), F32),
                   jax.ShapeDtypeStruct((t, TOK_WIDTH), BF16),
                   jax.ShapeDtypeStruct((t, TOK_WIDTH), BF16),
                   jax.ShapeDtypeStruct((FOX_HEADS, t), F32)),
        grid=(t // tm,),
        in_specs=[pl.BlockSpec((tm, D_MODEL), row),
                  pl.BlockSpec((1, D_MODEL), fixed),
                  pl.BlockSpec((D_MODEL, D_MODEL), fixed),
                  pl.BlockSpec((1, TOK_WIDTH), fixed),
                  pl.BlockSpec((FOX_HEADS, D_MODEL), fixed),
                  pl.BlockSpec((FOX_HEADS, 1), fixed)],
        out_specs=(pl.BlockSpec((tm, TOK_WIDTH), row), pl.BlockSpec((tm, TOK_WIDTH), row),
                   pl.BlockSpec((tm, TOK_WIDTH), row), pl.BlockSpec((tm, TOK_WIDTH), row),
                   pl.BlockSpec((FOX_HEADS, tm), lambda i: (0, i))),
        compiler_params=pltpu.CompilerParams(dimension_semantics=("arbitrary",),
                                             vmem_limit_bytes=VMEM_LIMIT),
        name="proj_kv",
    )(x, g, w, g_k, wfg_t, bfg)


def _proj_memkv_kernel(x_ref, g_ref, w_ref, gk_ref, k_ref, v_ref):
    xn = _rmsnorm(x_ref[...], g_ref[0]).astype(BF16)
    z = jnp.dot(xn, w_ref[0], preferred_element_type=F32)
    k = _headnorm128(z[:, :MEM_WIDTH], gk_ref[0])
    tm = x_ref.shape[0]
    for h in range(MEM_HEADS):
        rows = pl.ds(h, tm, stride=MEM_HEADS)
        k_ref[0, rows, :] = k[:, h * MEM_HEAD_DIM:(h + 1) * MEM_HEAD_DIM]
        v_ref[0, rows, :] = z[:, MEM_WIDTH + h * MEM_HEAD_DIM:MEM_WIDTH + (h + 1) * MEM_HEAD_DIM]


def _proj_memkv(x, g, w, g_k, *, tm):
    t = x.shape[0]
    n_layers = w.shape[0]
    out_block = pl.BlockSpec((1, tm * MEM_HEADS, MEM_HEAD_DIM), lambda l, i: (l, i, 0))
    return pl.pallas_call(
        _proj_memkv_kernel,
        out_shape=(jax.ShapeDtypeStruct((n_layers, t * MEM_HEADS, MEM_HEAD_DIM), F32),
                   jax.ShapeDtypeStruct((n_layers, t * MEM_HEADS, MEM_HEAD_DIM), F32)),
        grid=(n_layers, t // tm),
        in_specs=[pl.BlockSpec((tm, D_MODEL), lambda l, i: (i, 0)),
                  pl.BlockSpec((1, 1, D_MODEL), lambda l, i: (l, 0, 0)),
                  pl.BlockSpec((1, D_MODEL, D_MODEL), lambda l, i: (l, 0, 0)),
                  pl.BlockSpec((1, 1, MEM_WIDTH), lambda l, i: (l, 0, 0))],
        out_specs=(out_block, out_block),
        compiler_params=pltpu.CompilerParams(dimension_semantics=("arbitrary", "arbitrary"),
                                             vmem_limit_bytes=VMEM_LIMIT),
        name="proj_memkv",
    )(x, g, w, g_k)


def _pool_kernel(u_ref, prev_ref, w_ref, scale_ref, o_ref, ext_ref, *, n_prev, chunk):
    bb, s_len, _ = u_ref.shape
    ext_ref[:, 0:POOL_HIST, :] = prev_ref[...]
    ext_ref[:, POOL_HIST:, :] = u_ref[...]
    for c in range(s_len // chunk):
        r0 = POOL_HIST + c * chunk
        pos = c * chunk + lax.broadcasted_iota(jnp.int32, (1, chunk, 1), 1)
        for gi, win in enumerate(POOL_WINDOWS):
            lanes = slice(gi * POOL_GROUP, (gi + 1) * POOL_GROUP)
            u_new = ext_ref[:, r0:r0 + chunk, lanes]
            acc = u_new
            for k in range(1, win):
                acc = acc + ext_ref[:, r0 - k:r0 - k + chunk, lanes]
            count = jnp.minimum(win, n_prev + pos + 1).astype(F32)
            y = acc / count - u_new
            z = jnp.dot(y.reshape(bb * chunk, POOL_GROUP).astype(BF16), w_ref[gi],
                        preferred_element_type=F32)
            z = z.reshape(bb, chunk, POOL_GROUP) * scale_ref[:, lanes]
            o_ref[:, c * chunk:(c + 1) * chunk, lanes] = z.astype(o_ref.dtype)


def _pool_mix(u, prev, w_pool, scale, *, n_prev, bb, chunk, act):
    b, s_len, _ = u.shape
    return pl.pallas_call(
        functools.partial(_pool_kernel, n_prev=n_prev, chunk=chunk),
        out_shape=jax.ShapeDtypeStruct((b, s_len, TOK_WIDTH), act),
        grid=(b // bb,),
        in_specs=[pl.BlockSpec((bb, s_len, TOK_WIDTH), lambda i: (i, 0, 0)),
                  pl.BlockSpec((bb, POOL_HIST, TOK_WIDTH), lambda i: (i, 0, 0)),
                  pl.BlockSpec((len(POOL_WINDOWS), POOL_GROUP, POOL_GROUP), lambda i: (0, 0, 0)),
                  pl.BlockSpec((1, TOK_WIDTH), lambda i: (0, 0))],
        out_specs=pl.BlockSpec((bb, s_len, TOK_WIDTH), lambda i: (i, 0, 0)),
        scratch_shapes=[pltpu.VMEM((bb, POOL_HIST + s_len, TOK_WIDTH), F32)],
        compiler_params=pltpu.CompilerParams(dimension_semantics=("arbitrary",),
                                             vmem_limit_bytes=VMEM_LIMIT),
        name="pool_mix",
    )(u, prev, w_pool, scale)


def _mem_attn_kernel(q_ref, k_ref, v_ref, o_ref):
    for h in range(MEM_HEADS):
        lanes = slice(h * MEM_HEAD_DIM, (h + 1) * MEM_HEAD_DIM)
        q = q_ref[:, :, lanes].astype(BF16)
        k = k_ref[:, pl.ds(h, N_MEM, stride=MEM_HEADS), :].astype(BF16)
        v = v_ref[:, pl.ds(h, N_MEM, stride=MEM_HEADS), :].astype(BF16)
        s = jnp.einsum("bqd,bkd->bqk", q, k, preferred_element_type=F32) * (MEM_HEAD_DIM ** -0.5)
        m = jnp.max(s, axis=-1, keepdims=True)
        p = jnp.exp(s - m)
        l = jnp.sum(p, axis=-1, keepdims=True)
        o = jnp.einsum("bqk,bkd->bqd", p.astype(BF16), v, preferred_element_type=F32)
        o_ref[:, :, lanes] = (o / l).astype(o_ref.dtype)


def _mem_attn(q, mk, mv, layer, *, bb, ts):
    b, s_len, _ = q.shape
    mem_block = pl.BlockSpec((None, bb, N_MEM * MEM_HEADS, MEM_HEAD_DIM),
                             lambda i, j: (layer, i, 0, 0))
    return pl.pallas_call(
        _mem_attn_kernel,
        out_shape=jax.ShapeDtypeStruct((b, s_len, MEM_WIDTH), q.dtype),
        grid=(b // bb, s_len // ts),
        in_specs=[pl.BlockSpec((bb, ts, MEM_WIDTH), lambda i, j: (i, j, 0)), mem_block, mem_block],
        out_specs=pl.BlockSpec((bb, ts, MEM_WIDTH), lambda i, j: (i, j, 0)),
        compiler_params=pltpu.CompilerParams(dimension_semantics=("arbitrary", "arbitrary"),
                                             vmem_limit_bytes=VMEM_LIMIT),
        name="mem_attn",
    )(q, mk, mv)


def _out_ffn_kernel(h_ref, tok_ref, mo_ref, wout_ref, g_ref, wg_ref, wu_ref, wd_ref,
                    o_ref, xn_ref, acc_ref):
    kf = pl.program_id(1)

    @pl.when(kf == 0)
    def _():
        h1 = (h_ref[...]
              + jnp.dot(tok_ref[...].astype(BF16), wout_ref[:TOK_WIDTH, :], preferred_element_type=F32)
              + jnp.dot(mo_ref[...].astype(BF16), wout_ref[TOK_WIDTH:, :], preferred_element_type=F32))
        acc_ref[...] = h1
        xn_ref[...] = _rmsnorm(h1, g_ref[...]).astype(BF16)

    x = xn_ref[...]
    gate = jnp.dot(x, wg_ref[...], preferred_element_type=F32)
    up = jnp.dot(x, wu_ref[...], preferred_element_type=F32)
    act = (gate * (1.0 / (1.0 + jnp.exp(-gate))) * up).astype(BF16)
    acc_ref[...] += jnp.dot(act, wd_ref[...], preferred_element_type=F32)

    @pl.when(kf == pl.num_programs(1) - 1)
    def _():
        o_ref[...] = acc_ref[...]


def _out_ffn(h, tok, mo, w_out, g, w_gu, w_down, *, tm, tf):
    t = h.shape[0]
    nf = D_FF // tf
    row = lambda i, k: (i, 0)
    fixed = lambda i, k: (0, 0)
    return pl.pallas_call(
        _out_ffn_kernel,
        out_shape=jax.ShapeDtypeStruct((t, D_MODEL), F32),
        grid=(t // tm, nf),
        in_specs=[pl.BlockSpec((tm, D_MODEL), row),
                  pl.BlockSpec((tm, TOK_WIDTH), row),
                  pl.BlockSpec((tm, MEM_WIDTH), row),
                  pl.BlockSpec((D_MODEL, D_MODEL), fixed),
                  pl.BlockSpec((1, D_MODEL), fixed),
                  pl.BlockSpec((D_MODEL, tf), lambda i, k: (0, k)),
                  pl.BlockSpec((D_MODEL, tf), lambda i, k: (0, nf + k)),
                  pl.BlockSpec((tf, D_MODEL), lambda i, k: (k, 0))],
        out_specs=pl.BlockSpec((tm, D_MODEL), row),
        scratch_shapes=[pltpu.VMEM((tm, D_MODEL), BF16), pltpu.VMEM((tm, D_MODEL), F32)],
        compiler_params=pltpu.CompilerParams(dimension_semantics=("arbitrary", "arbitrary"),
                                             vmem_limit_bytes=VMEM_LIMIT),
        name="out_ffn",
    )(h, tok, mo, w_out, g, w_gu, w_gu, w_down)


def _lane_cumsum(x):
    n = x.shape[-1]
    lane = lax.broadcasted_iota(jnp.int32, x.shape, 1)
    k = 1
    while k < n:
        x = x + jnp.where(lane >= k, pltpu.roll(x, k, axis=1), 0.0)
        k *= 2
    return x


def _cumsum_kernel(x_ref, o_ref):
    o_ref[...] = _lane_cumsum(x_ref[...])


def _cumsum_rows(x, *, seg):
    rows, t = x.shape
    return pl.pallas_call(
        _cumsum_kernel,
        out_shape=jax.ShapeDtypeStruct((rows, t), F32),
        grid=(t // seg,),
        in_specs=[pl.BlockSpec((rows, seg), lambda i: (0, i))],
        out_specs=pl.BlockSpec((rows, seg), lambda i: (0, i)),
        compiler_params=pltpu.CompilerParams(dimension_semantics=("arbitrary",)),
        name="logf_cumsum",
    )(x)


def _fox_prompt_kernel(q_ref, k_ref, v_ref, frow_ref, o_ref, m_ref, l_ref, acc_ref, *, blk):
    qi = pl.program_id(2)
    q = q_ref[0]
    lane = lax.broadcasted_iota(jnp.int32, q.shape, 1)
    sub = blk // FOX_ROW_SPLIT
    n_rep = blk // LANES
    row = lax.broadcasted_iota(jnp.int32, (sub, blk), 0)
    col = lax.broadcasted_iota(jnp.int32, (sub, blk), 1)
    q_head = [jnp.where(lane < FOX_HEAD_DIM, q, jnp.zeros_like(q)),
              jnp.where(lane >= FOX_HEAD_DIM, q, jnp.zeros_like(q))]
    m_ref[...] = jnp.full(m_ref.shape, -jnp.inf, F32)
    l_ref[...] = jnp.zeros_like(l_ref)
    acc_ref[...] = jnp.zeros_like(acc_ref)
    last = pl.multiple_of((qi + 1) * blk - LANES, LANES)
    f_end = [frow_ref[0, j:j + 1, pl.ds(last, LANES)][:, LANES - 1:LANES] for j in range(2)]

    def tile(ki, masked):
        ks = pl.multiple_of(ki * blk, blk)
        kt = k_ref[0, pl.ds(ks, blk), :]
        vt = v_ref[0, pl.ds(ks, blk), :]
        for j in range(2):
            bias = f_end[j] - frow_ref[0, j:j + 1, pl.ds(ks, blk)]
            for r in range(FOX_ROW_SPLIT):
                rows = slice(r * sub, (r + 1) * sub)
                s = lax.dot_general(q_head[j][rows], kt, NT_DIMS, preferred_element_type=F32) + bias
                if masked:
                    s = jnp.where(col <= row + r * sub, s, -jnp.inf)
                m_prev = m_ref[j, rows]
                m_new = jnp.maximum(m_prev, jnp.max(s, axis=-1, keepdims=True))
                alpha = jnp.exp(m_prev - m_new)
                p = jnp.exp(s - jnp.concatenate([m_new] * n_rep, axis=1))
                p_lanes = p[:, :LANES]
                for c in range(1, n_rep):
                    p_lanes = p_lanes + p[:, c * LANES:(c + 1) * LANES]
                l_ref[j, rows] = alpha * l_ref[j, rows] + p_lanes
                acc_ref[j, rows] = alpha * acc_ref[j, rows] + jnp.dot(p.astype(BF16), vt,
                                                                      preferred_element_type=F32)
                m_ref[j, rows] = m_new

    def body(ki, carry):
        tile(ki, False)
        return carry

    lax.fori_loop(0, qi, body, 0)
    tile(qi, True)

    l0 = jnp.sum(l_ref[0], axis=-1, keepdims=True)
    l1 = jnp.sum(l_ref[1], axis=-1, keepdims=True)
    o = jnp.where(lane < FOX_HEAD_DIM, acc_ref[0] / l0, acc_ref[1] / l1)
    o_ref[0] = o.astype(o_ref.dtype)


def _fox_prompt(q, k, v, frow, *, blk):
    b, s_len, _ = q.shape
    n_pairs = TOK_WIDTH // LANES
    nq = s_len // blk
    return pl.pallas_call(
        functools.partial(_fox_prompt_kernel, blk=blk),
        out_shape=jax.ShapeDtypeStruct((b, s_len, TOK_WIDTH), BF16),
        grid=(b, n_pairs, nq),
        in_specs=[pl.BlockSpec((1, blk, LANES), lambda bi, hp, qi: (bi, qi, hp)),
                  pl.BlockSpec((1, s_len, LANES), lambda bi, hp, qi: (bi, 0, hp)),
                  pl.BlockSpec((1, s_len, LANES), lambda bi, hp, qi: (bi, 0, hp)),
                  pl.BlockSpec((1, 2, s_len), lambda bi, hp, qi: (hp, 0, bi))],
        out_specs=pl.BlockSpec((1, blk, LANES), lambda bi, hp, qi: (bi, qi, hp)),
        scratch_shapes=[pltpu.VMEM((2, blk, LANES), F32), pltpu.VMEM((2, blk, LANES), F32),
                        pltpu.VMEM((2, blk, LANES), F32)],
        compiler_params=pltpu.CompilerParams(
            dimension_semantics=("arbitrary", "arbitrary", "arbitrary"),
            vmem_limit_bytes=VMEM_LIMIT),
        name="fox_prompt",
    )(q, k, v, frow)


def _page_bias_kernel(pt_ref, logf_ref, o_ref, x_ref, *, n_pages):
    b = pl.program_id(0)
    n = x_ref.shape[-1]
    for j in range(n_pages):
        x_ref[j:j + 1, :] = logf_ref[pl.ds(pt_ref[b * n_pages + j], 1), :]
    x = x_ref[...]
    lane = lax.broadcasted_iota(jnp.int32, x.shape, 1)
    incl = x
    k = FOX_HEADS
    while k < n:
        incl = incl + jnp.where(lane + k < n, pltpu.roll(incl, n - k, axis=1), 0.0)
        k *= 2
    total = jnp.where(lane < FOX_HEADS, incl, 0.0)
    k = FOX_HEADS
    while k < n:
        total = total + pltpu.roll(total, k, axis=1)
        k *= 2
    excl = incl - x
    later_pages = jnp.zeros((1, n), F32)
    for j in reversed(range(n_pages)):
        o_ref[0, j:j + 1, :] = later_pages + excl[j:j + 1, :]
        later_pages = later_pages + total[j:j + 1, :]


def _page_bias(page_table, cache_logf2d, *, n_batch):
    n_pages = page_table.shape[0] // n_batch
    n_phys, width = cache_logf2d.shape
    grid_spec = pltpu.PrefetchScalarGridSpec(
        num_scalar_prefetch=1,
        grid=(n_batch,),
        in_specs=[pl.BlockSpec((n_phys, width), lambda bi, pt: (0, 0))],
        out_specs=pl.BlockSpec((1, n_pages, width), lambda bi, pt: (bi, 0, 0)),
        scratch_shapes=[pltpu.VMEM((n_pages, width), F32)])
    return pl.pallas_call(
        functools.partial(_page_bias_kernel, n_pages=n_pages),
        out_shape=jax.ShapeDtypeStruct((n_batch, n_pages, width), F32),
        grid_spec=grid_spec,
        compiler_params=pltpu.CompilerParams(dimension_semantics=("arbitrary",),
                                             vmem_limit_bytes=VMEM_LIMIT),
        name="page_bias",
    )(page_table, cache_logf2d)


def _fox_sample_kernel(pt_ref, q_ref, kn_ref, vn_ref, lfrow_ref, lfcol_ref, bias_ref, k_hbm, v_hbm,
                       o_ref, kbuf, vbuf, sem, mb_ref, m_ref, l_ref, acc_ref, *, n_pages):
    b = pl.program_id(0)
    slot = lax.rem(b, 2)
    n_rows = q_ref.shape[1]
    n_q = n_rows // FOX_HEADS
    page_rows = PAGE_SIZE * FOX_HEADS

    def page_copies(batch, slot_):
        copies = []
        for j in range(n_pages):
            page = pt_ref[batch * n_pages + j]
            copies.append(pltpu.make_async_copy(k_hbm.at[page], kbuf.at[slot_, j], sem.at[0, slot_]))
            copies.append(pltpu.make_async_copy(v_hbm.at[page], vbuf.at[slot_, j], sem.at[1, slot_]))
        return copies

    @pl.when(b == 0)
    def _():
        for c in page_copies(0, 0):
            c.start()

    @pl.when(b + 1 < pl.num_programs(0))
    def _():
        for c in page_copies(b + 1, 1 - slot):
            c.start()

    x_col = lfcol_ref[0]
    pieces = [x_col[0:FOX_HEADS]]
    for t in range(1, n_q):
        pieces.append(pieces[-1] + x_col[t * FOX_HEADS:(t + 1) * FOX_HEADS])
    cq = jnp.concatenate(pieces, axis=0)
    c_row = jnp.broadcast_to(lfrow_ref[0], (FOX_HEADS, LANES))
    lane = lax.broadcasted_iota(jnp.int32, c_row.shape, 1)
    k = FOX_HEADS
    while k < n_rows:
        c_row = c_row + jnp.where(lane >= k, pltpu.roll(c_row, k, axis=1), 0.0)
        k *= 2
    c_keys = jnp.concatenate([c_row[:, :n_rows]] * n_q, axis=0)

    q2 = q_ref[0].astype(BF16)
    row = lax.broadcasted_iota(jnp.int32, (n_rows, n_rows), 0)
    col = lax.broadcasted_iota(jnp.int32, (n_rows, n_rows), 1)
    same_head = jnp.bitwise_and(row, FOX_HEADS - 1) == jnp.bitwise_and(col, FOX_HEADS - 1)
    causal = jnp.right_shift(col, 3) <= jnp.right_shift(row, 3)
    s = lax.dot_general(q2, kn_ref[0].astype(BF16), NT_DIMS, preferred_element_type=F32)
    s = jnp.where(same_head, jnp.where(causal, s + (cq - c_keys), -jnp.inf), -jnp.inf)
    m0 = jnp.max(s, axis=-1, keepdims=True)
    p = jnp.exp(s - m0)
    m_ref[...] = m0
    l_ref[...] = jnp.sum(p, axis=-1, keepdims=True)
    acc_ref[...] = jnp.dot(p.astype(BF16), vn_ref[0].astype(BF16), preferred_element_type=F32)

    row = lax.broadcasted_iota(jnp.int32, (n_rows, page_rows), 0)
    col = lax.broadcasted_iota(jnp.int32, (n_rows, page_rows), 1)
    same_head = jnp.bitwise_and(row, FOX_HEADS - 1) == jnp.bitwise_and(col, FOX_HEADS - 1)
    mb_ref[...] = jnp.where(same_head, jnp.broadcast_to(cq, (n_rows, page_rows)), -jnp.inf)

    for c in page_copies(b, slot):
        c.wait()

    def chunk_step(i, carry):
        scores = []
        for pg in range(PAGES_PER_STEP):
            j = i * PAGES_PER_STEP + pg
            k2 = kbuf[slot, j].reshape(page_rows, FOX_HEAD_DIM).astype(BF16)
            s = lax.dot_general(q2, k2, NT_DIMS, preferred_element_type=F32)
            scores.append(s + (mb_ref[...] + bias_ref[0, pl.ds(j, 1), :]))
        m_prev = m_ref[...]
        m_new = m_prev
        for s in scores:
            m_new = jnp.maximum(m_new, jnp.max(s, axis=-1, keepdims=True))
        alpha = jnp.exp(m_prev - m_new)
        l_new = alpha * l_ref[...]
        acc = alpha * acc_ref[...]
        for pg, s in enumerate(scores):
            j = i * PAGES_PER_STEP + pg
            v2 = vbuf[slot, j].reshape(page_rows, FOX_HEAD_DIM).astype(BF16)
            p = jnp.exp(s - m_new)
            l_new = l_new + jnp.sum(p, axis=-1, keepdims=True)
            acc = acc + jnp.dot(p.astype(BF16), v2, preferred_element_type=F32)
        m_ref[...] = m_new
        l_ref[...] = l_new
        acc_ref[...] = acc
        return carry

    lax.fori_loop(0, n_pages // PAGES_PER_STEP, chunk_step, 0)
    o_ref[0] = acc_ref[...] / l_ref[...]


def _fox_sample(page_table, q2, kn2, vn2, lf_row, lf_col, bias, cache_k, cache_v):
    b, n_rows, _ = q2.shape
    n_pages = page_table.shape[0] // b
    cur = lambda bi, pt: (bi, 0, 0)
    page_shape = cache_k.shape[1:]
    grid_spec = pltpu.PrefetchScalarGridSpec(
        num_scalar_prefetch=1,
        grid=(b,),
        in_specs=[pl.BlockSpec((1, n_rows, FOX_HEAD_DIM), cur),
                  pl.BlockSpec((1, n_rows, FOX_HEAD_DIM), cur),
                  pl.BlockSpec((1, n_rows, FOX_HEAD_DIM), cur),
                  pl.BlockSpec((1, 1, LANES), cur),
                  pl.BlockSpec((1, n_rows, 1), cur),
                  pl.BlockSpec((1, n_pages, PAGE_SIZE * FOX_HEADS), cur),
                  pl.BlockSpec(memory_space=pl.ANY),
                  pl.BlockSpec(memory_space=pl.ANY)],
        out_specs=pl.BlockSpec((1, n_rows, FOX_HEAD_DIM), cur),
        scratch_shapes=[pltpu.VMEM((2, n_pages) + page_shape, F32),
                        pltpu.VMEM((2, n_pages) + page_shape, F32),
                        pltpu.SemaphoreType.DMA((2, 2)),
                        pltpu.VMEM((n_rows, PAGE_SIZE * FOX_HEADS), F32),
                        pltpu.VMEM((n_rows, 1), F32),
                        pltpu.VMEM((n_rows, 1), F32),
                        pltpu.VMEM((n_rows, FOX_HEAD_DIM), F32)])
    return pl.pallas_call(
        functools.partial(_fox_sample_kernel, n_pages=n_pages),
        out_shape=jax.ShapeDtypeStruct((b, n_rows, FOX_HEAD_DIM), F32),
        grid_spec=grid_spec,
        compiler_params=pltpu.CompilerParams(dimension_semantics=("arbitrary",),
                                             vmem_limit_bytes=VMEM_LIMIT),
        name="fox_sample",
    )(page_table, q2, kn2, vn2, lf_row, lf_col, bias, cache_k, cache_v)


def _lane_suffix_sum(x):
    n = x.shape[-1]
    lane = lax.broadcasted_iota(jnp.int32, x.shape, 1)
    k = 1
    while k < n:
        x = x + jnp.where(lane + k < n, pltpu.roll(x, n - k, axis=1), 0.0)
        k *= 2
    return x


def _fox_paged_kernel(pt_ref, q_ref, kn_ref, vn_ref, lfn_ref, logf_ref, k_hbm, v_hbm, o_ref,
                      kbuf, vbuf, sem, pad_k_ref, pad_v_ref, *, n_pages, n_batch):
    b = pl.program_id(0)
    slot = lax.rem(b, 2)
    n_q = q_ref.shape[1]
    n_rows = n_q * FOX_HEADS

    def page_copies(batch, slot_):
        copies = []
        for j in range(n_pages):
            page = pt_ref[j * n_batch + batch]
            copies.append(pltpu.make_async_copy(k_hbm.at[page], kbuf.at[slot_, j], sem.at[0, slot_]))
            copies.append(pltpu.make_async_copy(v_hbm.at[page], vbuf.at[slot_, j], sem.at[1, slot_]))
        return copies

    @pl.when(b == 0)
    def _():
        for c in page_copies(0, 0):
            c.start()

    @pl.when(b + 1 < pl.num_programs(0))
    def _():
        for c in page_copies(b + 1, 1 - slot):
            c.start()

    head = lax.broadcasted_iota(jnp.int32, (FOX_HEADS, TOK_WIDTH), 0)
    lane_head = jnp.right_shift(lax.broadcasted_iota(jnp.int32, (FOX_HEADS, TOK_WIDTH), 1), 6)
    own_head = head == lane_head
    tile_q = lambda x: jnp.concatenate([x] * n_q, axis=0)

    q = q_ref[0]
    qblk = jnp.concatenate(
        [jnp.where(own_head, jnp.broadcast_to(q[t:t + 1, :], (FOX_HEADS, TOK_WIDTH)), 0.0)
         for t in range(n_q)], axis=0).astype(BF16)

    pad_k_ref[...] = jnp.zeros_like(pad_k_ref)
    pad_v_ref[...] = jnp.zeros_like(pad_v_ref)
    pad_k_ref[0:n_q, :] = kn_ref[0]
    pad_v_ref[0:n_q, :] = vn_ref[0]
    c_new = _lane_cumsum(lfn_ref[0])
    cq = jnp.concatenate([c_new[:, t:t + 1] for t in range(n_q)], axis=0)
    s_own = lax.dot_general(qblk, pad_k_ref[...].astype(BF16), NT_DIMS, preferred_element_type=F32)
    t_of_row = jnp.right_shift(lax.broadcasted_iota(jnp.int32, (n_rows, PAGE_SIZE), 0), 3)
    key = lax.broadcasted_iota(jnp.int32, (n_rows, PAGE_SIZE), 1)
    s_own = jnp.where(key <= t_of_row, s_own + (tile_q(-c_new) + cq), -jnp.inf)

    logf = [logf_ref[pt_ref[j * n_batch + b]] for j in range(n_pages)]
    incl = _lane_suffix_sum(jnp.concatenate(logf, axis=0))
    later_pages = jnp.zeros((FOX_HEADS, 1), F32)
    bias = [None] * n_pages
    for j in reversed(range(n_pages)):
        rows = slice(j * FOX_HEADS, (j + 1) * FOX_HEADS)
        bias[j] = tile_q(later_pages + (incl[rows] - logf[j])) + cq
        later_pages = later_pages + incl[rows, 0:1]

    for c in page_copies(b, slot):
        c.wait()

    scores = [s_own]
    for j in range(n_pages):
        k_t = kbuf[slot, j].reshape(TOK_WIDTH, PAGE_SIZE).astype(BF16)
        scores.append(jnp.dot(qblk, k_t, preferred_element_type=F32) + bias[j])
    s_max = scores[0]
    for s in scores[1:]:
        s_max = jnp.maximum(s_max, s)
    m = jnp.max(s_max, axis=-1, keepdims=True)
    p = jnp.exp(scores[0] - m)
    p_sum = p
    acc = jnp.dot(p.astype(BF16), pad_v_ref[...].astype(BF16), preferred_element_type=F32)
    for j in range(n_pages):
        p = jnp.exp(scores[j + 1] - m)
        p_sum = p_sum + p
        v_t = vbuf[slot, j].reshape(TOK_WIDTH, PAGE_SIZE).astype(BF16)
        acc = acc + lax.dot_general(p.astype(BF16), v_t, NT_DIMS, preferred_element_type=F32)
    o = acc / jnp.sum(p_sum, axis=-1, keepdims=True)
    for t in range(n_q):
        o_t = jnp.where(own_head, o[t * FOX_HEADS:(t + 1) * FOX_HEADS, :], 0.0)
        o_ref[0, t:t + 1, :] = jnp.sum(o_t, axis=0, keepdims=True)


def _fox_paged(page_table, q, k_new, v_new, lfn, cache_logf_t, cache_k_t, cache_v_t):
    b, n_q, _ = q.shape
    n_pages = page_table.shape[0] // b
    cur = lambda bi, pt: (bi, 0, 0)
    page_shape = cache_k_t.shape[1:]
    grid_spec = pltpu.PrefetchScalarGridSpec(
        num_scalar_prefetch=1,
        grid=(b,),
        in_specs=[pl.BlockSpec((1, n_q, TOK_WIDTH), cur),
                  pl.BlockSpec((1, n_q, TOK_WIDTH), cur),
                  pl.BlockSpec((1, n_q, TOK_WIDTH), cur),
                  pl.BlockSpec((1, FOX_HEADS, LANES), cur),
                  pl.BlockSpec(cache_logf_t.shape, lambda bi, pt: (0, 0, 0)),
                  pl.BlockSpec(memory_space=pl.ANY),
                  pl.BlockSpec(memory_space=pl.ANY)],
        out_specs=pl.BlockSpec((1, n_q, TOK_WIDTH), cur),
        scratch_shapes=[pltpu.VMEM((2, n_pages) + page_shape, F32),
                        pltpu.VMEM((2, n_pages) + page_shape, F32),
                        pltpu.SemaphoreType.DMA((2, 2)),
                        pltpu.VMEM((PAGE_SIZE, TOK_WIDTH), F32),
                        pltpu.VMEM((PAGE_SIZE, TOK_WIDTH), F32)])
    return pl.pallas_call(
        functools.partial(_fox_paged_kernel, n_pages=n_pages, n_batch=b),
        out_shape=jax.ShapeDtypeStruct((b, n_q, TOK_WIDTH), F32),
        grid_spec=grid_spec,
        compiler_params=pltpu.CompilerParams(dimension_semantics=("arbitrary",),
                                             vmem_limit_bytes=VMEM_LIMIT),
        name="fox_paged",
    )(page_table, q, k_new, v_new, lfn, cache_logf_t, cache_k_t, cache_v_t)


def _decoder(x, pool_prev, n_prev, mem_k, mem_v, past, wts, *, tm, pool_bb, pool_chunk, mem_bb, mem_ts,
             act):
    b, s_len, _ = x.shape
    t = b * s_len
    h = x.reshape(t, D_MODEL)

    z_tok, qm = _proj_in(h, wts["g_mix"][0], wts["w_in"][0], wts["g_fox_q_t"], wts["g_mem_q_t"][0],
                         fox=False, tm=tm, act=act)
    z_tok = z_tok.reshape(b, s_len, TOK_WIDTH)
    tok = _pool_mix(z_tok, pool_prev, wts["w_pool"], wts["pool_scale"],
                    n_prev=n_prev, bb=pool_bb, chunk=pool_chunk, act=act)
    mo = _mem_attn(qm.reshape(b, s_len, MEM_WIDTH), mem_k, mem_v, 0, bb=mem_bb, ts=mem_ts)
    h = _out_ffn(h, tok.reshape(t, TOK_WIDTH), mo.reshape(t, MEM_WIDTH), wts["w_out"][0],
                 wts["g_ffn"][0], wts["w_gu"][0], wts["w_down"][0], tm=tm, tf=D_FF // 2)

    k_new, v_new, k_bf, v_bf, logf_t = _proj_kv(h, wts["g_kv"], wts["w_kv"], wts["g_fox_k_t"],
                                                wts["w_fg_t"], wts["b_fg"], tm=tm)

    q, qm = _proj_in(h, wts["g_mix"][1], wts["w_in"][1], wts["g_fox_q_t"], wts["g_mem_q_t"][1],
                     fox=True, tm=tm, act=act)
    if past is None:
        f_row = _cumsum_rows(logf_t, seg=s_len)
        n_pairs = FOX_HEADS // 2
        f_row = f_row.reshape(n_pairs, 2, t)
        tok = _fox_prompt(q.reshape(b, s_len, TOK_WIDTH), k_bf.reshape(b, s_len, TOK_WIDTH),
                          v_bf.reshape(b, s_len, TOK_WIDTH), f_row, blk=512)
    else:
        page_table, cache_k_t, cache_v_t, cache_logf_t = past
        lfn = logf_t.reshape(FOX_HEADS, b, s_len).transpose(1, 0, 2)
        lfn = jnp.pad(lfn, ((0, 0), (0, 0), (0, LANES - s_len)))
        per_row = lambda a: a.reshape(b, s_len, TOK_WIDTH)
        tok = _fox_paged(page_table, per_row(q), per_row(k_new), per_row(v_new), lfn,
                         cache_logf_t, cache_k_t, cache_v_t)
    mo = _mem_attn(qm.reshape(b, s_len, MEM_WIDTH), mem_k, mem_v, 1, bb=mem_bb, ts=mem_ts)
    h = _out_ffn(h, tok.reshape(t, TOK_WIDTH), mo.reshape(t, MEM_WIDTH), wts["w_out"][1],
                 wts["g_ffn"][1], wts["w_gu"][1], wts["w_down"][1], tm=tm, tf=D_FF // 2)
    return h.reshape(b, s_len, D_MODEL), z_tok, k_new, v_new, logf_t.T


def kernel(x_prompt, x_sample, cache_mem_k, cache_mem_v, state_pool, cache_k, cache_v, cache_logf,
           page_table, mem_prompt, g_mix, w_in, w_out, g_ffn, w_gu, w_down, g_mem, w_mem_kv,
           g_mem_q, g_mem_k, w_pool, pool_scale, g_kv, w_kv, g_fox_k, w_fg, b_fg, g_fox_q):
    depth = w_in.shape[0]
    b, s_len, _ = x_prompt.shape
    db, ds, _ = x_sample.shape
    wts = {
        "g_mix": g_mix.reshape(depth, 1, D_MODEL),
        "w_in": w_in.astype(BF16),
        "w_out": w_out.astype(BF16),
        "g_ffn": g_ffn.reshape(depth, 1, D_MODEL),
        "w_gu": w_gu.astype(BF16),
        "w_down": w_down.astype(BF16),
        "g_mem_q_t": jnp.tile(g_mem_q, (1, MEM_HEADS)).reshape(depth, 1, MEM_WIDTH),
        "w_pool": w_pool[0].astype(BF16),
        "pool_scale": pool_scale[0].reshape(1, TOK_WIDTH),
        "g_kv": g_kv.reshape(1, D_MODEL),
        "w_kv": w_kv.astype(BF16),
        "g_fox_k_t": jnp.tile(g_fox_k, FOX_HEADS).reshape(1, TOK_WIDTH),
        "w_fg_t": w_fg.T.astype(BF16),
        "b_fg": b_fg.reshape(FOX_HEADS, 1),
        "g_fox_q_t": jnp.tile(g_fox_q[0], FOX_HEADS).reshape(1, TOK_WIDTH),
    }

    mem_k_p, mem_v_p = _proj_memkv(mem_prompt.reshape(b * N_MEM, D_MODEL),
                                   g_mem.reshape(depth, 1, D_MODEL), w_mem_kv.astype(BF16),
                                   jnp.tile(g_mem_k, (1, MEM_HEADS)).reshape(depth, 1, MEM_WIDTH), tm=512)
    mem_rows = (N_MEM * MEM_HEADS, MEM_HEAD_DIM)
    mem_k_p = mem_k_p.reshape((depth, b) + mem_rows)
    mem_v_p = mem_v_p.reshape((depth, b) + mem_rows)
    y_p, ztok_p, k_p, v_p, logf_p = _decoder(
        x_prompt, jnp.zeros((b, POOL_HIST, TOK_WIDTH), F32), 0, mem_k_p, mem_v_p, None, wts,
        tm=512, pool_bb=1, pool_chunk=256, mem_bb=1, mem_ts=512, act=BF16)
    n_pp = s_len // PAGE_SIZE
    head_shape = (FOX_HEADS, FOX_HEAD_DIM)

    past = (page_table.T.reshape(-1),
            cache_k.transpose(0, 2, 3, 1), cache_v.transpose(0, 2, 3, 1),
            cache_logf.transpose(0, 2, 1))
    prev = jnp.pad(state_pool[0], ((0, 0), (POOL_HIST - POOL_STATE, 0), (0, 0)))
    y_s, ztok_s, k_s, v_s, logf_s = _decoder(
        x_sample, prev, POOL_STATE, cache_mem_k.reshape((depth, db) + mem_rows),
        cache_mem_v.reshape((depth, db) + mem_rows), past, wts,
        tm=512, pool_bb=32, pool_chunk=ds, mem_bb=8, mem_ts=ds, act=F32)

    pool_state_p = ztok_p[:, s_len - POOL_STATE:][None]
    pool_state_s = jnp.concatenate([state_pool[0], ztok_s], axis=1)[:, -POOL_STATE:][None]
    return (y_p, y_s,
            k_p.reshape((b, n_pp, PAGE_SIZE) + head_shape),
            v_p.reshape((b, n_pp, PAGE_SIZE) + head_shape),
            logf_p.reshape(b, n_pp, PAGE_SIZE, FOX_HEADS),
            mem_k_p.reshape(depth, b, N_MEM, MEM_HEADS, MEM_HEAD_DIM),
            mem_v_p.reshape(depth, b, N_MEM, MEM_HEADS, MEM_HEAD_DIM),
            pool_state_p,
            k_s.reshape((db, ds) + head_shape),
            v_s.reshape((db, ds) + head_shape),
            logf_s.reshape(db, ds, FOX_HEADS),
            pool_state_s)
```

```python
import functools

import jax
import jax.numpy as jnp
from jax import lax
from jax.experimental import pallas as pl
from jax.experimental.pallas import tpu as pltpu

D_MODEL = 1024
TOK_WIDTH = 512
MEM_WIDTH = 512
POOL_WINDOWS = (2, 4, 8, 16)
POOL_GROUP = 128
POOL_STATE = 15
POOL_HIST = 16
FOX_HEADS = 8
FOX_HEAD_DIM = 64
MEM_HEADS = 4
MEM_HEAD_DIM = 128
N_MEM = 256
D_FF = 2816
PAGE_SIZE = 128
EPS = 1e-6
LANES = 128
VMEM_LIMIT = 56 * 1024 * 1024
FOX_CHUNK_ROWS = 64
PAGES_PER_STEP = 4

F32 = jnp.float32
BF16 = jnp.bfloat16
NT_DIMS = (((1,), (1,)), ((), ()))


def _rmsnorm(x, g):
    return x * lax.rsqrt(jnp.mean(x * x, axis=-1, keepdims=True) + EPS) * g


def _headnorm128(z, g):
    outs = []
    for h in range(z.shape[-1] // LANES):
        zh = z[:, h * LANES:(h + 1) * LANES]
        r = lax.rsqrt(jnp.mean(zh * zh, axis=-1, keepdims=True) + EPS)
        outs.append(zh * r * g[:, h * LANES:(h + 1) * LANES])
    return jnp.concatenate(outs, axis=-1)


def _headnorm64(z, g):
    outs = []
    for p in range(z.shape[-1] // LANES):
        zp = z[:, p * LANES:(p + 1) * LANES]
        sq = zp * zp
        lane = lax.broadcasted_iota(jnp.int32, zp.shape, 1)
        lo = lane < FOX_HEAD_DIM
        s_lo = jnp.sum(jnp.where(lo, sq, 0.0), axis=-1, keepdims=True)
        s_hi = jnp.sum(jnp.where(lo, 0.0, sq), axis=-1, keepdims=True)
        r = jnp.where(lo, lax.rsqrt(s_lo / FOX_HEAD_DIM + EPS), lax.rsqrt(s_hi / FOX_HEAD_DIM + EPS))
        outs.append(zp * r * g[:, p * LANES:(p + 1) * LANES])
    return jnp.concatenate(outs, axis=-1)


def _proj_in_kernel(x_ref, g_ref, w_ref, gtok_ref, gmem_ref, tok_ref, qm_ref, *, fox):
    xn = _rmsnorm(x_ref[...], g_ref[...]).astype(BF16)
    z = jnp.dot(xn, w_ref[...], preferred_element_type=F32)
    z_tok = z[:, :TOK_WIDTH]
    if fox:
        tok_ref[...] = (_headnorm64(z_tok, gtok_ref[...]) * (FOX_HEAD_DIM ** -0.5)).astype(tok_ref.dtype)
    else:
        tok_ref[...] = z_tok
    qm_ref[...] = _headnorm128(z[:, TOK_WIDTH:], gmem_ref[...]).astype(qm_ref.dtype)


def _proj_in(x, g, w, g_tok, g_mem, *, fox, tm, act):
    t = x.shape[0]
    row = lambda i: (i, 0)
    fixed = lambda i: (0, 0)
    return pl.pallas_call(
        functools.partial(_proj_in_kernel, fox=fox),
        out_shape=(jax.ShapeDtypeStruct((t, TOK_WIDTH), act if fox else F32),
                   jax.ShapeDtypeStruct((t, MEM_WIDTH), act)),
        grid=(t // tm,),
        in_specs=[pl.BlockSpec((tm, D_MODEL), row),
                  pl.BlockSpec((1, D_MODEL), fixed),
                  pl.BlockSpec((D_MODEL, D_MODEL), fixed),
                  pl.BlockSpec((1, TOK_WIDTH), fixed),
                  pl.BlockSpec((1, MEM_WIDTH), fixed)],
        out_specs=(pl.BlockSpec((tm, TOK_WIDTH), row), pl.BlockSpec((tm, MEM_WIDTH), row)),
        compiler_params=pltpu.CompilerParams(dimension_semantics=("arbitrary",),
                                             vmem_limit_bytes=VMEM_LIMIT),
        name="proj_in_fox" if fox else "proj_in_pool",
    )(x, g, w, g_tok, g_mem)


def _log_sigmoid(x):
    return jnp.minimum(x, 0.0) - jnp.log(1.0 + jnp.exp(-jnp.abs(x)))


def _proj_kv_kernel(x_ref, g_ref, w_ref, gk_ref, wfg_ref, bfg_ref,
                    k_ref, v_ref, kb_ref, vb_ref, logf_ref):
    xn = _rmsnorm(x_ref[...], g_ref[...]).astype(BF16)
    z = jnp.dot(xn, w_ref[...], preferred_element_type=F32)
    k = _headnorm64(z[:, :TOK_WIDTH], gk_ref[...])
    v = z[:, TOK_WIDTH:]
    k_ref[...] = k
    v_ref[...] = v
    kb_ref[...] = k.astype(BF16)
    vb_ref[...] = v.astype(BF16)
    gate = lax.dot_general(wfg_ref[...], xn, NT_DIMS, preferred_element_type=F32) + bfg_ref[...]
    logf_ref[...] = _log_sigmoid(gate)


def _proj_kv_paged_kernel(x_ref, g_ref, w_ref, gk_ref, wfg_ref, bfg_ref,
                          kp_ref, vp_ref, kb_ref, vb_ref, logf_ref, logfp_ref):
    xn = _rmsnorm(x_ref[...], g_ref[...]).astype(BF16)
    z = jnp.dot(xn, w_ref[...], preferred_element_type=F32)
    k = _headnorm64(z[:, :TOK_WIDTH], gk_ref[...])
    v = z[:, TOK_WIDTH:]
    kb_ref[...] = k.astype(BF16)
    vb_ref[...] = v.astype(BF16)
    gate = lax.dot_general(wfg_ref[...], xn, NT_DIMS, preferred_element_type=F32) + bfg_ref[...]
    logf = _log_sigmoid(gate)
    logf_ref[...] = logf
    k_t = k.T
    v_t = v.T
    for pg in range(kp_ref.shape[0]):
        rows = slice(pg * PAGE_SIZE, (pg + 1) * PAGE_SIZE)
        kp_ref[pg] = k_t[:, rows]
        vp_ref[pg] = v_t[:, rows]
        logfp_ref[pg] = logf[:, rows]


def _proj_kv_paged(x, g, w, g_k, wfg_t, bfg, *, tm):
    t = x.shape[0]
    row = lambda i: (i, 0)
    fixed = lambda i: (0, 0)
    ppt = tm // PAGE_SIZE
    page_rows = lambda i: (i, 0, 0)
    return pl.pallas_call(
        _proj_kv_paged_kernel,
        out_shape=(jax.ShapeDtypeStruct((t // PAGE_SIZE, TOK_WIDTH, PAGE_SIZE), F32),
                   jax.ShapeDtypeStruct((t // PAGE_SIZE, TOK_WIDTH, PAGE_SIZE), F32),
                   jax.ShapeDtypeStruct((t, TOK_WIDTH), BF16),
                   jax.ShapeDtypeStruct((t, TOK_WIDTH), BF16),
                   jax.ShapeDtypeStruct((FOX_HEADS, t), F32),
                   jax.ShapeDtypeStruct((t // PAGE_SIZE, FOX_HEADS, PAGE_SIZE), F32)),
        grid=(t // tm,),
        in_specs=[pl.BlockSpec((tm, D_MODEL), row),
                  pl.BlockSpec((1, D_MODEL), fixed),
                  pl.BlockSpec((D_MODEL, D_MODEL), fixed),
                  pl.BlockSpec((1, TOK_WIDTH), fixed),
                  pl.BlockSpec((FOX_HEADS, D_MODEL), fixed),
                  pl.BlockSpec((FOX_HEADS, 1), fixed)],
        out_specs=(pl.BlockSpec((ppt, TOK_WIDTH, PAGE_SIZE), page_rows),
                   pl.BlockSpec((ppt, TOK_WIDTH, PAGE_SIZE), page_rows),
                   pl.BlockSpec((tm, TOK_WIDTH), row), pl.BlockSpec((tm, TOK_WIDTH), row),
                   pl.BlockSpec((FOX_HEADS, tm), lambda i: (0, i)),
                   pl.BlockSpec((ppt, FOX_HEADS, PAGE_SIZE), page_rows)),
        compiler_params=pltpu.CompilerParams(dimension_semantics=("arbitrary",),
                                             vmem_limit_bytes=VMEM_LIMIT),
        name="proj_kv_paged",
    )(x, g, w, g_k, wfg_t, bfg)


def _proj_kv(x, g, w, g_k, wfg_t, bfg, *, tm):
    t = x.shape[0]
    row = lambda i: (i, 0)
    fixed = lambda i: (0, 0)
    return pl.pallas_call(
        _proj_kv_kernel,
        out_shape=(jax.ShapeDtypeStruct((t, TOK_WIDTH), F32),
                   jax.ShapeDtypeStruct((t, TOK_WIDTH), F32),
                   jax.ShapeDtypeStruct((t, TOK_WIDTH), BF16),
                   jax.ShapeDtypeStruct((t, TOK_WIDTH), BF16),
                   jax.ShapeDtypeStruct((FOX_HEADS, t), F32)),
        grid=(t // tm,),
        in_specs=[pl.BlockSpec((tm, D_MODEL), row),
                  pl.BlockSpec((1, D_MODEL), fixed),
                  pl.BlockSpec((D_MODEL, D_MODEL), fixed),
                  pl.BlockSpec((1, TOK_WIDTH), fixed),
                  pl.BlockSpec((FOX_HEADS, D_MODEL), fixed),
                  pl.BlockSpec((FOX_HEADS, 1), fixed)],
        out_specs=(pl.BlockSpec((tm, TOK_WIDTH), row), pl.BlockSpec((tm, TOK_WIDTH), row),
                   pl.BlockSpec((tm, TOK_WIDTH), row), pl.BlockSpec((tm, TOK_WIDTH), row),
                   pl.BlockSpec((FOX_HEADS, tm), lambda i: (0, i))),
        compiler_params=pltpu.CompilerParams(dimension_semantics=("arbitrary",),
                                             vmem_limit_bytes=VMEM_LIMIT),
        name="proj_kv",
    )(x, g, w, g_k, wfg_t, bfg)


def _proj_memkv_kernel(x_ref, g_ref, w_ref, gk_ref, k_ref, v_ref):
    xn = _rmsnorm(x_ref[...], g_ref[0]).astype(BF16)
    z = jnp.dot(xn, w_ref[0], preferred_element_type=F32)
    k = _headnorm128(z[:, :MEM_WIDTH], gk_ref[0])
    tm = x_ref.shape[0]
    for h in range(MEM_HEADS):
        rows = pl.ds(h, tm, stride=MEM_HEADS)
        k_ref[0, rows, :] = k[:, h * MEM_HEAD_DIM:(h + 1) * MEM_HEAD_DIM]
        v_ref[0, rows, :] = z[:, MEM_WIDTH + h * MEM_HEAD_DIM:MEM_WIDTH + (h + 1) * MEM_HEAD_DIM]


def _proj_memkv(x, g, w, g_k, *, tm):
    t = x.shape[0]
    n_layers = w.shape[0]
    out_block = pl.BlockSpec((1, tm * MEM_HEADS, MEM_HEAD_DIM), lambda l, i: (l, i, 0))
    return pl.pallas_call(
        _proj_memkv_kernel,
        out_shape=(jax.ShapeDtypeStruct((n_layers, t * MEM_HEADS, MEM_HEAD_DIM), F32),
                   jax.ShapeDtypeStruct((n_layers, t * MEM_HEADS, MEM_HEAD_DIM), F32)),
        grid=(n_layers, t // tm),
        in_specs=[pl.BlockSpec((tm, D_MODEL), lambda l, i: (i, 0)),
                  pl.BlockSpec((1, 1, D_MODEL), lambda l, i: (l, 0, 0)),
                  pl.BlockSpec((1, D_MODEL, D_MODEL), lambda l, i: (l, 0, 0)),
                  pl.BlockSpec((1, 1, MEM_WIDTH), lambda l, i: (l, 0, 0))],
        out_specs=(out_block, out_block),
        compiler_params=pltpu.CompilerParams(dimension_semantics=("arbitrary", "arbitrary"),
                                             vmem_limit_bytes=VMEM_LIMIT),
        name="proj_memkv",
    )(x, g, w, g_k)


def _pool_kernel(u_ref, prev_ref, w_ref, scale_ref, o_ref, ext_ref, *, n_prev, chunk):
    bb, s_len, _ = u_ref.shape
    ext_ref[:, 0:POOL_HIST, :] = prev_ref[...]
    ext_ref[:, POOL_HIST:, :] = u_ref[...]
    for c in range(s_len // chunk):
        r0 = POOL_HIST + c * chunk
        pos = c * chunk + lax.broadcasted_iota(jnp.int32, (1, chunk, 1), 1)
        for gi, win in enumerate(POOL_WINDOWS):
            lanes = slice(gi * POOL_GROUP, (gi + 1) * POOL_GROUP)
            u_new = ext_ref[:, r0:r0 + chunk, lanes]
            acc = u_new
            for k in range(1, win):
                acc = acc + ext_ref[:, r0 - k:r0 - k + chunk, lanes]
            count = jnp.minimum(win, n_prev + pos + 1).astype(F32)
            y = acc / count - u_new
            z = jnp.dot(y.reshape(bb * chunk, POOL_GROUP).astype(BF16), w_ref[gi],
                        preferred_element_type=F32)
            z = z.reshape(bb, chunk, POOL_GROUP) * scale_ref[:, lanes]
            o_ref[:, c * chunk:(c + 1) * chunk, lanes] = z.astype(o_ref.dtype)


def _pool_mix(u, prev, w_pool, scale, *, n_prev, bb, chunk, act):
    b, s_len, _ = u.shape
    return pl.pallas_call(
        functools.partial(_pool_kernel, n_prev=n_prev, chunk=chunk),
        out_shape=jax.ShapeDtypeStruct((b, s_len, TOK_WIDTH), act),
        grid=(b // bb,),
        in_specs=[pl.BlockSpec((bb, s_len, TOK_WIDTH), lambda i: (i, 0, 0)),
                  pl.BlockSpec((bb, POOL_HIST, TOK_WIDTH), lambda i: (i, 0, 0)),
                  pl.BlockSpec((len(POOL_WINDOWS), POOL_GROUP, POOL_GROUP), lambda i: (0, 0, 0)),
                  pl.BlockSpec((1, TOK_WIDTH), lambda i: (0, 0))],
        out_specs=pl.BlockSpec((bb, s_len, TOK_WIDTH), lambda i: (i, 0, 0)),
        scratch_shapes=[pltpu.VMEM((bb, POOL_HIST + s_len, TOK_WIDTH), F32)],
        compiler_params=pltpu.CompilerParams(dimension_semantics=("arbitrary",),
                                             vmem_limit_bytes=VMEM_LIMIT),
        name="pool_mix",
    )(u, prev, w_pool, scale)


def _mem_attn_kernel(q_ref, k_ref, v_ref, o_ref):
    for h in range(MEM_HEADS):
        lanes = slice(h * MEM_HEAD_DIM, (h + 1) * MEM_HEAD_DIM)
        q = q_ref[:, :, lanes].astype(BF16)
        k = k_ref[:, pl.ds(h, N_MEM, stride=MEM_HEADS), :].astype(BF16)
        v = v_ref[:, pl.ds(h, N_MEM, stride=MEM_HEADS), :].astype(BF16)
        s = jnp.einsum("bqd,bkd->bqk", q, k, preferred_element_type=F32) * (MEM_HEAD_DIM ** -0.5)
        m = jnp.max(s, axis=-1, keepdims=True)
        p = jnp.exp(s - m)
        l = jnp.sum(p, axis=-1, keepdims=True)
        o = jnp.einsum("bqk,bkd->bqd", p.astype(BF16), v, preferred_element_type=F32)
        o_ref[:, :, lanes] = (o / l).astype(o_ref.dtype)


def _mem_attn(q, mk, mv, layer, *, bb, ts):
    b, s_len, _ = q.shape
    mem_block = pl.BlockSpec((None, bb, N_MEM * MEM_HEADS, MEM_HEAD_DIM),
                             lambda i, j: (layer, i, 0, 0))
    return pl.pallas_call(
        _mem_attn_kernel,
        out_shape=jax.ShapeDtypeStruct((b, s_len, MEM_WIDTH), q.dtype),
        grid=(b // bb, s_len // ts),
        in_specs=[pl.BlockSpec((bb, ts, MEM_WIDTH), lambda i, j: (i, j, 0)), mem_block, mem_block],
        out_specs=pl.BlockSpec((bb, ts, MEM_WIDTH), lambda i, j: (i, j, 0)),
        compiler_params=pltpu.CompilerParams(dimension_semantics=("arbitrary", "arbitrary"),
                                             vmem_limit_bytes=VMEM_LIMIT),
        name="mem_attn",
    )(q, mk, mv)


def _out_ffn_kernel(h_ref, tok_ref, mo_ref, wout_ref, g_ref, wg_ref, wu_ref, wd_ref,
                    o_ref, xn_ref, acc_ref):
    kf = pl.program_id(1)

    @pl.when(kf == 0)
    def _():
        h1 = (h_ref[...]
              + jnp.dot(tok_ref[...].astype(BF16), wout_ref[:TOK_WIDTH, :], preferred_element_type=F32)
              + jnp.dot(mo_ref[...].astype(BF16), wout_ref[TOK_WIDTH:, :], preferred_element_type=F32))
        acc_ref[...] = h1
        xn_ref[...] = _rmsnorm(h1, g_ref[...]).astype(BF16)

    x = xn_ref[...]
    gate = jnp.dot(x, wg_ref[...], preferred_element_type=F32)
    up = jnp.dot(x, wu_ref[...], preferred_element_type=F32)
    act = (gate * (1.0 / (1.0 + jnp.exp(-gate))) * up).astype(BF16)
    acc_ref[...] += jnp.dot(act, wd_ref[...], preferred_element_type=F32)

    @pl.when(kf == pl.num_programs(1) - 1)
    def _():
        o_ref[...] = acc_ref[...]


def _out_ffn(h, tok, mo, w_out, g, w_gu, w_down, *, tm, tf):
    t = h.shape[0]
    nf = D_FF // tf
    row = lambda i, k: (i, 0)
    fixed = lambda i, k: (0, 0)
    return pl.pallas_call(
        _out_ffn_kernel,
        out_shape=jax.ShapeDtypeStruct((t, D_MODEL), F32),
        grid=(t // tm, nf),
        in_specs=[pl.BlockSpec((tm, D_MODEL), row),
                  pl.BlockSpec((tm, TOK_WIDTH), row),
                  pl.BlockSpec((tm, MEM_WIDTH), row),
                  pl.BlockSpec((D_MODEL, D_MODEL), fixed),
                  pl.BlockSpec((1, D_MODEL), fixed),
                  pl.BlockSpec((D_MODEL, tf), lambda i, k: (0, k)),
                  pl.BlockSpec((D_MODEL, tf), lambda i, k: (0, nf + k)),
                  pl.BlockSpec((tf, D_MODEL), lambda i, k: (k, 0))],
        out_specs=pl.BlockSpec((tm, D_MODEL), row),
        scratch_shapes=[pltpu.VMEM((tm, D_MODEL), BF16), pltpu.VMEM((tm, D_MODEL), F32)],
        compiler_params=pltpu.CompilerParams(dimension_semantics=("arbitrary", "arbitrary"),
                                             vmem_limit_bytes=VMEM_LIMIT),
        name="out_ffn",
    )(h, tok, mo, w_out, g, w_gu, w_gu, w_down)


def _lane_cumsum(x):
    n = x.shape[-1]
    lane = lax.broadcasted_iota(jnp.int32, x.shape, 1)
    k = 1
    while k < n:
        x = x + jnp.where(lane >= k, pltpu.roll(x, k, axis=1), 0.0)
        k *= 2
    return x


def _cumsum_kernel(x_ref, o_ref):
    o_ref[...] = _lane_cumsum(x_ref[...])


def _cumsum_rows(x, *, seg):
    rows, t = x.shape
    return pl.pallas_call(
        _cumsum_kernel,
        out_shape=jax.ShapeDtypeStruct((rows, t), F32),
        grid=(t // seg,),
        in_specs=[pl.BlockSpec((rows, seg), lambda i: (0, i))],
        out_specs=pl.BlockSpec((rows, seg), lambda i: (0, i)),
        compiler_params=pltpu.CompilerParams(dimension_semantics=("arbitrary",)),
        name="logf_cumsum",
    )(x)


def _fox_prompt_kernel(q_ref, k_ref, v_ref, frow_ref, o_ref, m_ref, l_ref, acc_ref, alpha_ref,
                       sa_ref, sb_ref, p_ref, *, blk, n_q_blocks):
    qi = pl.program_id(2)
    q = q_ref[0]
    lane = lax.broadcasted_iota(jnp.int32, q.shape, 1)
    sub = FOX_CHUNK_ROWS
    n_rep = blk // LANES
    row = lax.broadcasted_iota(jnp.int32, (sub, blk), 0)
    col = lax.broadcasted_iota(jnp.int32, (sub, blk), 1)
    q_head = [jnp.where(lane < FOX_HEAD_DIM, q, jnp.zeros_like(q)),
              jnp.where(lane >= FOX_HEAD_DIM, q, jnp.zeros_like(q))]
    m_ref[...] = jnp.full(m_ref.shape, -jnp.inf, F32)
    l_ref[...] = jnp.zeros_like(l_ref)
    acc_ref[...] = jnp.zeros_like(acc_ref)

    def scores(ki, s_buf):
        kt = k_ref[0, ki * blk:(ki + 1) * blk, :]
        for j in range(2):
            s_buf[j] = lax.dot_general(q_head[j], kt, NT_DIMS, preferred_element_type=F32)

    def update(qc, ki, s_buf, masked):
        keys = slice(ki * blk, (ki + 1) * blk)
        vt = v_ref[0, keys, :]
        for j in range(2):
            f_end = frow_ref[0, j:j + 1, (qc + 1) * blk - 1:(qc + 1) * blk]
            bias = f_end - frow_ref[0, j:j + 1, keys]
            for r in range(blk // sub):
                rows = slice(r * sub, (r + 1) * sub)
                s = s_buf[j, rows] + bias
                if masked:
                    s = jnp.where(col <= row + r * sub, s, -jnp.inf)
                m_prev = m_ref[j, rows]
                m_new = jnp.maximum(m_prev, jnp.max(s, axis=-1, keepdims=True))
                alpha = jnp.exp(m_prev - m_new)
                p = jnp.exp(s - jnp.concatenate([m_new] * n_rep, axis=1))
                p_lanes = p[:, :LANES]
                for c in range(1, n_rep):
                    p_lanes = p_lanes + p[:, c * LANES:(c + 1) * LANES]
                l_ref[j, rows] = alpha * l_ref[j, rows] + p_lanes
                m_ref[j, rows] = m_new
                alpha_ref[j, rows] = alpha
                p_ref[j, rows] = p.astype(BF16)
            acc_ref[j] = alpha_ref[j] * acc_ref[j] + jnp.dot(p_ref[j], vt, preferred_element_type=F32)

    bufs = (sa_ref, sb_ref)
    for qc in range(n_q_blocks):
        @pl.when(qi == qc)
        def _(qc=qc):
            scores(0, bufs[0])
            for ki in range(qc):
                scores(ki + 1, bufs[(ki + 1) % 2])
                update(qc, ki, bufs[ki % 2], False)
            update(qc, qc, bufs[qc % 2], True)

    l0 = jnp.sum(l_ref[0], axis=-1, keepdims=True)
    l1 = jnp.sum(l_ref[1], axis=-1, keepdims=True)
    o = jnp.where(lane < FOX_HEAD_DIM, acc_ref[0] / l0, acc_ref[1] / l1)
    o_ref[0] = o.astype(o_ref.dtype)


def _fox_prompt(q, k, v, frow, *, blk):
    b, s_len, _ = q.shape
    n_pairs = TOK_WIDTH // LANES
    nq = s_len // blk
    return pl.pallas_call(
        functools.partial(_fox_prompt_kernel, blk=blk, n_q_blocks=nq),
        out_shape=jax.ShapeDtypeStruct((b, s_len, TOK_WIDTH), BF16),
        grid=(b, n_pairs, nq),
        in_specs=[pl.BlockSpec((1, blk, LANES), lambda bi, hp, qi: (bi, qi, hp)),
                  pl.BlockSpec((1, s_len, LANES), lambda bi, hp, qi: (bi, 0, hp)),
                  pl.BlockSpec((1, s_len, LANES), lambda bi, hp, qi: (bi, 0, hp)),
                  pl.BlockSpec((1, 2, s_len), lambda bi, hp, qi: (hp, 0, bi))],
        out_specs=pl.BlockSpec((1, blk, LANES), lambda bi, hp, qi: (bi, qi, hp)),
        scratch_shapes=[pltpu.VMEM((2, blk, LANES), F32), pltpu.VMEM((2, blk, LANES), F32),
                        pltpu.VMEM((2, blk, LANES), F32), pltpu.VMEM((2, blk, LANES), F32),
                        pltpu.VMEM((2, blk, blk), F32), pltpu.VMEM((2, blk, blk), F32),
                        pltpu.VMEM((2, blk, blk), BF16)],
        compiler_params=pltpu.CompilerParams(
            dimension_semantics=("arbitrary", "arbitrary", "arbitrary"),
            vmem_limit_bytes=VMEM_LIMIT),
        name="fox_prompt",
    )(q, k, v, frow)


def _page_bias_kernel(pt_ref, logf_ref, o_ref, x_ref, *, n_pages):
    b = pl.program_id(0)
    n = x_ref.shape[-1]
    for j in range(n_pages):
        x_ref[j:j + 1, :] = logf_ref[pl.ds(pt_ref[b * n_pages + j], 1), :]
    x = x_ref[...]
    lane = lax.broadcasted_iota(jnp.int32, x.shape, 1)
    incl = x
    k = FOX_HEADS
    while k < n:
        incl = incl + jnp.where(lane + k < n, pltpu.roll(incl, n - k, axis=1), 0.0)
        k *= 2
    total = jnp.where(lane < FOX_HEADS, incl, 0.0)
    k = FOX_HEADS
    while k < n:
        total = total + pltpu.roll(total, k, axis=1)
        k *= 2
    excl = incl - x
    later_pages = jnp.zeros((1, n), F32)
    for j in reversed(range(n_pages)):
        o_ref[0, j:j + 1, :] = later_pages + excl[j:j + 1, :]
        later_pages = later_pages + total[j:j + 1, :]


def _page_bias(page_table, cache_logf2d, *, n_batch):
    n_pages = page_table.shape[0] // n_batch
    n_phys, width = cache_logf2d.shape
    grid_spec = pltpu.PrefetchScalarGridSpec(
        num_scalar_prefetch=1,
        grid=(n_batch,),
        in_specs=[pl.BlockSpec((n_phys, width), lambda bi, pt: (0, 0))],
        out_specs=pl.BlockSpec((1, n_pages, width), lambda bi, pt: (bi, 0, 0)),
        scratch_shapes=[pltpu.VMEM((n_pages, width), F32)])
    return pl.pallas_call(
        functools.partial(_page_bias_kernel, n_pages=n_pages),
        out_shape=jax.ShapeDtypeStruct((n_batch, n_pages, width), F32),
        grid_spec=grid_spec,
        compiler_params=pltpu.CompilerParams(dimension_semantics=("arbitrary",),
                                             vmem_limit_bytes=VMEM_LIMIT),
        name="page_bias",
    )(page_table, cache_logf2d)


def _fox_sample_kernel(pt_ref, q_ref, kn_ref, vn_ref, lfrow_ref, lfcol_ref, bias_ref, k_hbm, v_hbm,
                       o_ref, kbuf, vbuf, sem, mb_ref, m_ref, l_ref, acc_ref, *, n_pages):
    b = pl.program_id(0)
    slot = lax.rem(b, 2)
    n_rows = q_ref.shape[1]
    n_q = n_rows // FOX_HEADS
    page_rows = PAGE_SIZE * FOX_HEADS

    def page_copies(batch, slot_):
        copies = []
        for j in range(n_pages):
            page = pt_ref[batch * n_pages + j]
            copies.append(pltpu.make_async_copy(k_hbm.at[page], kbuf.at[slot_, j], sem.at[0, slot_]))
            copies.append(pltpu.make_async_copy(v_hbm.at[page], vbuf.at[slot_, j], sem.at[1, slot_]))
        return copies

    @pl.when(b == 0)
    def _():
        for c in page_copies(0, 0):
            c.start()

    @pl.when(b + 1 < pl.num_programs(0))
    def _():
        for c in page_copies(b + 1, 1 - slot):
            c.start()

    x_col = lfcol_ref[0]
    pieces = [x_col[0:FOX_HEADS]]
    for t in range(1, n_q):
        pieces.append(pieces[-1] + x_col[t * FOX_HEADS:(t + 1) * FOX_HEADS])
    cq = jnp.concatenate(pieces, axis=0)
    c_row = jnp.broadcast_to(lfrow_ref[0], (FOX_HEADS, LANES))
    lane = lax.broadcasted_iota(jnp.int32, c_row.shape, 1)
    k = FOX_HEADS
    while k < n_rows:
        c_row = c_row + jnp.where(lane >= k, pltpu.roll(c_row, k, axis=1), 0.0)
        k *= 2
    c_keys = jnp.concatenate([c_row[:, :n_rows]] * n_q, axis=0)

    q2 = q_ref[0].astype(BF16)
    row = lax.broadcasted_iota(jnp.int32, (n_rows, n_rows), 0)
    col = lax.broadcasted_iota(jnp.int32, (n_rows, n_rows), 1)
    same_head = jnp.bitwise_and(row, FOX_HEADS - 1) == jnp.bitwise_and(col, FOX_HEADS - 1)
    causal = jnp.right_shift(col, 3) <= jnp.right_shift(row, 3)
    s = lax.dot_general(q2, kn_ref[0].astype(BF16), NT_DIMS, preferred_element_type=F32)
    s = jnp.where(same_head, jnp.where(causal, s + (cq - c_keys), -jnp.inf), -jnp.inf)
    m0 = jnp.max(s, axis=-1, keepdims=True)
    p = jnp.exp(s - m0)
    m_ref[...] = m0
    l_ref[...] = jnp.sum(p, axis=-1, keepdims=True)
    acc_ref[...] = jnp.dot(p.astype(BF16), vn_ref[0].astype(BF16), preferred_element_type=F32)

    row = lax.broadcasted_iota(jnp.int32, (n_rows, page_rows), 0)
    col = lax.broadcasted_iota(jnp.int32, (n_rows, page_rows), 1)
    same_head = jnp.bitwise_and(row, FOX_HEADS - 1) == jnp.bitwise_and(col, FOX_HEADS - 1)
    mb_ref[...] = jnp.where(same_head, jnp.broadcast_to(cq, (n_rows, page_rows)), -jnp.inf)

    for c in page_copies(b, slot):
        c.wait()

    def chunk_step(i, carry):
        scores = []
        for pg in range(PAGES_PER_STEP):
            j = i * PAGES_PER_STEP + pg
            k2 = kbuf[slot, j].reshape(page_rows, FOX_HEAD_DIM).astype(BF16)
            s = lax.dot_general(q2, k2, NT_DIMS, preferred_element_type=F32)
            scores.append(s + (mb_ref[...] + bias_ref[0, pl.ds(j, 1), :]))
        m_prev = m_ref[...]
        m_new = m_prev
        for s in scores:
            m_new = jnp.maximum(m_new, jnp.max(s, axis=-1, keepdims=True))
        alpha = jnp.exp(m_prev - m_new)
        l_new = alpha * l_ref[...]
        acc = alpha * acc_ref[...]
        for pg, s in enumerate(scores):
            j = i * PAGES_PER_STEP + pg
            v2 = vbuf[slot, j].reshape(page_rows, FOX_HEAD_DIM).astype(BF16)
            p = jnp.exp(s - m_new)
            l_new = l_new + jnp.sum(p, axis=-1, keepdims=True)
            acc = acc + jnp.dot(p.astype(BF16), v2, preferred_element_type=F32)
        m_ref[...] = m_new
        l_ref[...] = l_new
        acc_ref[...] = acc
        return carry

    lax.fori_loop(0, n_pages // PAGES_PER_STEP, chunk_step, 0)
    o_ref[0] = acc_ref[...] / l_ref[...]


def _fox_sample(page_table, q2, kn2, vn2, lf_row, lf_col, bias, cache_k, cache_v):
    b, n_rows, _ = q2.shape
    n_pages = page_table.shape[0] // b
    cur = lambda bi, pt: (bi, 0, 0)
    page_shape = cache_k.shape[1:]
    grid_spec = pltpu.PrefetchScalarGridSpec(
        num_scalar_prefetch=1,
        grid=(b,),
        in_specs=[pl.BlockSpec((1, n_rows, FOX_HEAD_DIM), cur),
                  pl.BlockSpec((1, n_rows, FOX_HEAD_DIM), cur),
                  pl.BlockSpec((1, n_rows, FOX_HEAD_DIM), cur),
                  pl.BlockSpec((1, 1, LANES), cur),
                  pl.BlockSpec((1, n_rows, 1), cur),
                  pl.BlockSpec((1, n_pages, PAGE_SIZE * FOX_HEADS), cur),
                  pl.BlockSpec(memory_space=pl.ANY),
                  pl.BlockSpec(memory_space=pl.ANY)],
        out_specs=pl.BlockSpec((1, n_rows, FOX_HEAD_DIM), cur),
        scratch_shapes=[pltpu.VMEM((2, n_pages) + page_shape, F32),
                        pltpu.VMEM((2, n_pages) + page_shape, F32),
                        pltpu.SemaphoreType.DMA((2, 2)),
                        pltpu.VMEM((n_rows, PAGE_SIZE * FOX_HEADS), F32),
                        pltpu.VMEM((n_rows, 1), F32),
                        pltpu.VMEM((n_rows, 1), F32),
                        pltpu.VMEM((n_rows, FOX_HEAD_DIM), F32)])
    return pl.pallas_call(
        functools.partial(_fox_sample_kernel, n_pages=n_pages),
        out_shape=jax.ShapeDtypeStruct((b, n_rows, FOX_HEAD_DIM), F32),
        grid_spec=grid_spec,
        compiler_params=pltpu.CompilerParams(dimension_semantics=("arbitrary",),
                                             vmem_limit_bytes=VMEM_LIMIT),
        name="fox_sample",
    )(page_table, q2, kn2, vn2, lf_row, lf_col, bias, cache_k, cache_v)


def _lane_suffix_sum(x):
    n = x.shape[-1]
    lane = lax.broadcasted_iota(jnp.int32, x.shape, 1)
    k = 1
    while k < n:
        x = x + jnp.where(lane + k < n, pltpu.roll(x, n - k, axis=1), 0.0)
        k *= 2
    return x


def _fox_paged_kernel(pt_ref, q_ref, kn_ref, vn_ref, lfn_ref, logf_ref, k_hbm, v_hbm, o_ref,
                      kbuf, vbuf, sem, pad_k_ref, pad_v_ref, *, n_pages, n_batch):
    b = pl.program_id(0)
    slot = lax.rem(b, 2)
    n_q = q_ref.shape[1]
    n_rows = n_q * FOX_HEADS

    def page_copies(batch, slot_):
        copies = []
        for j in range(n_pages):
            page = pt_ref[j * n_batch + batch]
            copies.append(pltpu.make_async_copy(k_hbm.at[page], kbuf.at[slot_, j], sem.at[0, slot_]))
            copies.append(pltpu.make_async_copy(v_hbm.at[page], vbuf.at[slot_, j], sem.at[1, slot_]))
        return copies

    @pl.when(b == 0)
    def _():
        for c in page_copies(0, 0):
            c.start()

    @pl.when(b + 1 < pl.num_programs(0))
    def _():
        for c in page_copies(b + 1, 1 - slot):
            c.start()

    head = lax.broadcasted_iota(jnp.int32, (FOX_HEADS, TOK_WIDTH), 0)
    lane_head = jnp.right_shift(lax.broadcasted_iota(jnp.int32, (FOX_HEADS, TOK_WIDTH), 1), 6)
    own_head = head == lane_head
    tile_q = lambda x: jnp.concatenate([x] * n_q, axis=0)

    q = q_ref[0]
    qblk = jnp.concatenate(
        [jnp.where(own_head, jnp.broadcast_to(q[t:t + 1, :], (FOX_HEADS, TOK_WIDTH)), 0.0)
         for t in range(n_q)], axis=0).astype(BF16)

    pad_k_ref[...] = jnp.zeros_like(pad_k_ref)
    pad_v_ref[...] = jnp.zeros_like(pad_v_ref)
    pad_k_ref[0:n_q, :] = kn_ref[0]
    pad_v_ref[0:n_q, :] = vn_ref[0]
    c_new = _lane_cumsum(lfn_ref[0])
    cq = jnp.concatenate([c_new[:, t:t + 1] for t in range(n_q)], axis=0)
    s_own = lax.dot_general(qblk, pad_k_ref[...].astype(BF16), NT_DIMS, preferred_element_type=F32)
    t_of_row = jnp.right_shift(lax.broadcasted_iota(jnp.int32, (n_rows, PAGE_SIZE), 0), 3)
    key = lax.broadcasted_iota(jnp.int32, (n_rows, PAGE_SIZE), 1)
    s_own = jnp.where(key <= t_of_row, s_own + (tile_q(-c_new) + cq), -jnp.inf)

    logf = [logf_ref[pt_ref[j * n_batch + b]] for j in range(n_pages)]
    incl = _lane_suffix_sum(jnp.concatenate(logf, axis=0))
    later_pages = jnp.zeros((FOX_HEADS, 1), F32)
    bias = [None] * n_pages
    for j in reversed(range(n_pages)):
        rows = slice(j * FOX_HEADS, (j + 1) * FOX_HEADS)
        bias[j] = tile_q(later_pages + (incl[rows] - logf[j])) + cq
        later_pages = later_pages + incl[rows, 0:1]

    for c in page_copies(b, slot):
        c.wait()

    scores = [s_own]
    for j in range(n_pages):
        k_t = kbuf[slot, j].reshape(TOK_WIDTH, PAGE_SIZE).astype(BF16)
        scores.append(jnp.dot(qblk, k_t, preferred_element_type=F32) + bias[j])
    s_max = scores[0]
    for s in scores[1:]:
        s_max = jnp.maximum(s_max, s)
    m = jnp.max(s_max, axis=-1, keepdims=True)
    p = jnp.exp(scores[0] - m)
    p_sum = p
    acc = jnp.dot(p.astype(BF16), pad_v_ref[...].astype(BF16), preferred_element_type=F32)
    for j in range(n_pages):
        p = jnp.exp(scores[j + 1] - m)
        p_sum = p_sum + p
        v_t = vbuf[slot, j].reshape(TOK_WIDTH, PAGE_SIZE).astype(BF16)
        acc = acc + lax.dot_general(p.astype(BF16), v_t, NT_DIMS, preferred_element_type=F32)
    o = acc / jnp.sum(p_sum, axis=-1, keepdims=True)
    for t in range(n_q):
        o_t = jnp.where(own_head, o[t * FOX_HEADS:(t + 1) * FOX_HEADS, :], 0.0)
        o_ref[0, t:t + 1, :] = jnp.sum(o_t, axis=0, keepdims=True)


def _fox_paged(page_table, q, k_new, v_new, lfn, cache_logf_t, cache_k_t, cache_v_t):
    b, n_q, _ = q.shape
    n_pages = page_table.shape[0] // b
    cur = lambda bi, pt: (bi, 0, 0)
    page_shape = cache_k_t.shape[1:]
    grid_spec = pltpu.PrefetchScalarGridSpec(
        num_scalar_prefetch=1,
        grid=(b,),
        in_specs=[pl.BlockSpec((1, n_q, TOK_WIDTH), cur),
                  pl.BlockSpec((1, n_q, TOK_WIDTH), cur),
                  pl.BlockSpec((1, n_q, TOK_WIDTH), cur),
                  pl.BlockSpec((1, FOX_HEADS, LANES), cur),
                  pl.BlockSpec(cache_logf_t.shape, lambda bi, pt: (0, 0, 0)),
                  pl.BlockSpec(memory_space=pl.ANY),
                  pl.BlockSpec(memory_space=pl.ANY)],
        out_specs=pl.BlockSpec((1, n_q, TOK_WIDTH), cur),
        scratch_shapes=[pltpu.VMEM((2, n_pages) + page_shape, F32),
                        pltpu.VMEM((2, n_pages) + page_shape, F32),
                        pltpu.SemaphoreType.DMA((2, 2)),
                        pltpu.VMEM((PAGE_SIZE, TOK_WIDTH), F32),
                        pltpu.VMEM((PAGE_SIZE, TOK_WIDTH), F32)])
    return pl.pallas_call(
        functools.partial(_fox_paged_kernel, n_pages=n_pages, n_batch=b),
        out_shape=jax.ShapeDtypeStruct((b, n_q, TOK_WIDTH), F32),
        grid_spec=grid_spec,
        compiler_params=pltpu.CompilerParams(dimension_semantics=("arbitrary",),
                                             vmem_limit_bytes=VMEM_LIMIT),
        name="fox_paged",
    )(page_table, q, k_new, v_new, lfn, cache_logf_t, cache_k_t, cache_v_t)


def _decoder(x, pool_prev, n_prev, mem_k, mem_v, past, wts, *, tm, pool_bb, pool_chunk, mem_bb, mem_ts,
             act):
    b, s_len, _ = x.shape
    t = b * s_len
    h = x.reshape(t, D_MODEL)

    z_tok, qm = _proj_in(h, wts["g_mix"][0], wts["w_in"][0], wts["g_fox_q_t"], wts["g_mem_q_t"][0],
                         fox=False, tm=tm, act=act)
    z_tok = z_tok.reshape(b, s_len, TOK_WIDTH)
    tok = _pool_mix(z_tok, pool_prev, wts["w_pool"], wts["pool_scale"],
                    n_prev=n_prev, bb=pool_bb, chunk=pool_chunk, act=act)
    mo = _mem_attn(qm.reshape(b, s_len, MEM_WIDTH), mem_k, mem_v, 0, bb=mem_bb, ts=mem_ts)
    h = _out_ffn(h, tok.reshape(t, TOK_WIDTH), mo.reshape(t, MEM_WIDTH), wts["w_out"][0],
                 wts["g_ffn"][0], wts["w_gu"][0], wts["w_down"][0], tm=tm, tf=D_FF // 2)

    kv_args = (h, wts["g_kv"], wts["w_kv"], wts["g_fox_k_t"], wts["w_fg_t"], wts["b_fg"])
    if past is None:
        k_new, v_new, k_bf, v_bf, logf_t, logf_new = _proj_kv_paged(*kv_args, tm=tm)
    else:
        k_new, v_new, k_bf, v_bf, logf_t = _proj_kv(*kv_args, tm=tm)
        logf_new = logf_t.T

    q, qm = _proj_in(h, wts["g_mix"][1], wts["w_in"][1], wts["g_fox_q_t"], wts["g_mem_q_t"][1],
                     fox=True, tm=tm, act=act)
    if past is None:
        f_row = _cumsum_rows(logf_t, seg=s_len)
        n_pairs = FOX_HEADS // 2
        f_row = f_row.reshape(n_pairs, 2, t)
        tok = _fox_prompt(q.reshape(b, s_len, TOK_WIDTH), k_bf.reshape(b, s_len, TOK_WIDTH),
                          v_bf.reshape(b, s_len, TOK_WIDTH), f_row, blk=512)
    else:
        page_table, cache_k_t, cache_v_t, cache_logf_t = past
        lfn = logf_t.reshape(FOX_HEADS, b, s_len).transpose(1, 0, 2)
        lfn = jnp.pad(lfn, ((0, 0), (0, 0), (0, LANES - s_len)))
        per_row = lambda a: a.reshape(b, s_len, TOK_WIDTH)
        tok = _fox_paged(page_table, per_row(q), per_row(k_new), per_row(v_new), lfn,
                         cache_logf_t, cache_k_t, cache_v_t)
    mo = _mem_attn(qm.reshape(b, s_len, MEM_WIDTH), mem_k, mem_v, 1, bb=mem_bb, ts=mem_ts)
    h = _out_ffn(h, tok.reshape(t, TOK_WIDTH), mo.reshape(t, MEM_WIDTH), wts["w_out"][1],
                 wts["g_ffn"][1], wts["w_gu"][1], wts["w_down"][1], tm=tm, tf=D_FF // 2)
    return h.reshape(b, s_len, D_MODEL), z_tok, k_new, v_new, logf_new


def kernel(x_prompt, x_sample, cache_mem_k, cache_mem_v, state_pool, cache_k, cache_v, cache_logf,
           page_table, mem_prompt, g_mix, w_in, w_out, g_ffn, w_gu, w_down, g_mem, w_mem_kv,
           g_mem_q, g_mem_k, w_pool, pool_scale, g_kv, w_kv, g_fox_k, w_fg, b_fg, g_fox_q):
    depth = w_in.shape[0]
    b, s_len, _ = x_prompt.shape
    db, ds, _ = x_sample.shape
    wts = {
        "g_mix": g_mix.reshape(depth, 1, D_MODEL),
        "w_in": w_in.astype(BF16),
        "w_out": w_out.astype(BF16),
        "g_ffn": g_ffn.reshape(depth, 1, D_MODEL),
        "w_gu": w_gu.astype(BF16),
        "w_down": w_down.astype(BF16),
        "g_mem_q_t": jnp.tile(g_mem_q, (1, MEM_HEADS)).reshape(depth, 1, MEM_WIDTH),
        "w_pool": w_pool[0].astype(BF16),
        "pool_scale": pool_scale[0].reshape(1, TOK_WIDTH),
        "g_kv": g_kv.reshape(1, D_MODEL),
        "w_kv": w_kv.astype(BF16),
        "g_fox_k_t": jnp.tile(g_fox_k, FOX_HEADS).reshape(1, TOK_WIDTH),
        "w_fg_t": w_fg.T.astype(BF16),
        "b_fg": b_fg.reshape(FOX_HEADS, 1),
        "g_fox_q_t": jnp.tile(g_fox_q[0], FOX_HEADS).reshape(1, TOK_WIDTH),
    }

    mem_k_p, mem_v_p = _proj_memkv(mem_prompt.reshape(b * N_MEM, D_MODEL),
                                   g_mem.reshape(depth, 1, D_MODEL), w_mem_kv.astype(BF16),
                                   jnp.tile(g_mem_k, (1, MEM_HEADS)).reshape(depth, 1, MEM_WIDTH), tm=512)
    mem_rows = (N_MEM * MEM_HEADS, MEM_HEAD_DIM)
    mem_k_p = mem_k_p.reshape((depth, b) + mem_rows)
    mem_v_p = mem_v_p.reshape((depth, b) + mem_rows)
    y_p, ztok_p, k_p, v_p, logf_p = _decoder(
        x_prompt, jnp.zeros((b, POOL_HIST, TOK_WIDTH), F32), 0, mem_k_p, mem_v_p, None, wts,
        tm=512, pool_bb=1, pool_chunk=256, mem_bb=1, mem_ts=512, act=BF16)
    n_pp = s_len // PAGE_SIZE
    head_shape = (FOX_HEADS, FOX_HEAD_DIM)

    past = (page_table.T.reshape(-1),
            cache_k.transpose(0, 2, 3, 1), cache_v.transpose(0, 2, 3, 1),
            cache_logf.transpose(0, 2, 1))
    prev = jnp.pad(state_pool[0], ((0, 0), (POOL_HIST - POOL_STATE, 0), (0, 0)))
    y_s, ztok_s, k_s, v_s, logf_s = _decoder(
        x_sample, prev, POOL_STATE, cache_mem_k.reshape((depth, db) + mem_rows),
        cache_mem_v.reshape((depth, db) + mem_rows), past, wts,
        tm=512, pool_bb=32, pool_chunk=ds, mem_bb=8, mem_ts=ds, act=F32)

    pool_state_p = ztok_p[:, s_len - POOL_STATE:][None]
    pool_state_s = jnp.concatenate([state_pool[0], ztok_s], axis=1)[:, -POOL_STATE:][None]
    paged = lambda a: a.reshape((b, n_pp) + head_shape + (PAGE_SIZE,)).transpose(0, 1, 4, 2, 3)
    return (y_p, y_s, paged(k_p), paged(v_p),
            logf_p.reshape(b, n_pp, FOX_HEADS, PAGE_SIZE).transpose(0, 1, 3, 2),
            mem_k_p.reshape(depth, b, N_MEM, MEM_HEADS, MEM_HEAD_DIM),
            mem_v_p.reshape(depth, b, N_MEM, MEM_HEADS, MEM_HEAD_DIM),
            pool_state_p,
            k_s.reshape((db, ds) + head_shape),
            v_s.reshape((db, ds) + head_shape),
            logf_s.reshape(db, ds, FOX_HEADS),
            pool_state_s)
```

```python
import functools

import jax
import jax.numpy as jnp
from jax import lax
from jax.experimental import pallas as pl
from jax.experimental.pallas import tpu as pltpu

D_MODEL = 1024
TOK_WIDTH = 512
MEM_WIDTH = 512
POOL_WINDOWS = (2, 4, 8, 16)
POOL_GROUP = 128
POOL_STATE = 15
POOL_HIST = 16
FOX_HEADS = 8
FOX_HEAD_DIM = 64
MEM_HEADS = 4
MEM_HEAD_DIM = 128
N_MEM = 256
D_FF = 2816
PAGE_SIZE = 128
EPS = 1e-6
LANES = 128
BF16_ROWS = 16
VMEM_LIMIT = 56 * 1024 * 1024
FOX_CHUNK_ROWS = 64
PAGE_SLOTS = 3

F32 = jnp.float32
BF16 = jnp.bfloat16
NT_DIMS = (((1,), (1,)), ((), ()))


def _rmsnorm(x, g):
    return x * lax.rsqrt(jnp.mean(x * x, axis=-1, keepdims=True) + EPS) * g


def _headnorm128(z, g):
    outs = []
    for h in range(z.shape[-1] // LANES):
        zh = z[:, h * LANES:(h + 1) * LANES]
        r = lax.rsqrt(jnp.mean(zh * zh, axis=-1, keepdims=True) + EPS)
        outs.append(zh * r * g[:, h * LANES:(h + 1) * LANES])
    return jnp.concatenate(outs, axis=-1)


def _headnorm64(z, g):
    outs = []
    for p in range(z.shape[-1] // LANES):
        zp = z[:, p * LANES:(p + 1) * LANES]
        sq = zp * zp
        lane = lax.broadcasted_iota(jnp.int32, zp.shape, 1)
        lo = lane < FOX_HEAD_DIM
        s_lo = jnp.sum(jnp.where(lo, sq, 0.0), axis=-1, keepdims=True)
        s_hi = jnp.sum(jnp.where(lo, 0.0, sq), axis=-1, keepdims=True)
        r = jnp.where(lo, lax.rsqrt(s_lo / FOX_HEAD_DIM + EPS), lax.rsqrt(s_hi / FOX_HEAD_DIM + EPS))
        outs.append(zp * r * g[:, p * LANES:(p + 1) * LANES])
    return jnp.concatenate(outs, axis=-1)


def _proj_in_kernel(x_ref, g_ref, w_ref, gtok_ref, gmem_ref, tok_ref, qm_ref, *, fox):
    xn = _rmsnorm(x_ref[...], g_ref[...]).astype(BF16)
    z = jnp.dot(xn, w_ref[...], preferred_element_type=F32)
    z_tok = z[:, :TOK_WIDTH]
    if fox:
        tok_ref[...] = (_headnorm64(z_tok, gtok_ref[...]) * (FOX_HEAD_DIM ** -0.5)).astype(tok_ref.dtype)
    else:
        tok_ref[...] = z_tok
    qm_ref[...] = _headnorm128(z[:, TOK_WIDTH:], gmem_ref[...]).astype(qm_ref.dtype)


def _proj_in(x, g, w, g_tok, g_mem, *, fox, tm, act):
    t = x.shape[0]
    row = lambda i: (i, 0)
    fixed = lambda i: (0, 0)
    return pl.pallas_call(
        functools.partial(_proj_in_kernel, fox=fox),
        out_shape=(jax.ShapeDtypeStruct((t, TOK_WIDTH), act if fox else F32),
                   jax.ShapeDtypeStruct((t, MEM_WIDTH), act)),
        grid=(t // tm,),
        in_specs=[pl.BlockSpec((tm, D_MODEL), row),
                  pl.BlockSpec((1, D_MODEL), fixed),
                  pl.BlockSpec((D_MODEL, D_MODEL), fixed),
                  pl.BlockSpec((1, TOK_WIDTH), fixed),
                  pl.BlockSpec((1, MEM_WIDTH), fixed)],
        out_specs=(pl.BlockSpec((tm, TOK_WIDTH), row), pl.BlockSpec((tm, MEM_WIDTH), row)),
        compiler_params=pltpu.CompilerParams(dimension_semantics=("arbitrary",),
                                             vmem_limit_bytes=VMEM_LIMIT),
        name="proj_in_fox" if fox else "proj_in_pool",
    )(x, g, w, g_tok, g_mem)


def _log_sigmoid(x):
    return jnp.minimum(x, 0.0) - jnp.log(1.0 + jnp.exp(-jnp.abs(x)))


def _proj_kv_kernel(x_ref, g_ref, w_ref, gk_ref, wfg_ref, bfg_ref, k_ref, v_ref, logf_ref):
    xn = _rmsnorm(x_ref[...], g_ref[...]).astype(BF16)
    z = jnp.dot(xn, w_ref[...], preferred_element_type=F32)
    k_ref[...] = _headnorm64(z[:, :TOK_WIDTH], gk_ref[...])
    v_ref[...] = z[:, TOK_WIDTH:]
    gate = lax.dot_general(wfg_ref[...], xn, NT_DIMS, preferred_element_type=F32) + bfg_ref[...]
    logf_ref[...] = _log_sigmoid(gate)


def _proj_kv_paged_kernel(x_ref, g_ref, w_ref, gk_ref, wfg_ref, bfg_ref,
                          kp_ref, vp_ref, kb_ref, vb_ref, logf_ref, logfp_ref):
    xn = _rmsnorm(x_ref[...], g_ref[...]).astype(BF16)
    z = jnp.dot(xn, w_ref[...], preferred_element_type=F32)
    k = _headnorm64(z[:, :TOK_WIDTH], gk_ref[...])
    v = z[:, TOK_WIDTH:]
    vb_ref[...] = v.astype(BF16)
    gate = lax.dot_general(wfg_ref[...], xn, NT_DIMS, preferred_element_type=F32) + bfg_ref[...]
    logf = _log_sigmoid(gate)
    logf_ref[...] = logf
    k_t = k.T
    v_t = v.T
    kb_ref[...] = k_t.astype(BF16)
    for pg in range(kp_ref.shape[0]):
        rows = slice(pg * PAGE_SIZE, (pg + 1) * PAGE_SIZE)
        kp_ref[pg] = k_t[:, rows]
        vp_ref[pg] = v_t[:, rows]
        logfp_ref[pg] = logf[:, rows]


def _proj_kv_paged(x, g, w, g_k, wfg_t, bfg, *, tm):
    t = x.shape[0]
    row = lambda i: (i, 0)
    fixed = lambda i: (0, 0)
    ppt = tm // PAGE_SIZE
    page_rows = lambda i: (i, 0, 0)
    return pl.pallas_call(
        _proj_kv_paged_kernel,
        out_shape=(jax.ShapeDtypeStruct((t // PAGE_SIZE, TOK_WIDTH, PAGE_SIZE), F32),
                   jax.ShapeDtypeStruct((t // PAGE_SIZE, TOK_WIDTH, PAGE_SIZE), F32),
                   jax.ShapeDtypeStruct((TOK_WIDTH, t), BF16),
                   jax.ShapeDtypeStruct((t, TOK_WIDTH), BF16),
                   jax.ShapeDtypeStruct((FOX_HEADS, t), F32),
                   jax.ShapeDtypeStruct((t // PAGE_SIZE, FOX_HEADS, PAGE_SIZE), F32)),
        grid=(t // tm,),
        in_specs=[pl.BlockSpec((tm, D_MODEL), row),
                  pl.BlockSpec((1, D_MODEL), fixed),
                  pl.BlockSpec((D_MODEL, D_MODEL), fixed),
                  pl.BlockSpec((1, TOK_WIDTH), fixed),
                  pl.BlockSpec((FOX_HEADS, D_MODEL), fixed),
                  pl.BlockSpec((FOX_HEADS, 1), fixed)],
        out_specs=(pl.BlockSpec((ppt, TOK_WIDTH, PAGE_SIZE), page_rows),
                   pl.BlockSpec((ppt, TOK_WIDTH, PAGE_SIZE), page_rows),
                   pl.BlockSpec((TOK_WIDTH, tm), lambda i: (0, i)), pl.BlockSpec((tm, TOK_WIDTH), row),
                   pl.BlockSpec((FOX_HEADS, tm), lambda i: (0, i)),
                   pl.BlockSpec((ppt, FOX_HEADS, PAGE_SIZE), page_rows)),
        compiler_params=pltpu.CompilerParams(dimension_semantics=("arbitrary",),
                                             vmem_limit_bytes=VMEM_LIMIT),
        name="proj_kv_paged",
    )(x, g, w, g_k, wfg_t, bfg)


def _proj_kv(x, g, w, g_k, wfg_t, bfg, *, tm):
    t = x.shape[0]
    row = lambda i: (i, 0)
    fixed = lambda i: (0, 0)
    return pl.pallas_call(
        _proj_kv_kernel,
        out_shape=(jax.ShapeDtypeStruct((t, TOK_WIDTH), F32),
                   jax.ShapeDtypeStruct((t, TOK_WIDTH), F32),
                   jax.ShapeDtypeStruct((FOX_HEADS, t), F32)),
        grid=(t // tm,),
        in_specs=[pl.BlockSpec((tm, D_MODEL), row),
                  pl.BlockSpec((1, D_MODEL), fixed),
                  pl.BlockSpec((D_MODEL, D_MODEL), fixed),
                  pl.BlockSpec((1, TOK_WIDTH), fixed),
                  pl.BlockSpec((FOX_HEADS, D_MODEL), fixed),
                  pl.BlockSpec((FOX_HEADS, 1), fixed)],
        out_specs=(pl.BlockSpec((tm, TOK_WIDTH), row), pl.BlockSpec((tm, TOK_WIDTH), row),
                   pl.BlockSpec((FOX_HEADS, tm), lambda i: (0, i))),
        compiler_params=pltpu.CompilerParams(dimension_semantics=("arbitrary",),
                                             vmem_limit_bytes=VMEM_LIMIT),
        name="proj_kv",
    )(x, g, w, g_k, wfg_t, bfg)


def _proj_memkv_kernel(x_ref, g_ref, w_ref, gk_ref, k_ref, v_ref):
    xn = _rmsnorm(x_ref[...], g_ref[0]).astype(BF16)
    z = jnp.dot(xn, w_ref[0], preferred_element_type=F32)
    k = _headnorm128(z[:, :MEM_WIDTH], gk_ref[0])
    tm = x_ref.shape[0]
    for h in range(MEM_HEADS):
        rows = pl.ds(h, tm, stride=MEM_HEADS)
        k_ref[0, rows, :] = k[:, h * MEM_HEAD_DIM:(h + 1) * MEM_HEAD_DIM]
        v_ref[0, rows, :] = z[:, MEM_WIDTH + h * MEM_HEAD_DIM:MEM_WIDTH + (h + 1) * MEM_HEAD_DIM]


def _proj_memkv(x, g, w, g_k, *, tm):
    t = x.shape[0]
    n_layers = w.shape[0]
    out_block = pl.BlockSpec((1, tm * MEM_HEADS, MEM_HEAD_DIM), lambda l, i: (l, i, 0))
    return pl.pallas_call(
        _proj_memkv_kernel,
        out_shape=(jax.ShapeDtypeStruct((n_layers, t * MEM_HEADS, MEM_HEAD_DIM), F32),
                   jax.ShapeDtypeStruct((n_layers, t * MEM_HEADS, MEM_HEAD_DIM), F32)),
        grid=(n_layers, t // tm),
        in_specs=[pl.BlockSpec((tm, D_MODEL), lambda l, i: (i, 0)),
                  pl.BlockSpec((1, 1, D_MODEL), lambda l, i: (l, 0, 0)),
                  pl.BlockSpec((1, D_MODEL, D_MODEL), lambda l, i: (l, 0, 0)),
                  pl.BlockSpec((1, 1, MEM_WIDTH), lambda l, i: (l, 0, 0))],
        out_specs=(out_block, out_block),
        compiler_params=pltpu.CompilerParams(dimension_semantics=("arbitrary", "arbitrary"),
                                             vmem_limit_bytes=VMEM_LIMIT),
        name="proj_memkv",
    )(x, g, w, g_k)


def _pool_kernel(u_ref, prev_ref, w_ref, scale_ref, o_ref, ext_ref, *, n_prev, chunk):
    bb, s_len, _ = u_ref.shape
    ext_ref[:, 0:POOL_HIST, :] = prev_ref[...]
    ext_ref[:, POOL_HIST:, :] = u_ref[...]
    for c in range(s_len // chunk):
        r0 = POOL_HIST + c * chunk
        pos = c * chunk + lax.broadcasted_iota(jnp.int32, (1, chunk, 1), 1)
        for gi, win in enumerate(POOL_WINDOWS):
            lanes = slice(gi * POOL_GROUP, (gi + 1) * POOL_GROUP)
            u_new = ext_ref[:, r0:r0 + chunk, lanes]
            acc = u_new
            for k in range(1, win):
                acc = acc + ext_ref[:, r0 - k:r0 - k + chunk, lanes]
            count = jnp.minimum(win, n_prev + pos + 1).astype(F32)
            y = acc / count - u_new
            z = jnp.dot(y.reshape(bb * chunk, POOL_GROUP).astype(BF16), w_ref[gi],
                        preferred_element_type=F32)
            z = z.reshape(bb, chunk, POOL_GROUP) * scale_ref[:, lanes]
            o_ref[:, c * chunk:(c + 1) * chunk, lanes] = z.astype(o_ref.dtype)


def _pool_mix(u, prev, w_pool, scale, *, n_prev, bb, chunk, act):
    b, s_len, _ = u.shape
    return pl.pallas_call(
        functools.partial(_pool_kernel, n_prev=n_prev, chunk=chunk),
        out_shape=jax.ShapeDtypeStruct((b, s_len, TOK_WIDTH), act),
        grid=(b // bb,),
        in_specs=[pl.BlockSpec((bb, s_len, TOK_WIDTH), lambda i: (i, 0, 0)),
                  pl.BlockSpec((bb, POOL_HIST, TOK_WIDTH), lambda i: (i, 0, 0)),
                  pl.BlockSpec((len(POOL_WINDOWS), POOL_GROUP, POOL_GROUP), lambda i: (0, 0, 0)),
                  pl.BlockSpec((1, TOK_WIDTH), lambda i: (0, 0))],
        out_specs=pl.BlockSpec((bb, s_len, TOK_WIDTH), lambda i: (i, 0, 0)),
        scratch_shapes=[pltpu.VMEM((bb, POOL_HIST + s_len, TOK_WIDTH), F32)],
        compiler_params=pltpu.CompilerParams(dimension_semantics=("arbitrary",),
                                             vmem_limit_bytes=VMEM_LIMIT),
        name="pool_mix",
    )(u, prev, w_pool, scale)


def _mem_attn_kernel(q_ref, k_ref, v_ref, o_ref):
    for h in range(MEM_HEADS):
        lanes = slice(h * MEM_HEAD_DIM, (h + 1) * MEM_HEAD_DIM)
        q = q_ref[:, :, lanes].astype(BF16)
        k = k_ref[:, pl.ds(h, N_MEM, stride=MEM_HEADS), :].astype(BF16)
        v = v_ref[:, pl.ds(h, N_MEM, stride=MEM_HEADS), :].astype(BF16)
        s = jnp.einsum("bqd,bkd->bqk", q, k, preferred_element_type=F32) * (MEM_HEAD_DIM ** -0.5)
        m = jnp.max(s, axis=-1, keepdims=True)
        p = jnp.exp(s - m)
        l = jnp.sum(p, axis=-1, keepdims=True)
        o = jnp.einsum("bqk,bkd->bqd", p.astype(BF16), v, preferred_element_type=F32)
        o_ref[:, :, lanes] = (o / l).astype(o_ref.dtype)


def _mem_attn(q, mk, mv, layer, *, bb, ts):
    b, s_len, _ = q.shape
    mem_block = pl.BlockSpec((None, bb, N_MEM * MEM_HEADS, MEM_HEAD_DIM),
                             lambda i, j: (layer, i, 0, 0))
    return pl.pallas_call(
        _mem_attn_kernel,
        out_shape=jax.ShapeDtypeStruct((b, s_len, MEM_WIDTH), q.dtype),
        grid=(b // bb, s_len // ts),
        in_specs=[pl.BlockSpec((bb, ts, MEM_WIDTH), lambda i, j: (i, j, 0)), mem_block, mem_block],
        out_specs=pl.BlockSpec((bb, ts, MEM_WIDTH), lambda i, j: (i, j, 0)),
        compiler_params=pltpu.CompilerParams(dimension_semantics=("arbitrary", "arbitrary"),
                                             vmem_limit_bytes=VMEM_LIMIT),
        name="mem_attn",
    )(q, mk, mv)


def _out_ffn_kernel(h_ref, tok_ref, mo_ref, wout_ref, g_ref, wg_ref, wu_ref, wd_ref,
                    o_ref, xn_ref, acc_ref):
    kf = pl.program_id(1)

    @pl.when(kf == 0)
    def _():
        h1 = (h_ref[...]
              + jnp.dot(tok_ref[...].astype(BF16), wout_ref[:TOK_WIDTH, :], preferred_element_type=F32)
              + jnp.dot(mo_ref[...].astype(BF16), wout_ref[TOK_WIDTH:, :], preferred_element_type=F32))
        acc_ref[...] = h1
        xn_ref[...] = _rmsnorm(h1, g_ref[...]).astype(BF16)

    x = xn_ref[...]
    gate = jnp.dot(x, wg_ref[...], preferred_element_type=F32)
    up = jnp.dot(x, wu_ref[...], preferred_element_type=F32)
    act = (gate * (1.0 / (1.0 + jnp.exp(-gate))) * up).astype(BF16)
    acc_ref[...] += jnp.dot(act, wd_ref[...], preferred_element_type=F32)

    @pl.when(kf == pl.num_programs(1) - 1)
    def _():
        o_ref[...] = acc_ref[...]


def _out_ffn(h, tok, mo, w_out, g, w_gu, w_down, *, tm, tf):
    t = h.shape[0]
    nf = D_FF // tf
    row = lambda i, k: (i, 0)
    fixed = lambda i, k: (0, 0)
    return pl.pallas_call(
        _out_ffn_kernel,
        out_shape=jax.ShapeDtypeStruct((t, D_MODEL), F32),
        grid=(t // tm, nf),
        in_specs=[pl.BlockSpec((tm, D_MODEL), row),
                  pl.BlockSpec((tm, TOK_WIDTH), row),
                  pl.BlockSpec((tm, MEM_WIDTH), row),
                  pl.BlockSpec((D_MODEL, D_MODEL), fixed),
                  pl.BlockSpec((1, D_MODEL), fixed),
                  pl.BlockSpec((D_MODEL, tf), lambda i, k: (0, k)),
                  pl.BlockSpec((D_MODEL, tf), lambda i, k: (0, nf + k)),
                  pl.BlockSpec((tf, D_MODEL), lambda i, k: (k, 0))],
        out_specs=pl.BlockSpec((tm, D_MODEL), row),
        scratch_shapes=[pltpu.VMEM((tm, D_MODEL), BF16), pltpu.VMEM((tm, D_MODEL), F32)],
        compiler_params=pltpu.CompilerParams(dimension_semantics=("arbitrary", "arbitrary"),
                                             vmem_limit_bytes=VMEM_LIMIT),
        name="out_ffn",
    )(h, tok, mo, w_out, g, w_gu, w_gu, w_down)


def _lane_cumsum(x):
    n = x.shape[-1]
    lane = lax.broadcasted_iota(jnp.int32, x.shape, 1)
    k = 1
    while k < n:
        x = x + jnp.where(lane >= k, pltpu.roll(x, k, axis=1), 0.0)
        k *= 2
    return x


def _cumsum_kernel(x_ref, o_ref):
    o_ref[...] = _lane_cumsum(x_ref[...])


def _cumsum_rows(x, *, seg):
    rows, t = x.shape
    return pl.pallas_call(
        _cumsum_kernel,
        out_shape=jax.ShapeDtypeStruct((rows, t), F32),
        grid=(t // seg,),
        in_specs=[pl.BlockSpec((rows, seg), lambda i: (0, i))],
        out_specs=pl.BlockSpec((rows, seg), lambda i: (0, i)),
        compiler_params=pltpu.CompilerParams(dimension_semantics=("arbitrary",)),
        name="logf_cumsum",
    )(x)


def _round_to_bf16(x):
    return x.astype(BF16).astype(F32)


def _fox_prompt_kernel(q_ref, k_ref, v_ref, frow_ref, o_ref, m_ref, l_ref, acc_ref, alpha_ref,
                       s_ref, p_all_ref, *, blk, n_q_blocks):
    qi = pl.program_id(2)
    q = q_ref[0]
    lane = lax.broadcasted_iota(jnp.int32, q.shape, 1)
    sub = FOX_CHUNK_ROWS
    n_rep = blk // LANES
    bias_lane = [FOX_HEAD_DIM, 0]
    q_f32 = q.astype(F32)
    q_head = []
    for j in range(2):
        own = (lane < FOX_HEAD_DIM) if j == 0 else (lane >= FOX_HEAD_DIM)
        ones_from = jnp.where(lane >= bias_lane[j], 1.0, 0.0)
        bias_ones = jnp.where(lane < bias_lane[j] + 3, ones_from, 0.0)
        q_head.append(jnp.where(own, q_f32, bias_ones).astype(BF16))
    m_ref[...] = jnp.full(m_ref.shape, -jnp.inf, F32)
    l_ref[...] = jnp.zeros_like(l_ref)
    acc_ref[...] = jnp.zeros_like(acc_ref)

    def scores(qc, ki, s_buf):
        keys = slice(ki * blk, (ki + 1) * blk)
        kt = k_ref[:, keys]
        tile_row = lax.broadcasted_iota(jnp.int32, (BF16_ROWS, blk), 0)
        for j in range(2):
            f_end = frow_ref[0, j:j + 1, (qc + 1) * blk - 1:(qc + 1) * blk]
            bias = f_end - frow_ref[0, j:j + 1, keys]
            hi = _round_to_bf16(bias)
            mid = _round_to_bf16(bias - hi)
            lo = _round_to_bf16(bias - hi - mid)
            bias_rows = jnp.where(tile_row == 0, hi, jnp.where(tile_row == 1, mid,
                                                               jnp.where(tile_row == 2, lo, 0.0)))
            b0 = bias_lane[j]
            pieces = [kt[:b0], bias_rows.astype(BF16), kt[b0 + BF16_ROWS:]]
            kt_j = jnp.concatenate([x for x in pieces if x.shape[0]], axis=0)
            s_buf[j] = jnp.dot(q_head[j], kt_j, preferred_element_type=F32)

    def update(ki, s_buf, masked):
        p_ref = p_all_ref.at[ki]
        vt = v_ref[0, ki * blk:(ki + 1) * blk, :]
        for j in range(2):
            def chunk(r):
                n_use = -(-(r + 1) * sub // LANES) if masked else n_rep
                rows = slice(r * sub, (r + 1) * sub)
                s = s_buf[j, rows, 0:n_use * LANES]
                if masked:
                    row = lax.broadcasted_iota(jnp.int32, s.shape, 0) + r * sub
                    col = lax.broadcasted_iota(jnp.int32, s.shape, 1)
                    s = jnp.where(col <= row, s, -jnp.inf)
                return rows, n_use, s

            for r in range(blk // sub):
                rows, n_use, s = chunk(r)
                m_prev = m_ref[j, rows]
                m_new = jnp.maximum(m_prev, jnp.max(s, axis=-1, keepdims=True))
                alpha_ref[j, rows] = jnp.exp(m_prev - m_new)
                m_ref[j, rows] = m_new
            for r in range(blk // sub):
                rows, n_use, s = chunk(r)
                p = jnp.exp(s - jnp.concatenate([m_ref[j, rows]] * n_use, axis=1))
                p_lanes = p[:, :LANES]
                for c in range(1, n_use):
                    p_lanes = p_lanes + p[:, c * LANES:(c + 1) * LANES]
                l_ref[j, rows] = alpha_ref[j, rows] * l_ref[j, rows] + p_lanes
                p_ref[j, rows, 0:n_use * LANES] = p.astype(BF16)
                if n_use < n_rep:
                    p_ref[j, rows, n_use * LANES:] = jnp.zeros((sub, blk - n_use * LANES), BF16)
            acc_ref[j] = alpha_ref[j] * acc_ref[j] + jnp.dot(p_ref[j], vt, preferred_element_type=F32)

    for qc in range(n_q_blocks):
        @pl.when(qi == qc)
        def _(qc=qc):
            scores(qc, 0, s_ref.at[0])
            for ki in range(qc):
                scores(qc, ki + 1, s_ref.at[ki + 1])
                update(ki, s_ref.at[ki], False)
            update(qc, s_ref.at[qc], True)

    l0 = jnp.sum(l_ref[0], axis=-1, keepdims=True)
    l1 = jnp.sum(l_ref[1], axis=-1, keepdims=True)
    o = jnp.where(lane < FOX_HEAD_DIM, acc_ref[0] / l0, acc_ref[1] / l1)
    o_ref[0] = o.astype(o_ref.dtype)


def _fox_prompt(q, k, v, frow, *, blk):
    b, s_len, _ = q.shape
    n_pairs = TOK_WIDTH // LANES
    nq = s_len // blk
    return pl.pallas_call(
        functools.partial(_fox_prompt_kernel, blk=blk, n_q_blocks=nq),
        out_shape=jax.ShapeDtypeStruct((b, s_len, TOK_WIDTH), BF16),
        grid=(b, n_pairs, nq),
        in_specs=[pl.BlockSpec((1, blk, LANES), lambda bi, hp, qi: (bi, qi, hp)),
                  pl.BlockSpec((LANES, s_len), lambda bi, hp, qi: (hp, bi)),
                  pl.BlockSpec((1, s_len, LANES), lambda bi, hp, qi: (bi, 0, hp)),
                  pl.BlockSpec((1, 2, s_len), lambda bi, hp, qi: (hp, 0, bi))],
        out_specs=pl.BlockSpec((1, blk, LANES), lambda bi, hp, qi: (bi, qi, hp)),
        scratch_shapes=[pltpu.VMEM((2, blk, LANES), F32), pltpu.VMEM((2, blk, LANES), F32),
                        pltpu.VMEM((2, blk, LANES), F32), pltpu.VMEM((2, blk, LANES), F32),
                        pltpu.VMEM((nq, 2, blk, blk), F32),
                        pltpu.VMEM((nq, 2, blk, blk), BF16)],
        compiler_params=pltpu.CompilerParams(
            dimension_semantics=("arbitrary", "arbitrary", "arbitrary"),
            vmem_limit_bytes=VMEM_LIMIT),
        name="fox_prompt",
    )(q, k, v, frow)


def _lane_suffix_sum(x):
    n = x.shape[-1]
    lane = lax.broadcasted_iota(jnp.int32, x.shape, 1)
    k = 1
    while k < n:
        x = x + jnp.where(lane + k < n, pltpu.roll(x, n - k, axis=1), 0.0)
        k *= 2
    return x


def _fox_paged_kernel(pt_ref, q_ref, kn_ref, vn_ref, lfn_ref, logf_ref, k_hbm, v_hbm, o_ref,
                      kbuf, vbuf, sem, pad_k_ref, pad_v_ref, *, n_pages, n_batch):
    b = pl.program_id(0)
    n_steps = pl.num_programs(0)
    slot = lax.rem(b, PAGE_SLOTS)
    n_q = q_ref.shape[1]
    n_rows = n_q * FOX_HEADS

    def page_copies(batch, slot_):
        copies = []
        for j in range(n_pages):
            page = pt_ref[j * n_batch + batch]
            copies.append(pltpu.make_async_copy(k_hbm.at[page], kbuf.at[slot_, j], sem.at[0, slot_]))
            copies.append(pltpu.make_async_copy(v_hbm.at[page], vbuf.at[slot_, j], sem.at[1, slot_]))
        return copies

    @pl.when(b == 0)
    def _():
        for c in page_copies(0, 0):
            c.start()

    @pl.when(jnp.logical_and(b == 0, n_steps > 1))
    def _():
        for c in page_copies(1, 1):
            c.start()

    @pl.when(b + 2 < n_steps)
    def _():
        for c in page_copies(b + 2, lax.rem(b + 2, PAGE_SLOTS)):
            c.start()

    head = lax.broadcasted_iota(jnp.int32, (FOX_HEADS, TOK_WIDTH), 0)
    lane_head = jnp.right_shift(lax.broadcasted_iota(jnp.int32, (FOX_HEADS, TOK_WIDTH), 1), 6)
    own_head = head == lane_head
    tile_q = lambda x: jnp.concatenate([x] * n_q, axis=0)

    q = q_ref[0]
    qblk = jnp.concatenate(
        [jnp.where(own_head, jnp.broadcast_to(q[t:t + 1, :], (FOX_HEADS, TOK_WIDTH)), 0.0)
         for t in range(n_q)], axis=0).astype(BF16)

    pad_k_ref[...] = jnp.zeros_like(pad_k_ref)
    pad_v_ref[...] = jnp.zeros_like(pad_v_ref)
    pad_k_ref[0:n_q, :] = kn_ref[0]
    pad_v_ref[0:n_q, :] = vn_ref[0]
    c_new = _lane_cumsum(lfn_ref[0])
    cq = jnp.concatenate([c_new[:, t:t + 1] for t in range(n_q)], axis=0)
    s_own = lax.dot_general(qblk, pad_k_ref[...].astype(BF16), NT_DIMS, preferred_element_type=F32)
    t_of_row = jnp.right_shift(lax.broadcasted_iota(jnp.int32, (n_rows, PAGE_SIZE), 0), 3)
    key = lax.broadcasted_iota(jnp.int32, (n_rows, PAGE_SIZE), 1)
    s_own = jnp.where(key <= t_of_row, s_own + (tile_q(-c_new) + cq), -jnp.inf)

    logf = [logf_ref[pt_ref[j * n_batch + b]] for j in range(n_pages)]
    incl = _lane_suffix_sum(jnp.concatenate(logf, axis=0))
    later_pages = jnp.zeros((FOX_HEADS, 1), F32)
    bias = [None] * n_pages
    for j in reversed(range(n_pages)):
        rows = slice(j * FOX_HEADS, (j + 1) * FOX_HEADS)
        bias[j] = tile_q(later_pages + (incl[rows] - logf[j])) + cq
        later_pages = later_pages + incl[rows, 0:1]

    for c in page_copies(b, slot):
        c.wait()

    scores = [s_own]
    for j in range(n_pages):
        k_t = kbuf[slot, j].reshape(TOK_WIDTH, PAGE_SIZE).astype(BF16)
        scores.append(jnp.dot(qblk, k_t, preferred_element_type=F32) + bias[j])
    s_max = scores[0]
    for s in scores[1:]:
        s_max = jnp.maximum(s_max, s)
    m = jnp.max(s_max, axis=-1, keepdims=True)
    p = jnp.exp(scores[0] - m)
    p_sum = p
    acc = jnp.dot(p.astype(BF16), pad_v_ref[...].astype(BF16), preferred_element_type=F32)
    for j in range(n_pages):
        p = jnp.exp(scores[j + 1] - m)
        p_sum = p_sum + p
        v_t = vbuf[slot, j].reshape(TOK_WIDTH, PAGE_SIZE).astype(BF16)
        acc = acc + lax.dot_general(p.astype(BF16), v_t, NT_DIMS, preferred_element_type=F32)
    o = acc / jnp.sum(p_sum, axis=-1, keepdims=True)
    for t in range(n_q):
        o_t = jnp.where(own_head, o[t * FOX_HEADS:(t + 1) * FOX_HEADS, :], 0.0)
        o_ref[0, t:t + 1, :] = jnp.sum(o_t, axis=0, keepdims=True)


def _fox_paged(page_table, q, k_new, v_new, lfn, cache_logf_t, cache_k_t, cache_v_t):
    b, n_q, _ = q.shape
    n_pages = page_table.shape[0] // b
    cur = lambda bi, pt: (bi, 0, 0)
    page_shape = cache_k_t.shape[1:]
    grid_spec = pltpu.PrefetchScalarGridSpec(
        num_scalar_prefetch=1,
        grid=(b,),
        in_specs=[pl.BlockSpec((1, n_q, TOK_WIDTH), cur),
                  pl.BlockSpec((1, n_q, TOK_WIDTH), cur),
                  pl.BlockSpec((1, n_q, TOK_WIDTH), cur),
                  pl.BlockSpec((1, FOX_HEADS, LANES), cur),
                  pl.BlockSpec(cache_logf_t.shape, lambda bi, pt: (0, 0, 0)),
                  pl.BlockSpec(memory_space=pl.ANY),
                  pl.BlockSpec(memory_space=pl.ANY)],
        out_specs=pl.BlockSpec((1, n_q, TOK_WIDTH), cur),
        scratch_shapes=[pltpu.VMEM((PAGE_SLOTS, n_pages) + page_shape, F32),
                        pltpu.VMEM((PAGE_SLOTS, n_pages) + page_shape, F32),
                        pltpu.SemaphoreType.DMA((2, PAGE_SLOTS)),
                        pltpu.VMEM((PAGE_SIZE, TOK_WIDTH), F32),
                        pltpu.VMEM((PAGE_SIZE, TOK_WIDTH), F32)])
    return pl.pallas_call(
        functools.partial(_fox_paged_kernel, n_pages=n_pages, n_batch=b),
        out_shape=jax.ShapeDtypeStruct((b, n_q, TOK_WIDTH), F32),
        grid_spec=grid_spec,
        compiler_params=pltpu.CompilerParams(dimension_semantics=("arbitrary",),
                                             vmem_limit_bytes=VMEM_LIMIT),
        name="fox_paged",
    )(page_table, q, k_new, v_new, lfn, cache_logf_t, cache_k_t, cache_v_t)


def _decoder(x, pool_prev, n_prev, mem_k, mem_v, past, wts, *, tm, pool_bb, pool_chunk, mem_bb, mem_ts,
             act):
    b, s_len, _ = x.shape
    t = b * s_len
    h = x.reshape(t, D_MODEL)

    z_tok, qm = _proj_in(h, wts["g_mix"][0], wts["w_in"][0], wts["g_fox_q_t"], wts["g_mem_q_t"][0],
                         fox=False, tm=tm, act=act)
    z_tok = z_tok.reshape(b, s_len, TOK_WIDTH)
    tok = _pool_mix(z_tok, pool_prev, wts["w_pool"], wts["pool_scale"],
                    n_prev=n_prev, bb=pool_bb, chunk=pool_chunk, act=act)
    mo = _mem_attn(qm.reshape(b, s_len, MEM_WIDTH), mem_k, mem_v, 0, bb=mem_bb, ts=mem_ts)
    h = _out_ffn(h, tok.reshape(t, TOK_WIDTH), mo.reshape(t, MEM_WIDTH), wts["w_out"][0],
                 wts["g_ffn"][0], wts["w_gu"][0], wts["w_down"][0], tm=tm, tf=D_FF // 2)

    kv_args = (h, wts["g_kv"], wts["w_kv"], wts["g_fox_k_t"], wts["w_fg_t"], wts["b_fg"])
    if past is None:
        k_new, v_new, k_bf, v_bf, logf_t, logf_new = _proj_kv_paged(*kv_args, tm=tm)
    else:
        k_new, v_new, logf_t = _proj_kv(*kv_args, tm=tm)
        logf_new = logf_t.T

    q, qm = _proj_in(h, wts["g_mix"][1], wts["w_in"][1], wts["g_fox_q_t"], wts["g_mem_q_t"][1],
                     fox=True, tm=tm, act=act)
    if past is None:
        f_row = _cumsum_rows(logf_t, seg=s_len)
        n_pairs = FOX_HEADS // 2
        f_row = f_row.reshape(n_pairs, 2, t)
        tok = _fox_prompt(q.reshape(b, s_len, TOK_WIDTH), k_bf,
                          v_bf.reshape(b, s_len, TOK_WIDTH), f_row, blk=512)
    else:
        page_table, cache_k_t, cache_v_t, cache_logf_t = past
        lfn = logf_t.reshape(FOX_HEADS, b, s_len).transpose(1, 0, 2)
        lfn = jnp.pad(lfn, ((0, 0), (0, 0), (0, LANES - s_len)))
        per_row = lambda a: a.reshape(b, s_len, TOK_WIDTH)
        tok = _fox_paged(page_table, per_row(q), per_row(k_new), per_row(v_new), lfn,
                         cache_logf_t, cache_k_t, cache_v_t)
    mo = _mem_attn(qm.reshape(b, s_len, MEM_WIDTH), mem_k, mem_v, 1, bb=mem_bb, ts=mem_ts)
    h = _out_ffn(h, tok.reshape(t, TOK_WIDTH), mo.reshape(t, MEM_WIDTH), wts["w_out"][1],
                 wts["g_ffn"][1], wts["w_gu"][1], wts["w_down"][1], tm=tm, tf=D_FF // 2)
    return h.reshape(b, s_len, D_MODEL), z_tok, k_new, v_new, logf_new


def kernel(x_prompt, x_sample, cache_mem_k, cache_mem_v, state_pool, cache_k, cache_v, cache_logf,
           page_table, mem_prompt, g_mix, w_in, w_out, g_ffn, w_gu, w_down, g_mem, w_mem_kv,
           g_mem_q, g_mem_k, w_pool, pool_scale, g_kv, w_kv, g_fox_k, w_fg, b_fg, g_fox_q):
    depth = w_in.shape[0]
    b, s_len, _ = x_prompt.shape
    db, ds, _ = x_sample.shape
    wts = {
        "g_mix": g_mix.reshape(depth, 1, D_MODEL),
        "w_in": w_in.astype(BF16),
        "w_out": w_out.astype(BF16),
        "g_ffn": g_ffn.reshape(depth, 1, D_MODEL),
        "w_gu": w_gu.astype(BF16),
        "w_down": w_down.astype(BF16),
        "g_mem_q_t": jnp.tile(g_mem_q, (1, MEM_HEADS)).reshape(depth, 1, MEM_WIDTH),
        "w_pool": w_pool[0].astype(BF16),
        "pool_scale": pool_scale[0].reshape(1, TOK_WIDTH),
        "g_kv": g_kv.reshape(1, D_MODEL),
        "w_kv": w_kv.astype(BF16),
        "g_fox_k_t": jnp.tile(g_fox_k, FOX_HEADS).reshape(1, TOK_WIDTH),
        "w_fg_t": w_fg.T.astype(BF16),
        "b_fg": b_fg.reshape(FOX_HEADS, 1),
        "g_fox_q_t": jnp.tile(g_fox_q[0], FOX_HEADS).reshape(1, TOK_WIDTH),
    }

    mem_k_p, mem_v_p = _proj_memkv(mem_prompt.reshape(b * N_MEM, D_MODEL),
                                   g_mem.reshape(depth, 1, D_MODEL), w_mem_kv.astype(BF16),
                                   jnp.tile(g_mem_k, (1, MEM_HEADS)).reshape(depth, 1, MEM_WIDTH), tm=512)
    mem_rows = (N_MEM * MEM_HEADS, MEM_HEAD_DIM)
    mem_k_p = mem_k_p.reshape((depth, b) + mem_rows)
    mem_v_p = mem_v_p.reshape((depth, b) + mem_rows)
    y_p, ztok_p, k_p, v_p, logf_p = _decoder(
        x_prompt, jnp.zeros((b, POOL_HIST, TOK_WIDTH), F32), 0, mem_k_p, mem_v_p, None, wts,
        tm=512, pool_bb=1, pool_chunk=256, mem_bb=1, mem_ts=512, act=BF16)
    n_pp = s_len // PAGE_SIZE
    head_shape = (FOX_HEADS, FOX_HEAD_DIM)

    past = (page_table.T.reshape(-1),
            cache_k.transpose(0, 2, 3, 1), cache_v.transpose(0, 2, 3, 1),
            cache_logf.transpose(0, 2, 1))
    prev = jnp.pad(state_pool[0], ((0, 0), (POOL_HIST - POOL_STATE, 0), (0, 0)))
    y_s, ztok_s, k_s, v_s, logf_s = _decoder(
        x_sample, prev, POOL_STATE, cache_mem_k.reshape((depth, db) + mem_rows),
        cache_mem_v.reshape((depth, db) + mem_rows), past, wts,
        tm=512, pool_bb=32, pool_chunk=ds, mem_bb=8, mem_ts=ds, act=F32)

    pool_state_p = ztok_p[:, s_len - POOL_STATE:][None]
    pool_state_s = jnp.concatenate([state_pool[0], ztok_s], axis=1)[:, -POOL_STATE:][None]
    paged = lambda a: a.reshape((b, n_pp) + head_shape + (PAGE_SIZE,)).transpose(0, 1, 4, 2, 3)
    return (y_p, y_s, paged(k_p), paged(v_p),
            logf_p.reshape(b, n_pp, FOX_HEADS, PAGE_SIZE).transpose(0, 1, 3, 2),
            mem_k_p.reshape(depth, b, N_MEM, MEM_HEADS, MEM_HEAD_DIM),
            mem_v_p.reshape(depth, b, N_MEM, MEM_HEADS, MEM_HEAD_DIM),
            pool_state_p,
            k_s.reshape((db, ds) + head_shape),
            v_s.reshape((db, ds) + head_shape),
            logf_s.reshape(db, ds, FOX_HEADS),
            pool_state_s)
```

```python
import functools

import jax
import jax.numpy as jnp
from jax import lax
from jax.experimental import pallas as pl
from jax.experimental.pallas import tpu as pltpu

D_MODEL = 1024
TOK_WIDTH = 512
MEM_WIDTH = 512
POOL_WINDOWS = (2, 4, 8, 16)
POOL_GROUP = 128
POOL_STATE = 15
POOL_HIST = 16
FOX_HEADS = 8
FOX_HEAD_DIM = 64
MEM_HEADS = 4
MEM_HEAD_DIM = 128
N_MEM = 256
D_FF = 2816
PAGE_SIZE = 128
EPS = 1e-6
LANES = 128
BF16_ROWS = 16
VMEM_LIMIT = 56 * 1024 * 1024
FOX_CHUNK_ROWS = 64
PAGE_SLOTS = 3

F32 = jnp.float32
BF16 = jnp.bfloat16
NT_DIMS = (((1,), (1,)), ((), ()))


def _rmsnorm(x, g):
    return x * lax.rsqrt(jnp.mean(x * x, axis=-1, keepdims=True) + EPS) * g


def _headnorm128(z, g):
    outs = []
    for h in range(z.shape[-1] // LANES):
        zh = z[:, h * LANES:(h + 1) * LANES]
        r = lax.rsqrt(jnp.mean(zh * zh, axis=-1, keepdims=True) + EPS)
        outs.append(zh * r * g[:, h * LANES:(h + 1) * LANES])
    return jnp.concatenate(outs, axis=-1)


def _headnorm64(z, g):
    outs = []
    for p in range(z.shape[-1] // LANES):
        zp = z[:, p * LANES:(p + 1) * LANES]
        sq = zp * zp
        lane = lax.broadcasted_iota(jnp.int32, zp.shape, 1)
        lo = lane < FOX_HEAD_DIM
        s_lo = jnp.sum(jnp.where(lo, sq, 0.0), axis=-1, keepdims=True)
        s_hi = jnp.sum(jnp.where(lo, 0.0, sq), axis=-1, keepdims=True)
        r = jnp.where(lo, lax.rsqrt(s_lo / FOX_HEAD_DIM + EPS), lax.rsqrt(s_hi / FOX_HEAD_DIM + EPS))
        outs.append(zp * r * g[:, p * LANES:(p + 1) * LANES])
    return jnp.concatenate(outs, axis=-1)


def _mem_attend(q, k_ref, v_ref):
    outs = []
    for h in range(MEM_HEADS):
        qh = q[:, h * MEM_HEAD_DIM:(h + 1) * MEM_HEAD_DIM].astype(BF16)
        k = k_ref[pl.ds(h, N_MEM, stride=MEM_HEADS), :].astype(BF16)
        v = v_ref[pl.ds(h, N_MEM, stride=MEM_HEADS), :].astype(BF16)
        s = lax.dot_general(qh, k, NT_DIMS, preferred_element_type=F32) * (MEM_HEAD_DIM ** -0.5)
        p = jnp.exp(s - jnp.max(s, axis=-1, keepdims=True))
        o = jnp.dot(p.astype(BF16), v, preferred_element_type=F32)
        outs.append(o / jnp.sum(p, axis=-1, keepdims=True))
    return jnp.concatenate(outs, axis=-1)


def _proj_in_kernel(x_ref, g_ref, w_ref, gtok_ref, gmem_ref, *refs, fox, fuse_mem):
    xn = _rmsnorm(x_ref[...], g_ref[...]).astype(BF16)
    z = jnp.dot(xn, w_ref[...], preferred_element_type=F32)
    z_tok = z[:, :TOK_WIDTH]
    qm = _headnorm128(z[:, TOK_WIDTH:], gmem_ref[...])
    if fuse_mem:
        mk_ref, mv_ref, tok_ref, mem_ref = refs
        mem_ref[...] = _mem_attend(qm, mk_ref, mv_ref).astype(mem_ref.dtype)
    else:
        tok_ref, mem_ref = refs
        mem_ref[...] = qm.astype(mem_ref.dtype)
    if fox:
        tok_ref[...] = (_headnorm64(z_tok, gtok_ref[...]) * (FOX_HEAD_DIM ** -0.5)).astype(tok_ref.dtype)
    else:
        tok_ref[...] = z_tok


def _proj_in(x, g, w, g_tok, g_mem, mem=None, *, fox, tm, act):
    t = x.shape[0]
    row = lambda i: (i, 0)
    fixed = lambda i: (0, 0)
    operands = [x, g, w, g_tok, g_mem]
    in_specs = [pl.BlockSpec((tm, D_MODEL), row),
                pl.BlockSpec((1, D_MODEL), fixed),
                pl.BlockSpec((D_MODEL, D_MODEL), fixed),
                pl.BlockSpec((1, TOK_WIDTH), fixed),
                pl.BlockSpec((1, MEM_WIDTH), fixed)]
    if mem is not None:
        mk, mv, layer, rows_per_batch = mem
        tiles = rows_per_batch // tm
        mem_block = pl.BlockSpec((None, None, N_MEM * MEM_HEADS, MEM_HEAD_DIM),
                                 lambda i: (layer, i // tiles, 0, 0))
        operands += [mk, mv]
        in_specs += [mem_block, mem_block]
    return pl.pallas_call(
        functools.partial(_proj_in_kernel, fox=fox, fuse_mem=mem is not None),
        out_shape=(jax.ShapeDtypeStruct((t, TOK_WIDTH), act if fox else F32),
                   jax.ShapeDtypeStruct((t, MEM_WIDTH), act)),
        grid=(t // tm,),
        in_specs=in_specs,
        out_specs=(pl.BlockSpec((tm, TOK_WIDTH), row), pl.BlockSpec((tm, MEM_WIDTH), row)),
        compiler_params=pltpu.CompilerParams(dimension_semantics=("arbitrary",),
                                             vmem_limit_bytes=VMEM_LIMIT),
        name="proj_in_fox" if fox else "proj_in_pool",
    )(*operands)


def _log_sigmoid(x):
    return jnp.minimum(x, 0.0) - jnp.log(1.0 + jnp.exp(-jnp.abs(x)))


def _proj_kv_kernel(x_ref, g_ref, w_ref, gk_ref, wfg_ref, bfg_ref, k_ref, v_ref, logf_ref):
    xn = _rmsnorm(x_ref[...], g_ref[...]).astype(BF16)
    z = jnp.dot(xn, w_ref[...], preferred_element_type=F32)
    k_ref[...] = _headnorm64(z[:, :TOK_WIDTH], gk_ref[...])
    v_ref[...] = z[:, TOK_WIDTH:]
    gate = lax.dot_general(wfg_ref[...], xn, NT_DIMS, preferred_element_type=F32) + bfg_ref[...]
    logf_ref[...] = _log_sigmoid(gate)


def _proj_kv_paged_kernel(x_ref, g_ref, w_ref, gk_ref, wfg_ref, bfg_ref,
                          kp_ref, vp_ref, kb_ref, vb_ref, logf_ref, logfp_ref):
    xn = _rmsnorm(x_ref[...], g_ref[...]).astype(BF16)
    z = jnp.dot(xn, w_ref[...], preferred_element_type=F32)
    k = _headnorm64(z[:, :TOK_WIDTH], gk_ref[...])
    v = z[:, TOK_WIDTH:]
    vb_ref[...] = v.astype(BF16)
    gate = lax.dot_general(wfg_ref[...], xn, NT_DIMS, preferred_element_type=F32) + bfg_ref[...]
    logf = _log_sigmoid(gate)
    logf_ref[...] = logf
    k_t = k.T
    v_t = v.T
    kb_ref[...] = k_t.astype(BF16)
    for pg in range(kp_ref.shape[0]):
        rows = slice(pg * PAGE_SIZE, (pg + 1) * PAGE_SIZE)
        kp_ref[pg] = k_t[:, rows]
        vp_ref[pg] = v_t[:, rows]
        logfp_ref[pg] = logf[:, rows]


def _proj_kv_paged(x, g, w, g_k, wfg_t, bfg, *, tm):
    t = x.shape[0]
    row = lambda i: (i, 0)
    fixed = lambda i: (0, 0)
    ppt = tm // PAGE_SIZE
    page_rows = lambda i: (i, 0, 0)
    return pl.pallas_call(
        _proj_kv_paged_kernel,
        out_shape=(jax.ShapeDtypeStruct((t // PAGE_SIZE, TOK_WIDTH, PAGE_SIZE), F32),
                   jax.ShapeDtypeStruct((t // PAGE_SIZE, TOK_WIDTH, PAGE_SIZE), F32),
                   jax.ShapeDtypeStruct((TOK_WIDTH, t), BF16),
                   jax.ShapeDtypeStruct((t, TOK_WIDTH), BF16),
                   jax.ShapeDtypeStruct((FOX_HEADS, t), F32),
                   jax.ShapeDtypeStruct((t // PAGE_SIZE, FOX_HEADS, PAGE_SIZE), F32)),
        grid=(t // tm,),
        in_specs=[pl.BlockSpec((tm, D_MODEL), row),
                  pl.BlockSpec((1, D_MODEL), fixed),
                  pl.BlockSpec((D_MODEL, D_MODEL), fixed),
                  pl.BlockSpec((1, TOK_WIDTH), fixed),
                  pl.BlockSpec((FOX_HEADS, D_MODEL), fixed),
                  pl.BlockSpec((FOX_HEADS, 1), fixed)],
        out_specs=(pl.BlockSpec((ppt, TOK_WIDTH, PAGE_SIZE), page_rows),
                   pl.BlockSpec((ppt, TOK_WIDTH, PAGE_SIZE), page_rows),
                   pl.BlockSpec((TOK_WIDTH, tm), lambda i: (0, i)), pl.BlockSpec((tm, TOK_WIDTH), row),
                   pl.BlockSpec((FOX_HEADS, tm), lambda i: (0, i)),
                   pl.BlockSpec((ppt, FOX_HEADS, PAGE_SIZE), page_rows)),
        compiler_params=pltpu.CompilerParams(dimension_semantics=("arbitrary",),
                                             vmem_limit_bytes=VMEM_LIMIT),
        name="proj_kv_paged",
    )(x, g, w, g_k, wfg_t, bfg)


def _proj_kv(x, g, w, g_k, wfg_t, bfg, *, tm):
    t = x.shape[0]
    row = lambda i: (i, 0)
    fixed = lambda i: (0, 0)
    return pl.pallas_call(
        _proj_kv_kernel,
        out_shape=(jax.ShapeDtypeStruct((t, TOK_WIDTH), F32),
                   jax.ShapeDtypeStruct((t, TOK_WIDTH), F32),
                   jax.ShapeDtypeStruct((FOX_HEADS, t), F32)),
        grid=(t // tm,),
        in_specs=[pl.BlockSpec((tm, D_MODEL), row),
                  pl.BlockSpec((1, D_MODEL), fixed),
                  pl.BlockSpec((D_MODEL, D_MODEL), fixed),
                  pl.BlockSpec((1, TOK_WIDTH), fixed),
                  pl.BlockSpec((FOX_HEADS, D_MODEL), fixed),
                  pl.BlockSpec((FOX_HEADS, 1), fixed)],
        out_specs=(pl.BlockSpec((tm, TOK_WIDTH), row), pl.BlockSpec((tm, TOK_WIDTH), row),
                   pl.BlockSpec((FOX_HEADS, tm), lambda i: (0, i))),
        compiler_params=pltpu.CompilerParams(dimension_semantics=("arbitrary",),
                                             vmem_limit_bytes=VMEM_LIMIT),
        name="proj_kv",
    )(x, g, w, g_k, wfg_t, bfg)


def _proj_memkv_kernel(x_ref, g_ref, w_ref, gk_ref, k_ref, v_ref):
    xn = _rmsnorm(x_ref[...], g_ref[0]).astype(BF16)
    z = jnp.dot(xn, w_ref[0], preferred_element_type=F32)
    k = _headnorm128(z[:, :MEM_WIDTH], gk_ref[0])
    tm = x_ref.shape[0]
    for h in range(MEM_HEADS):
        rows = pl.ds(h, tm, stride=MEM_HEADS)
        k_ref[0, rows, :] = k[:, h * MEM_HEAD_DIM:(h + 1) * MEM_HEAD_DIM]
        v_ref[0, rows, :] = z[:, MEM_WIDTH + h * MEM_HEAD_DIM:MEM_WIDTH + (h + 1) * MEM_HEAD_DIM]


def _proj_memkv(x, g, w, g_k, *, tm):
    t = x.shape[0]
    n_layers = w.shape[0]
    out_block = pl.BlockSpec((1, tm * MEM_HEADS, MEM_HEAD_DIM), lambda l, i: (l, i, 0))
    return pl.pallas_call(
        _proj_memkv_kernel,
        out_shape=(jax.ShapeDtypeStruct((n_layers, t * MEM_HEADS, MEM_HEAD_DIM), F32),
                   jax.ShapeDtypeStruct((n_layers, t * MEM_HEADS, MEM_HEAD_DIM), F32)),
        grid=(n_layers, t // tm),
        in_specs=[pl.BlockSpec((tm, D_MODEL), lambda l, i: (i, 0)),
                  pl.BlockSpec((1, 1, D_MODEL), lambda l, i: (l, 0, 0)),
                  pl.BlockSpec((1, D_MODEL, D_MODEL), lambda l, i: (l, 0, 0)),
                  pl.BlockSpec((1, 1, MEM_WIDTH), lambda l, i: (l, 0, 0))],
        out_specs=(out_block, out_block),
        compiler_params=pltpu.CompilerParams(dimension_semantics=("arbitrary", "arbitrary"),
                                             vmem_limit_bytes=VMEM_LIMIT),
        name="proj_memkv",
    )(x, g, w, g_k)


def _pool_kernel(u_ref, prev_ref, w_ref, scale_ref, o_ref, ext_ref, *, n_prev, chunk):
    bb, s_len, _ = u_ref.shape
    ext_ref[:, 0:POOL_HIST, :] = prev_ref[...]
    ext_ref[:, POOL_HIST:, :] = u_ref[...]
    for c in range(s_len // chunk):
        r0 = POOL_HIST + c * chunk
        pos = c * chunk + lax.broadcasted_iota(jnp.int32, (1, chunk, 1), 1)
        for gi, win in enumerate(POOL_WINDOWS):
            lanes = slice(gi * POOL_GROUP, (gi + 1) * POOL_GROUP)
            u_new = ext_ref[:, r0:r0 + chunk, lanes]
            acc = u_new
            for k in range(1, win):
                acc = acc + ext_ref[:, r0 - k:r0 - k + chunk, lanes]
            count = jnp.minimum(win, n_prev + pos + 1).astype(F32)
            y = acc / count - u_new
            z = jnp.dot(y.reshape(bb * chunk, POOL_GROUP).astype(BF16), w_ref[gi],
                        preferred_element_type=F32)
            z = z.reshape(bb, chunk, POOL_GROUP) * scale_ref[:, lanes]
            o_ref[:, c * chunk:(c + 1) * chunk, lanes] = z.astype(o_ref.dtype)


def _pool_mix(u, prev, w_pool, scale, *, n_prev, bb, chunk, act):
    b, s_len, _ = u.shape
    return pl.pallas_call(
        functools.partial(_pool_kernel, n_prev=n_prev, chunk=chunk),
        out_shape=jax.ShapeDtypeStruct((b, s_len, TOK_WIDTH), act),
        grid=(b // bb,),
        in_specs=[pl.BlockSpec((bb, s_len, TOK_WIDTH), lambda i: (i, 0, 0)),
                  pl.BlockSpec((bb, POOL_HIST, TOK_WIDTH), lambda i: (i, 0, 0)),
                  pl.BlockSpec((len(POOL_WINDOWS), POOL_GROUP, POOL_GROUP), lambda i: (0, 0, 0)),
                  pl.BlockSpec((1, TOK_WIDTH), lambda i: (0, 0))],
        out_specs=pl.BlockSpec((bb, s_len, TOK_WIDTH), lambda i: (i, 0, 0)),
        scratch_shapes=[pltpu.VMEM((bb, POOL_HIST + s_len, TOK_WIDTH), F32)],
        compiler_params=pltpu.CompilerParams(dimension_semantics=("arbitrary",),
                                             vmem_limit_bytes=VMEM_LIMIT),
        name="pool_mix",
    )(u, prev, w_pool, scale)


def _mem_attn_kernel(q_ref, k_ref, v_ref, o_ref):
    for h in range(MEM_HEADS):
        lanes = slice(h * MEM_HEAD_DIM, (h + 1) * MEM_HEAD_DIM)
        q = q_ref[:, :, lanes].astype(BF16)
        k = k_ref[:, pl.ds(h, N_MEM, stride=MEM_HEADS), :].astype(BF16)
        v = v_ref[:, pl.ds(h, N_MEM, stride=MEM_HEADS), :].astype(BF16)
        s = jnp.einsum("bqd,bkd->bqk", q, k, preferred_element_type=F32) * (MEM_HEAD_DIM ** -0.5)
        m = jnp.max(s, axis=-1, keepdims=True)
        p = jnp.exp(s - m)
        l = jnp.sum(p, axis=-1, keepdims=True)
        o = jnp.einsum("bqk,bkd->bqd", p.astype(BF16), v, preferred_element_type=F32)
        o_ref[:, :, lanes] = (o / l).astype(o_ref.dtype)


def _mem_attn(q, mk, mv, layer, *, bb, ts):
    b, s_len, _ = q.shape
    mem_block = pl.BlockSpec((None, bb, N_MEM * MEM_HEADS, MEM_HEAD_DIM),
                             lambda i, j: (layer, i, 0, 0))
    return pl.pallas_call(
        _mem_attn_kernel,
        out_shape=jax.ShapeDtypeStruct((b, s_len, MEM_WIDTH), q.dtype),
        grid=(b // bb, s_len // ts),
        in_specs=[pl.BlockSpec((bb, ts, MEM_WIDTH), lambda i, j: (i, j, 0)), mem_block, mem_block],
        out_specs=pl.BlockSpec((bb, ts, MEM_WIDTH), lambda i, j: (i, j, 0)),
        compiler_params=pltpu.CompilerParams(dimension_semantics=("arbitrary", "arbitrary"),
                                             vmem_limit_bytes=VMEM_LIMIT),
        name="mem_attn",
    )(q, mk, mv)


def _out_ffn_kernel(h_ref, tok_ref, mo_ref, wout_ref, g_ref, wg_ref, wu_ref, wd_ref,
                    o_ref, xn_ref, acc_ref):
    kf = pl.program_id(1)

    @pl.when(kf == 0)
    def _():
        h1 = (h_ref[...]
              + jnp.dot(tok_ref[...].astype(BF16), wout_ref[:TOK_WIDTH, :], preferred_element_type=F32)
              + jnp.dot(mo_ref[...].astype(BF16), wout_ref[TOK_WIDTH:, :], preferred_element_type=F32))
        acc_ref[...] = h1
        xn_ref[...] = _rmsnorm(h1, g_ref[...]).astype(BF16)

    x = xn_ref[...]
    gate = jnp.dot(x, wg_ref[...], preferred_element_type=F32)
    up = jnp.dot(x, wu_ref[...], preferred_element_type=F32)
    act = (gate * (1.0 / (1.0 + jnp.exp(-gate))) * up).astype(BF16)
    acc_ref[...] += jnp.dot(act, wd_ref[...], preferred_element_type=F32)

    @pl.when(kf == pl.num_programs(1) - 1)
    def _():
        o_ref[...] = acc_ref[...]


def _out_ffn(h, tok, mo, w_out, g, w_gu, w_down, *, tm, tf):
    t = h.shape[0]
    nf = D_FF // tf
    row = lambda i, k: (i, 0)
    fixed = lambda i, k: (0, 0)
    return pl.pallas_call(
        _out_ffn_kernel,
        out_shape=jax.ShapeDtypeStruct((t, D_MODEL), F32),
        grid=(t // tm, nf),
        in_specs=[pl.BlockSpec((tm, D_MODEL), row),
                  pl.BlockSpec((tm, TOK_WIDTH), row),
                  pl.BlockSpec((tm, MEM_WIDTH), row),
                  pl.BlockSpec((D_MODEL, D_MODEL), fixed),
                  pl.BlockSpec((1, D_MODEL), fixed),
                  pl.BlockSpec((D_MODEL, tf), lambda i, k: (0, k)),
                  pl.BlockSpec((D_MODEL, tf), lambda i, k: (0, nf + k)),
                  pl.BlockSpec((tf, D_MODEL), lambda i, k: (k, 0))],
        out_specs=pl.BlockSpec((tm, D_MODEL), row),
        scratch_shapes=[pltpu.VMEM((tm, D_MODEL), BF16), pltpu.VMEM((tm, D_MODEL), F32)],
        compiler_params=pltpu.CompilerParams(dimension_semantics=("arbitrary", "arbitrary"),
                                             vmem_limit_bytes=VMEM_LIMIT),
        name="out_ffn",
    )(h, tok, mo, w_out, g, w_gu, w_gu, w_down)


def _lane_cumsum(x):
    n = x.shape[-1]
    lane = lax.broadcasted_iota(jnp.int32, x.shape, 1)
    k = 1
    while k < n:
        x = x + jnp.where(lane >= k, pltpu.roll(x, k, axis=1), 0.0)
        k *= 2
    return x


def _cumsum_kernel(x_ref, o_ref):
    o_ref[...] = _lane_cumsum(x_ref[...])


def _cumsum_rows(x, *, seg):
    rows, t = x.shape
    return pl.pallas_call(
        _cumsum_kernel,
        out_shape=jax.ShapeDtypeStruct((rows, t), F32),
        grid=(t // seg,),
        in_specs=[pl.BlockSpec((rows, seg), lambda i: (0, i))],
        out_specs=pl.BlockSpec((rows, seg), lambda i: (0, i)),
        compiler_params=pltpu.CompilerParams(dimension_semantics=("arbitrary",)),
        name="logf_cumsum",
    )(x)


def _round_to_bf16(x):
    return x.astype(BF16).astype(F32)


def _fox_prompt_kernel(q_ref, k_ref, v_ref, frow_ref, o_ref, m_ref, l_ref, acc_ref, alpha_ref,
                       s_ref, p_all_ref, *, blk, n_q_blocks):
    qi = pl.program_id(2)
    q = q_ref[0]
    lane = lax.broadcasted_iota(jnp.int32, q.shape, 1)
    sub = FOX_CHUNK_ROWS
    n_rep = blk // LANES
    bias_lane = [FOX_HEAD_DIM, 0]
    q_f32 = q.astype(F32)
    q_head = []
    for j in range(2):
        own = (lane < FOX_HEAD_DIM) if j == 0 else (lane >= FOX_HEAD_DIM)
        ones_from = jnp.where(lane >= bias_lane[j], 1.0, 0.0)
        bias_ones = jnp.where(lane < bias_lane[j] + 3, ones_from, 0.0)
        q_head.append(jnp.where(own, q_f32, bias_ones).astype(BF16))
    m_ref[...] = jnp.full(m_ref.shape, -jnp.inf, F32)
    l_ref[...] = jnp.zeros_like(l_ref)
    acc_ref[...] = jnp.zeros_like(acc_ref)

    def scores(qc, ki, s_buf):
        keys = slice(ki * blk, (ki + 1) * blk)
        kt = k_ref[:, keys]
        tile_row = lax.broadcasted_iota(jnp.int32, (BF16_ROWS, blk), 0)
        for j in range(2):
            f_end = frow_ref[0, j:j + 1, (qc + 1) * blk - 1:(qc + 1) * blk]
            bias = f_end - frow_ref[0, j:j + 1, keys]
            hi = _round_to_bf16(bias)
            mid = _round_to_bf16(bias - hi)
            lo = _round_to_bf16(bias - hi - mid)
            bias_rows = jnp.where(tile_row == 0, hi, jnp.where(tile_row == 1, mid,
                                                               jnp.where(tile_row == 2, lo, 0.0)))
            b0 = bias_lane[j]
            pieces = [kt[:b0], bias_rows.astype(BF16), kt[b0 + BF16_ROWS:]]
            kt_j = jnp.concatenate([x for x in pieces if x.shape[0]], axis=0)
            s_buf[j] = jnp.dot(q_head[j], kt_j, preferred_element_type=F32)

    def update(ki, s_buf, masked):
        p_ref = p_all_ref.at[ki]
        vt = v_ref[0, ki * blk:(ki + 1) * blk, :]
        for j in range(2):
            def chunk(r):
                n_use = -(-(r + 1) * sub // LANES) if masked else n_rep
                rows = slice(r * sub, (r + 1) * sub)
                s = s_buf[j, rows, 0:n_use * LANES]
                if masked:
                    row = lax.broadcasted_iota(jnp.int32, s.shape, 0) + r * sub
                    col = lax.broadcasted_iota(jnp.int32, s.shape, 1)
                    s = jnp.where(col <= row, s, -jnp.inf)
                return rows, n_use, s

            for r in range(blk // sub):
                rows, n_use, s = chunk(r)
                m_prev = m_ref[j, rows]
                m_new = jnp.maximum(m_prev, jnp.max(s, axis=-1, keepdims=True))
                alpha_ref[j, rows] = jnp.exp(m_prev - m_new)
                m_ref[j, rows] = m_new
            for r in range(blk // sub):
                rows, n_use, s = chunk(r)
                p = jnp.exp(s - jnp.concatenate([m_ref[j, rows]] * n_use, axis=1))
                p_lanes = p[:, :LANES]
                for c in range(1, n_use):
                    p_lanes = p_lanes + p[:, c * LANES:(c + 1) * LANES]
                l_ref[j, rows] = alpha_ref[j, rows] * l_ref[j, rows] + p_lanes
                p_ref[j, rows, 0:n_use * LANES] = p.astype(BF16)
                if n_use < n_rep:
                    p_ref[j, rows, n_use * LANES:] = jnp.zeros((sub, blk - n_use * LANES), BF16)
            acc_ref[j] = alpha_ref[j] * acc_ref[j] + jnp.dot(p_ref[j], vt, preferred_element_type=F32)

    for qc in range(n_q_blocks):
        @pl.when(qi == qc)
        def _(qc=qc):
            scores(qc, 0, s_ref.at[0])
            for ki in range(qc):
                scores(qc, ki + 1, s_ref.at[ki + 1])
                update(ki, s_ref.at[ki], False)
            update(qc, s_ref.at[qc], True)

    l0 = jnp.sum(l_ref[0], axis=-1, keepdims=True)
    l1 = jnp.sum(l_ref[1], axis=-1, keepdims=True)
    o = jnp.where(lane < FOX_HEAD_DIM, acc_ref[0] / l0, acc_ref[1] / l1)
    o_ref[0] = o.astype(o_ref.dtype)


def _fox_prompt(q, k, v, frow, *, blk):
    b, s_len, _ = q.shape
    n_pairs = TOK_WIDTH // LANES
    nq = s_len // blk
    return pl.pallas_call(
        functools.partial(_fox_prompt_kernel, blk=blk, n_q_blocks=nq),
        out_shape=jax.ShapeDtypeStruct((b, s_len, TOK_WIDTH), BF16),
        grid=(b, n_pairs, nq),
        in_specs=[pl.BlockSpec((1, blk, LANES), lambda bi, hp, qi: (bi, qi, hp)),
                  pl.BlockSpec((LANES, s_len), lambda bi, hp, qi: (hp, bi)),
                  pl.BlockSpec((1, s_len, LANES), lambda bi, hp, qi: (bi, 0, hp)),
                  pl.BlockSpec((1, 2, s_len), lambda bi, hp, qi: (hp, 0, bi))],
        out_specs=pl.BlockSpec((1, blk, LANES), lambda bi, hp, qi: (bi, qi, hp)),
        scratch_shapes=[pltpu.VMEM((2, blk, LANES), F32), pltpu.VMEM((2, blk, LANES), F32),
                        pltpu.VMEM((2, blk, LANES), F32), pltpu.VMEM((2, blk, LANES), F32),
                        pltpu.VMEM((nq, 2, blk, blk), F32),
                        pltpu.VMEM((nq, 2, blk, blk), BF16)],
        compiler_params=pltpu.CompilerParams(
            dimension_semantics=("arbitrary", "arbitrary", "arbitrary"),
            vmem_limit_bytes=VMEM_LIMIT),
        name="fox_prompt",
    )(q, k, v, frow)


def _lane_suffix_sum(x):
    n = x.shape[-1]
    lane = lax.broadcasted_iota(jnp.int32, x.shape, 1)
    k = 1
    while k < n:
        x = x + jnp.where(lane + k < n, pltpu.roll(x, n - k, axis=1), 0.0)
        k *= 2
    return x


def _fox_paged_kernel(pt_ref, q_ref, kn_ref, vn_ref, lfn_ref, logf_ref, k_hbm, v_hbm, o_ref,
                      kbuf, vbuf, sem, pad_k_ref, pad_v_ref, *, n_pages, n_batch):
    b = pl.program_id(0)
    n_steps = pl.num_programs(0)
    slot = lax.rem(b, PAGE_SLOTS)
    n_q = q_ref.shape[1]
    n_rows = n_q * FOX_HEADS

    def page_copies(batch, slot_):
        copies = []
        for j in range(n_pages):
            page = pt_ref[j * n_batch + batch]
            copies.append(pltpu.make_async_copy(k_hbm.at[page], kbuf.at[slot_, j], sem.at[0, slot_]))
            copies.append(pltpu.make_async_copy(v_hbm.at[page], vbuf.at[slot_, j], sem.at[1, slot_]))
        return copies

    @pl.when(b == 0)
    def _():
        for c in page_copies(0, 0):
            c.start()

    @pl.when(jnp.logical_and(b == 0, n_steps > 1))
    def _():
        for c in page_copies(1, 1):
            c.start()

    @pl.when(b + 2 < n_steps)
    def _():
        for c in page_copies(b + 2, lax.rem(b + 2, PAGE_SLOTS)):
            c.start()

    head = lax.broadcasted_iota(jnp.int32, (FOX_HEADS, TOK_WIDTH), 0)
    lane_head = jnp.right_shift(lax.broadcasted_iota(jnp.int32, (FOX_HEADS, TOK_WIDTH), 1), 6)
    own_head = head == lane_head
    tile_q = lambda x: jnp.concatenate([x] * n_q, axis=0)

    q = q_ref[0]
    qblk = jnp.concatenate(
        [jnp.where(own_head, jnp.broadcast_to(q[t:t + 1, :], (FOX_HEADS, TOK_WIDTH)), 0.0)
         for t in range(n_q)], axis=0).astype(BF16)

    pad_k_ref[...] = jnp.zeros_like(pad_k_ref)
    pad_v_ref[...] = jnp.zeros_like(pad_v_ref)
    pad_k_ref[0:n_q, :] = kn_ref[0]
    pad_v_ref[0:n_q, :] = vn_ref[0]
    c_new = _lane_cumsum(lfn_ref[0])
    cq = jnp.concatenate([c_new[:, t:t + 1] for t in range(n_q)], axis=0)
    s_own = lax.dot_general(qblk, pad_k_ref[...].astype(BF16), NT_DIMS, preferred_element_type=F32)
    t_of_row = jnp.right_shift(lax.broadcasted_iota(jnp.int32, (n_rows, PAGE_SIZE), 0), 3)
    key = lax.broadcasted_iota(jnp.int32, (n_rows, PAGE_SIZE), 1)
    s_own = jnp.where(key <= t_of_row, s_own + (tile_q(-c_new) + cq), -jnp.inf)

    logf = [logf_ref[pt_ref[j * n_batch + b]] for j in range(n_pages)]
    incl = _lane_suffix_sum(jnp.concatenate(logf, axis=0))
    later_pages = jnp.zeros((FOX_HEADS, 1), F32)
    bias = [None] * n_pages
    for j in reversed(range(n_pages)):
        rows = slice(j * FOX_HEADS, (j + 1) * FOX_HEADS)
        bias[j] = tile_q(later_pages + (incl[rows] - logf[j])) + cq
        later_pages = later_pages + incl[rows, 0:1]

    for c in page_copies(b, slot):
        c.wait()

    scores = [s_own]
    for j in range(n_pages):
        k_t = kbuf[slot, j].reshape(TOK_WIDTH, PAGE_SIZE).astype(BF16)
        scores.append(jnp.dot(qblk, k_t, preferred_element_type=F32) + bias[j])
    s_max = scores[0]
    for s in scores[1:]:
        s_max = jnp.maximum(s_max, s)
    m = jnp.max(s_max, axis=-1, keepdims=True)
    p = jnp.exp(scores[0] - m)
    p_sum = p
    acc = jnp.dot(p.astype(BF16), pad_v_ref[...].astype(BF16), preferred_element_type=F32)
    for j in range(n_pages):
        p = jnp.exp(scores[j + 1] - m)
        p_sum = p_sum + p
        v_t = vbuf[slot, j].reshape(TOK_WIDTH, PAGE_SIZE).astype(BF16)
        acc = acc + lax.dot_general(p.astype(BF16), v_t, NT_DIMS, preferred_element_type=F32)
    o = acc / jnp.sum(p_sum, axis=-1, keepdims=True)
    for t in range(n_q):
        o_t = jnp.where(own_head, o[t * FOX_HEADS:(t + 1) * FOX_HEADS, :], 0.0)
        o_ref[0, t:t + 1, :] = jnp.sum(o_t, axis=0, keepdims=True)


def _fox_paged(page_table, q, k_new, v_new, lfn, cache_logf_t, cache_k_t, cache_v_t):
    b, n_q, _ = q.shape
    n_pages = page_table.shape[0] // b
    cur = lambda bi, pt: (bi, 0, 0)
    page_shape = cache_k_t.shape[1:]
    grid_spec = pltpu.PrefetchScalarGridSpec(
        num_scalar_prefetch=1,
        grid=(b,),
        in_specs=[pl.BlockSpec((1, n_q, TOK_WIDTH), cur),
                  pl.BlockSpec((1, n_q, TOK_WIDTH), cur),
                  pl.BlockSpec((1, n_q, TOK_WIDTH), cur),
                  pl.BlockSpec((1, FOX_HEADS, LANES), cur),
                  pl.BlockSpec(cache_logf_t.shape, lambda bi, pt: (0, 0, 0)),
                  pl.BlockSpec(memory_space=pl.ANY),
                  pl.BlockSpec(memory_space=pl.ANY)],
        out_specs=pl.BlockSpec((1, n_q, TOK_WIDTH), cur),
        scratch_shapes=[pltpu.VMEM((PAGE_SLOTS, n_pages) + page_shape, F32),
                        pltpu.VMEM((PAGE_SLOTS, n_pages) + page_shape, F32),
                        pltpu.SemaphoreType.DMA((2, PAGE_SLOTS)),
                        pltpu.VMEM((PAGE_SIZE, TOK_WIDTH), F32),
                        pltpu.VMEM((PAGE_SIZE, TOK_WIDTH), F32)])
    return pl.pallas_call(
        functools.partial(_fox_paged_kernel, n_pages=n_pages, n_batch=b),
        out_shape=jax.ShapeDtypeStruct((b, n_q, TOK_WIDTH), F32),
        grid_spec=grid_spec,
        compiler_params=pltpu.CompilerParams(dimension_semantics=("arbitrary",),
                                             vmem_limit_bytes=VMEM_LIMIT),
        name="fox_paged",
    )(page_table, q, k_new, v_new, lfn, cache_logf_t, cache_k_t, cache_v_t)


def _decoder(x, pool_prev, n_prev, mem_k, mem_v, past, wts, *, tm, pool_bb, pool_chunk, mem_bb, mem_ts,
             act):
    b, s_len, _ = x.shape
    t = b * s_len
    h = x.reshape(t, D_MODEL)

    fuse_mem = s_len % tm == 0

    def project_and_attend(layer, fox):
        mem = (mem_k, mem_v, layer, s_len) if fuse_mem else None
        tok_part, mem_part = _proj_in(h, wts["g_mix"][layer], wts["w_in"][layer], wts["g_fox_q_t"],
                                      wts["g_mem_q_t"][layer], mem, fox=fox, tm=tm, act=act)
        if not fuse_mem:
            mem_part = _mem_attn(mem_part.reshape(b, s_len, MEM_WIDTH), mem_k, mem_v, layer,
                                 bb=mem_bb, ts=mem_ts)
        return tok_part, mem_part.reshape(t, MEM_WIDTH)

    z_tok, mo = project_and_attend(0, fox=False)
    z_tok = z_tok.reshape(b, s_len, TOK_WIDTH)
    tok = _pool_mix(z_tok, pool_prev, wts["w_pool"], wts["pool_scale"],
                    n_prev=n_prev, bb=pool_bb, chunk=pool_chunk, act=act)
    h = _out_ffn(h, tok.reshape(t, TOK_WIDTH), mo, wts["w_out"][0],
                 wts["g_ffn"][0], wts["w_gu"][0], wts["w_down"][0], tm=tm, tf=D_FF // 2)

    kv_args = (h, wts["g_kv"], wts["w_kv"], wts["g_fox_k_t"], wts["w_fg_t"], wts["b_fg"])
    if past is None:
        k_new, v_new, k_bf, v_bf, logf_t, logf_new = _proj_kv_paged(*kv_args, tm=tm)
    else:
        k_new, v_new, logf_t = _proj_kv(*kv_args, tm=tm)
        logf_new = logf_t.T

    q, mo = project_and_attend(1, fox=True)
    if past is None:
        f_row = _cumsum_rows(logf_t, seg=s_len)
        n_pairs = FOX_HEADS // 2
        f_row = f_row.reshape(n_pairs, 2, t)
        tok = _fox_prompt(q.reshape(b, s_len, TOK_WIDTH), k_bf,
                          v_bf.reshape(b, s_len, TOK_WIDTH), f_row, blk=512)
    else:
        page_table, cache_k_t, cache_v_t, cache_logf_t = past
        lfn = logf_t.reshape(FOX_HEADS, b, s_len).transpose(1, 0, 2)
        lfn = jnp.pad(lfn, ((0, 0), (0, 0), (0, LANES - s_len)))
        per_row = lambda a: a.reshape(b, s_len, TOK_WIDTH)
        tok = _fox_paged(page_table, per_row(q), per_row(k_new), per_row(v_new), lfn,
                         cache_logf_t, cache_k_t, cache_v_t)
    h = _out_ffn(h, tok.reshape(t, TOK_WIDTH), mo, wts["w_out"][1],
                 wts["g_ffn"][1], wts["w_gu"][1], wts["w_down"][1], tm=tm, tf=D_FF // 2)
    return h.reshape(b, s_len, D_MODEL), z_tok, k_new, v_new, logf_new


def kernel(x_prompt, x_sample, cache_mem_k, cache_mem_v, state_pool, cache_k, cache_v, cache_logf,
           page_table, mem_prompt, g_mix, w_in, w_out, g_ffn, w_gu, w_down, g_mem, w_mem_kv,
           g_mem_q, g_mem_k, w_pool, pool_scale, g_kv, w_kv, g_fox_k, w_fg, b_fg, g_fox_q):
    depth = w_in.shape[0]
    b, s_len, _ = x_prompt.shape
    db, ds, _ = x_sample.shape
    wts = {
        "g_mix": g_mix.reshape(depth, 1, D_MODEL),
        "w_in": w_in.astype(BF16),
        "w_out": w_out.astype(BF16),
        "g_ffn": g_ffn.reshape(depth, 1, D_MODEL),
        "w_gu": w_gu.astype(BF16),
        "w_down": w_down.astype(BF16),
        "g_mem_q_t": jnp.tile(g_mem_q, (1, MEM_HEADS)).reshape(depth, 1, MEM_WIDTH),
        "w_pool": w_pool[0].astype(BF16),
        "pool_scale": pool_scale[0].reshape(1, TOK_WIDTH),
        "g_kv": g_kv.reshape(1, D_MODEL),
        "w_kv": w_kv.astype(BF16),
        "g_fox_k_t": jnp.tile(g_fox_k, FOX_HEADS).reshape(1, TOK_WIDTH),
        "w_fg_t": w_fg.T.astype(BF16),
        "b_fg": b_fg.reshape(FOX_HEADS, 1),
        "g_fox_q_t": jnp.tile(g_fox_q[0], FOX_HEADS).reshape(1, TOK_WIDTH),
    }

    mem_k_p, mem_v_p = _proj_memkv(mem_prompt.reshape(b * N_MEM, D_MODEL),
                                   g_mem.reshape(depth, 1, D_MODEL), w_mem_kv.astype(BF16),
                                   jnp.tile(g_mem_k, (1, MEM_HEADS)).reshape(depth, 1, MEM_WIDTH), tm=512)
    mem_rows = (N_MEM * MEM_HEADS, MEM_HEAD_DIM)
    mem_k_p = mem_k_p.reshape((depth, b) + mem_rows)
    mem_v_p = mem_v_p.reshape((depth, b) + mem_rows)
    y_p, ztok_p, k_p, v_p, logf_p = _decoder(
        x_prompt, jnp.zeros((b, POOL_HIST, TOK_WIDTH), F32), 0, mem_k_p, mem_v_p, None, wts,
        tm=512, pool_bb=1, pool_chunk=256, mem_bb=1, mem_ts=512, act=BF16)
    n_pp = s_len // PAGE_SIZE
    head_shape = (FOX_HEADS, FOX_HEAD_DIM)

    past = (page_table.T.reshape(-1),
            cache_k.transpose(0, 2, 3, 1), cache_v.transpose(0, 2, 3, 1),
            cache_logf.transpose(0, 2, 1))
    prev = jnp.pad(state_pool[0], ((0, 0), (POOL_HIST - POOL_STATE, 0), (0, 0)))
    y_s, ztok_s, k_s, v_s, logf_s = _decoder(
        x_sample, prev, POOL_STATE, cache_mem_k.reshape((depth, db) + mem_rows),
        cache_mem_v.reshape((depth, db) + mem_rows), past, wts,
        tm=512, pool_bb=32, pool_chunk=ds, mem_bb=8, mem_ts=ds, act=F32)

    pool_state_p = ztok_p[:, s_len - POOL_STATE:][None]
    pool_state_s = jnp.concatenate([state_pool[0], ztok_s], axis=1)[:, -POOL_STATE:][None]
    paged = lambda a: a.reshape((b, n_pp) + head_shape + (PAGE_SIZE,)).transpose(0, 1, 4, 2, 3)
    return (y_p, y_s, paged(k_p), paged(v_p),
            logf_p.reshape(b, n_pp, FOX_HEADS, PAGE_SIZE).transpose(0, 1, 3, 2),
            mem_k_p.reshape(depth, b, N_MEM, MEM_HEADS, MEM_HEAD_DIM),
            mem_v_p.reshape(depth, b, N_MEM, MEM_HEADS, MEM_HEAD_DIM),
            pool_state_p,
            k_s.reshape((db, ds) + head_shape),
            v_s.reshape((db, ds) + head_shape),
            logf_s.reshape(db, ds, FOX_HEADS),
            pool_state_s)
```

```python
import functools

import jax
import jax.numpy as jnp
from jax import lax
from jax.experimental import pallas as pl
from jax.experimental.pallas import tpu as pltpu

D_MODEL = 1024
TOK_WIDTH = 512
MEM_WIDTH = 512
POOL_WINDOWS = (2, 4, 8, 16)
POOL_GROUP = 128
POOL_STATE = 15
POOL_HIST = 16
FOX_HEADS = 8
FOX_HEAD_DIM = 64
MEM_HEADS = 4
MEM_HEAD_DIM = 128
N_MEM = 256
D_FF = 2816
PAGE_SIZE = 128
EPS = 1e-6
LANES = 128
BF16_ROWS = 16
VMEM_LIMIT = 56 * 1024 * 1024
FOX_CHUNK_ROWS = 64
PAGE_SLOTS = 3

F32 = jnp.float32
BF16 = jnp.bfloat16
NT_DIMS = (((1,), (1,)), ((), ()))


def _rmsnorm(x, g):
    return x * lax.rsqrt(jnp.mean(x * x, axis=-1, keepdims=True) + EPS) * g


def _headnorm128(z, g):
    outs = []
    for h in range(z.shape[-1] // LANES):
        zh = z[:, h * LANES:(h + 1) * LANES]
        r = lax.rsqrt(jnp.mean(zh * zh, axis=-1, keepdims=True) + EPS)
        outs.append(zh * r * g[:, h * LANES:(h + 1) * LANES])
    return jnp.concatenate(outs, axis=-1)


def _headnorm64(z, g):
    outs = []
    for p in range(z.shape[-1] // LANES):
        zp = z[:, p * LANES:(p + 1) * LANES]
        sq = zp * zp
        lane = lax.broadcasted_iota(jnp.int32, zp.shape, 1)
        lo = lane < FOX_HEAD_DIM
        s_lo = jnp.sum(jnp.where(lo, sq, 0.0), axis=-1, keepdims=True)
        s_hi = jnp.sum(jnp.where(lo, 0.0, sq), axis=-1, keepdims=True)
        r = jnp.where(lo, lax.rsqrt(s_lo / FOX_HEAD_DIM + EPS), lax.rsqrt(s_hi / FOX_HEAD_DIM + EPS))
        outs.append(zp * r * g[:, p * LANES:(p + 1) * LANES])
    return jnp.concatenate(outs, axis=-1)


def _mem_attend(q, k_ref, v_ref):
    outs = []
    for h in range(MEM_HEADS):
        qh = q[:, h * MEM_HEAD_DIM:(h + 1) * MEM_HEAD_DIM].astype(BF16)
        k = k_ref[pl.ds(h, N_MEM, stride=MEM_HEADS), :].astype(BF16)
        v = v_ref[pl.ds(h, N_MEM, stride=MEM_HEADS), :].astype(BF16)
        s = lax.dot_general(qh, k, NT_DIMS, preferred_element_type=F32) * (MEM_HEAD_DIM ** -0.5)
        p = jnp.exp(s - jnp.max(s, axis=-1, keepdims=True))
        o = jnp.dot(p.astype(BF16), v, preferred_element_type=F32)
        outs.append(o / jnp.sum(p, axis=-1, keepdims=True))
    return jnp.concatenate(outs, axis=-1)


def _proj_in_kernel(x_ref, g_ref, w_ref, gtok_ref, gmem_ref, *refs, fox, fuse_mem):
    xn = _rmsnorm(x_ref[...], g_ref[...]).astype(BF16)
    z = jnp.dot(xn, w_ref[...], preferred_element_type=F32)
    z_tok = z[:, :TOK_WIDTH]
    qm = _headnorm128(z[:, TOK_WIDTH:], gmem_ref[...])
    if fuse_mem:
        mk_ref, mv_ref, tok_ref, mem_ref = refs
        mem_ref[...] = _mem_attend(qm, mk_ref, mv_ref).astype(mem_ref.dtype)
    else:
        tok_ref, mem_ref = refs
        mem_ref[...] = qm.astype(mem_ref.dtype)
    if fox:
        tok_ref[...] = (_headnorm64(z_tok, gtok_ref[...]) * (FOX_HEAD_DIM ** -0.5)).astype(tok_ref.dtype)
    else:
        tok_ref[...] = z_tok


def _proj_in(x, g, w, g_tok, g_mem, mem=None, *, fox, tm, act):
    t = x.shape[0]
    row = lambda i: (i, 0)
    fixed = lambda i: (0, 0)
    operands = [x, g, w, g_tok, g_mem]
    in_specs = [pl.BlockSpec((tm, D_MODEL), row),
                pl.BlockSpec((1, D_MODEL), fixed),
                pl.BlockSpec((D_MODEL, D_MODEL), fixed),
                pl.BlockSpec((1, TOK_WIDTH), fixed),
                pl.BlockSpec((1, MEM_WIDTH), fixed)]
    if mem is not None:
        mk, mv, layer, rows_per_batch = mem
        tiles = rows_per_batch // tm
        mem_block = pl.BlockSpec((None, None, N_MEM * MEM_HEADS, MEM_HEAD_DIM),
                                 lambda i: (layer, i // tiles, 0, 0))
        operands += [mk, mv]
        in_specs += [mem_block, mem_block]
    return pl.pallas_call(
        functools.partial(_proj_in_kernel, fox=fox, fuse_mem=mem is not None),
        out_shape=(jax.ShapeDtypeStruct((t, TOK_WIDTH), act if fox else F32),
                   jax.ShapeDtypeStruct((t, MEM_WIDTH), act)),
        grid=(t // tm,),
        in_specs=in_specs,
        out_specs=(pl.BlockSpec((tm, TOK_WIDTH), row), pl.BlockSpec((tm, MEM_WIDTH), row)),
        compiler_params=pltpu.CompilerParams(dimension_semantics=("arbitrary",),
                                             vmem_limit_bytes=VMEM_LIMIT),
        name="proj_in_fox" if fox else "proj_in_pool",
    )(*operands)


def _log_sigmoid(x):
    return jnp.minimum(x, 0.0) - jnp.log(1.0 + jnp.exp(-jnp.abs(x)))


def _proj_kv_kernel(x_ref, g_ref, w_ref, gk_ref, wfg_ref, bfg_ref, k_ref, v_ref, logf_ref):
    xn = _rmsnorm(x_ref[...], g_ref[...]).astype(BF16)
    z = jnp.dot(xn, w_ref[...], preferred_element_type=F32)
    k_ref[...] = _headnorm64(z[:, :TOK_WIDTH], gk_ref[...])
    v_ref[...] = z[:, TOK_WIDTH:]
    gate = lax.dot_general(wfg_ref[...], xn, NT_DIMS, preferred_element_type=F32) + bfg_ref[...]
    logf_ref[...] = _log_sigmoid(gate)


def _proj_kv_paged_kernel(x_ref, g_ref, w_ref, gk_ref, wfg_ref, bfg_ref,
                          kp_ref, vp_ref, kb_ref, vb_ref, logf_ref, logfp_ref):
    xn = _rmsnorm(x_ref[...], g_ref[...]).astype(BF16)
    z = jnp.dot(xn, w_ref[...], preferred_element_type=F32)
    k = _headnorm64(z[:, :TOK_WIDTH], gk_ref[...])
    v = z[:, TOK_WIDTH:]
    vb_ref[...] = v.astype(BF16)
    gate = lax.dot_general(wfg_ref[...], xn, NT_DIMS, preferred_element_type=F32) + bfg_ref[...]
    logf = _log_sigmoid(gate)
    logf_ref[...] = logf
    k_t = k.T
    v_t = v.T
    kb_ref[...] = k_t.astype(BF16)
    for pg in range(kp_ref.shape[0]):
        rows = slice(pg * PAGE_SIZE, (pg + 1) * PAGE_SIZE)
        kp_ref[pg] = k_t[:, rows]
        vp_ref[pg] = v_t[:, rows]
        logfp_ref[pg] = logf[:, rows]


def _proj_kv_paged(x, g, w, g_k, wfg_t, bfg, *, tm):
    t = x.shape[0]
    row = lambda i: (i, 0)
    fixed = lambda i: (0, 0)
    ppt = tm // PAGE_SIZE
    page_rows = lambda i: (i, 0, 0)
    return pl.pallas_call(
        _proj_kv_paged_kernel,
        out_shape=(jax.ShapeDtypeStruct((t // PAGE_SIZE, TOK_WIDTH, PAGE_SIZE), F32),
                   jax.ShapeDtypeStruct((t // PAGE_SIZE, TOK_WIDTH, PAGE_SIZE), F32),
                   jax.ShapeDtypeStruct((TOK_WIDTH, t), BF16),
                   jax.ShapeDtypeStruct((t, TOK_WIDTH), BF16),
                   jax.ShapeDtypeStruct((FOX_HEADS, t), F32),
                   jax.ShapeDtypeStruct((t // PAGE_SIZE, FOX_HEADS, PAGE_SIZE), F32)),
        grid=(t // tm,),
        in_specs=[pl.BlockSpec((tm, D_MODEL), row),
                  pl.BlockSpec((1, D_MODEL), fixed),
                  pl.BlockSpec((D_MODEL, D_MODEL), fixed),
                  pl.BlockSpec((1, TOK_WIDTH), fixed),
                  pl.BlockSpec((FOX_HEADS, D_MODEL), fixed),
                  pl.BlockSpec((FOX_HEADS, 1), fixed)],
        out_specs=(pl.BlockSpec((ppt, TOK_WIDTH, PAGE_SIZE), page_rows),
                   pl.BlockSpec((ppt, TOK_WIDTH, PAGE_SIZE), page_rows),
                   pl.BlockSpec((TOK_WIDTH, tm), lambda i: (0, i)), pl.BlockSpec((tm, TOK_WIDTH), row),
                   pl.BlockSpec((FOX_HEADS, tm), lambda i: (0, i)),
                   pl.BlockSpec((ppt, FOX_HEADS, PAGE_SIZE), page_rows)),
        compiler_params=pltpu.CompilerParams(dimension_semantics=("arbitrary",),
                                             vmem_limit_bytes=VMEM_LIMIT),
        name="proj_kv_paged",
    )(x, g, w, g_k, wfg_t, bfg)


def _proj_kv(x, g, w, g_k, wfg_t, bfg, *, tm):
    t = x.shape[0]
    row = lambda i: (i, 0)
    fixed = lambda i: (0, 0)
    return pl.pallas_call(
        _proj_kv_kernel,
        out_shape=(jax.ShapeDtypeStruct((t, TOK_WIDTH), F32),
                   jax.ShapeDtypeStruct((t, TOK_WIDTH), F32),
                   jax.ShapeDtypeStruct((FOX_HEADS, t), F32)),
        grid=(t // tm,),
        in_specs=[pl.BlockSpec((tm, D_MODEL), row),
                  pl.BlockSpec((1, D_MODEL), fixed),
                  pl.BlockSpec((D_MODEL, D_MODEL), fixed),
                  pl.BlockSpec((1, TOK_WIDTH), fixed),
                  pl.BlockSpec((FOX_HEADS, D_MODEL), fixed),
                  pl.BlockSpec((FOX_HEADS, 1), fixed)],
        out_specs=(pl.BlockSpec((tm, TOK_WIDTH), row), pl.BlockSpec((tm, TOK_WIDTH), row),
                   pl.BlockSpec((FOX_HEADS, tm), lambda i: (0, i))),
        compiler_params=pltpu.CompilerParams(dimension_semantics=("arbitrary",),
                                             vmem_limit_bytes=VMEM_LIMIT),
        name="proj_kv",
    )(x, g, w, g_k, wfg_t, bfg)


def _proj_memkv_kernel(x_ref, g_ref, w_ref, gk_ref, k_ref, v_ref):
    xn = _rmsnorm(x_ref[...], g_ref[0]).astype(BF16)
    z = jnp.dot(xn, w_ref[0], preferred_element_type=F32)
    k = _headnorm128(z[:, :MEM_WIDTH], gk_ref[0])
    tm = x_ref.shape[0]
    for h in range(MEM_HEADS):
        rows = pl.ds(h, tm, stride=MEM_HEADS)
        k_ref[0, rows, :] = k[:, h * MEM_HEAD_DIM:(h + 1) * MEM_HEAD_DIM]
        v_ref[0, rows, :] = z[:, MEM_WIDTH + h * MEM_HEAD_DIM:MEM_WIDTH + (h + 1) * MEM_HEAD_DIM]


def _proj_memkv(x, g, w, g_k, *, tm):
    t = x.shape[0]
    n_layers = w.shape[0]
    out_block = pl.BlockSpec((1, tm * MEM_HEADS, MEM_HEAD_DIM), lambda l, i: (l, i, 0))
    return pl.pallas_call(
        _proj_memkv_kernel,
        out_shape=(jax.ShapeDtypeStruct((n_layers, t * MEM_HEADS, MEM_HEAD_DIM), F32),
                   jax.ShapeDtypeStruct((n_layers, t * MEM_HEADS, MEM_HEAD_DIM), F32)),
        grid=(n_layers, t // tm),
        in_specs=[pl.BlockSpec((tm, D_MODEL), lambda l, i: (i, 0)),
                  pl.BlockSpec((1, 1, D_MODEL), lambda l, i: (l, 0, 0)),
                  pl.BlockSpec((1, D_MODEL, D_MODEL), lambda l, i: (l, 0, 0)),
                  pl.BlockSpec((1, 1, MEM_WIDTH), lambda l, i: (l, 0, 0))],
        out_specs=(out_block, out_block),
        compiler_params=pltpu.CompilerParams(dimension_semantics=("arbitrary", "arbitrary"),
                                             vmem_limit_bytes=VMEM_LIMIT),
        name="proj_memkv",
    )(x, g, w, g_k)


def _pool_kernel(u_ref, prev_ref, w_ref, scale_ref, o_ref, ext_ref, *, n_prev, chunk):
    bb, s_len, _ = u_ref.shape
    ext_ref[:, 0:POOL_HIST, :] = prev_ref[...]
    ext_ref[:, POOL_HIST:, :] = u_ref[...]
    for c in range(s_len // chunk):
        r0 = POOL_HIST + c * chunk
        pos = c * chunk + lax.broadcasted_iota(jnp.int32, (1, chunk, 1), 1)
        for gi, win in enumerate(POOL_WINDOWS):
            lanes = slice(gi * POOL_GROUP, (gi + 1) * POOL_GROUP)
            u_new = ext_ref[:, r0:r0 + chunk, lanes]
            acc = u_new
            for k in range(1, win):
                acc = acc + ext_ref[:, r0 - k:r0 - k + chunk, lanes]
            count = jnp.minimum(win, n_prev + pos + 1).astype(F32)
            y = acc / count - u_new
            z = jnp.dot(y.reshape(bb * chunk, POOL_GROUP).astype(BF16), w_ref[gi],
                        preferred_element_type=F32)
            z = z.reshape(bb, chunk, POOL_GROUP) * scale_ref[:, lanes]
            o_ref[:, c * chunk:(c + 1) * chunk, lanes] = z.astype(o_ref.dtype)


def _pool_mix(u, prev, w_pool, scale, *, n_prev, bb, chunk, act):
    b, s_len, _ = u.shape
    return pl.pallas_call(
        functools.partial(_pool_kernel, n_prev=n_prev, chunk=chunk),
        out_shape=jax.ShapeDtypeStruct((b, s_len, TOK_WIDTH), act),
        grid=(b // bb,),
        in_specs=[pl.BlockSpec((bb, s_len, TOK_WIDTH), lambda i: (i, 0, 0)),
                  pl.BlockSpec((bb, POOL_HIST, TOK_WIDTH), lambda i: (i, 0, 0)),
                  pl.BlockSpec((len(POOL_WINDOWS), POOL_GROUP, POOL_GROUP), lambda i: (0, 0, 0)),
                  pl.BlockSpec((1, TOK_WIDTH), lambda i: (0, 0))],
        out_specs=pl.BlockSpec((bb, s_len, TOK_WIDTH), lambda i: (i, 0, 0)),
        scratch_shapes=[pltpu.VMEM((bb, POOL_HIST + s_len, TOK_WIDTH), F32)],
        compiler_params=pltpu.CompilerParams(dimension_semantics=("arbitrary",),
                                             vmem_limit_bytes=VMEM_LIMIT),
        name="pool_mix",
    )(u, prev, w_pool, scale)


def _mem_attn_kernel(q_ref, k_ref, v_ref, o_ref):
    for h in range(MEM_HEADS):
        lanes = slice(h * MEM_HEAD_DIM, (h + 1) * MEM_HEAD_DIM)
        q = q_ref[:, :, lanes].astype(BF16)
        k = k_ref[:, pl.ds(h, N_MEM, stride=MEM_HEADS), :].astype(BF16)
        v = v_ref[:, pl.ds(h, N_MEM, stride=MEM_HEADS), :].astype(BF16)
        s = jnp.einsum("bqd,bkd->bqk", q, k, preferred_element_type=F32) * (MEM_HEAD_DIM ** -0.5)
        m = jnp.max(s, axis=-1, keepdims=True)
        p = jnp.exp(s - m)
        l = jnp.sum(p, axis=-1, keepdims=True)
        o = jnp.einsum("bqk,bkd->bqd", p.astype(BF16), v, preferred_element_type=F32)
        o_ref[:, :, lanes] = (o / l).astype(o_ref.dtype)


def _mem_attn(q, mk, mv, layer, *, bb, ts):
    b, s_len, _ = q.shape
    mem_block = pl.BlockSpec((None, bb, N_MEM * MEM_HEADS, MEM_HEAD_DIM),
                             lambda i, j: (layer, i, 0, 0))
    return pl.pallas_call(
        _mem_attn_kernel,
        out_shape=jax.ShapeDtypeStruct((b, s_len, MEM_WIDTH), q.dtype),
        grid=(b // bb, s_len // ts),
        in_specs=[pl.BlockSpec((bb, ts, MEM_WIDTH), lambda i, j: (i, j, 0)), mem_block, mem_block],
        out_specs=pl.BlockSpec((bb, ts, MEM_WIDTH), lambda i, j: (i, j, 0)),
        compiler_params=pltpu.CompilerParams(dimension_semantics=("arbitrary", "arbitrary"),
                                             vmem_limit_bytes=VMEM_LIMIT),
        name="mem_attn",
    )(q, mk, mv)


def _out_ffn_kernel(h_ref, tok_ref, mo_ref, wout_ref, g_ref, wg_ref, wu_ref, wd_ref,
                    o_ref, xn_ref, acc_ref):
    kf = pl.program_id(1)

    @pl.when(kf == 0)
    def _():
        h1 = (h_ref[...]
              + jnp.dot(tok_ref[...].astype(BF16), wout_ref[:TOK_WIDTH, :], preferred_element_type=F32)
              + jnp.dot(mo_ref[...].astype(BF16), wout_ref[TOK_WIDTH:, :], preferred_element_type=F32))
        acc_ref[...] = h1
        xn_ref[...] = _rmsnorm(h1, g_ref[...]).astype(BF16)

    x = xn_ref[...]
    gate = jnp.dot(x, wg_ref[...], preferred_element_type=F32)
    up = jnp.dot(x, wu_ref[...], preferred_element_type=F32)
    act = (gate * (1.0 / (1.0 + jnp.exp(-gate))) * up).astype(BF16)
    acc_ref[...] += jnp.dot(act, wd_ref[...], preferred_element_type=F32)

    @pl.when(kf == pl.num_programs(1) - 1)
    def _():
        o_ref[...] = acc_ref[...]


def _out_ffn(h, tok, mo, w_out, g, w_gu, w_down, *, tm, tf):
    t = h.shape[0]
    nf = D_FF // tf
    row = lambda i, k: (i, 0)
    fixed = lambda i, k: (0, 0)
    once = pl.Buffered(1)
    ffn_mode = once if nf == 1 else None
    return pl.pallas_call(
        _out_ffn_kernel,
        out_shape=jax.ShapeDtypeStruct((t, D_MODEL), F32),
        grid=(t // tm, nf),
        in_specs=[pl.BlockSpec((tm, D_MODEL), row),
                  pl.BlockSpec((tm, TOK_WIDTH), row),
                  pl.BlockSpec((tm, MEM_WIDTH), row),
                  pl.BlockSpec((D_MODEL, D_MODEL), fixed, pipeline_mode=once),
                  pl.BlockSpec((1, D_MODEL), fixed),
                  pl.BlockSpec((D_MODEL, tf), lambda i, k: (0, k), pipeline_mode=ffn_mode),
                  pl.BlockSpec((D_MODEL, tf), lambda i, k: (0, nf + k), pipeline_mode=ffn_mode),
                  pl.BlockSpec((tf, D_MODEL), lambda i, k: (k, 0), pipeline_mode=ffn_mode)],
        out_specs=pl.BlockSpec((tm, D_MODEL), row),
        scratch_shapes=[pltpu.VMEM((tm, D_MODEL), BF16), pltpu.VMEM((tm, D_MODEL), F32)],
        compiler_params=pltpu.CompilerParams(dimension_semantics=("arbitrary", "arbitrary"),
                                             vmem_limit_bytes=VMEM_LIMIT),
        name="out_ffn",
    )(h, tok, mo, w_out, g, w_gu, w_gu, w_down)


def _lane_cumsum(x):
    n = x.shape[-1]
    lane = lax.broadcasted_iota(jnp.int32, x.shape, 1)
    k = 1
    while k < n:
        x = x + jnp.where(lane >= k, pltpu.roll(x, k, axis=1), 0.0)
        k *= 2
    return x


def _cumsum_kernel(x_ref, o_ref):
    o_ref[...] = _lane_cumsum(x_ref[...])


def _cumsum_rows(x, *, seg):
    rows, t = x.shape
    return pl.pallas_call(
        _cumsum_kernel,
        out_shape=jax.ShapeDtypeStruct((rows, t), F32),
        grid=(t // seg,),
        in_specs=[pl.BlockSpec((rows, seg), lambda i: (0, i))],
        out_specs=pl.BlockSpec((rows, seg), lambda i: (0, i)),
        compiler_params=pltpu.CompilerParams(dimension_semantics=("arbitrary",)),
        name="logf_cumsum",
    )(x)


def _round_to_bf16(x):
    return x.astype(BF16).astype(F32)


def _fox_prompt_kernel(q_ref, k_ref, v_ref, frow_ref, o_ref, m_ref, l_ref, acc_ref, alpha_ref,
                       s_ref, p_all_ref, *, blk, n_q_blocks):
    qi = pl.program_id(2)
    q = q_ref[0]
    lane = lax.broadcasted_iota(jnp.int32, q.shape, 1)
    sub = FOX_CHUNK_ROWS
    n_rep = blk // LANES
    bias_lane = [FOX_HEAD_DIM, 0]
    q_f32 = q.astype(F32)
    q_head = []
    for j in range(2):
        own = (lane < FOX_HEAD_DIM) if j == 0 else (lane >= FOX_HEAD_DIM)
        ones_from = jnp.where(lane >= bias_lane[j], 1.0, 0.0)
        bias_ones = jnp.where(lane < bias_lane[j] + 3, ones_from, 0.0)
        q_head.append(jnp.where(own, q_f32, bias_ones).astype(BF16))
    m_ref[...] = jnp.full(m_ref.shape, -jnp.inf, F32)
    l_ref[...] = jnp.zeros_like(l_ref)
    acc_ref[...] = jnp.zeros_like(acc_ref)

    def scores(qc, ki, s_buf):
        keys = slice(ki * blk, (ki + 1) * blk)
        kt = k_ref[:, keys]
        tile_row = lax.broadcasted_iota(jnp.int32, (BF16_ROWS, blk), 0)
        for j in range(2):
            f_end = frow_ref[0, j:j + 1, (qc + 1) * blk - 1:(qc + 1) * blk]
            bias = f_end - frow_ref[0, j:j + 1, keys]
            hi = _round_to_bf16(bias)
            mid = _round_to_bf16(bias - hi)
            lo = _round_to_bf16(bias - hi - mid)
            bias_rows = jnp.where(tile_row == 0, hi, jnp.where(tile_row == 1, mid,
                                                               jnp.where(tile_row == 2, lo, 0.0)))
            b0 = bias_lane[j]
            pieces = [kt[:b0], bias_rows.astype(BF16), kt[b0 + BF16_ROWS:]]
            kt_j = jnp.concatenate([x for x in pieces if x.shape[0]], axis=0)
            s_buf[j] = jnp.dot(q_head[j], kt_j, preferred_element_type=F32)

    def update(ki, s_buf, masked):
        p_ref = p_all_ref.at[ki]
        vt = v_ref[0, ki * blk:(ki + 1) * blk, :]
        for j in range(2):
            def chunk(r):
                n_use = -(-(r + 1) * sub // LANES) if masked else n_rep
                rows = slice(r * sub, (r + 1) * sub)
                s = s_buf[j, rows, 0:n_use * LANES]
                if masked:
                    row = lax.broadcasted_iota(jnp.int32, s.shape, 0) + r * sub
                    col = lax.broadcasted_iota(jnp.int32, s.shape, 1)
                    s = jnp.where(col <= row, s, -jnp.inf)
                return rows, n_use, s

            for r in range(blk // sub):
                rows, n_use, s = chunk(r)
                m_prev = m_ref[j, rows]
                m_new = jnp.maximum(m_prev, jnp.max(s, axis=-1, keepdims=True))
                alpha_ref[j, rows] = jnp.exp(m_prev - m_new)
                m_ref[j, rows] = m_new
            for r in range(blk // sub):
                rows, n_use, s = chunk(r)
                p = jnp.exp(s - jnp.concatenate([m_ref[j, rows]] * n_use, axis=1))
                p_lanes = p[:, :LANES]
                for c in range(1, n_use):
                    p_lanes = p_lanes + p[:, c * LANES:(c + 1) * LANES]
                l_ref[j, rows] = alpha_ref[j, rows] * l_ref[j, rows] + p_lanes
                p_ref[j, rows, 0:n_use * LANES] = p.astype(BF16)
                if n_use < n_rep:
                    p_ref[j, rows, n_use * LANES:] = jnp.zeros((sub, blk - n_use * LANES), BF16)
            acc_ref[j] = alpha_ref[j] * acc_ref[j] + jnp.dot(p_ref[j], vt, preferred_element_type=F32)

    for qc in range(n_q_blocks):
        @pl.when(qi == qc)
        def _(qc=qc):
            scores(qc, 0, s_ref.at[0])
            for ki in range(qc):
                scores(qc, ki + 1, s_ref.at[ki + 1])
                update(ki, s_ref.at[ki], False)
            update(qc, s_ref.at[qc], True)

    l0 = jnp.sum(l_ref[0], axis=-1, keepdims=True)
    l1 = jnp.sum(l_ref[1], axis=-1, keepdims=True)
    o = jnp.where(lane < FOX_HEAD_DIM, acc_ref[0] / l0, acc_ref[1] / l1)
    o_ref[0] = o.astype(o_ref.dtype)


def _fox_prompt(q, k, v, frow, *, blk):
    b, s_len, _ = q.shape
    n_pairs = TOK_WIDTH // LANES
    nq = s_len // blk
    return pl.pallas_call(
        functools.partial(_fox_prompt_kernel, blk=blk, n_q_blocks=nq),
        out_shape=jax.ShapeDtypeStruct((b, s_len, TOK_WIDTH), BF16),
        grid=(b, n_pairs, nq),
        in_specs=[pl.BlockSpec((1, blk, LANES), lambda bi, hp, qi: (bi, qi, hp)),
                  pl.BlockSpec((LANES, s_len), lambda bi, hp, qi: (hp, bi)),
                  pl.BlockSpec((1, s_len, LANES), lambda bi, hp, qi: (bi, 0, hp)),
                  pl.BlockSpec((1, 2, s_len), lambda bi, hp, qi: (hp, 0, bi))],
        out_specs=pl.BlockSpec((1, blk, LANES), lambda bi, hp, qi: (bi, qi, hp)),
        scratch_shapes=[pltpu.VMEM((2, blk, LANES), F32), pltpu.VMEM((2, blk, LANES), F32),
                        pltpu.VMEM((2, blk, LANES), F32), pltpu.VMEM((2, blk, LANES), F32),
                        pltpu.VMEM((nq, 2, blk, blk), F32),
                        pltpu.VMEM((nq, 2, blk, blk), BF16)],
        compiler_params=pltpu.CompilerParams(
            dimension_semantics=("arbitrary", "arbitrary", "arbitrary"),
            vmem_limit_bytes=VMEM_LIMIT),
        name="fox_prompt",
    )(q, k, v, frow)


def _lane_suffix_sum(x):
    n = x.shape[-1]
    lane = lax.broadcasted_iota(jnp.int32, x.shape, 1)
    k = 1
    while k < n:
        x = x + jnp.where(lane + k < n, pltpu.roll(x, n - k, axis=1), 0.0)
        k *= 2
    return x


def _fox_paged_kernel(pt_ref, q_ref, kn_ref, vn_ref, lfn_ref, logf_ref, k_hbm, v_hbm, o_ref,
                      kbuf, vbuf, sem, pad_k_ref, pad_v_ref, *, n_pages, n_batch):
    b = pl.program_id(0)
    n_steps = pl.num_programs(0)
    slot = lax.rem(b, PAGE_SLOTS)
    n_q = q_ref.shape[1]
    n_rows = n_q * FOX_HEADS

    def page_copies(batch, slot_):
        copies = []
        for j in range(n_pages):
            page = pt_ref[j * n_batch + batch]
            copies.append(pltpu.make_async_copy(k_hbm.at[page], kbuf.at[slot_, j], sem.at[0, slot_]))
            copies.append(pltpu.make_async_copy(v_hbm.at[page], vbuf.at[slot_, j], sem.at[1, slot_]))
        return copies

    @pl.when(b == 0)
    def _():
        for c in page_copies(0, 0):
            c.start()

    @pl.when(jnp.logical_and(b == 0, n_steps > 1))
    def _():
        for c in page_copies(1, 1):
            c.start()

    @pl.when(b + 2 < n_steps)
    def _():
        for c in page_copies(b + 2, lax.rem(b + 2, PAGE_SLOTS)):
            c.start()

    head = lax.broadcasted_iota(jnp.int32, (FOX_HEADS, TOK_WIDTH), 0)
    lane_head = jnp.right_shift(lax.broadcasted_iota(jnp.int32, (FOX_HEADS, TOK_WIDTH), 1), 6)
    own_head = head == lane_head
    tile_q = lambda x: jnp.concatenate([x] * n_q, axis=0)

    q = q_ref[0]
    qblk = jnp.concatenate(
        [jnp.where(own_head, jnp.broadcast_to(q[t:t + 1, :], (FOX_HEADS, TOK_WIDTH)), 0.0)
         for t in range(n_q)], axis=0).astype(BF16)

    pad_k_ref[...] = jnp.zeros_like(pad_k_ref)
    pad_v_ref[...] = jnp.zeros_like(pad_v_ref)
    pad_k_ref[0:n_q, :] = kn_ref[0]
    pad_v_ref[0:n_q, :] = vn_ref[0]
    c_new = _lane_cumsum(lfn_ref[0])
    cq = jnp.concatenate([c_new[:, t:t + 1] for t in range(n_q)], axis=0)
    s_own = lax.dot_general(qblk, pad_k_ref[...].astype(BF16), NT_DIMS, preferred_element_type=F32)
    t_of_row = jnp.right_shift(lax.broadcasted_iota(jnp.int32, (n_rows, PAGE_SIZE), 0), 3)
    key = lax.broadcasted_iota(jnp.int32, (n_rows, PAGE_SIZE), 1)
    s_own = jnp.where(key <= t_of_row, s_own + (tile_q(-c_new) + cq), -jnp.inf)

    logf = [logf_ref[pt_ref[j * n_batch + b]] for j in range(n_pages)]
    incl = _lane_suffix_sum(jnp.concatenate(logf, axis=0))
    later_pages = jnp.zeros((FOX_HEADS, 1), F32)
    bias = [None] * n_pages
    for j in reversed(range(n_pages)):
        rows = slice(j * FOX_HEADS, (j + 1) * FOX_HEADS)
        bias[j] = tile_q(later_pages + (incl[rows] - logf[j])) + cq
        later_pages = later_pages + incl[rows, 0:1]

    for c in page_copies(b, slot):
        c.wait()

    scores = [s_own]
    for j in range(n_pages):
        k_t = kbuf[slot, j].reshape(TOK_WIDTH, PAGE_SIZE).astype(BF16)
        scores.append(jnp.dot(qblk, k_t, preferred_element_type=F32) + bias[j])
    s_max = scores[0]
    for s in scores[1:]:
        s_max = jnp.maximum(s_max, s)
    m = jnp.max(s_max, axis=-1, keepdims=True)
    p = jnp.exp(scores[0] - m)
    p_sum = p
    acc = jnp.dot(p.astype(BF16), pad_v_ref[...].astype(BF16), preferred_element_type=F32)
    for j in range(n_pages):
        p = jnp.exp(scores[j + 1] - m)
        p_sum = p_sum + p
        v_t = vbuf[slot, j].reshape(TOK_WIDTH, PAGE_SIZE).astype(BF16)
        acc = acc + lax.dot_general(p.astype(BF16), v_t, NT_DIMS, preferred_element_type=F32)
    o = acc / jnp.sum(p_sum, axis=-1, keepdims=True)
    for t in range(n_q):
        o_t = jnp.where(own_head, o[t * FOX_HEADS:(t + 1) * FOX_HEADS, :], 0.0)
        o_ref[0, t:t + 1, :] = jnp.sum(o_t, axis=0, keepdims=True)


def _fox_paged(page_table, q, k_new, v_new, lfn, cache_logf_t, cache_k_t, cache_v_t):
    b, n_q, _ = q.shape
    n_pages = page_table.shape[0] // b
    cur = lambda bi, pt: (bi, 0, 0)
    page_shape = cache_k_t.shape[1:]
    grid_spec = pltpu.PrefetchScalarGridSpec(
        num_scalar_prefetch=1,
        grid=(b,),
        in_specs=[pl.BlockSpec((1, n_q, TOK_WIDTH), cur),
                  pl.BlockSpec((1, n_q, TOK_WIDTH), cur),
                  pl.BlockSpec((1, n_q, TOK_WIDTH), cur),
                  pl.BlockSpec((1, FOX_HEADS, LANES), cur),
                  pl.BlockSpec(cache_logf_t.shape, lambda bi, pt: (0, 0, 0)),
                  pl.BlockSpec(memory_space=pl.ANY),
                  pl.BlockSpec(memory_space=pl.ANY)],
        out_specs=pl.BlockSpec((1, n_q, TOK_WIDTH), cur),
        scratch_shapes=[pltpu.VMEM((PAGE_SLOTS, n_pages) + page_shape, F32),
                        pltpu.VMEM((PAGE_SLOTS, n_pages) + page_shape, F32),
                        pltpu.SemaphoreType.DMA((2, PAGE_SLOTS)),
                        pltpu.VMEM((PAGE_SIZE, TOK_WIDTH), F32),
                        pltpu.VMEM((PAGE_SIZE, TOK_WIDTH), F32)])
    return pl.pallas_call(
        functools.partial(_fox_paged_kernel, n_pages=n_pages, n_batch=b),
        out_shape=jax.ShapeDtypeStruct((b, n_q, TOK_WIDTH), F32),
        grid_spec=grid_spec,
        compiler_params=pltpu.CompilerParams(dimension_semantics=("arbitrary",),
                                             vmem_limit_bytes=VMEM_LIMIT),
        name="fox_paged",
    )(page_table, q, k_new, v_new, lfn, cache_logf_t, cache_k_t, cache_v_t)


def _decoder(x, pool_prev, n_prev, mem_k, mem_v, past, wts, *, tm, pool_bb, pool_chunk, mem_bb, mem_ts,
             act):
    b, s_len, _ = x.shape
    t = b * s_len
    h = x.reshape(t, D_MODEL)

    fuse_mem = s_len % tm == 0

    def project_and_attend(layer, fox):
        mem = (mem_k, mem_v, layer, s_len) if fuse_mem else None
        tok_part, mem_part = _proj_in(h, wts["g_mix"][layer], wts["w_in"][layer], wts["g_fox_q_t"],
                                      wts["g_mem_q_t"][layer], mem, fox=fox, tm=tm, act=act)
        if not fuse_mem:
            mem_part = _mem_attn(mem_part.reshape(b, s_len, MEM_WIDTH), mem_k, mem_v, layer,
                                 bb=mem_bb, ts=mem_ts)
        return tok_part, mem_part.reshape(t, MEM_WIDTH)

    z_tok, mo = project_and_attend(0, fox=False)
    z_tok = z_tok.reshape(b, s_len, TOK_WIDTH)
    tok = _pool_mix(z_tok, pool_prev, wts["w_pool"], wts["pool_scale"],
                    n_prev=n_prev, bb=pool_bb, chunk=pool_chunk, act=act)
    h = _out_ffn(h, tok.reshape(t, TOK_WIDTH), mo, wts["w_out"][0],
                 wts["g_ffn"][0], wts["w_gu"][0], wts["w_down"][0], tm=tm, tf=D_FF)

    kv_args = (h, wts["g_kv"], wts["w_kv"], wts["g_fox_k_t"], wts["w_fg_t"], wts["b_fg"])
    if past is None:
        k_new, v_new, k_bf, v_bf, logf_t, logf_new = _proj_kv_paged(*kv_args, tm=tm)
    else:
        k_new, v_new, logf_t = _proj_kv(*kv_args, tm=tm)
        logf_new = logf_t.T

    q, mo = project_and_attend(1, fox=True)
    if past is None:
        f_row = _cumsum_rows(logf_t, seg=s_len)
        n_pairs = FOX_HEADS // 2
        f_row = f_row.reshape(n_pairs, 2, t)
        tok = _fox_prompt(q.reshape(b, s_len, TOK_WIDTH), k_bf,
                          v_bf.reshape(b, s_len, TOK_WIDTH), f_row, blk=512)
    else:
        page_table, cache_k_t, cache_v_t, cache_logf_t = past
        lfn = logf_t.reshape(FOX_HEADS, b, s_len).transpose(1, 0, 2)
        lfn = jnp.pad(lfn, ((0, 0), (0, 0), (0, LANES - s_len)))
        per_row = lambda a: a.reshape(b, s_len, TOK_WIDTH)
        tok = _fox_paged(page_table, per_row(q), per_row(k_new), per_row(v_new), lfn,
                         cache_logf_t, cache_k_t, cache_v_t)
    h = _out_ffn(h, tok.reshape(t, TOK_WIDTH), mo, wts["w_out"][1],
                 wts["g_ffn"][1], wts["w_gu"][1], wts["w_down"][1], tm=tm, tf=D_FF)
    return h.reshape(b, s_len, D_MODEL), z_tok, k_new, v_new, logf_new


def kernel(x_prompt, x_sample, cache_mem_k, cache_mem_v, state_pool, cache_k, cache_v, cache_logf,
           page_table, mem_prompt, g_mix, w_in, w_out, g_ffn, w_gu, w_down, g_mem, w_mem_kv,
           g_mem_q, g_mem_k, w_pool, pool_scale, g_kv, w_kv, g_fox_k, w_fg, b_fg, g_fox_q):
    depth = w_in.shape[0]
    b, s_len, _ = x_prompt.shape
    db, ds, _ = x_sample.shape
    wts = {
        "g_mix": g_mix.reshape(depth, 1, D_MODEL),
        "w_in": w_in.astype(BF16),
        "w_out": w_out.astype(BF16),
        "g_ffn": g_ffn.reshape(depth, 1, D_MODEL),
        "w_gu": w_gu.astype(BF16),
        "w_down": w_down.astype(BF16),
        "g_mem_q_t": jnp.tile(g_mem_q, (1, MEM_HEADS)).reshape(depth, 1, MEM_WIDTH),
        "w_pool": w_pool[0].astype(BF16),
        "pool_scale": pool_scale[0].reshape(1, TOK_WIDTH),
        "g_kv": g_kv.reshape(1, D_MODEL),
        "w_kv": w_kv.astype(BF16),
        "g_fox_k_t": jnp.tile(g_fox_k, FOX_HEADS).reshape(1, TOK_WIDTH),
        "w_fg_t": w_fg.T.astype(BF16),
        "b_fg": b_fg.reshape(FOX_HEADS, 1),
        "g_fox_q_t": jnp.tile(g_fox_q[0], FOX_HEADS).reshape(1, TOK_WIDTH),
    }

    mem_k_p, mem_v_p = _proj_memkv(mem_prompt.reshape(b * N_MEM, D_MODEL),
                                   g_mem.reshape(depth, 1, D_MODEL), w_mem_kv.astype(BF16),
                                   jnp.tile(g_mem_k, (1, MEM_HEADS)).reshape(depth, 1, MEM_WIDTH), tm=512)
    mem_rows = (N_MEM * MEM_HEADS, MEM_HEAD_DIM)
    mem_k_p = mem_k_p.reshape((depth, b) + mem_rows)
    mem_v_p = mem_v_p.reshape((depth, b) + mem_rows)
    y_p, ztok_p, k_p, v_p, logf_p = _decoder(
        x_prompt, jnp.zeros((b, POOL_HIST, TOK_WIDTH), F32), 0, mem_k_p, mem_v_p, None, wts,
        tm=512, pool_bb=1, pool_chunk=256, mem_bb=1, mem_ts=512, act=BF16)
    n_pp = s_len // PAGE_SIZE
    head_shape = (FOX_HEADS, FOX_HEAD_DIM)

    past = (page_table.T.reshape(-1),
            cache_k.transpose(0, 2, 3, 1), cache_v.transpose(0, 2, 3, 1),
            cache_logf.transpose(0, 2, 1))
    prev = jnp.pad(state_pool[0], ((0, 0), (POOL_HIST - POOL_STATE, 0), (0, 0)))
    y_s, ztok_s, k_s, v_s, logf_s = _decoder(
        x_sample, prev, POOL_STATE, cache_mem_k.reshape((depth, db) + mem_rows),
        cache_mem_v.reshape((depth, db) + mem_rows), past, wts,
        tm=512, pool_bb=32, pool_chunk=ds, mem_bb=8, mem_ts=ds, act=F32)

    pool_state_p = ztok_p[:, s_len - POOL_STATE:][None]
    pool_state_s = jnp.concatenate([state_pool[0], ztok_s], axis=1)[:, -POOL_STATE:][None]
    paged = lambda a: a.reshape((b, n_pp) + head_shape + (PAGE_SIZE,)).transpose(0, 1, 4, 2, 3)
    return (y_p, y_s, paged(k_p), paged(v_p),
            logf_p.reshape(b, n_pp, FOX_HEADS, PAGE_SIZE).transpose(0, 1, 3, 2),
            mem_k_p.reshape(depth, b, N_MEM, MEM_HEADS, MEM_HEAD_DIM),
            mem_v_p.reshape(depth, b, N_MEM, MEM_HEADS, MEM_HEAD_DIM),
            pool_state_p,
            k_s.reshape((db, ds) + head_shape),
            v_s.reshape((db, ds) + head_shape),
            logf_s.reshape(db, ds, FOX_HEADS),
            pool_state_s)
```

```python
import functools

import jax
import jax.numpy as jnp
from jax import lax
from jax.experimental import pallas as pl
from jax.experimental.pallas import tpu as pltpu

D_MODEL = 1024
TOK_WIDTH = 512
MEM_WIDTH = 512
POOL_WINDOWS = (2, 4, 8, 16)
POOL_GROUP = 128
POOL_STATE = 15
POOL_HIST = 16
FOX_HEADS = 8
FOX_HEAD_DIM = 64
MEM_HEADS = 4
MEM_HEAD_DIM = 128
N_MEM = 256
D_FF = 2816
PAGE_SIZE = 128
EPS = 1e-6
LANES = 128
BF16_ROWS = 16
VMEM_LIMIT = 56 * 1024 * 1024
FOX_CHUNK_ROWS = 16
PAGE_SLOTS = 3

F32 = jnp.float32
BF16 = jnp.bfloat16
NT_DIMS = (((1,), (1,)), ((), ()))


def _rmsnorm(x, g):
    return x * lax.rsqrt(jnp.mean(x * x, axis=-1, keepdims=True) + EPS) * g


def _headnorm128(z, g):
    outs = []
    for h in range(z.shape[-1] // LANES):
        zh = z[:, h * LANES:(h + 1) * LANES]
        r = lax.rsqrt(jnp.mean(zh * zh, axis=-1, keepdims=True) + EPS)
        outs.append(zh * r * g[:, h * LANES:(h + 1) * LANES])
    return jnp.concatenate(outs, axis=-1)


def _headnorm64(z, g):
    outs = []
    for p in range(z.shape[-1] // LANES):
        zp = z[:, p * LANES:(p + 1) * LANES]
        sq = zp * zp
        lane = lax.broadcasted_iota(jnp.int32, zp.shape, 1)
        lo = lane < FOX_HEAD_DIM
        s_lo = jnp.sum(jnp.where(lo, sq, 0.0), axis=-1, keepdims=True)
        s_hi = jnp.sum(jnp.where(lo, 0.0, sq), axis=-1, keepdims=True)
        r = jnp.where(lo, lax.rsqrt(s_lo / FOX_HEAD_DIM + EPS), lax.rsqrt(s_hi / FOX_HEAD_DIM + EPS))
        outs.append(zp * r * g[:, p * LANES:(p + 1) * LANES])
    return jnp.concatenate(outs, axis=-1)


def _mem_attend(q, k_ref, v_ref):
    outs = []
    for h in range(MEM_HEADS):
        qh = q[:, h * MEM_HEAD_DIM:(h + 1) * MEM_HEAD_DIM].astype(BF16)
        k = k_ref[pl.ds(h, N_MEM, stride=MEM_HEADS), :].astype(BF16)
        v = v_ref[pl.ds(h, N_MEM, stride=MEM_HEADS), :].astype(BF16)
        s = lax.dot_general(qh, k, NT_DIMS, preferred_element_type=F32) * (MEM_HEAD_DIM ** -0.5)
        p = jnp.exp(s - jnp.max(s, axis=-1, keepdims=True))
        o = jnp.dot(p.astype(BF16), v, preferred_element_type=F32)
        outs.append(o / jnp.sum(p, axis=-1, keepdims=True))
    return jnp.concatenate(outs, axis=-1)


def _proj_in_kernel(x_ref, g_ref, w_ref, gtok_ref, gmem_ref, *refs, fox, fuse_mem):
    xn = _rmsnorm(x_ref[...], g_ref[...]).astype(BF16)
    z = jnp.dot(xn, w_ref[...], preferred_element_type=F32)
    z_tok = z[:, :TOK_WIDTH]
    qm = _headnorm128(z[:, TOK_WIDTH:], gmem_ref[...])
    if fuse_mem:
        mk_ref, mv_ref, tok_ref, mem_ref = refs
        mem_ref[...] = _mem_attend(qm, mk_ref, mv_ref).astype(mem_ref.dtype)
    else:
        tok_ref, mem_ref = refs
        mem_ref[...] = qm.astype(mem_ref.dtype)
    if fox:
        tok_ref[...] = (_headnorm64(z_tok, gtok_ref[...]) * (FOX_HEAD_DIM ** -0.5)).astype(tok_ref.dtype)
    else:
        tok_ref[...] = z_tok


def _proj_in(x, g, w, g_tok, g_mem, mem=None, *, fox, tm, act):
    t = x.shape[0]
    row = lambda i: (i, 0)
    fixed = lambda i: (0, 0)
    operands = [x, g, w, g_tok, g_mem]
    in_specs = [pl.BlockSpec((tm, D_MODEL), row),
                pl.BlockSpec((1, D_MODEL), fixed),
                pl.BlockSpec((D_MODEL, D_MODEL), fixed),
                pl.BlockSpec((1, TOK_WIDTH), fixed),
                pl.BlockSpec((1, MEM_WIDTH), fixed)]
    if mem is not None:
        mk, mv, layer, rows_per_batch = mem
        tiles = rows_per_batch // tm
        mem_block = pl.BlockSpec((None, None, N_MEM * MEM_HEADS, MEM_HEAD_DIM),
                                 lambda i: (layer, i // tiles, 0, 0))
        operands += [mk, mv]
        in_specs += [mem_block, mem_block]
    return pl.pallas_call(
        functools.partial(_proj_in_kernel, fox=fox, fuse_mem=mem is not None),
        out_shape=(jax.ShapeDtypeStruct((t, TOK_WIDTH), act if fox else F32),
                   jax.ShapeDtypeStruct((t, MEM_WIDTH), act)),
        grid=(t // tm,),
        in_specs=in_specs,
        out_specs=(pl.BlockSpec((tm, TOK_WIDTH), row), pl.BlockSpec((tm, MEM_WIDTH), row)),
        compiler_params=pltpu.CompilerParams(dimension_semantics=("arbitrary",),
                                             vmem_limit_bytes=VMEM_LIMIT),
        name="proj_in_fox" if fox else "proj_in_pool",
    )(*operands)


def _log_sigmoid(x):
    return jnp.minimum(x, 0.0) - jnp.log(1.0 + jnp.exp(-jnp.abs(x)))


def _proj_kv_kernel(x_ref, g_ref, w_ref, gk_ref, wfg_ref, bfg_ref, k_ref, v_ref, logf_ref):
    xn = _rmsnorm(x_ref[...], g_ref[...]).astype(BF16)
    z = jnp.dot(xn, w_ref[...], preferred_element_type=F32)
    k_ref[...] = _headnorm64(z[:, :TOK_WIDTH], gk_ref[...])
    v_ref[...] = z[:, TOK_WIDTH:]
    gate = lax.dot_general(wfg_ref[...], xn, NT_DIMS, preferred_element_type=F32) + bfg_ref[...]
    logf_ref[...] = _log_sigmoid(gate)


def _proj_kv_paged_kernel(x_ref, g_ref, w_ref, gk_ref, wfg_ref, bfg_ref,
                          kp_ref, vp_ref, kb_ref, vb_ref, logf_ref, logfp_ref):
    xn = _rmsnorm(x_ref[...], g_ref[...]).astype(BF16)
    z = jnp.dot(xn, w_ref[...], preferred_element_type=F32)
    k = _headnorm64(z[:, :TOK_WIDTH], gk_ref[...])
    v = z[:, TOK_WIDTH:]
    vb_ref[...] = v.astype(BF16)
    gate = lax.dot_general(wfg_ref[...], xn, NT_DIMS, preferred_element_type=F32) + bfg_ref[...]
    logf = _log_sigmoid(gate)
    logf_ref[...] = logf
    k_t = k.T
    v_t = v.T
    kb_ref[...] = k_t.astype(BF16)
    for pg in range(kp_ref.shape[0]):
        rows = slice(pg * PAGE_SIZE, (pg + 1) * PAGE_SIZE)
        kp_ref[pg] = k_t[:, rows]
        vp_ref[pg] = v_t[:, rows]
        logfp_ref[pg] = logf[:, rows]


def _proj_kv_paged(x, g, w, g_k, wfg_t, bfg, *, tm):
    t = x.shape[0]
    row = lambda i: (i, 0)
    fixed = lambda i: (0, 0)
    ppt = tm // PAGE_SIZE
    page_rows = lambda i: (i, 0, 0)
    return pl.pallas_call(
        _proj_kv_paged_kernel,
        out_shape=(jax.ShapeDtypeStruct((t // PAGE_SIZE, TOK_WIDTH, PAGE_SIZE), F32),
                   jax.ShapeDtypeStruct((t // PAGE_SIZE, TOK_WIDTH, PAGE_SIZE), F32),
                   jax.ShapeDtypeStruct((TOK_WIDTH, t), BF16),
                   jax.ShapeDtypeStruct((t, TOK_WIDTH), BF16),
                   jax.ShapeDtypeStruct((FOX_HEADS, t), F32),
                   jax.ShapeDtypeStruct((t // PAGE_SIZE, FOX_HEADS, PAGE_SIZE), F32)),
        grid=(t // tm,),
        in_specs=[pl.BlockSpec((tm, D_MODEL), row),
                  pl.BlockSpec((1, D_MODEL), fixed),
                  pl.BlockSpec((D_MODEL, D_MODEL), fixed),
                  pl.BlockSpec((1, TOK_WIDTH), fixed),
                  pl.BlockSpec((FOX_HEADS, D_MODEL), fixed),
                  pl.BlockSpec((FOX_HEADS, 1), fixed)],
        out_specs=(pl.BlockSpec((ppt, TOK_WIDTH, PAGE_SIZE), page_rows),
                   pl.BlockSpec((ppt, TOK_WIDTH, PAGE_SIZE), page_rows),
                   pl.BlockSpec((TOK_WIDTH, tm), lambda i: (0, i)), pl.BlockSpec((tm, TOK_WIDTH), row),
                   pl.BlockSpec((FOX_HEADS, tm), lambda i: (0, i)),
                   pl.BlockSpec((ppt, FOX_HEADS, PAGE_SIZE), page_rows)),
        compiler_params=pltpu.CompilerParams(dimension_semantics=("arbitrary",),
                                             vmem_limit_bytes=VMEM_LIMIT),
        name="proj_kv_paged",
    )(x, g, w, g_k, wfg_t, bfg)


def _proj_kv(x, g, w, g_k, wfg_t, bfg, *, tm):
    t = x.shape[0]
    row = lambda i: (i, 0)
    fixed = lambda i: (0, 0)
    return pl.pallas_call(
        _proj_kv_kernel,
        out_shape=(jax.ShapeDtypeStruct((t, TOK_WIDTH), F32),
                   jax.ShapeDtypeStruct((t, TOK_WIDTH), F32),
                   jax.ShapeDtypeStruct((FOX_HEADS, t), F32)),
        grid=(t // tm,),
        in_specs=[pl.BlockSpec((tm, D_MODEL), row),
                  pl.BlockSpec((1, D_MODEL), fixed),
                  pl.BlockSpec((D_MODEL, D_MODEL), fixed),
                  pl.BlockSpec((1, TOK_WIDTH), fixed),
                  pl.BlockSpec((FOX_HEADS, D_MODEL), fixed),
                  pl.BlockSpec((FOX_HEADS, 1), fixed)],
        out_specs=(pl.BlockSpec((tm, TOK_WIDTH), row), pl.BlockSpec((tm, TOK_WIDTH), row),
                   pl.BlockSpec((FOX_HEADS, tm), lambda i: (0, i))),
        compiler_params=pltpu.CompilerParams(dimension_semantics=("arbitrary",),
                                             vmem_limit_bytes=VMEM_LIMIT),
        name="proj_kv",
    )(x, g, w, g_k, wfg_t, bfg)


def _proj_memkv_kernel(x_ref, g_ref, w_ref, gk_ref, k_ref, v_ref):
    xn = _rmsnorm(x_ref[...], g_ref[0]).astype(BF16)
    z = jnp.dot(xn, w_ref[0], preferred_element_type=F32)
    k = _headnorm128(z[:, :MEM_WIDTH], gk_ref[0])
    tm = x_ref.shape[0]
    for h in range(MEM_HEADS):
        rows = pl.ds(h, tm, stride=MEM_HEADS)
        k_ref[0, rows, :] = k[:, h * MEM_HEAD_DIM:(h + 1) * MEM_HEAD_DIM]
        v_ref[0, rows, :] = z[:, MEM_WIDTH + h * MEM_HEAD_DIM:MEM_WIDTH + (h + 1) * MEM_HEAD_DIM]


def _proj_memkv(x, g, w, g_k, *, tm):
    t = x.shape[0]
    n_layers = w.shape[0]
    out_block = pl.BlockSpec((1, tm * MEM_HEADS, MEM_HEAD_DIM), lambda l, i: (l, i, 0))
    return pl.pallas_call(
        _proj_memkv_kernel,
        out_shape=(jax.ShapeDtypeStruct((n_layers, t * MEM_HEADS, MEM_HEAD_DIM), F32),
                   jax.ShapeDtypeStruct((n_layers, t * MEM_HEADS, MEM_HEAD_DIM), F32)),
        grid=(n_layers, t // tm),
        in_specs=[pl.BlockSpec((tm, D_MODEL), lambda l, i: (i, 0)),
                  pl.BlockSpec((1, 1, D_MODEL), lambda l, i: (l, 0, 0)),
                  pl.BlockSpec((1, D_MODEL, D_MODEL), lambda l, i: (l, 0, 0)),
                  pl.BlockSpec((1, 1, MEM_WIDTH), lambda l, i: (l, 0, 0))],
        out_specs=(out_block, out_block),
        compiler_params=pltpu.CompilerParams(dimension_semantics=("arbitrary", "arbitrary"),
                                             vmem_limit_bytes=VMEM_LIMIT),
        name="proj_memkv",
    )(x, g, w, g_k)


def _pool_kernel(u_ref, prev_ref, w_ref, scale_ref, o_ref, ext_ref, *, n_prev, chunk):
    bb, s_len, _ = u_ref.shape
    ext_ref[:, 0:POOL_HIST, :] = prev_ref[...]
    ext_ref[:, POOL_HIST:, :] = u_ref[...]
    for c in range(s_len // chunk):
        r0 = POOL_HIST + c * chunk
        pos = c * chunk + lax.broadcasted_iota(jnp.int32, (1, chunk, 1), 1)
        for gi, win in enumerate(POOL_WINDOWS):
            lanes = slice(gi * POOL_GROUP, (gi + 1) * POOL_GROUP)
            u_new = ext_ref[:, r0:r0 + chunk, lanes]
            acc = u_new
            for k in range(1, win):
                acc = acc + ext_ref[:, r0 - k:r0 - k + chunk, lanes]
            count = jnp.minimum(win, n_prev + pos + 1).astype(F32)
            y = acc / count - u_new
            z = jnp.dot(y.reshape(bb * chunk, POOL_GROUP).astype(BF16), w_ref[gi],
                        preferred_element_type=F32)
            z = z.reshape(bb, chunk, POOL_GROUP) * scale_ref[:, lanes]
            o_ref[:, c * chunk:(c + 1) * chunk, lanes] = z.astype(o_ref.dtype)


def _pool_mix(u, prev, w_pool, scale, *, n_prev, bb, chunk, act):
    b, s_len, _ = u.shape
    return pl.pallas_call(
        functools.partial(_pool_kernel, n_prev=n_prev, chunk=chunk),
        out_shape=jax.ShapeDtypeStruct((b, s_len, TOK_WIDTH), act),
        grid=(b // bb,),
        in_specs=[pl.BlockSpec((bb, s_len, TOK_WIDTH), lambda i: (i, 0, 0)),
                  pl.BlockSpec((bb, POOL_HIST, TOK_WIDTH), lambda i: (i, 0, 0)),
                  pl.BlockSpec((len(POOL_WINDOWS), POOL_GROUP, POOL_GROUP), lambda i: (0, 0, 0)),
                  pl.BlockSpec((1, TOK_WIDTH), lambda i: (0, 0))],
        out_specs=pl.BlockSpec((bb, s_len, TOK_WIDTH), lambda i: (i, 0, 0)),
        scratch_shapes=[pltpu.VMEM((bb, POOL_HIST + s_len, TOK_WIDTH), F32)],
        compiler_params=pltpu.CompilerParams(dimension_semantics=("arbitrary",),
                                             vmem_limit_bytes=VMEM_LIMIT),
        name="pool_mix",
    )(u, prev, w_pool, scale)


def _mem_attn_kernel(q_ref, k_ref, v_ref, o_ref):
    for h in range(MEM_HEADS):
        lanes = slice(h * MEM_HEAD_DIM, (h + 1) * MEM_HEAD_DIM)
        q = q_ref[:, :, lanes].astype(BF16)
        k = k_ref[:, pl.ds(h, N_MEM, stride=MEM_HEADS), :].astype(BF16)
        v = v_ref[:, pl.ds(h, N_MEM, stride=MEM_HEADS), :].astype(BF16)
        s = jnp.einsum("bqd,bkd->bqk", q, k, preferred_element_type=F32) * (MEM_HEAD_DIM ** -0.5)
        m = jnp.max(s, axis=-1, keepdims=True)
        p = jnp.exp(s - m)
        l = jnp.sum(p, axis=-1, keepdims=True)
        o = jnp.einsum("bqk,bkd->bqd", p.astype(BF16), v, preferred_element_type=F32)
        o_ref[:, :, lanes] = (o / l).astype(o_ref.dtype)


def _mem_attn(q, mk, mv, layer, *, bb, ts):
    b, s_len, _ = q.shape
    mem_block = pl.BlockSpec((None, bb, N_MEM * MEM_HEADS, MEM_HEAD_DIM),
                             lambda i, j: (layer, i, 0, 0))
    return pl.pallas_call(
        _mem_attn_kernel,
        out_shape=jax.ShapeDtypeStruct((b, s_len, MEM_WIDTH), q.dtype),
        grid=(b // bb, s_len // ts),
        in_specs=[pl.BlockSpec((bb, ts, MEM_WIDTH), lambda i, j: (i, j, 0)), mem_block, mem_block],
        out_specs=pl.BlockSpec((bb, ts, MEM_WIDTH), lambda i, j: (i, j, 0)),
        compiler_params=pltpu.CompilerParams(dimension_semantics=("arbitrary", "arbitrary"),
                                             vmem_limit_bytes=VMEM_LIMIT),
        name="mem_attn",
    )(q, mk, mv)


def _out_ffn_kernel(h_ref, tok_ref, mo_ref, wout_ref, g_ref, wg_ref, wu_ref, wd_ref, o_ref):
    h1 = (h_ref[...]
          + jnp.dot(tok_ref[...].astype(BF16), wout_ref[:TOK_WIDTH, :], preferred_element_type=F32)
          + jnp.dot(mo_ref[...].astype(BF16), wout_ref[TOK_WIDTH:, :], preferred_element_type=F32))
    x = _rmsnorm(h1, g_ref[...]).astype(BF16)
    gate = jnp.dot(x, wg_ref[...], preferred_element_type=F32)
    up = jnp.dot(x, wu_ref[...], preferred_element_type=F32)
    act = (gate * (1.0 / (1.0 + jnp.exp(-gate))) * up).astype(BF16)
    o_ref[...] = h1 + jnp.dot(act, wd_ref[...], preferred_element_type=F32)


def _out_ffn(h, tok, mo, w_out, g, w_gu, w_down, *, tm):
    t = h.shape[0]
    row = lambda i: (i, 0)
    once = pl.Buffered(1)
    return pl.pallas_call(
        _out_ffn_kernel,
        out_shape=jax.ShapeDtypeStruct((t, D_MODEL), F32),
        grid=(t // tm,),
        in_specs=[pl.BlockSpec((tm, D_MODEL), row),
                  pl.BlockSpec((tm, TOK_WIDTH), row),
                  pl.BlockSpec((tm, MEM_WIDTH), row),
                  pl.BlockSpec((D_MODEL, D_MODEL), lambda i: (0, 0), pipeline_mode=once),
                  pl.BlockSpec((1, D_MODEL), lambda i: (0, 0)),
                  pl.BlockSpec((D_MODEL, D_FF), lambda i: (0, 0), pipeline_mode=once),
                  pl.BlockSpec((D_MODEL, D_FF), lambda i: (0, 1), pipeline_mode=once),
                  pl.BlockSpec((D_FF, D_MODEL), lambda i: (0, 0), pipeline_mode=once)],
        out_specs=pl.BlockSpec((tm, D_MODEL), row),
        compiler_params=pltpu.CompilerParams(dimension_semantics=("arbitrary",),
                                             vmem_limit_bytes=VMEM_LIMIT),
        name="out_ffn",
    )(h, tok, mo, w_out, g, w_gu, w_gu, w_down)


def _lane_cumsum(x):
    n = x.shape[-1]
    lane = lax.broadcasted_iota(jnp.int32, x.shape, 1)
    k = 1
    while k < n:
        x = x + jnp.where(lane >= k, pltpu.roll(x, k, axis=1), 0.0)
        k *= 2
    return x


def _cumsum_kernel(x_ref, o_ref):
    o_ref[...] = _lane_cumsum(x_ref[...])


def _cumsum_rows(x, *, seg):
    rows, t = x.shape
    return pl.pallas_call(
        _cumsum_kernel,
        out_shape=jax.ShapeDtypeStruct((rows, t), F32),
        grid=(t // seg,),
        in_specs=[pl.BlockSpec((rows, seg), lambda i: (0, i))],
        out_specs=pl.BlockSpec((rows, seg), lambda i: (0, i)),
        compiler_params=pltpu.CompilerParams(dimension_semantics=("arbitrary",)),
        name="logf_cumsum",
    )(x)


def _round_to_bf16(x):
    return x.astype(BF16).astype(F32)


def _fox_prompt_kernel(q_ref, k_ref, v_ref, frow_ref, o_ref, m_ref, l_ref, acc_ref, alpha_ref,
                       s_ref, p_all_ref, *, blk, n_q_blocks):
    qi = pl.program_id(2)
    q = q_ref[0]
    lane = lax.broadcasted_iota(jnp.int32, q.shape, 1)
    sub = FOX_CHUNK_ROWS
    n_rep = blk // LANES
    bias_lane = [FOX_HEAD_DIM, 0]
    q_f32 = q.astype(F32)
    q_head = []
    for j in range(2):
        own = (lane < FOX_HEAD_DIM) if j == 0 else (lane >= FOX_HEAD_DIM)
        ones_from = jnp.where(lane >= bias_lane[j], 1.0, 0.0)
        bias_ones = jnp.where(lane < bias_lane[j] + 3, ones_from, 0.0)
        q_head.append(jnp.where(own, q_f32, bias_ones).astype(BF16))
    m_ref[...] = jnp.full(m_ref.shape, -jnp.inf, F32)
    l_ref[...] = jnp.zeros_like(l_ref)
    acc_ref[...] = jnp.zeros_like(acc_ref)

    def scores(qc, ki, s_buf):
        keys = slice(ki * blk, (ki + 1) * blk)
        kt = k_ref[:, keys]
        tile_row = lax.broadcasted_iota(jnp.int32, (BF16_ROWS, blk), 0)
        for j in range(2):
            f_end = frow_ref[0, j:j + 1, (qc + 1) * blk - 1:(qc + 1) * blk]
            bias = f_end - frow_ref[0, j:j + 1, keys]
            hi = _round_to_bf16(bias)
            mid = _round_to_bf16(bias - hi)
            lo = _round_to_bf16(bias - hi - mid)
            bias_rows = jnp.where(tile_row == 0, hi, jnp.where(tile_row == 1, mid,
                                                               jnp.where(tile_row == 2, lo, 0.0)))
            b0 = bias_lane[j]
            pieces = [kt[:b0], bias_rows.astype(BF16), kt[b0 + BF16_ROWS:]]
            kt_j = jnp.concatenate([x for x in pieces if x.shape[0]], axis=0)
            s_buf[j] = jnp.dot(q_head[j], kt_j, preferred_element_type=F32)

    def update(ki, s_buf, masked):
        p_ref = p_all_ref.at[ki]
        vt = v_ref[0, ki * blk:(ki + 1) * blk, :]
        for j in range(2):
            def chunk(r):
                n_use = -(-(r + 1) * sub // LANES) if masked else n_rep
                rows = slice(r * sub, (r + 1) * sub)
                s = s_buf[j, rows, 0:n_use * LANES]
                if masked:
                    row = lax.broadcasted_iota(jnp.int32, s.shape, 0) + r * sub
                    col = lax.broadcasted_iota(jnp.int32, s.shape, 1)
                    s = jnp.where(col <= row, s, -jnp.inf)
                return rows, n_use, s

            for r in range(blk // sub):
                rows, n_use, s = chunk(r)
                m_prev = m_ref[j, rows]
                m_new = jnp.maximum(m_prev, jnp.max(s, axis=-1, keepdims=True))
                alpha_ref[j, rows] = jnp.exp(m_prev - m_new)
                m_ref[j, rows] = m_new
            for r in range(blk // sub):
                rows, n_use, s = chunk(r)
                p = jnp.exp(s - jnp.concatenate([m_ref[j, rows]] * n_use, axis=1))
                p_lanes = p[:, :LANES]
                for c in range(1, n_use):
                    p_lanes = p_lanes + p[:, c * LANES:(c + 1) * LANES]
                l_ref[j, rows] = alpha_ref[j, rows] * l_ref[j, rows] + p_lanes
                p_ref[j, rows, 0:n_use * LANES] = p.astype(BF16)
                if n_use < n_rep:
                    p_ref[j, rows, n_use * LANES:] = jnp.zeros((sub, blk - n_use * LANES), BF16)
            acc_ref[j] = alpha_ref[j] * acc_ref[j] + jnp.dot(p_ref[j], vt, preferred_element_type=F32)

    for qc in range(n_q_blocks):
        @pl.when(qi == qc)
        def _(qc=qc):
            scores(qc, 0, s_ref.at[0])
            for ki in range(qc):
                scores(qc, ki + 1, s_ref.at[ki + 1])
                update(ki, s_ref.at[ki], False)
            update(qc, s_ref.at[qc], True)

    l0 = jnp.sum(l_ref[0], axis=-1, keepdims=True)
    l1 = jnp.sum(l_ref[1], axis=-1, keepdims=True)
    o = jnp.where(lane < FOX_HEAD_DIM, acc_ref[0] / l0, acc_ref[1] / l1)
    o_ref[0] = o.astype(o_ref.dtype)


def _fox_prompt(q, k, v, frow, *, blk):
    b, s_len, _ = q.shape
    n_pairs = TOK_WIDTH // LANES
    nq = s_len // blk
    return pl.pallas_call(
        functools.partial(_fox_prompt_kernel, blk=blk, n_q_blocks=nq),
        out_shape=jax.ShapeDtypeStruct((b, s_len, TOK_WIDTH), BF16),
        grid=(b, n_pairs, nq),
        in_specs=[pl.BlockSpec((1, blk, LANES), lambda bi, hp, qi: (bi, qi, hp)),
                  pl.BlockSpec((LANES, s_len), lambda bi, hp, qi: (hp, bi)),
                  pl.BlockSpec((1, s_len, LANES), lambda bi, hp, qi: (bi, 0, hp)),
                  pl.BlockSpec((1, 2, s_len), lambda bi, hp, qi: (hp, 0, bi))],
        out_specs=pl.BlockSpec((1, blk, LANES), lambda bi, hp, qi: (bi, qi, hp)),
        scratch_shapes=[pltpu.VMEM((2, blk, LANES), F32), pltpu.VMEM((2, blk, LANES), F32),
                        pltpu.VMEM((2, blk, LANES), F32), pltpu.VMEM((2, blk, LANES), F32),
                        pltpu.VMEM((nq, 2, blk, blk), F32),
                        pltpu.VMEM((nq, 2, blk, blk), BF16)],
        compiler_params=pltpu.CompilerParams(
            dimension_semantics=("arbitrary", "arbitrary", "arbitrary"),
            vmem_limit_bytes=VMEM_LIMIT),
        name="fox_prompt",
    )(q, k, v, frow)


def _lane_suffix_sum(x):
    n = x.shape[-1]
    lane = lax.broadcasted_iota(jnp.int32, x.shape, 1)
    k = 1
    while k < n:
        x = x + jnp.where(lane + k < n, pltpu.roll(x, n - k, axis=1), 0.0)
        k *= 2
    return x


def _fox_paged_kernel(pt_ref, q_ref, kn_ref, vn_ref, lfn_ref, logf_ref, k_hbm, v_hbm, o_ref,
                      kbuf, vbuf, sem, pad_k_ref, pad_v_ref, *, n_pages, n_batch):
    b = pl.program_id(0)
    n_steps = pl.num_programs(0)
    slot = lax.rem(b, PAGE_SLOTS)
    n_q = q_ref.shape[1]
    n_rows = n_q * FOX_HEADS

    def page_copies(batch, slot_):
        copies = []
        for j in range(n_pages):
            page = pt_ref[j * n_batch + batch]
            copies.append(pltpu.make_async_copy(k_hbm.at[page], kbuf.at[slot_, j], sem.at[0, slot_]))
            copies.append(pltpu.make_async_copy(v_hbm.at[page], vbuf.at[slot_, j], sem.at[1, slot_]))
        return copies

    @pl.when(b == 0)
    def _():
        for c in page_copies(0, 0):
            c.start()

    @pl.when(jnp.logical_and(b == 0, n_steps > 1))
    def _():
        for c in page_copies(1, 1):
            c.start()

    @pl.when(b + 2 < n_steps)
    def _():
        for c in page_copies(b + 2, lax.rem(b + 2, PAGE_SLOTS)):
            c.start()

    head = lax.broadcasted_iota(jnp.int32, (FOX_HEADS, TOK_WIDTH), 0)
    lane_head = jnp.right_shift(lax.broadcasted_iota(jnp.int32, (FOX_HEADS, TOK_WIDTH), 1), 6)
    own_head = head == lane_head
    tile_q = lambda x: jnp.concatenate([x] * n_q, axis=0)

    q = q_ref[0]
    qblk = jnp.concatenate(
        [jnp.where(own_head, jnp.broadcast_to(q[t:t + 1, :], (FOX_HEADS, TOK_WIDTH)), 0.0)
         for t in range(n_q)], axis=0).astype(BF16)

    pad_k_ref[...] = jnp.zeros_like(pad_k_ref)
    pad_v_ref[...] = jnp.zeros_like(pad_v_ref)
    pad_k_ref[0:n_q, :] = kn_ref[0]
    pad_v_ref[0:n_q, :] = vn_ref[0]
    c_new = _lane_cumsum(lfn_ref[0])
    cq = jnp.concatenate([c_new[:, t:t + 1] for t in range(n_q)], axis=0)
    s_own = lax.dot_general(qblk, pad_k_ref[...].astype(BF16), NT_DIMS, preferred_element_type=F32)
    t_of_row = jnp.right_shift(lax.broadcasted_iota(jnp.int32, (n_rows, PAGE_SIZE), 0), 3)
    key = lax.broadcasted_iota(jnp.int32, (n_rows, PAGE_SIZE), 1)
    s_own = jnp.where(key <= t_of_row, s_own + (tile_q(-c_new) + cq), -jnp.inf)

    logf = [logf_ref[pt_ref[j * n_batch + b]] for j in range(n_pages)]
    incl = _lane_suffix_sum(jnp.concatenate(logf, axis=0))
    later_pages = jnp.zeros((FOX_HEADS, 1), F32)
    bias = [None] * n_pages
    for j in reversed(range(n_pages)):
        rows = slice(j * FOX_HEADS, (j + 1) * FOX_HEADS)
        bias[j] = tile_q(later_pages + (incl[rows] - logf[j])) + cq
        later_pages = later_pages + incl[rows, 0:1]

    for c in page_copies(b, slot):
        c.wait()

    scores = [s_own]
    for j in range(n_pages):
        k_t = kbuf[slot, j].reshape(TOK_WIDTH, PAGE_SIZE).astype(BF16)
        scores.append(jnp.dot(qblk, k_t, preferred_element_type=F32) + bias[j])
    s_max = scores[0]
    for s in scores[1:]:
        s_max = jnp.maximum(s_max, s)
    m = jnp.max(s_max, axis=-1, keepdims=True)
    p = jnp.exp(scores[0] - m)
    p_sum = p
    acc = jnp.dot(p.astype(BF16), pad_v_ref[...].astype(BF16), preferred_element_type=F32)
    for j in range(n_pages):
        p = jnp.exp(scores[j + 1] - m)
        p_sum = p_sum + p
        v_t = vbuf[slot, j].reshape(TOK_WIDTH, PAGE_SIZE).astype(BF16)
        acc = acc + lax.dot_general(p.astype(BF16), v_t, NT_DIMS, preferred_element_type=F32)
    o = acc / jnp.sum(p_sum, axis=-1, keepdims=True)
    for t in range(n_q):
        o_t = jnp.where(own_head, o[t * FOX_HEADS:(t + 1) * FOX_HEADS, :], 0.0)
        o_ref[0, t:t + 1, :] = jnp.sum(o_t, axis=0, keepdims=True)


def _fox_paged(page_table, q, k_new, v_new, lfn, cache_logf_t, cache_k_t, cache_v_t):
    b, n_q, _ = q.shape
    n_pages = page_table.shape[0] // b
    cur = lambda bi, pt: (bi, 0, 0)
    page_shape = cache_k_t.shape[1:]
    grid_spec = pltpu.PrefetchScalarGridSpec(
        num_scalar_prefetch=1,
        grid=(b,),
        in_specs=[pl.BlockSpec((1, n_q, TOK_WIDTH), cur),
                  pl.BlockSpec((1, n_q, TOK_WIDTH), cur),
                  pl.BlockSpec((1, n_q, TOK_WIDTH), cur),
                  pl.BlockSpec((1, FOX_HEADS, LANES), cur),
                  pl.BlockSpec(cache_logf_t.shape, lambda bi, pt: (0, 0, 0)),
                  pl.BlockSpec(memory_space=pl.ANY),
                  pl.BlockSpec(memory_space=pl.ANY)],
        out_specs=pl.BlockSpec((1, n_q, TOK_WIDTH), cur),
        scratch_shapes=[pltpu.VMEM((PAGE_SLOTS, n_pages) + page_shape, F32),
                        pltpu.VMEM((PAGE_SLOTS, n_pages) + page_shape, F32),
                        pltpu.SemaphoreType.DMA((2, PAGE_SLOTS)),
                        pltpu.VMEM((PAGE_SIZE, TOK_WIDTH), F32),
                        pltpu.VMEM((PAGE_SIZE, TOK_WIDTH), F32)])
    return pl.pallas_call(
        functools.partial(_fox_paged_kernel, n_pages=n_pages, n_batch=b),
        out_shape=jax.ShapeDtypeStruct((b, n_q, TOK_WIDTH), F32),
        grid_spec=grid_spec,
        compiler_params=pltpu.CompilerParams(dimension_semantics=("arbitrary",),
                                             vmem_limit_bytes=VMEM_LIMIT),
        name="fox_paged",
    )(page_table, q, k_new, v_new, lfn, cache_logf_t, cache_k_t, cache_v_t)


def _decoder(x, pool_prev, n_prev, mem_k, mem_v, past, wts, *, tm, pool_bb, pool_chunk, mem_bb, mem_ts,
             act):
    b, s_len, _ = x.shape
    t = b * s_len
    h = x.reshape(t, D_MODEL)

    fuse_mem = s_len % tm == 0

    def project_and_attend(layer, fox):
        mem = (mem_k, mem_v, layer, s_len) if fuse_mem else None
        tok_part, mem_part = _proj_in(h, wts["g_mix"][layer], wts["w_in"][layer], wts["g_fox_q_t"],
                                      wts["g_mem_q_t"][layer], mem, fox=fox, tm=tm, act=act)
        if not fuse_mem:
            mem_part = _mem_attn(mem_part.reshape(b, s_len, MEM_WIDTH), mem_k, mem_v, layer,
                                 bb=mem_bb, ts=mem_ts)
        return tok_part, mem_part.reshape(t, MEM_WIDTH)

    z_tok, mo = project_and_attend(0, fox=False)
    z_tok = z_tok.reshape(b, s_len, TOK_WIDTH)
    tok = _pool_mix(z_tok, pool_prev, wts["w_pool"], wts["pool_scale"],
                    n_prev=n_prev, bb=pool_bb, chunk=pool_chunk, act=act)
    h = _out_ffn(h, tok.reshape(t, TOK_WIDTH), mo, wts["w_out"][0],
                 wts["g_ffn"][0], wts["w_gu"][0], wts["w_down"][0], tm=tm)

    kv_args = (h, wts["g_kv"], wts["w_kv"], wts["g_fox_k_t"], wts["w_fg_t"], wts["b_fg"])
    if past is None:
        k_new, v_new, k_bf, v_bf, logf_t, logf_new = _proj_kv_paged(*kv_args, tm=tm)
    else:
        k_new, v_new, logf_t = _proj_kv(*kv_args, tm=tm)
        logf_new = logf_t.T

    q, mo = project_and_attend(1, fox=True)
    if past is None:
        f_row = _cumsum_rows(logf_t, seg=s_len)
        n_pairs = FOX_HEADS // 2
        f_row = f_row.reshape(n_pairs, 2, t)
        tok = _fox_prompt(q.reshape(b, s_len, TOK_WIDTH), k_bf,
                          v_bf.reshape(b, s_len, TOK_WIDTH), f_row, blk=512)
    else:
        page_table, cache_k_t, cache_v_t, cache_logf_t = past
        lfn = logf_t.reshape(FOX_HEADS, b, s_len).transpose(1, 0, 2)
        lfn = jnp.pad(lfn, ((0, 0), (0, 0), (0, LANES - s_len)))
        per_row = lambda a: a.reshape(b, s_len, TOK_WIDTH)
        tok = _fox_paged(page_table, per_row(q), per_row(k_new), per_row(v_new), lfn,
                         cache_logf_t, cache_k_t, cache_v_t)
    h = _out_ffn(h, tok.reshape(t, TOK_WIDTH), mo, wts["w_out"][1],
                 wts["g_ffn"][1], wts["w_gu"][1], wts["w_down"][1], tm=tm)
    return h.reshape(b, s_len, D_MODEL), z_tok, k_new, v_new, logf_new


def kernel(x_prompt, x_sample, cache_mem_k, cache_mem_v, state_pool, cache_k, cache_v, cache_logf,
           page_table, mem_prompt, g_mix, w_in, w_out, g_ffn, w_gu, w_down, g_mem, w_mem_kv,
           g_mem_q, g_mem_k, w_pool, pool_scale, g_kv, w_kv, g_fox_k, w_fg, b_fg, g_fox_q):
    depth = w_in.shape[0]
    b, s_len, _ = x_prompt.shape
    db, ds, _ = x_sample.shape
    wts = {
        "g_mix": g_mix.reshape(depth, 1, D_MODEL),
        "w_in": w_in.astype(BF16),
        "w_out": w_out.astype(BF16),
        "g_ffn": g_ffn.reshape(depth, 1, D_MODEL),
        "w_gu": w_gu.astype(BF16),
        "w_down": w_down.astype(BF16),
        "g_mem_q_t": jnp.tile(g_mem_q, (1, MEM_HEADS)).reshape(depth, 1, MEM_WIDTH),
        "w_pool": w_pool[0].astype(BF16),
        "pool_scale": pool_scale[0].reshape(1, TOK_WIDTH),
        "g_kv": g_kv.reshape(1, D_MODEL),
        "w_kv": w_kv.astype(BF16),
        "g_fox_k_t": jnp.tile(g_fox_k, FOX_HEADS).reshape(1, TOK_WIDTH),
        "w_fg_t": w_fg.T.astype(BF16),
        "b_fg": b_fg.reshape(FOX_HEADS, 1),
        "g_fox_q_t": jnp.tile(g_fox_q[0], FOX_HEADS).reshape(1, TOK_WIDTH),
    }

    mem_k_p, mem_v_p = _proj_memkv(mem_prompt.reshape(b * N_MEM, D_MODEL),
                                   g_mem.reshape(depth, 1, D_MODEL), w_mem_kv.astype(BF16),
                                   jnp.tile(g_mem_k, (1, MEM_HEADS)).reshape(depth, 1, MEM_WIDTH), tm=512)
    mem_rows = (N_MEM * MEM_HEADS, MEM_HEAD_DIM)
    mem_k_p = mem_k_p.reshape((depth, b) + mem_rows)
    mem_v_p = mem_v_p.reshape((depth, b) + mem_rows)
    y_p, ztok_p, k_p, v_p, logf_p = _decoder(
        x_prompt, jnp.zeros((b, POOL_HIST, TOK_WIDTH), F32), 0, mem_k_p, mem_v_p, None, wts,
        tm=512, pool_bb=1, pool_chunk=256, mem_bb=1, mem_ts=512, act=BF16)
    n_pp = s_len // PAGE_SIZE
    head_shape = (FOX_HEADS, FOX_HEAD_DIM)

    past = (page_table.T.reshape(-1),
            cache_k.transpose(0, 2, 3, 1), cache_v.transpose(0, 2, 3, 1),
            cache_logf.transpose(0, 2, 1))
    prev = jnp.pad(state_pool[0], ((0, 0), (POOL_HIST - POOL_STATE, 0), (0, 0)))
    y_s, ztok_s, k_s, v_s, logf_s = _decoder(
        x_sample, prev, POOL_STATE, cache_mem_k.reshape((depth, db) + mem_rows),
        cache_mem_v.reshape((depth, db) + mem_rows), past, wts,
        tm=512, pool_bb=32, pool_chunk=ds, mem_bb=8, mem_ts=ds, act=F32)

    pool_state_p = ztok_p[:, s_len - POOL_STATE:][None]
    pool_state_s = jnp.concatenate([state_pool[0], ztok_s], axis=1)[:, -POOL_STATE:][None]
    paged = lambda a: a.reshape((b, n_pp) + head_shape + (PAGE_SIZE,)).transpose(0, 1, 4, 2, 3)
    return (y_p, y_s, paged(k_p), paged(v_p),
            logf_p.reshape(b, n_pp, FOX_HEADS, PAGE_SIZE).transpose(0, 1, 3, 2),
            mem_k_p.reshape(depth, b, N_MEM, MEM_HEADS, MEM_HEAD_DIM),
            mem_v_p.reshape(depth, b, N_MEM, MEM_HEADS, MEM_HEAD_DIM),
            pool_state_p,
            k_s.reshape((db, ds) + head_shape),
            v_s.reshape((db, ds) + head_shape),
            logf_s.reshape(db, ds, FOX_HEADS),
            pool_state_s)
```

```python
import functools

import jax
import jax.numpy as jnp
from jax import lax
from jax.experimental import pallas as pl
from jax.experimental.pallas import tpu as pltpu

D_MODEL = 1024
TOK_WIDTH = 512
MEM_WIDTH = 512
POOL_WINDOWS = (2, 4, 8, 16)
POOL_GROUP = 128
POOL_STATE = 15
POOL_HIST = 16
FOX_HEADS = 8
FOX_HEAD_DIM = 64
MEM_HEADS = 4
MEM_HEAD_DIM = 128
N_MEM = 256
D_FF = 2816
PAGE_SIZE = 128
EPS = 1e-6
LANES = 128
BF16_ROWS = 16
VMEM_LIMIT = 56 * 1024 * 1024
FOX_CHUNK_ROWS = 16
PAGE_SLOTS = 4

F32 = jnp.float32
BF16 = jnp.bfloat16
NT_DIMS = (((1,), (1,)), ((), ()))


def _rmsnorm(x, g):
    return x * lax.rsqrt(jnp.mean(x * x, axis=-1, keepdims=True) + EPS) * g


def _headnorm128(z, g):
    outs = []
    for h in range(z.shape[-1] // LANES):
        zh = z[:, h * LANES:(h + 1) * LANES]
        r = lax.rsqrt(jnp.mean(zh * zh, axis=-1, keepdims=True) + EPS)
        outs.append(zh * r * g[:, h * LANES:(h + 1) * LANES])
    return jnp.concatenate(outs, axis=-1)


def _headnorm64(z, g):
    outs = []
    for p in range(z.shape[-1] // LANES):
        zp = z[:, p * LANES:(p + 1) * LANES]
        sq = zp * zp
        lane = lax.broadcasted_iota(jnp.int32, zp.shape, 1)
        lo = lane < FOX_HEAD_DIM
        s_lo = jnp.sum(jnp.where(lo, sq, 0.0), axis=-1, keepdims=True)
        s_hi = jnp.sum(jnp.where(lo, 0.0, sq), axis=-1, keepdims=True)
        r = jnp.where(lo, lax.rsqrt(s_lo / FOX_HEAD_DIM + EPS), lax.rsqrt(s_hi / FOX_HEAD_DIM + EPS))
        outs.append(zp * r * g[:, p * LANES:(p + 1) * LANES])
    return jnp.concatenate(outs, axis=-1)


def _mem_attend(q, k_ref, v_ref):
    outs = []
    for h in range(MEM_HEADS):
        qh = q[:, h * MEM_HEAD_DIM:(h + 1) * MEM_HEAD_DIM].astype(BF16)
        k = k_ref[pl.ds(h, N_MEM, stride=MEM_HEADS), :].astype(BF16)
        v = v_ref[pl.ds(h, N_MEM, stride=MEM_HEADS), :].astype(BF16)
        s = lax.dot_general(qh, k, NT_DIMS, preferred_element_type=F32) * (MEM_HEAD_DIM ** -0.5)
        p = jnp.exp(s - jnp.max(s, axis=-1, keepdims=True))
        o = jnp.dot(p.astype(BF16), v, preferred_element_type=F32)
        outs.append(o / jnp.sum(p, axis=-1, keepdims=True))
    return jnp.concatenate(outs, axis=-1)


def _proj_in_kernel(x_ref, g_ref, w_ref, gtok_ref, gmem_ref, *refs, fox, fuse_mem):
    xn = _rmsnorm(x_ref[...], g_ref[...]).astype(BF16)
    z = jnp.dot(xn, w_ref[...], preferred_element_type=F32)
    z_tok = z[:, :TOK_WIDTH]
    qm = _headnorm128(z[:, TOK_WIDTH:], gmem_ref[...])
    if fuse_mem:
        mk_ref, mv_ref, tok_ref, mem_ref = refs
        mem_ref[...] = _mem_attend(qm, mk_ref, mv_ref).astype(mem_ref.dtype)
    else:
        tok_ref, mem_ref = refs
        mem_ref[...] = qm.astype(mem_ref.dtype)
    if fox:
        tok_ref[...] = (_headnorm64(z_tok, gtok_ref[...]) * (FOX_HEAD_DIM ** -0.5)).astype(tok_ref.dtype)
    else:
        tok_ref[...] = z_tok


def _proj_in(x, g, w, g_tok, g_mem, mem=None, *, fox, tm, act):
    t = x.shape[0]
    row = lambda i: (i, 0)
    fixed = lambda i: (0, 0)
    operands = [x, g, w, g_tok, g_mem]
    in_specs = [pl.BlockSpec((tm, D_MODEL), row),
                pl.BlockSpec((1, D_MODEL), fixed),
                pl.BlockSpec((D_MODEL, D_MODEL), fixed),
                pl.BlockSpec((1, TOK_WIDTH), fixed),
                pl.BlockSpec((1, MEM_WIDTH), fixed)]
    if mem is not None:
        mk, mv, layer, rows_per_batch = mem
        tiles = rows_per_batch // tm
        mem_block = pl.BlockSpec((None, None, N_MEM * MEM_HEADS, MEM_HEAD_DIM),
                                 lambda i: (layer, i // tiles, 0, 0))
        operands += [mk, mv]
        in_specs += [mem_block, mem_block]
    return pl.pallas_call(
        functools.partial(_proj_in_kernel, fox=fox, fuse_mem=mem is not None),
        out_shape=(jax.ShapeDtypeStruct((t, TOK_WIDTH), act if fox else F32),
                   jax.ShapeDtypeStruct((t, MEM_WIDTH), act)),
        grid=(t // tm,),
        in_specs=in_specs,
        out_specs=(pl.BlockSpec((tm, TOK_WIDTH), row), pl.BlockSpec((tm, MEM_WIDTH), row)),
        compiler_params=pltpu.CompilerParams(dimension_semantics=("arbitrary",),
                                             vmem_limit_bytes=VMEM_LIMIT),
        name="proj_in_fox" if fox else "proj_in_pool",
    )(*operands)


def _log_sigmoid(x):
    return jnp.minimum(x, 0.0) - jnp.log(1.0 + jnp.exp(-jnp.abs(x)))


def _proj_kv_kernel(x_ref, g_ref, w_ref, gk_ref, wfg_ref, bfg_ref, k_ref, v_ref, logf_ref):
    xn = _rmsnorm(x_ref[...], g_ref[...]).astype(BF16)
    z = jnp.dot(xn, w_ref[...], preferred_element_type=F32)
    k_ref[...] = _headnorm64(z[:, :TOK_WIDTH], gk_ref[...])
    v_ref[...] = z[:, TOK_WIDTH:]
    gate = lax.dot_general(wfg_ref[...], xn, NT_DIMS, preferred_element_type=F32) + bfg_ref[...]
    logf_ref[...] = _log_sigmoid(gate)


def _proj_kv_paged_kernel(x_ref, g_ref, w_ref, gk_ref, wfg_ref, bfg_ref,
                          kp_ref, vp_ref, kb_ref, vb_ref, logf_ref, logfp_ref):
    xn = _rmsnorm(x_ref[...], g_ref[...]).astype(BF16)
    z = jnp.dot(xn, w_ref[...], preferred_element_type=F32)
    k = _headnorm64(z[:, :TOK_WIDTH], gk_ref[...])
    v = z[:, TOK_WIDTH:]
    vb_ref[...] = v.astype(BF16)
    gate = lax.dot_general(wfg_ref[...], xn, NT_DIMS, preferred_element_type=F32) + bfg_ref[...]
    logf = _log_sigmoid(gate)
    logf_ref[...] = logf
    k_t = k.T
    v_t = v.T
    kb_ref[...] = k_t.astype(BF16)
    for pg in range(kp_ref.shape[0]):
        rows = slice(pg * PAGE_SIZE, (pg + 1) * PAGE_SIZE)
        kp_ref[pg] = k_t[:, rows]
        vp_ref[pg] = v_t[:, rows]
        logfp_ref[pg] = logf[:, rows]


def _proj_kv_paged(x, g, w, g_k, wfg_t, bfg, *, tm):
    t = x.shape[0]
    row = lambda i: (i, 0)
    fixed = lambda i: (0, 0)
    ppt = tm // PAGE_SIZE
    page_rows = lambda i: (i, 0, 0)
    return pl.pallas_call(
        _proj_kv_paged_kernel,
        out_shape=(jax.ShapeDtypeStruct((t // PAGE_SIZE, TOK_WIDTH, PAGE_SIZE), F32),
                   jax.ShapeDtypeStruct((t // PAGE_SIZE, TOK_WIDTH, PAGE_SIZE), F32),
                   jax.ShapeDtypeStruct((TOK_WIDTH, t), BF16),
                   jax.ShapeDtypeStruct((t, TOK_WIDTH), BF16),
                   jax.ShapeDtypeStruct((FOX_HEADS, t), F32),
                   jax.ShapeDtypeStruct((t // PAGE_SIZE, FOX_HEADS, PAGE_SIZE), F32)),
        grid=(t // tm,),
        in_specs=[pl.BlockSpec((tm, D_MODEL), row),
                  pl.BlockSpec((1, D_MODEL), fixed),
                  pl.BlockSpec((D_MODEL, D_MODEL), fixed),
                  pl.BlockSpec((1, TOK_WIDTH), fixed),
                  pl.BlockSpec((FOX_HEADS, D_MODEL), fixed),
                  pl.BlockSpec((FOX_HEADS, 1), fixed)],
        out_specs=(pl.BlockSpec((ppt, TOK_WIDTH, PAGE_SIZE), page_rows),
                   pl.BlockSpec((ppt, TOK_WIDTH, PAGE_SIZE), page_rows),
                   pl.BlockSpec((TOK_WIDTH, tm), lambda i: (0, i)), pl.BlockSpec((tm, TOK_WIDTH), row),
                   pl.BlockSpec((FOX_HEADS, tm), lambda i: (0, i)),
                   pl.BlockSpec((ppt, FOX_HEADS, PAGE_SIZE), page_rows)),
        compiler_params=pltpu.CompilerParams(dimension_semantics=("arbitrary",),
                                             vmem_limit_bytes=VMEM_LIMIT),
        name="proj_kv_paged",
    )(x, g, w, g_k, wfg_t, bfg)


def _proj_kv(x, g, w, g_k, wfg_t, bfg, *, tm):
    t = x.shape[0]
    row = lambda i: (i, 0)
    fixed = lambda i: (0, 0)
    return pl.pallas_call(
        _proj_kv_kernel,
        out_shape=(jax.ShapeDtypeStruct((t, TOK_WIDTH), F32),
                   jax.ShapeDtypeStruct((t, TOK_WIDTH), F32),
                   jax.ShapeDtypeStruct((FOX_HEADS, t), F32)),
        grid=(t // tm,),
        in_specs=[pl.BlockSpec((tm, D_MODEL), row),
                  pl.BlockSpec((1, D_MODEL), fixed),
                  pl.BlockSpec((D_MODEL, D_MODEL), fixed),
                  pl.BlockSpec((1, TOK_WIDTH), fixed),
                  pl.BlockSpec((FOX_HEADS, D_MODEL), fixed),
                  pl.BlockSpec((FOX_HEADS, 1), fixed)],
        out_specs=(pl.BlockSpec((tm, TOK_WIDTH), row), pl.BlockSpec((tm, TOK_WIDTH), row),
                   pl.BlockSpec((FOX_HEADS, tm), lambda i: (0, i))),
        compiler_params=pltpu.CompilerParams(dimension_semantics=("arbitrary",),
                                             vmem_limit_bytes=VMEM_LIMIT),
        name="proj_kv",
    )(x, g, w, g_k, wfg_t, bfg)


def _proj_memkv_kernel(x_ref, g_ref, w_ref, gk_ref, k_ref, v_ref):
    xn = _rmsnorm(x_ref[...], g_ref[0]).astype(BF16)
    z = jnp.dot(xn, w_ref[0], preferred_element_type=F32)
    k = _headnorm128(z[:, :MEM_WIDTH], gk_ref[0])
    tm = x_ref.shape[0]
    for h in range(MEM_HEADS):
        rows = pl.ds(h, tm, stride=MEM_HEADS)
        k_ref[0, rows, :] = k[:, h * MEM_HEAD_DIM:(h + 1) * MEM_HEAD_DIM]
        v_ref[0, rows, :] = z[:, MEM_WIDTH + h * MEM_HEAD_DIM:MEM_WIDTH + (h + 1) * MEM_HEAD_DIM]


def _proj_memkv(x, g, w, g_k, *, tm):
    t = x.shape[0]
    n_layers = w.shape[0]
    out_block = pl.BlockSpec((1, tm * MEM_HEADS, MEM_HEAD_DIM), lambda l, i: (l, i, 0))
    return pl.pallas_call(
        _proj_memkv_kernel,
        out_shape=(jax.ShapeDtypeStruct((n_layers, t * MEM_HEADS, MEM_HEAD_DIM), F32),
                   jax.ShapeDtypeStruct((n_layers, t * MEM_HEADS, MEM_HEAD_DIM), F32)),
        grid=(n_layers, t // tm),
        in_specs=[pl.BlockSpec((tm, D_MODEL), lambda l, i: (i, 0)),
                  pl.BlockSpec((1, 1, D_MODEL), lambda l, i: (l, 0, 0)),
                  pl.BlockSpec((1, D_MODEL, D_MODEL), lambda l, i: (l, 0, 0)),
                  pl.BlockSpec((1, 1, MEM_WIDTH), lambda l, i: (l, 0, 0))],
        out_specs=(out_block, out_block),
        compiler_params=pltpu.CompilerParams(dimension_semantics=("arbitrary", "arbitrary"),
                                             vmem_limit_bytes=VMEM_LIMIT),
        name="proj_memkv",
    )(x, g, w, g_k)


def _pool_kernel(u_ref, prev_ref, w_ref, scale_ref, o_ref, ext_ref, *, n_prev, chunk):
    bb, s_len, _ = u_ref.shape
    ext_ref[:, 0:POOL_HIST, :] = prev_ref[...]
    ext_ref[:, POOL_HIST:, :] = u_ref[...]
    for c in range(s_len // chunk):
        r0 = POOL_HIST + c * chunk
        pos = c * chunk + lax.broadcasted_iota(jnp.int32, (1, chunk, 1), 1)
        for gi, win in enumerate(POOL_WINDOWS):
            lanes = slice(gi * POOL_GROUP, (gi + 1) * POOL_GROUP)
            u_new = ext_ref[:, r0:r0 + chunk, lanes]
            acc = u_new
            for k in range(1, win):
                acc = acc + ext_ref[:, r0 - k:r0 - k + chunk, lanes]
            count = jnp.minimum(win, n_prev + pos + 1).astype(F32)
            y = acc / count - u_new
            z = jnp.dot(y.reshape(bb * chunk, POOL_GROUP).astype(BF16), w_ref[gi],
                        preferred_element_type=F32)
            z = z.reshape(bb, chunk, POOL_GROUP) * scale_ref[:, lanes]
            o_ref[:, c * chunk:(c + 1) * chunk, lanes] = z.astype(o_ref.dtype)


def _pool_mix(u, prev, w_pool, scale, *, n_prev, bb, chunk, act):
    b, s_len, _ = u.shape
    return pl.pallas_call(
        functools.partial(_pool_kernel, n_prev=n_prev, chunk=chunk),
        out_shape=jax.ShapeDtypeStruct((b, s_len, TOK_WIDTH), act),
        grid=(b // bb,),
        in_specs=[pl.BlockSpec((bb, s_len, TOK_WIDTH), lambda i: (i, 0, 0)),
                  pl.BlockSpec((bb, POOL_HIST, TOK_WIDTH), lambda i: (i, 0, 0)),
                  pl.BlockSpec((len(POOL_WINDOWS), POOL_GROUP, POOL_GROUP), lambda i: (0, 0, 0)),
                  pl.BlockSpec((1, TOK_WIDTH), lambda i: (0, 0))],
        out_specs=pl.BlockSpec((bb, s_len, TOK_WIDTH), lambda i: (i, 0, 0)),
        scratch_shapes=[pltpu.VMEM((bb, POOL_HIST + s_len, TOK_WIDTH), F32)],
        compiler_params=pltpu.CompilerParams(dimension_semantics=("arbitrary",),
                                             vmem_limit_bytes=VMEM_LIMIT),
        name="pool_mix",
    )(u, prev, w_pool, scale)


def _mem_attn_kernel(q_ref, k_ref, v_ref, o_ref):
    for h in range(MEM_HEADS):
        lanes = slice(h * MEM_HEAD_DIM, (h + 1) * MEM_HEAD_DIM)
        q = q_ref[:, :, lanes].astype(BF16)
        k = k_ref[:, pl.ds(h, N_MEM, stride=MEM_HEADS), :].astype(BF16)
        v = v_ref[:, pl.ds(h, N_MEM, stride=MEM_HEADS), :].astype(BF16)
        s = jnp.einsum("bqd,bkd->bqk", q, k, preferred_element_type=F32) * (MEM_HEAD_DIM ** -0.5)
        m = jnp.max(s, axis=-1, keepdims=True)
        p = jnp.exp(s - m)
        l = jnp.sum(p, axis=-1, keepdims=True)
        o = jnp.einsum("bqk,bkd->bqd", p.astype(BF16), v, preferred_element_type=F32)
        o_ref[:, :, lanes] = (o / l).astype(o_ref.dtype)


def _mem_attn(q, mk, mv, layer, *, bb, ts):
    b, s_len, _ = q.shape
    mem_block = pl.BlockSpec((None, bb, N_MEM * MEM_HEADS, MEM_HEAD_DIM),
                             lambda i, j: (layer, i, 0, 0))
    return pl.pallas_call(
        _mem_attn_kernel,
        out_shape=jax.ShapeDtypeStruct((b, s_len, MEM_WIDTH), q.dtype),
        grid=(b // bb, s_len // ts),
        in_specs=[pl.BlockSpec((bb, ts, MEM_WIDTH), lambda i, j: (i, j, 0)), mem_block, mem_block],
        out_specs=pl.BlockSpec((bb, ts, MEM_WIDTH), lambda i, j: (i, j, 0)),
        compiler_params=pltpu.CompilerParams(dimension_semantics=("arbitrary", "arbitrary"),
                                             vmem_limit_bytes=VMEM_LIMIT),
        name="mem_attn",
    )(q, mk, mv)


def _out_ffn_kernel(h_ref, tok_ref, mo_ref, wout_ref, g_ref, wg_ref, wu_ref, wd_ref, o_ref):
    h1 = (h_ref[...]
          + jnp.dot(tok_ref[...].astype(BF16), wout_ref[:TOK_WIDTH, :], preferred_element_type=F32)
          + jnp.dot(mo_ref[...].astype(BF16), wout_ref[TOK_WIDTH:, :], preferred_element_type=F32))
    x = _rmsnorm(h1, g_ref[...]).astype(BF16)
    gate = jnp.dot(x, wg_ref[...], preferred_element_type=F32)
    up = jnp.dot(x, wu_ref[...], preferred_element_type=F32)
    act = (gate * (1.0 / (1.0 + jnp.exp(-gate))) * up).astype(BF16)
    o_ref[...] = h1 + jnp.dot(act, wd_ref[...], preferred_element_type=F32)


def _out_ffn(h, tok, mo, w_out, g, w_gu, w_down, layer, *, tm):
    t = h.shape[0]
    row = lambda i: (i, 0)
    once = pl.Buffered(1)
    return pl.pallas_call(
        _out_ffn_kernel,
        out_shape=jax.ShapeDtypeStruct((t, D_MODEL), F32),
        grid=(t // tm,),
        in_specs=[pl.BlockSpec((tm, D_MODEL), row),
                  pl.BlockSpec((tm, TOK_WIDTH), row),
                  pl.BlockSpec((tm, MEM_WIDTH), row),
                  pl.BlockSpec((None, D_MODEL, D_MODEL), lambda i: (layer, 0, 0), pipeline_mode=once),
                  pl.BlockSpec((1, D_MODEL), lambda i: (0, 0)),
                  pl.BlockSpec((None, D_MODEL, D_FF), lambda i: (layer, 0, 0), pipeline_mode=once),
                  pl.BlockSpec((None, D_MODEL, D_FF), lambda i: (layer, 0, 1), pipeline_mode=once),
                  pl.BlockSpec((None, D_FF, D_MODEL), lambda i: (layer, 0, 0), pipeline_mode=once)],
        out_specs=pl.BlockSpec((tm, D_MODEL), row),
        compiler_params=pltpu.CompilerParams(dimension_semantics=("arbitrary",),
                                             vmem_limit_bytes=VMEM_LIMIT),
        name="out_ffn",
    )(h, tok, mo, w_out, g, w_gu, w_gu, w_down)


def _lane_cumsum(x):
    n = x.shape[-1]
    lane = lax.broadcasted_iota(jnp.int32, x.shape, 1)
    k = 1
    while k < n:
        x = x + jnp.where(lane >= k, pltpu.roll(x, k, axis=1), 0.0)
        k *= 2
    return x


def _cumsum_kernel(x_ref, o_ref):
    o_ref[...] = _lane_cumsum(x_ref[...])


def _cumsum_rows(x, *, seg):
    rows, t = x.shape
    return pl.pallas_call(
        _cumsum_kernel,
        out_shape=jax.ShapeDtypeStruct((rows, t), F32),
        grid=(t // seg,),
        in_specs=[pl.BlockSpec((rows, seg), lambda i: (0, i))],
        out_specs=pl.BlockSpec((rows, seg), lambda i: (0, i)),
        compiler_params=pltpu.CompilerParams(dimension_semantics=("arbitrary",)),
        name="logf_cumsum",
    )(x)


def _round_to_bf16(x):
    return x.astype(BF16).astype(F32)


def _fox_prompt_kernel(q_ref, k_ref, v_ref, frow_ref, o_ref, m_ref, l_ref, acc_ref, alpha_ref,
                       s_ref, p_all_ref, *, blk, n_q_blocks):
    qi = pl.program_id(2)
    q = q_ref[0]
    lane = lax.broadcasted_iota(jnp.int32, q.shape, 1)
    sub = FOX_CHUNK_ROWS
    n_rep = blk // LANES
    bias_lane = [FOX_HEAD_DIM, 0]
    q_f32 = q.astype(F32)
    q_head = []
    for j in range(2):
        own = (lane < FOX_HEAD_DIM) if j == 0 else (lane >= FOX_HEAD_DIM)
        ones_from = jnp.where(lane >= bias_lane[j], 1.0, 0.0)
        bias_ones = jnp.where(lane < bias_lane[j] + 3, ones_from, 0.0)
        q_head.append(jnp.where(own, q_f32, bias_ones).astype(BF16))
    m_ref[...] = jnp.full(m_ref.shape, -jnp.inf, F32)
    l_ref[...] = jnp.zeros_like(l_ref)
    acc_ref[...] = jnp.zeros_like(acc_ref)

    def scores(qc, ki, s_buf):
        keys = slice(ki * blk, (ki + 1) * blk)
        kt = k_ref[:, keys]
        tile_row = lax.broadcasted_iota(jnp.int32, (BF16_ROWS, blk), 0)
        for j in range(2):
            f_end = frow_ref[0, j:j + 1, (qc + 1) * blk - 1:(qc + 1) * blk]
            bias = f_end - frow_ref[0, j:j + 1, keys]
            hi = _round_to_bf16(bias)
            mid = _round_to_bf16(bias - hi)
            lo = _round_to_bf16(bias - hi - mid)
            bias_rows = jnp.where(tile_row == 0, hi, jnp.where(tile_row == 1, mid,
                                                               jnp.where(tile_row == 2, lo, 0.0)))
            b0 = bias_lane[j]
            pieces = [kt[:b0], bias_rows.astype(BF16), kt[b0 + BF16_ROWS:]]
            kt_j = jnp.concatenate([x for x in pieces if x.shape[0]], axis=0)
            s_buf[j] = jnp.dot(q_head[j], kt_j, preferred_element_type=F32)

    def update(ki, s_buf, masked):
        p_ref = p_all_ref.at[ki]
        vt = v_ref[0, ki * blk:(ki + 1) * blk, :]
        for j in range(2):
            def chunk(r):
                n_use = -(-(r + 1) * sub // LANES) if masked else n_rep
                rows = slice(r * sub, (r + 1) * sub)
                s = s_buf[j, rows, 0:n_use * LANES]
                if masked:
                    row = lax.broadcasted_iota(jnp.int32, s.shape, 0) + r * sub
                    col = lax.broadcasted_iota(jnp.int32, s.shape, 1)
                    s = jnp.where(col <= row, s, -jnp.inf)
                return rows, n_use, s

            for r in range(blk // sub):
                rows, n_use, s = chunk(r)
                m_prev = m_ref[j, rows]
                m_new = jnp.maximum(m_prev, jnp.max(s, axis=-1, keepdims=True))
                alpha_ref[j, rows] = jnp.exp(m_prev - m_new)
                m_ref[j, rows] = m_new
            for r in range(blk // sub):
                rows, n_use, s = chunk(r)
                p = jnp.exp(s - jnp.concatenate([m_ref[j, rows]] * n_use, axis=1))
                p_lanes = p[:, :LANES]
                for c in range(1, n_use):
                    p_lanes = p_lanes + p[:, c * LANES:(c + 1) * LANES]
                l_ref[j, rows] = alpha_ref[j, rows] * l_ref[j, rows] + p_lanes
                p_ref[j, rows, 0:n_use * LANES] = p.astype(BF16)
                if n_use < n_rep:
                    p_ref[j, rows, n_use * LANES:] = jnp.zeros((sub, blk - n_use * LANES), BF16)
            acc_ref[j] = alpha_ref[j] * acc_ref[j] + jnp.dot(p_ref[j], vt, preferred_element_type=F32)

    for qc in range(n_q_blocks):
        @pl.when(qi == qc)
        def _(qc=qc):
            scores(qc, 0, s_ref.at[0])
            for ki in range(qc):
                scores(qc, ki + 1, s_ref.at[ki + 1])
                update(ki, s_ref.at[ki], False)
            update(qc, s_ref.at[qc], True)

    l0 = jnp.sum(l_ref[0], axis=-1, keepdims=True)
    l1 = jnp.sum(l_ref[1], axis=-1, keepdims=True)
    o = jnp.where(lane < FOX_HEAD_DIM, acc_ref[0] / l0, acc_ref[1] / l1)
    o_ref[0] = o.astype(o_ref.dtype)


def _fox_prompt(q, k, v, frow, *, blk):
    b, s_len, _ = q.shape
    n_pairs = TOK_WIDTH // LANES
    nq = s_len // blk
    return pl.pallas_call(
        functools.partial(_fox_prompt_kernel, blk=blk, n_q_blocks=nq),
        out_shape=jax.ShapeDtypeStruct((b, s_len, TOK_WIDTH), BF16),
        grid=(b, n_pairs, nq),
        in_specs=[pl.BlockSpec((1, blk, LANES), lambda bi, hp, qi: (bi, qi, hp)),
                  pl.BlockSpec((LANES, s_len), lambda bi, hp, qi: (hp, bi)),
                  pl.BlockSpec((1, s_len, LANES), lambda bi, hp, qi: (bi, 0, hp)),
                  pl.BlockSpec((1, 2, s_len), lambda bi, hp, qi: (hp, 0, bi))],
        out_specs=pl.BlockSpec((1, blk, LANES), lambda bi, hp, qi: (bi, qi, hp)),
        scratch_shapes=[pltpu.VMEM((2, blk, LANES), F32), pltpu.VMEM((2, blk, LANES), F32),
                        pltpu.VMEM((2, blk, LANES), F32), pltpu.VMEM((2, blk, LANES), F32),
                        pltpu.VMEM((nq, 2, blk, blk), F32),
                        pltpu.VMEM((nq, 2, blk, blk), BF16)],
        compiler_params=pltpu.CompilerParams(
            dimension_semantics=("arbitrary", "arbitrary", "arbitrary"),
            vmem_limit_bytes=VMEM_LIMIT),
        name="fox_prompt",
    )(q, k, v, frow)


def _lane_suffix_sum(x):
    n = x.shape[-1]
    lane = lax.broadcasted_iota(jnp.int32, x.shape, 1)
    k = 1
    while k < n:
        x = x + jnp.where(lane + k < n, pltpu.roll(x, n - k, axis=1), 0.0)
        k *= 2
    return x


def _fox_paged_kernel(pt_ref, q_ref, kn_ref, vn_ref, lfn_ref, logf_ref, k_hbm, v_hbm, o_ref,
                      kbuf, vbuf, sem, pad_k_ref, pad_v_ref, *, n_pages, n_batch):
    b = pl.program_id(0)
    n_steps = pl.num_programs(0)
    slot = lax.rem(b, PAGE_SLOTS)
    n_q = q_ref.shape[1]
    n_rows = n_q * FOX_HEADS

    def page_copies(batch, slot_):
        copies = []
        for j in range(n_pages):
            page = pt_ref[j * n_batch + batch]
            copies.append(pltpu.make_async_copy(k_hbm.at[page], kbuf.at[slot_, j], sem.at[0, slot_]))
            copies.append(pltpu.make_async_copy(v_hbm.at[page], vbuf.at[slot_, j], sem.at[1, slot_]))
        return copies

    ahead = PAGE_SLOTS - 1
    for r in range(ahead):
        @pl.when(jnp.logical_and(b == 0, r < n_steps))
        def _(r=r):
            for c in page_copies(r, r):
                c.start()

    @pl.when(b + ahead < n_steps)
    def _():
        for c in page_copies(b + ahead, lax.rem(b + ahead, PAGE_SLOTS)):
            c.start()

    head = lax.broadcasted_iota(jnp.int32, (FOX_HEADS, TOK_WIDTH), 0)
    lane_head = jnp.right_shift(lax.broadcasted_iota(jnp.int32, (FOX_HEADS, TOK_WIDTH), 1), 6)
    own_head = head == lane_head
    tile_q = lambda x: jnp.concatenate([x] * n_q, axis=0)

    q = q_ref[0]
    qblk = jnp.concatenate(
        [jnp.where(own_head, jnp.broadcast_to(q[t:t + 1, :], (FOX_HEADS, TOK_WIDTH)), 0.0)
         for t in range(n_q)], axis=0).astype(BF16)

    pad_k_ref[...] = jnp.zeros_like(pad_k_ref)
    pad_v_ref[...] = jnp.zeros_like(pad_v_ref)
    pad_k_ref[0:n_q, :] = kn_ref[0]
    pad_v_ref[0:n_q, :] = vn_ref[0]
    c_new = _lane_cumsum(lfn_ref[0])
    cq = jnp.concatenate([c_new[:, t:t + 1] for t in range(n_q)], axis=0)
    s_own = lax.dot_general(qblk, pad_k_ref[...].astype(BF16), NT_DIMS, preferred_element_type=F32)
    t_of_row = jnp.right_shift(lax.broadcasted_iota(jnp.int32, (n_rows, PAGE_SIZE), 0), 3)
    key = lax.broadcasted_iota(jnp.int32, (n_rows, PAGE_SIZE), 1)
    s_own = jnp.where(key <= t_of_row, s_own + (tile_q(-c_new) + cq), -jnp.inf)

    logf = [logf_ref[pt_ref[j * n_batch + b]] for j in range(n_pages)]
    incl = _lane_suffix_sum(jnp.concatenate(logf, axis=0))
    later_pages = jnp.zeros((FOX_HEADS, 1), F32)
    bias = [None] * n_pages
    for j in reversed(range(n_pages)):
        rows = slice(j * FOX_HEADS, (j + 1) * FOX_HEADS)
        bias[j] = tile_q(later_pages + (incl[rows] - logf[j])) + cq
        later_pages = later_pages + incl[rows, 0:1]

    for c in page_copies(b, slot):
        c.wait()

    scores = [s_own]
    for j in range(n_pages):
        k_t = kbuf[slot, j].reshape(TOK_WIDTH, PAGE_SIZE).astype(BF16)
        scores.append(jnp.dot(qblk, k_t, preferred_element_type=F32) + bias[j])
    s_max = scores[0]
    for s in scores[1:]:
        s_max = jnp.maximum(s_max, s)
    m = jnp.max(s_max, axis=-1, keepdims=True)
    p = jnp.exp(scores[0] - m)
    p_sum = p
    acc = jnp.dot(p.astype(BF16), pad_v_ref[...].astype(BF16), preferred_element_type=F32)
    for j in range(n_pages):
        p = jnp.exp(scores[j + 1] - m)
        p_sum = p_sum + p
        v_t = vbuf[slot, j].reshape(TOK_WIDTH, PAGE_SIZE).astype(BF16)
        acc = acc + lax.dot_general(p.astype(BF16), v_t, NT_DIMS, preferred_element_type=F32)
    o = acc / jnp.sum(p_sum, axis=-1, keepdims=True)
    for t in range(n_q):
        o_t = jnp.where(own_head, o[t * FOX_HEADS:(t + 1) * FOX_HEADS, :], 0.0)
        o_ref[0, t:t + 1, :] = jnp.sum(o_t, axis=0, keepdims=True)


def _fox_paged(page_table, q, k_new, v_new, lfn, cache_logf_t, cache_k_t, cache_v_t):
    b, n_q, _ = q.shape
    n_pages = page_table.shape[0] // b
    cur = lambda bi, pt: (bi, 0, 0)
    page_shape = cache_k_t.shape[1:]
    grid_spec = pltpu.PrefetchScalarGridSpec(
        num_scalar_prefetch=1,
        grid=(b,),
        in_specs=[pl.BlockSpec((1, n_q, TOK_WIDTH), cur),
                  pl.BlockSpec((1, n_q, TOK_WIDTH), cur),
                  pl.BlockSpec((1, n_q, TOK_WIDTH), cur),
                  pl.BlockSpec((1, FOX_HEADS, LANES), cur),
                  pl.BlockSpec(cache_logf_t.shape, lambda bi, pt: (0, 0, 0), pipeline_mode=pl.Buffered(1)),
                  pl.BlockSpec(memory_space=pl.ANY),
                  pl.BlockSpec(memory_space=pl.ANY)],
        out_specs=pl.BlockSpec((1, n_q, TOK_WIDTH), cur),
        scratch_shapes=[pltpu.VMEM((PAGE_SLOTS, n_pages) + page_shape, F32),
                        pltpu.VMEM((PAGE_SLOTS, n_pages) + page_shape, F32),
                        pltpu.SemaphoreType.DMA((2, PAGE_SLOTS)),
                        pltpu.VMEM((PAGE_SIZE, TOK_WIDTH), F32),
                        pltpu.VMEM((PAGE_SIZE, TOK_WIDTH), F32)])
    return pl.pallas_call(
        functools.partial(_fox_paged_kernel, n_pages=n_pages, n_batch=b),
        out_shape=jax.ShapeDtypeStruct((b, n_q, TOK_WIDTH), F32),
        grid_spec=grid_spec,
        compiler_params=pltpu.CompilerParams(dimension_semantics=("arbitrary",),
                                             vmem_limit_bytes=VMEM_LIMIT),
        name="fox_paged",
    )(page_table, q, k_new, v_new, lfn, cache_logf_t, cache_k_t, cache_v_t)


def _decoder(x, pool_prev, n_prev, mem_k, mem_v, past, wts, *, tm, pool_bb, pool_chunk, mem_bb, mem_ts,
             act):
    b, s_len, _ = x.shape
    t = b * s_len
    h = x.reshape(t, D_MODEL)

    fuse_mem = s_len % tm == 0

    def project_and_attend(layer, fox):
        mem = (mem_k, mem_v, layer, s_len) if fuse_mem else None
        tok_part, mem_part = _proj_in(h, wts["g_mix"][layer], wts["w_in"][layer], wts["g_fox_q_t"],
                                      wts["g_mem_q_t"][layer], mem, fox=fox, tm=tm, act=act)
        if not fuse_mem:
            mem_part = _mem_attn(mem_part.reshape(b, s_len, MEM_WIDTH), mem_k, mem_v, layer,
                                 bb=mem_bb, ts=mem_ts)
        return tok_part, mem_part.reshape(t, MEM_WIDTH)

    z_tok, mo = project_and_attend(0, fox=False)
    z_tok = z_tok.reshape(b, s_len, TOK_WIDTH)
    tok = _pool_mix(z_tok, pool_prev, wts["w_pool"], wts["pool_scale"],
                    n_prev=n_prev, bb=pool_bb, chunk=pool_chunk, act=act)
    h = _out_ffn(h, tok.reshape(t, TOK_WIDTH), mo, wts["w_out"],
                 wts["g_ffn"][0], wts["w_gu"], wts["w_down"], 0, tm=tm)

    kv_args = (h, wts["g_kv"], wts["w_kv"], wts["g_fox_k_t"], wts["w_fg_t"], wts["b_fg"])
    if past is None:
        k_new, v_new, k_bf, v_bf, logf_t, logf_new = _proj_kv_paged(*kv_args, tm=tm)
    else:
        k_new, v_new, logf_t = _proj_kv(*kv_args, tm=tm)
        logf_new = logf_t.T

    q, mo = project_and_attend(1, fox=True)
    if past is None:
        f_row = _cumsum_rows(logf_t, seg=s_len)
        n_pairs = FOX_HEADS // 2
        f_row = f_row.reshape(n_pairs, 2, t)
        tok = _fox_prompt(q.reshape(b, s_len, TOK_WIDTH), k_bf,
                          v_bf.reshape(b, s_len, TOK_WIDTH), f_row, blk=512)
    else:
        page_table, cache_k_t, cache_v_t, cache_logf_t = past
        lfn = logf_t.reshape(FOX_HEADS, b, s_len).transpose(1, 0, 2)
        lfn = jnp.pad(lfn, ((0, 0), (0, 0), (0, LANES - s_len)))
        per_row = lambda a: a.reshape(b, s_len, TOK_WIDTH)
        tok = _fox_paged(page_table, per_row(q), per_row(k_new), per_row(v_new), lfn,
                         cache_logf_t, cache_k_t, cache_v_t)
    h = _out_ffn(h, tok.reshape(t, TOK_WIDTH), mo, wts["w_out"],
                 wts["g_ffn"][1], wts["w_gu"], wts["w_down"], 1, tm=tm)
    return h.reshape(b, s_len, D_MODEL), z_tok, k_new, v_new, logf_new


def kernel(x_prompt, x_sample, cache_mem_k, cache_mem_v, state_pool, cache_k, cache_v, cache_logf,
           page_table, mem_prompt, g_mix, w_in, w_out, g_ffn, w_gu, w_down, g_mem, w_mem_kv,
           g_mem_q, g_mem_k, w_pool, pool_scale, g_kv, w_kv, g_fox_k, w_fg, b_fg, g_fox_q):
    depth = w_in.shape[0]
    b, s_len, _ = x_prompt.shape
    db, ds, _ = x_sample.shape
    wts = {
        "g_mix": g_mix.reshape(depth, 1, D_MODEL),
        "w_in": w_in.astype(BF16),
        "w_out": w_out.astype(BF16),
        "g_ffn": g_ffn.reshape(depth, 1, D_MODEL),
        "w_gu": w_gu.astype(BF16),
        "w_down": w_down.astype(BF16),
        "g_mem_q_t": jnp.tile(g_mem_q, (1, MEM_HEADS)).reshape(depth, 1, MEM_WIDTH),
        "w_pool": w_pool[0].astype(BF16),
        "pool_scale": pool_scale[0].reshape(1, TOK_WIDTH),
        "g_kv": g_kv.reshape(1, D_MODEL),
        "w_kv": w_kv.astype(BF16),
        "g_fox_k_t": jnp.tile(g_fox_k, FOX_HEADS).reshape(1, TOK_WIDTH),
        "w_fg_t": w_fg.T.astype(BF16),
        "b_fg": b_fg.reshape(FOX_HEADS, 1),
        "g_fox_q_t": jnp.tile(g_fox_q[0], FOX_HEADS).reshape(1, TOK_WIDTH),
    }

    mem_k_p, mem_v_p = _proj_memkv(mem_prompt.reshape(b * N_MEM, D_MODEL),
                                   g_mem.reshape(depth, 1, D_MODEL), w_mem_kv.astype(BF16),
                                   jnp.tile(g_mem_k, (1, MEM_HEADS)).reshape(depth, 1, MEM_WIDTH), tm=512)
    mem_rows = (N_MEM * MEM_HEADS, MEM_HEAD_DIM)
    mem_k_p = mem_k_p.reshape((depth, b) + mem_rows)
    mem_v_p = mem_v_p.reshape((depth, b) + mem_rows)
    y_p, ztok_p, k_p, v_p, logf_p = _decoder(
        x_prompt, jnp.zeros((b, POOL_HIST, TOK_WIDTH), F32), 0, mem_k_p, mem_v_p, None, wts,
        tm=512, pool_bb=1, pool_chunk=256, mem_bb=1, mem_ts=512, act=BF16)
    n_pp = s_len // PAGE_SIZE
    head_shape = (FOX_HEADS, FOX_HEAD_DIM)

    past = (page_table.T.reshape(-1),
            cache_k.transpose(0, 2, 3, 1), cache_v.transpose(0, 2, 3, 1),
            cache_logf.transpose(0, 2, 1))
    prev = jnp.pad(state_pool[0], ((0, 0), (POOL_HIST - POOL_STATE, 0), (0, 0)))
    y_s, ztok_s, k_s, v_s, logf_s = _decoder(
        x_sample, prev, POOL_STATE, cache_mem_k.reshape((depth, db) + mem_rows),
        cache_mem_v.reshape((depth, db) + mem_rows), past, wts,
        tm=512, pool_bb=32, pool_chunk=ds, mem_bb=8, mem_ts=ds, act=F32)

    pool_state_p = ztok_p[:, s_len - POOL_STATE:][None]
    pool_state_s = jnp.concatenate([state_pool[0], ztok_s], axis=1)[:, -POOL_STATE:][None]
    paged = lambda a: a.reshape((b, n_pp) + head_shape + (PAGE_SIZE,)).transpose(0, 1, 4, 2, 3)
    return (y_p, y_s, paged(k_p), paged(v_p),
            logf_p.reshape(b, n_pp, FOX_HEADS, PAGE_SIZE).transpose(0, 1, 3, 2),
            mem_k_p.reshape(depth, b, N_MEM, MEM_HEADS, MEM_HEAD_DIM),
            mem_v_p.reshape(depth, b, N_MEM, MEM_HEADS, MEM_HEAD_DIM),
            pool_state_p,
            k_s.reshape((db, ds) + head_shape),
            v_s.reshape((db, ds) + head_shape),
            logf_s.reshape(db, ds, FOX_HEADS),
            pool_state_s)
```

```python
import functools

import jax
import jax.numpy as jnp
from jax import lax
from jax.experimental import pallas as pl
from jax.experimental.pallas import tpu as pltpu

D_MODEL = 1024
TOK_WIDTH = 512
MEM_WIDTH = 512
POOL_WINDOWS = (2, 4, 8, 16)
POOL_GROUP = 128
POOL_STATE = 15
POOL_HIST = 16
FOX_HEADS = 8
FOX_HEAD_DIM = 64
MEM_HEADS = 4
MEM_HEAD_DIM = 128
N_MEM = 256
D_FF = 2816
PAGE_SIZE = 128
EPS = 1e-6
LANES = 128
BF16_ROWS = 16
VMEM_LIMIT = 56 * 1024 * 1024
FOX_CHUNK_ROWS = 16
PAGE_SLOTS = 4

F32 = jnp.float32
BF16 = jnp.bfloat16
NT_DIMS = (((1,), (1,)), ((), ()))


def _rmsnorm(x, g):
    return x * lax.rsqrt(jnp.mean(x * x, axis=-1, keepdims=True) + EPS) * g


def _headnorm128(z, g):
    outs = []
    for h in range(z.shape[-1] // LANES):
        zh = z[:, h * LANES:(h + 1) * LANES]
        r = lax.rsqrt(jnp.mean(zh * zh, axis=-1, keepdims=True) + EPS)
        outs.append(zh * r * g[:, h * LANES:(h + 1) * LANES])
    return jnp.concatenate(outs, axis=-1)


def _headnorm64(z, g):
    outs = []
    for p in range(z.shape[-1] // LANES):
        zp = z[:, p * LANES:(p + 1) * LANES]
        sq = zp * zp
        lane = lax.broadcasted_iota(jnp.int32, zp.shape, 1)
        lo = lane < FOX_HEAD_DIM
        s_lo = jnp.sum(jnp.where(lo, sq, 0.0), axis=-1, keepdims=True)
        s_hi = jnp.sum(jnp.where(lo, 0.0, sq), axis=-1, keepdims=True)
        r = jnp.where(lo, lax.rsqrt(s_lo / FOX_HEAD_DIM + EPS), lax.rsqrt(s_hi / FOX_HEAD_DIM + EPS))
        outs.append(zp * r * g[:, p * LANES:(p + 1) * LANES])
    return jnp.concatenate(outs, axis=-1)


def _mem_attend(q, k_ref, v_ref):
    outs = []
    for h in range(MEM_HEADS):
        qh = q[:, h * MEM_HEAD_DIM:(h + 1) * MEM_HEAD_DIM].astype(BF16)
        k = k_ref[pl.ds(h, N_MEM, stride=MEM_HEADS), :].astype(BF16)
        v = v_ref[pl.ds(h, N_MEM, stride=MEM_HEADS), :].astype(BF16)
        s = lax.dot_general(qh, k, NT_DIMS, preferred_element_type=F32) * (MEM_HEAD_DIM ** -0.5)
        p = jnp.exp(s - jnp.max(s, axis=-1, keepdims=True))
        o = jnp.dot(p.astype(BF16), v, preferred_element_type=F32)
        outs.append(o / jnp.sum(p, axis=-1, keepdims=True))
    return jnp.concatenate(outs, axis=-1)


def _proj_in_kernel(x_ref, g_ref, w_ref, gtok_ref, gmem_ref, *refs, fox, fuse_mem):
    xn = _rmsnorm(x_ref[...], g_ref[...]).astype(BF16)
    z = jnp.dot(xn, w_ref[...], preferred_element_type=F32)
    z_tok = z[:, :TOK_WIDTH]
    qm = _headnorm128(z[:, TOK_WIDTH:], gmem_ref[...])
    if fuse_mem:
        mk_ref, mv_ref, tok_ref, mem_ref = refs
        mem_ref[...] = _mem_attend(qm, mk_ref, mv_ref).astype(mem_ref.dtype)
    else:
        tok_ref, mem_ref = refs
        mem_ref[...] = qm.astype(mem_ref.dtype)
    if fox:
        tok_ref[...] = (_headnorm64(z_tok, gtok_ref[...]) * (FOX_HEAD_DIM ** -0.5)).astype(tok_ref.dtype)
    else:
        tok_ref[...] = z_tok


def _proj_in(x, g, w, g_tok, g_mem, mem=None, *, fox, tm, act):
    t = x.shape[0]
    row = lambda i: (i, 0)
    fixed = lambda i: (0, 0)
    operands = [x, g, w, g_tok, g_mem]
    in_specs = [pl.BlockSpec((tm, D_MODEL), row),
                pl.BlockSpec((1, D_MODEL), fixed),
                pl.BlockSpec((D_MODEL, D_MODEL), fixed),
                pl.BlockSpec((1, TOK_WIDTH), fixed),
                pl.BlockSpec((1, MEM_WIDTH), fixed)]
    if mem is not None:
        mk, mv, layer, rows_per_batch = mem
        tiles = rows_per_batch // tm
        mem_block = pl.BlockSpec((None, None, N_MEM * MEM_HEADS, MEM_HEAD_DIM),
                                 lambda i: (layer, i // tiles, 0, 0))
        operands += [mk, mv]
        in_specs += [mem_block, mem_block]
    return pl.pallas_call(
        functools.partial(_proj_in_kernel, fox=fox, fuse_mem=mem is not None),
        out_shape=(jax.ShapeDtypeStruct((t, TOK_WIDTH), act if fox else F32),
                   jax.ShapeDtypeStruct((t, MEM_WIDTH), act)),
        grid=(t // tm,),
        in_specs=in_specs,
        out_specs=(pl.BlockSpec((tm, TOK_WIDTH), row), pl.BlockSpec((tm, MEM_WIDTH), row)),
        compiler_params=pltpu.CompilerParams(dimension_semantics=("arbitrary",),
                                             vmem_limit_bytes=VMEM_LIMIT),
        name="proj_in_fox" if fox else "proj_in_pool",
    )(*operands)


def _log_sigmoid(x):
    return jnp.minimum(x, 0.0) - jnp.log(1.0 + jnp.exp(-jnp.abs(x)))


def _proj_kv_kernel(x_ref, g_ref, w_ref, gk_ref, wfg_ref, bfg_ref, k_ref, v_ref, logf_ref):
    xn = _rmsnorm(x_ref[...], g_ref[...]).astype(BF16)
    z = jnp.dot(xn, w_ref[...], preferred_element_type=F32)
    k_ref[...] = _headnorm64(z[:, :TOK_WIDTH], gk_ref[...])
    v_ref[...] = z[:, TOK_WIDTH:]
    gate = lax.dot_general(wfg_ref[...], xn, NT_DIMS, preferred_element_type=F32) + bfg_ref[...]
    logf_ref[...] = _log_sigmoid(gate)


def _proj_kv_paged_kernel(x_ref, g_ref, w_ref, gk_ref, wfg_ref, bfg_ref,
                          kp_ref, vp_ref, kb_ref, vb_ref, logf_ref, logfp_ref):
    xn = _rmsnorm(x_ref[...], g_ref[...]).astype(BF16)
    z = jnp.dot(xn, w_ref[...], preferred_element_type=F32)
    k = _headnorm64(z[:, :TOK_WIDTH], gk_ref[...])
    v = z[:, TOK_WIDTH:]
    vb_ref[...] = v.astype(BF16)
    gate = lax.dot_general(wfg_ref[...], xn, NT_DIMS, preferred_element_type=F32) + bfg_ref[...]
    logf = _log_sigmoid(gate)
    logf_ref[...] = logf
    k_t = k.T
    v_t = v.T
    kb_ref[...] = k_t.astype(BF16)
    for pg in range(kp_ref.shape[0]):
        rows = slice(pg * PAGE_SIZE, (pg + 1) * PAGE_SIZE)
        kp_ref[pg] = k_t[:, rows]
        vp_ref[pg] = v_t[:, rows]
        logfp_ref[pg] = logf[:, rows]


def _proj_kv_paged(x, g, w, g_k, wfg_t, bfg, *, tm):
    t = x.shape[0]
    row = lambda i: (i, 0)
    fixed = lambda i: (0, 0)
    ppt = tm // PAGE_SIZE
    page_rows = lambda i: (i, 0, 0)
    return pl.pallas_call(
        _proj_kv_paged_kernel,
        out_shape=(jax.ShapeDtypeStruct((t // PAGE_SIZE, TOK_WIDTH, PAGE_SIZE), F32),
                   jax.ShapeDtypeStruct((t // PAGE_SIZE, TOK_WIDTH, PAGE_SIZE), F32),
                   jax.ShapeDtypeStruct((TOK_WIDTH, t), BF16),
                   jax.ShapeDtypeStruct((t, TOK_WIDTH), BF16),
                   jax.ShapeDtypeStruct((FOX_HEADS, t), F32),
                   jax.ShapeDtypeStruct((t // PAGE_SIZE, FOX_HEADS, PAGE_SIZE), F32)),
        grid=(t // tm,),
        in_specs=[pl.BlockSpec((tm, D_MODEL), row),
                  pl.BlockSpec((1, D_MODEL), fixed),
                  pl.BlockSpec((D_MODEL, D_MODEL), fixed),
                  pl.BlockSpec((1, TOK_WIDTH), fixed),
                  pl.BlockSpec((FOX_HEADS, D_MODEL), fixed),
                  pl.BlockSpec((FOX_HEADS, 1), fixed)],
        out_specs=(pl.BlockSpec((ppt, TOK_WIDTH, PAGE_SIZE), page_rows),
                   pl.BlockSpec((ppt, TOK_WIDTH, PAGE_SIZE), page_rows),
                   pl.BlockSpec((TOK_WIDTH, tm), lambda i: (0, i)), pl.BlockSpec((tm, TOK_WIDTH), row),
                   pl.BlockSpec((FOX_HEADS, tm), lambda i: (0, i)),
                   pl.BlockSpec((ppt, FOX_HEADS, PAGE_SIZE), page_rows)),
        compiler_params=pltpu.CompilerParams(dimension_semantics=("arbitrary",),
                                             vmem_limit_bytes=VMEM_LIMIT),
        name="proj_kv_paged",
    )(x, g, w, g_k, wfg_t, bfg)


def _proj_kv(x, g, w, g_k, wfg_t, bfg, *, tm):
    t = x.shape[0]
    row = lambda i: (i, 0)
    fixed = lambda i: (0, 0)
    return pl.pallas_call(
        _proj_kv_kernel,
        out_shape=(jax.ShapeDtypeStruct((t, TOK_WIDTH), F32),
                   jax.ShapeDtypeStruct((t, TOK_WIDTH), F32),
                   jax.ShapeDtypeStruct((FOX_HEADS, t), F32)),
        grid=(t // tm,),
        in_specs=[pl.BlockSpec((tm, D_MODEL), row),
                  pl.BlockSpec((1, D_MODEL), fixed),
                  pl.BlockSpec((D_MODEL, D_MODEL), fixed),
                  pl.BlockSpec((1, TOK_WIDTH), fixed),
                  pl.BlockSpec((FOX_HEADS, D_MODEL), fixed),
                  pl.BlockSpec((FOX_HEADS, 1), fixed)],
        out_specs=(pl.BlockSpec((tm, TOK_WIDTH), row), pl.BlockSpec((tm, TOK_WIDTH), row),
                   pl.BlockSpec((FOX_HEADS, tm), lambda i: (0, i))),
        compiler_params=pltpu.CompilerParams(dimension_semantics=("arbitrary",),
                                             vmem_limit_bytes=VMEM_LIMIT),
        name="proj_kv",
    )(x, g, w, g_k, wfg_t, bfg)


def _proj_memkv_kernel(x_ref, g_ref, w_ref, gk_ref, k_ref, v_ref):
    xn = _rmsnorm(x_ref[...], g_ref[0]).astype(BF16)
    z = jnp.dot(xn, w_ref[0], preferred_element_type=F32)
    k = _headnorm128(z[:, :MEM_WIDTH], gk_ref[0])
    tm = x_ref.shape[0]
    for h in range(MEM_HEADS):
        rows = pl.ds(h, tm, stride=MEM_HEADS)
        k_ref[0, rows, :] = k[:, h * MEM_HEAD_DIM:(h + 1) * MEM_HEAD_DIM]
        v_ref[0, rows, :] = z[:, MEM_WIDTH + h * MEM_HEAD_DIM:MEM_WIDTH + (h + 1) * MEM_HEAD_DIM]


def _proj_memkv(x, g, w, g_k, *, tm):
    t = x.shape[0]
    n_layers = w.shape[0]
    out_block = pl.BlockSpec((1, tm * MEM_HEADS, MEM_HEAD_DIM), lambda l, i: (l, i, 0))
    return pl.pallas_call(
        _proj_memkv_kernel,
        out_shape=(jax.ShapeDtypeStruct((n_layers, t * MEM_HEADS, MEM_HEAD_DIM), F32),
                   jax.ShapeDtypeStruct((n_layers, t * MEM_HEADS, MEM_HEAD_DIM), F32)),
        grid=(n_layers, t // tm),
        in_specs=[pl.BlockSpec((tm, D_MODEL), lambda l, i: (i, 0)),
                  pl.BlockSpec((1, 1, D_MODEL), lambda l, i: (l, 0, 0)),
                  pl.BlockSpec((1, D_MODEL, D_MODEL), lambda l, i: (l, 0, 0)),
                  pl.BlockSpec((1, 1, MEM_WIDTH), lambda l, i: (l, 0, 0))],
        out_specs=(out_block, out_block),
        compiler_params=pltpu.CompilerParams(dimension_semantics=("arbitrary", "arbitrary"),
                                             vmem_limit_bytes=VMEM_LIMIT),
        name="proj_memkv",
    )(x, g, w, g_k)


def _pool_kernel(u_ref, prev_ref, w_ref, scale_ref, o_ref, ext_ref, *, n_prev, chunk):
    bb, s_len, _ = u_ref.shape
    ext_ref[:, 0:POOL_HIST, :] = prev_ref[...]
    ext_ref[:, POOL_HIST:, :] = u_ref[...]
    for c in range(s_len // chunk):
        r0 = POOL_HIST + c * chunk
        pos = c * chunk + lax.broadcasted_iota(jnp.int32, (1, chunk, 1), 1)
        for gi, win in enumerate(POOL_WINDOWS):
            lanes = slice(gi * POOL_GROUP, (gi + 1) * POOL_GROUP)
            u_new = ext_ref[:, r0:r0 + chunk, lanes]
            acc = u_new
            for k in range(1, win):
                acc = acc + ext_ref[:, r0 - k:r0 - k + chunk, lanes]
            count = jnp.minimum(win, n_prev + pos + 1).astype(F32)
            y = acc / count - u_new
            z = jnp.dot(y.reshape(bb * chunk, POOL_GROUP).astype(BF16), w_ref[gi],
                        preferred_element_type=F32)
            z = z.reshape(bb, chunk, POOL_GROUP) * scale_ref[:, lanes]
            o_ref[:, c * chunk:(c + 1) * chunk, lanes] = z.astype(o_ref.dtype)


def _pool_mix(u, prev, w_pool, scale, *, n_prev, bb, chunk, act):
    b, s_len, _ = u.shape
    return pl.pallas_call(
        functools.partial(_pool_kernel, n_prev=n_prev, chunk=chunk),
        out_shape=jax.ShapeDtypeStruct((b, s_len, TOK_WIDTH), act),
        grid=(b // bb,),
        in_specs=[pl.BlockSpec((bb, s_len, TOK_WIDTH), lambda i: (i, 0, 0)),
                  pl.BlockSpec((bb, POOL_HIST, TOK_WIDTH), lambda i: (i, 0, 0)),
                  pl.BlockSpec((len(POOL_WINDOWS), POOL_GROUP, POOL_GROUP), lambda i: (0, 0, 0)),
                  pl.BlockSpec((1, TOK_WIDTH), lambda i: (0, 0))],
        out_specs=pl.BlockSpec((bb, s_len, TOK_WIDTH), lambda i: (i, 0, 0)),
        scratch_shapes=[pltpu.VMEM((bb, POOL_HIST + s_len, TOK_WIDTH), F32)],
        compiler_params=pltpu.CompilerParams(dimension_semantics=("arbitrary",),
                                             vmem_limit_bytes=VMEM_LIMIT),
        name="pool_mix",
    )(u, prev, w_pool, scale)


def _mem_attn_kernel(q_ref, k_ref, v_ref, o_ref):
    for h in range(MEM_HEADS):
        lanes = slice(h * MEM_HEAD_DIM, (h + 1) * MEM_HEAD_DIM)
        q = q_ref[:, :, lanes].astype(BF16)
        k = k_ref[:, pl.ds(h, N_MEM, stride=MEM_HEADS), :].astype(BF16)
        v = v_ref[:, pl.ds(h, N_MEM, stride=MEM_HEADS), :].astype(BF16)
        s = jnp.einsum("bqd,bkd->bqk", q, k, preferred_element_type=F32) * (MEM_HEAD_DIM ** -0.5)
        m = jnp.max(s, axis=-1, keepdims=True)
        p = jnp.exp(s - m)
        l = jnp.sum(p, axis=-1, keepdims=True)
        o = jnp.einsum("bqk,bkd->bqd", p.astype(BF16), v, preferred_element_type=F32)
        o_ref[:, :, lanes] = (o / l).astype(o_ref.dtype)


def _mem_attn(q, mk, mv, layer, *, bb, ts):
    b, s_len, _ = q.shape
    mem_block = pl.BlockSpec((None, bb, N_MEM * MEM_HEADS, MEM_HEAD_DIM),
                             lambda i, j: (layer, i, 0, 0))
    return pl.pallas_call(
        _mem_attn_kernel,
        out_shape=jax.ShapeDtypeStruct((b, s_len, MEM_WIDTH), q.dtype),
        grid=(b // bb, s_len // ts),
        in_specs=[pl.BlockSpec((bb, ts, MEM_WIDTH), lambda i, j: (i, j, 0)), mem_block, mem_block],
        out_specs=pl.BlockSpec((bb, ts, MEM_WIDTH), lambda i, j: (i, j, 0)),
        compiler_params=pltpu.CompilerParams(dimension_semantics=("arbitrary", "arbitrary"),
                                             vmem_limit_bytes=VMEM_LIMIT),
        name="mem_attn",
    )(q, mk, mv)


def _out_ffn_kernel(h_ref, tok_ref, mo_ref, wout_ref, g_ref, wg_ref, wu_ref, wd_ref, o_ref):
    h1 = (h_ref[...]
          + jnp.dot(tok_ref[...].astype(BF16), wout_ref[:TOK_WIDTH, :], preferred_element_type=F32)
          + jnp.dot(mo_ref[...].astype(BF16), wout_ref[TOK_WIDTH:, :], preferred_element_type=F32))
    x = _rmsnorm(h1, g_ref[...]).astype(BF16)
    gate = jnp.dot(x, wg_ref[...], preferred_element_type=F32)
    up = jnp.dot(x, wu_ref[...], preferred_element_type=F32)
    act = (gate * (1.0 / (1.0 + jnp.exp(-gate))) * up).astype(BF16)
    o_ref[...] = h1 + jnp.dot(act, wd_ref[...], preferred_element_type=F32)


def _out_ffn(h, tok, mo, w_out, g, w_gu, w_down, layer, *, tm):
    t = h.shape[0]
    row = lambda i: (i, 0)
    once = pl.Buffered(1)
    return pl.pallas_call(
        _out_ffn_kernel,
        out_shape=jax.ShapeDtypeStruct((t, D_MODEL), F32),
        grid=(t // tm,),
        in_specs=[pl.BlockSpec((tm, D_MODEL), row),
                  pl.BlockSpec((tm, TOK_WIDTH), row),
                  pl.BlockSpec((tm, MEM_WIDTH), row),
                  pl.BlockSpec((None, D_MODEL, D_MODEL), lambda i: (layer, 0, 0), pipeline_mode=once),
                  pl.BlockSpec((1, D_MODEL), lambda i: (0, 0)),
                  pl.BlockSpec((None, D_MODEL, D_FF), lambda i: (layer, 0, 0), pipeline_mode=once),
                  pl.BlockSpec((None, D_MODEL, D_FF), lambda i: (layer, 0, 1), pipeline_mode=once),
                  pl.BlockSpec((None, D_FF, D_MODEL), lambda i: (layer, 0, 0), pipeline_mode=once)],
        out_specs=pl.BlockSpec((tm, D_MODEL), row),
        compiler_params=pltpu.CompilerParams(dimension_semantics=("arbitrary",),
                                             vmem_limit_bytes=VMEM_LIMIT),
        name="out_ffn",
    )(h, tok, mo, w_out, g, w_gu, w_gu, w_down)


def _lane_cumsum(x):
    n = x.shape[-1]
    lane = lax.broadcasted_iota(jnp.int32, x.shape, 1)
    k = 1
    while k < n:
        x = x + jnp.where(lane >= k, pltpu.roll(x, k, axis=1), 0.0)
        k *= 2
    return x


def _cumsum_kernel(x_ref, o_ref):
    o_ref[...] = _lane_cumsum(x_ref[...])


def _cumsum_rows(x, *, seg):
    rows, t = x.shape
    return pl.pallas_call(
        _cumsum_kernel,
        out_shape=jax.ShapeDtypeStruct((rows, t), F32),
        grid=(t // seg,),
        in_specs=[pl.BlockSpec((rows, seg), lambda i: (0, i))],
        out_specs=pl.BlockSpec((rows, seg), lambda i: (0, i)),
        compiler_params=pltpu.CompilerParams(dimension_semantics=("arbitrary",)),
        name="logf_cumsum",
    )(x)


def _round_to_bf16(x):
    return x.astype(BF16).astype(F32)


def _fox_prompt_kernel(q_ref, k_ref, v_ref, frow_ref, o_ref, m_ref, l_ref, acc_ref, alpha_ref,
                       s_ref, p_all_ref, *, blk, n_q_blocks):
    qi = pl.program_id(2)
    q = q_ref[0]
    lane = lax.broadcasted_iota(jnp.int32, q.shape, 1)
    sub = FOX_CHUNK_ROWS
    n_rep = blk // LANES
    bias_lane = [FOX_HEAD_DIM, 0]
    q_f32 = q.astype(F32)
    q_head = []
    for j in range(2):
        own = (lane < FOX_HEAD_DIM) if j == 0 else (lane >= FOX_HEAD_DIM)
        ones_from = jnp.where(lane >= bias_lane[j], 1.0, 0.0)
        bias_ones = jnp.where(lane < bias_lane[j] + 3, ones_from, 0.0)
        q_head.append(jnp.where(own, q_f32, bias_ones).astype(BF16))
    m_ref[...] = jnp.full(m_ref.shape, -jnp.inf, F32)
    l_ref[...] = jnp.zeros_like(l_ref)
    acc_ref[...] = jnp.zeros_like(acc_ref)

    def scores(qc, ki, s_buf):
        keys = slice(ki * blk, (ki + 1) * blk)
        kt = k_ref[:, keys]
        tile_row = lax.broadcasted_iota(jnp.int32, (BF16_ROWS, blk), 0)
        for j in range(2):
            f_end = frow_ref[0, j:j + 1, (qc + 1) * blk - 1:(qc + 1) * blk]
            bias = f_end - frow_ref[0, j:j + 1, keys]
            hi = _round_to_bf16(bias)
            mid = _round_to_bf16(bias - hi)
            lo = _round_to_bf16(bias - hi - mid)
            bias_rows = jnp.where(tile_row == 0, hi, jnp.where(tile_row == 1, mid,
                                                               jnp.where(tile_row == 2, lo, 0.0)))
            b0 = bias_lane[j]
            pieces = [kt[:b0], bias_rows.astype(BF16), kt[b0 + BF16_ROWS:]]
            kt_j = jnp.concatenate([x for x in pieces if x.shape[0]], axis=0)
            s_buf[j] = jnp.dot(q_head[j], kt_j, preferred_element_type=F32)

    def update(ki, s_buf, masked):
        p_ref = p_all_ref.at[ki]
        vt = v_ref[0, ki * blk:(ki + 1) * blk, :]
        for j in range(2):
            def chunk(r):
                n_use = -(-(r + 1) * sub // LANES) if masked else n_rep
                rows = slice(r * sub, (r + 1) * sub)
                s = s_buf[j, rows, 0:n_use * LANES]
                if masked:
                    row = lax.broadcasted_iota(jnp.int32, s.shape, 0) + r * sub
                    col = lax.broadcasted_iota(jnp.int32, s.shape, 1)
                    s = jnp.where(col <= row, s, -jnp.inf)
                return rows, n_use, s

            for r in range(blk // sub):
                rows, n_use, s = chunk(r)
                m_prev = m_ref[j, rows]
                m_new = jnp.maximum(m_prev, jnp.max(s, axis=-1, keepdims=True))
                alpha_ref[j, rows] = jnp.exp(m_prev - m_new)
                m_ref[j, rows] = m_new
            for r in range(blk // sub):
                rows, n_use, s = chunk(r)
                p = jnp.exp(s - jnp.concatenate([m_ref[j, rows]] * n_use, axis=1))
                p_lanes = p[:, :LANES]
                for c in range(1, n_use):
                    p_lanes = p_lanes + p[:, c * LANES:(c + 1) * LANES]
                l_ref[j, rows] = alpha_ref[j, rows] * l_ref[j, rows] + p_lanes
                p_ref[j, rows, 0:n_use * LANES] = p.astype(BF16)
                if n_use < n_rep:
                    p_ref[j, rows, n_use * LANES:] = jnp.zeros((sub, blk - n_use * LANES), BF16)
            acc_ref[j] = alpha_ref[j] * acc_ref[j] + jnp.dot(p_ref[j], vt, preferred_element_type=F32)

    for qc in range(n_q_blocks):
        @pl.when(qi == qc)
        def _(qc=qc):
            scores(qc, 0, s_ref.at[0])
            for ki in range(qc):
                scores(qc, ki + 1, s_ref.at[ki + 1])
                update(ki, s_ref.at[ki], False)
            update(qc, s_ref.at[qc], True)

    l0 = jnp.sum(l_ref[0], axis=-1, keepdims=True)
    l1 = jnp.sum(l_ref[1], axis=-1, keepdims=True)
    o = jnp.where(lane < FOX_HEAD_DIM, acc_ref[0] / l0, acc_ref[1] / l1)
    o_ref[0] = o.astype(o_ref.dtype)


def _fox_prompt(q, k, v, frow, *, blk):
    b, s_len, _ = q.shape
    n_pairs = TOK_WIDTH // LANES
    nq = s_len // blk
    return pl.pallas_call(
        functools.partial(_fox_prompt_kernel, blk=blk, n_q_blocks=nq),
        out_shape=jax.ShapeDtypeStruct((b, s_len, TOK_WIDTH), BF16),
        grid=(b, n_pairs, nq),
        in_specs=[pl.BlockSpec((1, blk, LANES), lambda bi, hp, qi: (bi, qi, hp)),
                  pl.BlockSpec((LANES, s_len), lambda bi, hp, qi: (hp, bi)),
                  pl.BlockSpec((1, s_len, LANES), lambda bi, hp, qi: (bi, 0, hp)),
                  pl.BlockSpec((1, 2, s_len), lambda bi, hp, qi: (hp, 0, bi))],
        out_specs=pl.BlockSpec((1, blk, LANES), lambda bi, hp, qi: (bi, qi, hp)),
        scratch_shapes=[pltpu.VMEM((2, blk, LANES), F32), pltpu.VMEM((2, blk, LANES), F32),
                        pltpu.VMEM((2, blk, LANES), F32), pltpu.VMEM((2, blk, LANES), F32),
                        pltpu.VMEM((nq, 2, blk, blk), F32),
                        pltpu.VMEM((nq, 2, blk, blk), BF16)],
        compiler_params=pltpu.CompilerParams(
            dimension_semantics=("arbitrary", "arbitrary", "arbitrary"),
            vmem_limit_bytes=VMEM_LIMIT),
        name="fox_prompt",
    )(q, k, v, frow)


def _lane_suffix_sum(x):
    n = x.shape[-1]
    lane = lax.broadcasted_iota(jnp.int32, x.shape, 1)
    k = 1
    while k < n:
        x = x + jnp.where(lane + k < n, pltpu.roll(x, n - k, axis=1), 0.0)
        k *= 2
    return x


def _fox_paged_kernel(pt_ref, q_ref, kn_ref, vn_ref, lfn_ref, logf_ref, k_hbm, v_hbm, o_ref,
                      kbuf, vbuf, sem, pad_k_ref, pad_v_ref, *, n_pages, n_batch):
    b = pl.program_id(0)
    n_steps = pl.num_programs(0)
    slot = lax.rem(b, PAGE_SLOTS)
    n_q = q_ref.shape[1]
    n_rows = n_q * FOX_HEADS

    def page_copies(batch, slot_):
        copies = []
        for j in range(n_pages):
            page = pt_ref[j * n_batch + batch]
            copies.append(pltpu.make_async_copy(k_hbm.at[page], kbuf.at[slot_, j], sem.at[0, slot_]))
            copies.append(pltpu.make_async_copy(v_hbm.at[page], vbuf.at[slot_, j], sem.at[1, slot_]))
        return copies

    ahead = PAGE_SLOTS - 1
    for r in range(ahead):
        @pl.when(jnp.logical_and(b == 0, r < n_steps))
        def _(r=r):
            for c in page_copies(r, r):
                c.start()

    @pl.when(b + ahead < n_steps)
    def _():
        for c in page_copies(b + ahead, lax.rem(b + ahead, PAGE_SLOTS)):
            c.start()

    head = lax.broadcasted_iota(jnp.int32, (FOX_HEADS, TOK_WIDTH), 0)
    lane_head = jnp.right_shift(lax.broadcasted_iota(jnp.int32, (FOX_HEADS, TOK_WIDTH), 1), 6)
    own_head = head == lane_head
    tile_q = lambda x: jnp.concatenate([x] * n_q, axis=0)

    q = q_ref[0]
    qblk = jnp.concatenate(
        [jnp.where(own_head, jnp.broadcast_to(q[t:t + 1, :], (FOX_HEADS, TOK_WIDTH)), 0.0)
         for t in range(n_q)], axis=0).astype(BF16)

    pad_k_ref[...] = jnp.zeros_like(pad_k_ref)
    pad_v_ref[...] = jnp.zeros_like(pad_v_ref)
    pad_k_ref[0:n_q, :] = kn_ref[0]
    pad_v_ref[0:n_q, :] = vn_ref[0]
    c_new = _lane_cumsum(lfn_ref[0])
    cq = jnp.concatenate([c_new[:, t:t + 1] for t in range(n_q)], axis=0)
    s_own = lax.dot_general(qblk, pad_k_ref[...].astype(BF16), NT_DIMS, preferred_element_type=F32)
    t_of_row = jnp.right_shift(lax.broadcasted_iota(jnp.int32, (n_rows, PAGE_SIZE), 0), 3)
    key = lax.broadcasted_iota(jnp.int32, (n_rows, PAGE_SIZE), 1)
    s_own = jnp.where(key <= t_of_row, s_own + (tile_q(-c_new) + cq), -jnp.inf)

    logf = [logf_ref[pt_ref[j * n_batch + b]] for j in range(n_pages)]
    incl = _lane_suffix_sum(jnp.concatenate(logf, axis=0))
    later_pages = jnp.zeros((FOX_HEADS, 1), F32)
    bias = [None] * n_pages
    for j in reversed(range(n_pages)):
        rows = slice(j * FOX_HEADS, (j + 1) * FOX_HEADS)
        bias[j] = tile_q(later_pages + (incl[rows] - logf[j])) + cq
        later_pages = later_pages + incl[rows, 0:1]

    for c in page_copies(b, slot):
        c.wait()

    scores = [s_own]
    for j in range(n_pages):
        k_t = kbuf[slot, j].reshape(TOK_WIDTH, PAGE_SIZE).astype(BF16)
        scores.append(jnp.dot(qblk, k_t, preferred_element_type=F32) + bias[j])
    s_max = scores[0]
    for s in scores[1:]:
        s_max = jnp.maximum(s_max, s)
    m = jnp.max(s_max, axis=-1, keepdims=True)
    p = jnp.exp(scores[0] - m)
    p_sum = p
    acc = jnp.dot(p.astype(BF16), pad_v_ref[...].astype(BF16), preferred_element_type=F32)
    for j in range(n_pages):
        p = jnp.exp(scores[j + 1] - m)
        p_sum = p_sum + p
        v_t = vbuf[slot, j].reshape(TOK_WIDTH, PAGE_SIZE).astype(BF16)
        acc = acc + lax.dot_general(p.astype(BF16), v_t, NT_DIMS, preferred_element_type=F32)
    o = acc / jnp.sum(p_sum, axis=-1, keepdims=True)
    for t in range(n_q):
        o_t = jnp.where(own_head, o[t * FOX_HEADS:(t + 1) * FOX_HEADS, :], 0.0)
        o_ref[0, t:t + 1, :] = jnp.sum(o_t, axis=0, keepdims=True)


def _fox_paged(page_table, q, k_new, v_new, lfn, cache_logf_t, cache_k_t, cache_v_t):
    b, n_q, _ = q.shape
    n_pages = page_table.shape[0] // b
    cur = lambda bi, pt: (bi, 0, 0)
    page_shape = cache_k_t.shape[1:]
    grid_spec = pltpu.PrefetchScalarGridSpec(
        num_scalar_prefetch=1,
        grid=(b,),
        in_specs=[pl.BlockSpec((1, n_q, TOK_WIDTH), cur),
                  pl.BlockSpec((1, n_q, TOK_WIDTH), cur),
                  pl.BlockSpec((1, n_q, TOK_WIDTH), cur),
                  pl.BlockSpec((1, FOX_HEADS, LANES), cur),
                  pl.BlockSpec(cache_logf_t.shape, lambda bi, pt: (0, 0, 0), pipeline_mode=pl.Buffered(1)),
                  pl.BlockSpec(memory_space=pl.ANY),
                  pl.BlockSpec(memory_space=pl.ANY)],
        out_specs=pl.BlockSpec((1, n_q, TOK_WIDTH), cur),
        scratch_shapes=[pltpu.VMEM((PAGE_SLOTS, n_pages) + page_shape, F32),
                        pltpu.VMEM((PAGE_SLOTS, n_pages) + page_shape, F32),
                        pltpu.SemaphoreType.DMA((2, PAGE_SLOTS)),
                        pltpu.VMEM((PAGE_SIZE, TOK_WIDTH), F32),
                        pltpu.VMEM((PAGE_SIZE, TOK_WIDTH), F32)])
    return pl.pallas_call(
        functools.partial(_fox_paged_kernel, n_pages=n_pages, n_batch=b),
        out_shape=jax.ShapeDtypeStruct((b, n_q, TOK_WIDTH), F32),
        grid_spec=grid_spec,
        compiler_params=pltpu.CompilerParams(dimension_semantics=("arbitrary",),
                                             vmem_limit_bytes=VMEM_LIMIT),
        name="fox_paged",
    )(page_table, q, k_new, v_new, lfn, cache_logf_t, cache_k_t, cache_v_t)


def _decoder(x, pool_prev, n_prev, mem_k, mem_v, past, wts, *, tm, tm_proj, pool_bb, pool_chunk,
             mem_bb, mem_ts, act):
    b, s_len, _ = x.shape
    t = b * s_len
    h = x.reshape(t, D_MODEL)

    fuse_mem = s_len % tm_proj == 0

    def project_and_attend(layer, fox):
        mem = (mem_k, mem_v, layer, s_len) if fuse_mem else None
        tok_part, mem_part = _proj_in(h, wts["g_mix"][layer], wts["w_in"][layer], wts["g_fox_q_t"],
                                      wts["g_mem_q_t"][layer], mem, fox=fox, tm=tm_proj, act=act)
        if not fuse_mem:
            mem_part = _mem_attn(mem_part.reshape(b, s_len, MEM_WIDTH), mem_k, mem_v, layer,
                                 bb=mem_bb, ts=mem_ts)
        return tok_part, mem_part.reshape(t, MEM_WIDTH)

    z_tok, mo = project_and_attend(0, fox=False)
    z_tok = z_tok.reshape(b, s_len, TOK_WIDTH)
    tok = _pool_mix(z_tok, pool_prev, wts["w_pool"], wts["pool_scale"],
                    n_prev=n_prev, bb=pool_bb, chunk=pool_chunk, act=act)
    h = _out_ffn(h, tok.reshape(t, TOK_WIDTH), mo, wts["w_out"],
                 wts["g_ffn"][0], wts["w_gu"], wts["w_down"], 0, tm=tm)

    kv_args = (h, wts["g_kv"], wts["w_kv"], wts["g_fox_k_t"], wts["w_fg_t"], wts["b_fg"])
    if past is None:
        k_new, v_new, k_bf, v_bf, logf_t, logf_new = _proj_kv_paged(*kv_args, tm=tm_proj)
    else:
        k_new, v_new, logf_t = _proj_kv(*kv_args, tm=tm_proj)
        logf_new = logf_t.T

    q, mo = project_and_attend(1, fox=True)
    if past is None:
        f_row = _cumsum_rows(logf_t, seg=s_len)
        n_pairs = FOX_HEADS // 2
        f_row = f_row.reshape(n_pairs, 2, t)
        tok = _fox_prompt(q.reshape(b, s_len, TOK_WIDTH), k_bf,
                          v_bf.reshape(b, s_len, TOK_WIDTH), f_row, blk=512)
    else:
        page_table, cache_k_t, cache_v_t, cache_logf_t = past
        lfn = logf_t.reshape(FOX_HEADS, b, s_len).transpose(1, 0, 2)
        lfn = jnp.pad(lfn, ((0, 0), (0, 0), (0, LANES - s_len)))
        per_row = lambda a: a.reshape(b, s_len, TOK_WIDTH)
        tok = _fox_paged(page_table, per_row(q), per_row(k_new), per_row(v_new), lfn,
                         cache_logf_t, cache_k_t, cache_v_t)
    h = _out_ffn(h, tok.reshape(t, TOK_WIDTH), mo, wts["w_out"],
                 wts["g_ffn"][1], wts["w_gu"], wts["w_down"], 1, tm=tm)
    return h.reshape(b, s_len, D_MODEL), z_tok, k_new, v_new, logf_new


def kernel(x_prompt, x_sample, cache_mem_k, cache_mem_v, state_pool, cache_k, cache_v, cache_logf,
           page_table, mem_prompt, g_mix, w_in, w_out, g_ffn, w_gu, w_down, g_mem, w_mem_kv,
           g_mem_q, g_mem_k, w_pool, pool_scale, g_kv, w_kv, g_fox_k, w_fg, b_fg, g_fox_q):
    depth = w_in.shape[0]
    b, s_len, _ = x_prompt.shape
    db, ds, _ = x_sample.shape
    wts = {
        "g_mix": g_mix.reshape(depth, 1, D_MODEL),
        "w_in": w_in.astype(BF16),
        "w_out": w_out.astype(BF16),
        "g_ffn": g_ffn.reshape(depth, 1, D_MODEL),
        "w_gu": w_gu.astype(BF16),
        "w_down": w_down.astype(BF16),
        "g_mem_q_t": jnp.tile(g_mem_q, (1, MEM_HEADS)).reshape(depth, 1, MEM_WIDTH),
        "w_pool": w_pool[0].astype(BF16),
        "pool_scale": pool_scale[0].reshape(1, TOK_WIDTH),
        "g_kv": g_kv.reshape(1, D_MODEL),
        "w_kv": w_kv.astype(BF16),
        "g_fox_k_t": jnp.tile(g_fox_k, FOX_HEADS).reshape(1, TOK_WIDTH),
        "w_fg_t": w_fg.T.astype(BF16),
        "b_fg": b_fg.reshape(FOX_HEADS, 1),
        "g_fox_q_t": jnp.tile(g_fox_q[0], FOX_HEADS).reshape(1, TOK_WIDTH),
    }

    mem_k_p, mem_v_p = _proj_memkv(mem_prompt.reshape(b * N_MEM, D_MODEL),
                                   g_mem.reshape(depth, 1, D_MODEL), w_mem_kv.astype(BF16),
                                   jnp.tile(g_mem_k, (1, MEM_HEADS)).reshape(depth, 1, MEM_WIDTH), tm=512)
    mem_rows = (N_MEM * MEM_HEADS, MEM_HEAD_DIM)
    mem_k_p = mem_k_p.reshape((depth, b) + mem_rows)
    mem_v_p = mem_v_p.reshape((depth, b) + mem_rows)
    y_p, ztok_p, k_p, v_p, logf_p = _decoder(
        x_prompt, jnp.zeros((b, POOL_HIST, TOK_WIDTH), F32), 0, mem_k_p, mem_v_p, None, wts,
        tm=512, tm_proj=1024, pool_bb=1, pool_chunk=256, mem_bb=1, mem_ts=512, act=BF16)
    n_pp = s_len // PAGE_SIZE
    head_shape = (FOX_HEADS, FOX_HEAD_DIM)

    past = (page_table.T.reshape(-1),
            cache_k.transpose(0, 2, 3, 1), cache_v.transpose(0, 2, 3, 1),
            cache_logf.transpose(0, 2, 1))
    prev = jnp.pad(state_pool[0], ((0, 0), (POOL_HIST - POOL_STATE, 0), (0, 0)))
    y_s, ztok_s, k_s, v_s, logf_s = _decoder(
        x_sample, prev, POOL_STATE, cache_mem_k.reshape((depth, db) + mem_rows),
        cache_mem_v.reshape((depth, db) + mem_rows), past, wts,
        tm=512, tm_proj=512, pool_bb=32, pool_chunk=ds, mem_bb=8, mem_ts=ds, act=F32)

    pool_state_p = ztok_p[:, s_len - POOL_STATE:][None]
    pool_state_s = jnp.concatenate([state_pool[0], ztok_s], axis=1)[:, -POOL_STATE:][None]
    paged = lambda a: a.reshape((b, n_pp) + head_shape + (PAGE_SIZE,)).transpose(0, 1, 4, 2, 3)
    return (y_p, y_s, paged(k_p), paged(v_p),
            logf_p.reshape(b, n_pp, FOX_HEADS, PAGE_SIZE).transpose(0, 1, 3, 2),
            mem_k_p.reshape(depth, b, N_MEM, MEM_HEADS, MEM_HEAD_DIM),
            mem_v_p.reshape(depth, b, N_MEM, MEM_HEADS, MEM_HEAD_DIM),
            pool_state_p,
            k_s.reshape((db, ds) + head_shape),
            v_s.reshape((db, ds) + head_shape),
            logf_s.reshape(db, ds, FOX_HEADS),
            pool_state_s)
```

```python
import functools

import jax
import jax.numpy as jnp
from jax import lax
from jax.experimental import pallas as pl
from jax.experimental.pallas import tpu as pltpu

D_MODEL = 1024
TOK_WIDTH = 512
MEM_WIDTH = 512
POOL_WINDOWS = (2, 4, 8, 16)
POOL_GROUP = 128
POOL_STATE = 15
POOL_HIST = 16
FOX_HEADS = 8
FOX_HEAD_DIM = 64
MEM_HEADS = 4
MEM_HEAD_DIM = 128
N_MEM = 256
D_FF = 2816
PAGE_SIZE = 128
EPS = 1e-6
LANES = 128
BF16_ROWS = 16
VMEM_LIMIT = 56 * 1024 * 1024
FOX_CHUNK_ROWS = 16
PAGE_SLOTS = 4

F32 = jnp.float32
BF16 = jnp.bfloat16
NT_DIMS = (((1,), (1,)), ((), ()))


def _rmsnorm(x, g):
    return x * lax.rsqrt(jnp.mean(x * x, axis=-1, keepdims=True) + EPS) * g


def _headnorm128(z, g):
    outs = []
    for h in range(z.shape[-1] // LANES):
        zh = z[:, h * LANES:(h + 1) * LANES]
        r = lax.rsqrt(jnp.mean(zh * zh, axis=-1, keepdims=True) + EPS)
        outs.append(zh * r * g[:, h * LANES:(h + 1) * LANES])
    return jnp.concatenate(outs, axis=-1)


def _headnorm64(z, g):
    outs = []
    for p in range(z.shape[-1] // LANES):
        zp = z[:, p * LANES:(p + 1) * LANES]
        sq = zp * zp
        lane = lax.broadcasted_iota(jnp.int32, zp.shape, 1)
        lo = lane < FOX_HEAD_DIM
        s_lo = jnp.sum(jnp.where(lo, sq, 0.0), axis=-1, keepdims=True)
        s_hi = jnp.sum(jnp.where(lo, 0.0, sq), axis=-1, keepdims=True)
        r = jnp.where(lo, lax.rsqrt(s_lo / FOX_HEAD_DIM + EPS), lax.rsqrt(s_hi / FOX_HEAD_DIM + EPS))
        outs.append(zp * r * g[:, p * LANES:(p + 1) * LANES])
    return jnp.concatenate(outs, axis=-1)


def _mem_attend(q, k_ref, v_ref):
    outs = []
    for h in range(MEM_HEADS):
        qh = q[:, h * MEM_HEAD_DIM:(h + 1) * MEM_HEAD_DIM].astype(BF16)
        k = k_ref[pl.ds(h, N_MEM, stride=MEM_HEADS), :].astype(BF16)
        v = v_ref[pl.ds(h, N_MEM, stride=MEM_HEADS), :].astype(BF16)
        s = lax.dot_general(qh, k, NT_DIMS, preferred_element_type=F32) * (MEM_HEAD_DIM ** -0.5)
        p = jnp.exp(s - jnp.max(s, axis=-1, keepdims=True))
        o = jnp.dot(p.astype(BF16), v, preferred_element_type=F32)
        outs.append(o / jnp.sum(p, axis=-1, keepdims=True))
    return jnp.concatenate(outs, axis=-1)


def _proj_in_kernel(x_ref, g_ref, w_ref, gtok_ref, gmem_ref, *refs, fox, fuse_mem):
    xn = _rmsnorm(x_ref[...], g_ref[...]).astype(BF16)
    z = jnp.dot(xn, w_ref[...], preferred_element_type=F32)
    z_tok = z[:, :TOK_WIDTH]
    qm = _headnorm128(z[:, TOK_WIDTH:], gmem_ref[...])
    if fuse_mem:
        mk_ref, mv_ref, tok_ref, mem_ref = refs
        mem_ref[...] = _mem_attend(qm, mk_ref, mv_ref).astype(mem_ref.dtype)
    else:
        tok_ref, mem_ref = refs
        mem_ref[...] = qm.astype(mem_ref.dtype)
    if fox:
        tok_ref[...] = (_headnorm64(z_tok, gtok_ref[...]) * (FOX_HEAD_DIM ** -0.5)).astype(tok_ref.dtype)
    else:
        tok_ref[...] = z_tok


def _proj_in(x, g, w, g_tok, g_mem, mem=None, *, fox, tm, act):
    t = x.shape[0]
    row = lambda i: (i, 0)
    fixed = lambda i: (0, 0)
    operands = [x, g, w, g_tok, g_mem]
    in_specs = [pl.BlockSpec((tm, D_MODEL), row),
                pl.BlockSpec((1, D_MODEL), fixed),
                pl.BlockSpec((D_MODEL, D_MODEL), fixed),
                pl.BlockSpec((1, TOK_WIDTH), fixed),
                pl.BlockSpec((1, MEM_WIDTH), fixed)]
    if mem is not None:
        mk, mv, layer, rows_per_batch = mem
        tiles = rows_per_batch // tm
        mem_block = pl.BlockSpec((None, None, N_MEM * MEM_HEADS, MEM_HEAD_DIM),
                                 lambda i: (layer, i // tiles, 0, 0))
        operands += [mk, mv]
        in_specs += [mem_block, mem_block]
    return pl.pallas_call(
        functools.partial(_proj_in_kernel, fox=fox, fuse_mem=mem is not None),
        out_shape=(jax.ShapeDtypeStruct((t, TOK_WIDTH), act if fox else F32),
                   jax.ShapeDtypeStruct((t, MEM_WIDTH), act)),
        grid=(t // tm,),
        in_specs=in_specs,
        out_specs=(pl.BlockSpec((tm, TOK_WIDTH), row), pl.BlockSpec((tm, MEM_WIDTH), row)),
        compiler_params=pltpu.CompilerParams(dimension_semantics=("arbitrary",),
                                             vmem_limit_bytes=VMEM_LIMIT),
        name="proj_in_fox" if fox else "proj_in_pool",
    )(*operands)


def _log_sigmoid(x):
    return jnp.minimum(x, 0.0) - jnp.log(1.0 + jnp.exp(-jnp.abs(x)))


def _proj_kv_kernel(x_ref, g_ref, w_ref, gk_ref, wfg_ref, bfg_ref, k_ref, v_ref, logf_ref):
    xn = _rmsnorm(x_ref[...], g_ref[...]).astype(BF16)
    z = jnp.dot(xn, w_ref[...], preferred_element_type=F32)
    k_ref[...] = _headnorm64(z[:, :TOK_WIDTH], gk_ref[...])
    v_ref[...] = z[:, TOK_WIDTH:]
    gate = lax.dot_general(wfg_ref[...], xn, NT_DIMS, preferred_element_type=F32) + bfg_ref[...]
    logf_ref[...] = _log_sigmoid(gate)


def _proj_kv_paged_kernel(x_ref, g_ref, w_ref, gk_ref, wfg_ref, bfg_ref,
                          kp_ref, vp_ref, kb_ref, vb_ref, logf_ref, logfp_ref):
    xn = _rmsnorm(x_ref[...], g_ref[...]).astype(BF16)
    z = jnp.dot(xn, w_ref[...], preferred_element_type=F32)
    v = z[:, TOK_WIDTH:]
    vb_ref[...] = v.astype(BF16)
    gate = lax.dot_general(wfg_ref[...], xn, NT_DIMS, preferred_element_type=F32) + bfg_ref[...]
    logf = _log_sigmoid(gate)
    logf_ref[...] = logf
    zk_t = z[:, :TOK_WIDTH].T
    heads = []
    for h in range(FOX_HEADS):
        zh = zk_t[h * FOX_HEAD_DIM:(h + 1) * FOX_HEAD_DIM, :]
        heads.append(zh * lax.rsqrt(jnp.mean(zh * zh, axis=0, keepdims=True) + EPS))
    k_t = jnp.concatenate(heads, axis=0) * gk_ref[...]
    v_t = v.T
    kb_ref[...] = k_t.astype(BF16)
    for pg in range(kp_ref.shape[0]):
        rows = slice(pg * PAGE_SIZE, (pg + 1) * PAGE_SIZE)
        kp_ref[pg] = k_t[:, rows]
        vp_ref[pg] = v_t[:, rows]
        logfp_ref[pg] = logf[:, rows]


def _proj_kv_paged(x, g, w, g_k, wfg_t, bfg, *, tm):
    t = x.shape[0]
    row = lambda i: (i, 0)
    fixed = lambda i: (0, 0)
    ppt = tm // PAGE_SIZE
    page_rows = lambda i: (i, 0, 0)
    return pl.pallas_call(
        _proj_kv_paged_kernel,
        out_shape=(jax.ShapeDtypeStruct((t // PAGE_SIZE, TOK_WIDTH, PAGE_SIZE), F32),
                   jax.ShapeDtypeStruct((t // PAGE_SIZE, TOK_WIDTH, PAGE_SIZE), F32),
                   jax.ShapeDtypeStruct((TOK_WIDTH, t), BF16),
                   jax.ShapeDtypeStruct((t, TOK_WIDTH), BF16),
                   jax.ShapeDtypeStruct((FOX_HEADS, t), F32),
                   jax.ShapeDtypeStruct((t // PAGE_SIZE, FOX_HEADS, PAGE_SIZE), F32)),
        grid=(t // tm,),
        in_specs=[pl.BlockSpec((tm, D_MODEL), row),
                  pl.BlockSpec((1, D_MODEL), fixed),
                  pl.BlockSpec((D_MODEL, D_MODEL), fixed),
                  pl.BlockSpec((TOK_WIDTH, 1), fixed),
                  pl.BlockSpec((FOX_HEADS, D_MODEL), fixed),
                  pl.BlockSpec((FOX_HEADS, 1), fixed)],
        out_specs=(pl.BlockSpec((ppt, TOK_WIDTH, PAGE_SIZE), page_rows),
                   pl.BlockSpec((ppt, TOK_WIDTH, PAGE_SIZE), page_rows),
                   pl.BlockSpec((TOK_WIDTH, tm), lambda i: (0, i)), pl.BlockSpec((tm, TOK_WIDTH), row),
                   pl.BlockSpec((FOX_HEADS, tm), lambda i: (0, i)),
                   pl.BlockSpec((ppt, FOX_HEADS, PAGE_SIZE), page_rows)),
        compiler_params=pltpu.CompilerParams(dimension_semantics=("arbitrary",),
                                             vmem_limit_bytes=VMEM_LIMIT),
        name="proj_kv_paged",
    )(x, g, w, g_k, wfg_t, bfg)


def _proj_kv(x, g, w, g_k, wfg_t, bfg, *, tm):
    t = x.shape[0]
    row = lambda i: (i, 0)
    fixed = lambda i: (0, 0)
    return pl.pallas_call(
        _proj_kv_kernel,
        out_shape=(jax.ShapeDtypeStruct((t, TOK_WIDTH), F32),
                   jax.ShapeDtypeStruct((t, TOK_WIDTH), F32),
                   jax.ShapeDtypeStruct((FOX_HEADS, t), F32)),
        grid=(t // tm,),
        in_specs=[pl.BlockSpec((tm, D_MODEL), row),
                  pl.BlockSpec((1, D_MODEL), fixed),
                  pl.BlockSpec((D_MODEL, D_MODEL), fixed),
                  pl.BlockSpec((1, TOK_WIDTH), fixed),
                  pl.BlockSpec((FOX_HEADS, D_MODEL), fixed),
                  pl.BlockSpec((FOX_HEADS, 1), fixed)],
        out_specs=(pl.BlockSpec((tm, TOK_WIDTH), row), pl.BlockSpec((tm, TOK_WIDTH), row),
                   pl.BlockSpec((FOX_HEADS, tm), lambda i: (0, i))),
        compiler_params=pltpu.CompilerParams(dimension_semantics=("arbitrary",),
                                             vmem_limit_bytes=VMEM_LIMIT),
        name="proj_kv",
    )(x, g, w, g_k, wfg_t, bfg)


def _proj_memkv_kernel(x_ref, g_ref, w_ref, gk_ref, k_ref, v_ref):
    xn = _rmsnorm(x_ref[...], g_ref[0]).astype(BF16)
    z = jnp.dot(xn, w_ref[0], preferred_element_type=F32)
    k = _headnorm128(z[:, :MEM_WIDTH], gk_ref[0])
    tm = x_ref.shape[0]
    for h in range(MEM_HEADS):
        rows = pl.ds(h, tm, stride=MEM_HEADS)
        k_ref[0, rows, :] = k[:, h * MEM_HEAD_DIM:(h + 1) * MEM_HEAD_DIM]
        v_ref[0, rows, :] = z[:, MEM_WIDTH + h * MEM_HEAD_DIM:MEM_WIDTH + (h + 1) * MEM_HEAD_DIM]


def _proj_memkv(x, g, w, g_k, *, tm):
    t = x.shape[0]
    n_layers = w.shape[0]
    out_block = pl.BlockSpec((1, tm * MEM_HEADS, MEM_HEAD_DIM), lambda l, i: (l, i, 0))
    return pl.pallas_call(
        _proj_memkv_kernel,
        out_shape=(jax.ShapeDtypeStruct((n_layers, t * MEM_HEADS, MEM_HEAD_DIM), F32),
                   jax.ShapeDtypeStruct((n_layers, t * MEM_HEADS, MEM_HEAD_DIM), F32)),
        grid=(n_layers, t // tm),
        in_specs=[pl.BlockSpec((tm, D_MODEL), lambda l, i: (i, 0)),
                  pl.BlockSpec((1, 1, D_MODEL), lambda l, i: (l, 0, 0)),
                  pl.BlockSpec((1, D_MODEL, D_MODEL), lambda l, i: (l, 0, 0)),
                  pl.BlockSpec((1, 1, MEM_WIDTH), lambda l, i: (l, 0, 0))],
        out_specs=(out_block, out_block),
        compiler_params=pltpu.CompilerParams(dimension_semantics=("arbitrary", "arbitrary"),
                                             vmem_limit_bytes=VMEM_LIMIT),
        name="proj_memkv",
    )(x, g, w, g_k)


def _pool_kernel(u_ref, prev_ref, w_ref, scale_ref, o_ref, ext_ref, *, n_prev, chunk):
    bb, s_len, _ = u_ref.shape
    ext_ref[:, 0:POOL_HIST, :] = prev_ref[...]
    ext_ref[:, POOL_HIST:, :] = u_ref[...]
    for c in range(s_len // chunk):
        r0 = POOL_HIST + c * chunk
        pos = c * chunk + lax.broadcasted_iota(jnp.int32, (1, chunk, 1), 1)
        for gi, win in enumerate(POOL_WINDOWS):
            lanes = slice(gi * POOL_GROUP, (gi + 1) * POOL_GROUP)
            u_new = ext_ref[:, r0:r0 + chunk, lanes]
            acc = u_new
            for k in range(1, win):
                acc = acc + ext_ref[:, r0 - k:r0 - k + chunk, lanes]
            count = jnp.minimum(win, n_prev + pos + 1).astype(F32)
            y = acc / count - u_new
            z = jnp.dot(y.reshape(bb * chunk, POOL_GROUP).astype(BF16), w_ref[gi],
                        preferred_element_type=F32)
            z = z.reshape(bb, chunk, POOL_GROUP) * scale_ref[:, lanes]
            o_ref[:, c * chunk:(c + 1) * chunk, lanes] = z.astype(o_ref.dtype)


def _pool_mix(u, prev, w_pool, scale, *, n_prev, bb, chunk, act):
    b, s_len, _ = u.shape
    return pl.pallas_call(
        functools.partial(_pool_kernel, n_prev=n_prev, chunk=chunk),
        out_shape=jax.ShapeDtypeStruct((b, s_len, TOK_WIDTH), act),
        grid=(b // bb,),
        in_specs=[pl.BlockSpec((bb, s_len, TOK_WIDTH), lambda i: (i, 0, 0)),
                  pl.BlockSpec((bb, POOL_HIST, TOK_WIDTH), lambda i: (i, 0, 0)),
                  pl.BlockSpec((len(POOL_WINDOWS), POOL_GROUP, POOL_GROUP), lambda i: (0, 0, 0)),
                  pl.BlockSpec((1, TOK_WIDTH), lambda i: (0, 0))],
        out_specs=pl.BlockSpec((bb, s_len, TOK_WIDTH), lambda i: (i, 0, 0)),
        scratch_shapes=[pltpu.VMEM((bb, POOL_HIST + s_len, TOK_WIDTH), F32)],
        compiler_params=pltpu.CompilerParams(dimension_semantics=("arbitrary",),
                                             vmem_limit_bytes=VMEM_LIMIT),
        name="pool_mix",
    )(u, prev, w_pool, scale)


def _mem_attn_kernel(q_ref, k_ref, v_ref, o_ref):
    for h in range(MEM_HEADS):
        lanes = slice(h * MEM_HEAD_DIM, (h + 1) * MEM_HEAD_DIM)
        q = q_ref[:, :, lanes].astype(BF16)
        k = k_ref[:, pl.ds(h, N_MEM, stride=MEM_HEADS), :].astype(BF16)
        v = v_ref[:, pl.ds(h, N_MEM, stride=MEM_HEADS), :].astype(BF16)
        s = jnp.einsum("bqd,bkd->bqk", q, k, preferred_element_type=F32) * (MEM_HEAD_DIM ** -0.5)
        m = jnp.max(s, axis=-1, keepdims=True)
        p = jnp.exp(s - m)
        l = jnp.sum(p, axis=-1, keepdims=True)
        o = jnp.einsum("bqk,bkd->bqd", p.astype(BF16), v, preferred_element_type=F32)
        o_ref[:, :, lanes] = (o / l).astype(o_ref.dtype)


def _mem_attn(q, mk, mv, layer, *, bb, ts):
    b, s_len, _ = q.shape
    mem_block = pl.BlockSpec((None, bb, N_MEM * MEM_HEADS, MEM_HEAD_DIM),
                             lambda i, j: (layer, i, 0, 0))
    return pl.pallas_call(
        _mem_attn_kernel,
        out_shape=jax.ShapeDtypeStruct((b, s_len, MEM_WIDTH), q.dtype),
        grid=(b // bb, s_len // ts),
        in_specs=[pl.BlockSpec((bb, ts, MEM_WIDTH), lambda i, j: (i, j, 0)), mem_block, mem_block],
        out_specs=pl.BlockSpec((bb, ts, MEM_WIDTH), lambda i, j: (i, j, 0)),
        compiler_params=pltpu.CompilerParams(dimension_semantics=("arbitrary", "arbitrary"),
                                             vmem_limit_bytes=VMEM_LIMIT),
        name="mem_attn",
    )(q, mk, mv)


def _out_ffn_kernel(h_ref, tok_ref, mo_ref, wout_ref, g_ref, wg_ref, wu_ref, wd_ref, o_ref):
    h1 = (h_ref[...]
          + jnp.dot(tok_ref[...].astype(BF16), wout_ref[:TOK_WIDTH, :], preferred_element_type=F32)
          + jnp.dot(mo_ref[...].astype(BF16), wout_ref[TOK_WIDTH:, :], preferred_element_type=F32))
    x = _rmsnorm(h1, g_ref[...]).astype(BF16)
    gate = jnp.dot(x, wg_ref[...], preferred_element_type=F32)
    up = jnp.dot(x, wu_ref[...], preferred_element_type=F32)
    act = (gate * (1.0 / (1.0 + jnp.exp(-gate))) * up).astype(BF16)
    o_ref[...] = h1 + jnp.dot(act, wd_ref[...], preferred_element_type=F32)


def _out_ffn(h, tok, mo, w_out, g, w_gu, w_down, layer, *, tm):
    t = h.shape[0]
    row = lambda i: (i, 0)
    once = pl.Buffered(1)
    return pl.pallas_call(
        _out_ffn_kernel,
        out_shape=jax.ShapeDtypeStruct((t, D_MODEL), F32),
        grid=(t // tm,),
        in_specs=[pl.BlockSpec((tm, D_MODEL), row),
                  pl.BlockSpec((tm, TOK_WIDTH), row),
                  pl.BlockSpec((tm, MEM_WIDTH), row),
                  pl.BlockSpec((None, D_MODEL, D_MODEL), lambda i: (layer, 0, 0), pipeline_mode=once),
                  pl.BlockSpec((1, D_MODEL), lambda i: (0, 0)),
                  pl.BlockSpec((None, D_MODEL, D_FF), lambda i: (layer, 0, 0), pipeline_mode=once),
                  pl.BlockSpec((None, D_MODEL, D_FF), lambda i: (layer, 0, 1), pipeline_mode=once),
                  pl.BlockSpec((None, D_FF, D_MODEL), lambda i: (layer, 0, 0), pipeline_mode=once)],
        out_specs=pl.BlockSpec((tm, D_MODEL), row),
        compiler_params=pltpu.CompilerParams(dimension_semantics=("arbitrary",),
                                             vmem_limit_bytes=VMEM_LIMIT),
        name="out_ffn",
    )(h, tok, mo, w_out, g, w_gu, w_gu, w_down)


def _lane_cumsum(x):
    n = x.shape[-1]
    lane = lax.broadcasted_iota(jnp.int32, x.shape, 1)
    k = 1
    while k < n:
        x = x + jnp.where(lane >= k, pltpu.roll(x, k, axis=1), 0.0)
        k *= 2
    return x


def _cumsum_kernel(x_ref, o_ref):
    o_ref[...] = _lane_cumsum(x_ref[...])


def _cumsum_rows(x, *, seg):
    rows, t = x.shape
    return pl.pallas_call(
        _cumsum_kernel,
        out_shape=jax.ShapeDtypeStruct((rows, t), F32),
        grid=(t // seg,),
        in_specs=[pl.BlockSpec((rows, seg), lambda i: (0, i))],
        out_specs=pl.BlockSpec((rows, seg), lambda i: (0, i)),
        compiler_params=pltpu.CompilerParams(dimension_semantics=("arbitrary",)),
        name="logf_cumsum",
    )(x)


def _round_to_bf16(x):
    return x.astype(BF16).astype(F32)


def _fox_prompt_kernel(q_ref, k_ref, v_ref, frow_ref, o_ref, m_ref, l_ref, acc_ref, alpha_ref,
                       s_ref, p_all_ref, *, blk, n_q_blocks):
    qi = pl.program_id(2)
    q = q_ref[0]
    lane = lax.broadcasted_iota(jnp.int32, q.shape, 1)
    sub = FOX_CHUNK_ROWS
    n_rep = blk // LANES
    bias_lane = [FOX_HEAD_DIM, 0]
    q_f32 = q.astype(F32)
    q_head = []
    for j in range(2):
        own = (lane < FOX_HEAD_DIM) if j == 0 else (lane >= FOX_HEAD_DIM)
        ones_from = jnp.where(lane >= bias_lane[j], 1.0, 0.0)
        bias_ones = jnp.where(lane < bias_lane[j] + 3, ones_from, 0.0)
        q_head.append(jnp.where(own, q_f32, bias_ones).astype(BF16))
    m_ref[...] = jnp.full(m_ref.shape, -jnp.inf, F32)
    l_ref[...] = jnp.zeros_like(l_ref)
    acc_ref[...] = jnp.zeros_like(acc_ref)

    def scores(qc, ki, s_buf):
        keys = slice(ki * blk, (ki + 1) * blk)
        kt = k_ref[:, keys]
        tile_row = lax.broadcasted_iota(jnp.int32, (BF16_ROWS, blk), 0)
        for j in range(2):
            f_end = frow_ref[0, j:j + 1, (qc + 1) * blk - 1:(qc + 1) * blk]
            bias = f_end - frow_ref[0, j:j + 1, keys]
            hi = _round_to_bf16(bias)
            mid = _round_to_bf16(bias - hi)
            lo = _round_to_bf16(bias - hi - mid)
            bias_rows = jnp.where(tile_row == 0, hi, jnp.where(tile_row == 1, mid,
                                                               jnp.where(tile_row == 2, lo, 0.0)))
            b0 = bias_lane[j]
            pieces = [kt[:b0], bias_rows.astype(BF16), kt[b0 + BF16_ROWS:]]
            kt_j = jnp.concatenate([x for x in pieces if x.shape[0]], axis=0)
            s_buf[j] = jnp.dot(q_head[j], kt_j, preferred_element_type=F32)

    def update(ki, s_buf, masked):
        p_ref = p_all_ref.at[ki]
        vt = v_ref[0, ki * blk:(ki + 1) * blk, :]
        for j in range(2):
            def chunk(r):
                n_use = -(-(r + 1) * sub // LANES) if masked else n_rep
                rows = slice(r * sub, (r + 1) * sub)
                s = s_buf[j, rows, 0:n_use * LANES]
                if masked:
                    row = lax.broadcasted_iota(jnp.int32, s.shape, 0) + r * sub
                    col = lax.broadcasted_iota(jnp.int32, s.shape, 1)
                    s = jnp.where(col <= row, s, -jnp.inf)
                return rows, n_use, s

            for r in range(blk // sub):
                rows, n_use, s = chunk(r)
                m_prev = m_ref[j, rows]
                m_new = jnp.maximum(m_prev, jnp.max(s, axis=-1, keepdims=True))
                alpha_ref[j, rows] = jnp.exp(m_prev - m_new)
                m_ref[j, rows] = m_new
            for r in range(blk // sub):
                rows, n_use, s = chunk(r)
                p = jnp.exp(s - jnp.concatenate([m_ref[j, rows]] * n_use, axis=1))
                p_lanes = p[:, :LANES]
                for c in range(1, n_use):
                    p_lanes = p_lanes + p[:, c * LANES:(c + 1) * LANES]
                l_ref[j, rows] = alpha_ref[j, rows] * l_ref[j, rows] + p_lanes
                p_ref[j, rows, 0:n_use * LANES] = p.astype(BF16)
                if n_use < n_rep:
                    p_ref[j, rows, n_use * LANES:] = jnp.zeros((sub, blk - n_use * LANES), BF16)
            acc_ref[j] = alpha_ref[j] * acc_ref[j] + jnp.dot(p_ref[j], vt, preferred_element_type=F32)

    for qc in range(n_q_blocks):
        @pl.when(qi == qc)
        def _(qc=qc):
            scores(qc, 0, s_ref.at[0])
            for ki in range(qc):
                scores(qc, ki + 1, s_ref.at[ki + 1])
                update(ki, s_ref.at[ki], False)
            update(qc, s_ref.at[qc], True)

    l0 = jnp.sum(l_ref[0], axis=-1, keepdims=True)
    l1 = jnp.sum(l_ref[1], axis=-1, keepdims=True)
    o = jnp.where(lane < FOX_HEAD_DIM, acc_ref[0] / l0, acc_ref[1] / l1)
    o_ref[0] = o.astype(o_ref.dtype)


def _fox_prompt(q, k, v, frow, *, blk):
    b, s_len, _ = q.shape
    n_pairs = TOK_WIDTH // LANES
    nq = s_len // blk
    return pl.pallas_call(
        functools.partial(_fox_prompt_kernel, blk=blk, n_q_blocks=nq),
        out_shape=jax.ShapeDtypeStruct((b, s_len, TOK_WIDTH), BF16),
        grid=(b, n_pairs, nq),
        in_specs=[pl.BlockSpec((1, blk, LANES), lambda bi, hp, qi: (bi, qi, hp)),
                  pl.BlockSpec((LANES, s_len), lambda bi, hp, qi: (hp, bi)),
                  pl.BlockSpec((1, s_len, LANES), lambda bi, hp, qi: (bi, 0, hp)),
                  pl.BlockSpec((1, 2, s_len), lambda bi, hp, qi: (hp, 0, bi))],
        out_specs=pl.BlockSpec((1, blk, LANES), lambda bi, hp, qi: (bi, qi, hp)),
        scratch_shapes=[pltpu.VMEM((2, blk, LANES), F32), pltpu.VMEM((2, blk, LANES), F32),
                        pltpu.VMEM((2, blk, LANES), F32), pltpu.VMEM((2, blk, LANES), F32),
                        pltpu.VMEM((nq, 2, blk, blk), F32),
                        pltpu.VMEM((nq, 2, blk, blk), BF16)],
        compiler_params=pltpu.CompilerParams(
            dimension_semantics=("arbitrary", "arbitrary", "arbitrary"),
            vmem_limit_bytes=VMEM_LIMIT),
        name="fox_prompt",
    )(q, k, v, frow)


def _lane_suffix_sum(x):
    n = x.shape[-1]
    lane = lax.broadcasted_iota(jnp.int32, x.shape, 1)
    k = 1
    while k < n:
        x = x + jnp.where(lane + k < n, pltpu.roll(x, n - k, axis=1), 0.0)
        k *= 2
    return x


def _fox_paged_kernel(pt_ref, q_ref, kn_ref, vn_ref, lfn_ref, logf_ref, k_hbm, v_hbm, o_ref,
                      kbuf, vbuf, sem, pad_k_ref, pad_v_ref, *, n_pages, n_batch):
    b = pl.program_id(0)
    n_steps = pl.num_programs(0)
    slot = lax.rem(b, PAGE_SLOTS)
    n_q = q_ref.shape[1]
    n_rows = n_q * FOX_HEADS

    def page_copies(batch, slot_):
        copies = []
        for j in range(n_pages):
            page = pt_ref[j * n_batch + batch]
            copies.append(pltpu.make_async_copy(k_hbm.at[page], kbuf.at[slot_, j], sem.at[0, slot_]))
            copies.append(pltpu.make_async_copy(v_hbm.at[page], vbuf.at[slot_, j], sem.at[1, slot_]))
        return copies

    ahead = PAGE_SLOTS - 1
    for r in range(ahead):
        @pl.when(jnp.logical_and(b == 0, r < n_steps))
        def _(r=r):
            for c in page_copies(r, r):
                c.start()

    @pl.when(b + ahead < n_steps)
    def _():
        for c in page_copies(b + ahead, lax.rem(b + ahead, PAGE_SLOTS)):
            c.start()

    head = lax.broadcasted_iota(jnp.int32, (FOX_HEADS, TOK_WIDTH), 0)
    lane_head = jnp.right_shift(lax.broadcasted_iota(jnp.int32, (FOX_HEADS, TOK_WIDTH), 1), 6)
    own_head = head == lane_head
    tile_q = lambda x: jnp.concatenate([x] * n_q, axis=0)

    q = q_ref[0]
    qblk = jnp.concatenate(
        [jnp.where(own_head, jnp.broadcast_to(q[t:t + 1, :], (FOX_HEADS, TOK_WIDTH)), 0.0)
         for t in range(n_q)], axis=0).astype(BF16)

    pad_k_ref[...] = jnp.zeros_like(pad_k_ref)
    pad_v_ref[...] = jnp.zeros_like(pad_v_ref)
    pad_k_ref[0:n_q, :] = kn_ref[0]
    pad_v_ref[0:n_q, :] = vn_ref[0]
    c_new = _lane_cumsum(lfn_ref[0])
    cq = jnp.concatenate([c_new[:, t:t + 1] for t in range(n_q)], axis=0)
    s_own = lax.dot_general(qblk, pad_k_ref[...].astype(BF16), NT_DIMS, preferred_element_type=F32)
    t_of_row = jnp.right_shift(lax.broadcasted_iota(jnp.int32, (n_rows, PAGE_SIZE), 0), 3)
    key = lax.broadcasted_iota(jnp.int32, (n_rows, PAGE_SIZE), 1)
    s_own = jnp.where(key <= t_of_row, s_own + (tile_q(-c_new) + cq), -jnp.inf)

    logf = [logf_ref[pt_ref[j * n_batch + b]] for j in range(n_pages)]
    incl = _lane_suffix_sum(jnp.concatenate(logf, axis=0))
    later_pages = jnp.zeros((FOX_HEADS, 1), F32)
    bias = [None] * n_pages
    for j in reversed(range(n_pages)):
        rows = slice(j * FOX_HEADS, (j + 1) * FOX_HEADS)
        bias[j] = tile_q(later_pages + (incl[rows] - logf[j])) + cq
        later_pages = later_pages + incl[rows, 0:1]

    for c in page_copies(b, slot):
        c.wait()

    scores = [s_own]
    for j in range(n_pages):
        k_t = kbuf[slot, j].reshape(TOK_WIDTH, PAGE_SIZE).astype(BF16)
        scores.append(jnp.dot(qblk, k_t, preferred_element_type=F32) + bias[j])
    s_max = scores[0]
    for s in scores[1:]:
        s_max = jnp.maximum(s_max, s)
    m = jnp.max(s_max, axis=-1, keepdims=True)
    p = jnp.exp(scores[0] - m)
    p_sum = p
    acc = jnp.dot(p.astype(BF16), pad_v_ref[...].astype(BF16), preferred_element_type=F32)
    for j in range(n_pages):
        p = jnp.exp(scores[j + 1] - m)
        p_sum = p_sum + p
        v_t = vbuf[slot, j].reshape(TOK_WIDTH, PAGE_SIZE).astype(BF16)
        acc = acc + lax.dot_general(p.astype(BF16), v_t, NT_DIMS, preferred_element_type=F32)
    o = acc / jnp.sum(p_sum, axis=-1, keepdims=True)
    for t in range(n_q):
        o_t = jnp.where(own_head, o[t * FOX_HEADS:(t + 1) * FOX_HEADS, :], 0.0)
        o_ref[0, t:t + 1, :] = jnp.sum(o_t, axis=0, keepdims=True)


def _fox_paged(page_table, q, k_new, v_new, lfn, cache_logf_t, cache_k_t, cache_v_t):
    b, n_q, _ = q.shape
    n_pages = page_table.shape[0] // b
    cur = lambda bi, pt: (bi, 0, 0)
    page_shape = cache_k_t.shape[1:]
    grid_spec = pltpu.PrefetchScalarGridSpec(
        num_scalar_prefetch=1,
        grid=(b,),
        in_specs=[pl.BlockSpec((1, n_q, TOK_WIDTH), cur),
                  pl.BlockSpec((1, n_q, TOK_WIDTH), cur),
                  pl.BlockSpec((1, n_q, TOK_WIDTH), cur),
                  pl.BlockSpec((1, FOX_HEADS, LANES), cur),
                  pl.BlockSpec(cache_logf_t.shape, lambda bi, pt: (0, 0, 0), pipeline_mode=pl.Buffered(1)),
                  pl.BlockSpec(memory_space=pl.ANY),
                  pl.BlockSpec(memory_space=pl.ANY)],
        out_specs=pl.BlockSpec((1, n_q, TOK_WIDTH), cur),
        scratch_shapes=[pltpu.VMEM((PAGE_SLOTS, n_pages) + page_shape, F32),
                        pltpu.VMEM((PAGE_SLOTS, n_pages) + page_shape, F32),
                        pltpu.SemaphoreType.DMA((2, PAGE_SLOTS)),
                        pltpu.VMEM((PAGE_SIZE, TOK_WIDTH), F32),
                        pltpu.VMEM((PAGE_SIZE, TOK_WIDTH), F32)])
    return pl.pallas_call(
        functools.partial(_fox_paged_kernel, n_pages=n_pages, n_batch=b),
        out_shape=jax.ShapeDtypeStruct((b, n_q, TOK_WIDTH), F32),
        grid_spec=grid_spec,
        compiler_params=pltpu.CompilerParams(dimension_semantics=("arbitrary",),
                                             vmem_limit_bytes=VMEM_LIMIT),
        name="fox_paged",
    )(page_table, q, k_new, v_new, lfn, cache_logf_t, cache_k_t, cache_v_t)


def _decoder(x, pool_prev, n_prev, mem_k, mem_v, past, wts, *, tm, tm_proj, pool_bb, pool_chunk,
             mem_bb, mem_ts, act):
    b, s_len, _ = x.shape
    t = b * s_len
    h = x.reshape(t, D_MODEL)

    fuse_mem = s_len % tm_proj == 0

    def project_and_attend(layer, fox):
        mem = (mem_k, mem_v, layer, s_len) if fuse_mem else None
        tok_part, mem_part = _proj_in(h, wts["g_mix"][layer], wts["w_in"][layer], wts["g_fox_q_t"],
                                      wts["g_mem_q_t"][layer], mem, fox=fox, tm=tm_proj, act=act)
        if not fuse_mem:
            mem_part = _mem_attn(mem_part.reshape(b, s_len, MEM_WIDTH), mem_k, mem_v, layer,
                                 bb=mem_bb, ts=mem_ts)
        return tok_part, mem_part.reshape(t, MEM_WIDTH)

    z_tok, mo = project_and_attend(0, fox=False)
    z_tok = z_tok.reshape(b, s_len, TOK_WIDTH)
    tok = _pool_mix(z_tok, pool_prev, wts["w_pool"], wts["pool_scale"],
                    n_prev=n_prev, bb=pool_bb, chunk=pool_chunk, act=act)
    h = _out_ffn(h, tok.reshape(t, TOK_WIDTH), mo, wts["w_out"],
                 wts["g_ffn"][0], wts["w_gu"], wts["w_down"], 0, tm=tm)

    if past is None:
        k_new, v_new, k_bf, v_bf, logf_t, logf_new = _proj_kv_paged(
            h, wts["g_kv"], wts["w_kv"], wts["g_fox_k_t"].reshape(TOK_WIDTH, 1), wts["w_fg_t"],
            wts["b_fg"], tm=tm_proj)
    else:
        k_new, v_new, logf_t = _proj_kv(h, wts["g_kv"], wts["w_kv"], wts["g_fox_k_t"], wts["w_fg_t"],
                                        wts["b_fg"], tm=tm_proj)
        logf_new = logf_t.T

    q, mo = project_and_attend(1, fox=True)
    if past is None:
        f_row = _cumsum_rows(logf_t, seg=s_len)
        n_pairs = FOX_HEADS // 2
        f_row = f_row.reshape(n_pairs, 2, t)
        tok = _fox_prompt(q.reshape(b, s_len, TOK_WIDTH), k_bf,
                          v_bf.reshape(b, s_len, TOK_WIDTH), f_row, blk=512)
    else:
        page_table, cache_k_t, cache_v_t, cache_logf_t = past
        lfn = logf_t.reshape(FOX_HEADS, b, s_len).transpose(1, 0, 2)
        lfn = jnp.pad(lfn, ((0, 0), (0, 0), (0, LANES - s_len)))
        per_row = lambda a: a.reshape(b, s_len, TOK_WIDTH)
        tok = _fox_paged(page_table, per_row(q), per_row(k_new), per_row(v_new), lfn,
                         cache_logf_t, cache_k_t, cache_v_t)
    h = _out_ffn(h, tok.reshape(t, TOK_WIDTH), mo, wts["w_out"],
                 wts["g_ffn"][1], wts["w_gu"], wts["w_down"], 1, tm=tm)
    return h.reshape(b, s_len, D_MODEL), z_tok, k_new, v_new, logf_new


def kernel(x_prompt, x_sample, cache_mem_k, cache_mem_v, state_pool, cache_k, cache_v, cache_logf,
           page_table, mem_prompt, g_mix, w_in, w_out, g_ffn, w_gu, w_down, g_mem, w_mem_kv,
           g_mem_q, g_mem_k, w_pool, pool_scale, g_kv, w_kv, g_fox_k, w_fg, b_fg, g_fox_q):
    depth = w_in.shape[0]
    b, s_len, _ = x_prompt.shape
    db, ds, _ = x_sample.shape
    wts = {
        "g_mix": g_mix.reshape(depth, 1, D_MODEL),
        "w_in": w_in.astype(BF16),
        "w_out": w_out.astype(BF16),
        "g_ffn": g_ffn.reshape(depth, 1, D_MODEL),
        "w_gu": w_gu.astype(BF16),
        "w_down": w_down.astype(BF16),
        "g_mem_q_t": jnp.tile(g_mem_q, (1, MEM_HEADS)).reshape(depth, 1, MEM_WIDTH),
        "w_pool": w_pool[0].astype(BF16),
        "pool_scale": pool_scale[0].reshape(1, TOK_WIDTH),
        "g_kv": g_kv.reshape(1, D_MODEL),
        "w_kv": w_kv.astype(BF16),
        "g_fox_k_t": jnp.tile(g_fox_k, FOX_HEADS).reshape(1, TOK_WIDTH),
        "w_fg_t": w_fg.T.astype(BF16),
        "b_fg": b_fg.reshape(FOX_HEADS, 1),
        "g_fox_q_t": jnp.tile(g_fox_q[0], FOX_HEADS).reshape(1, TOK_WIDTH),
    }

    mem_k_p, mem_v_p = _proj_memkv(mem_prompt.reshape(b * N_MEM, D_MODEL),
                                   g_mem.reshape(depth, 1, D_MODEL), w_mem_kv.astype(BF16),
                                   jnp.tile(g_mem_k, (1, MEM_HEADS)).reshape(depth, 1, MEM_WIDTH), tm=512)
    mem_rows = (N_MEM * MEM_HEADS, MEM_HEAD_DIM)
    mem_k_p = mem_k_p.reshape((depth, b) + mem_rows)
    mem_v_p = mem_v_p.reshape((depth, b) + mem_rows)
    y_p, ztok_p, k_p, v_p, logf_p = _decoder(
        x_prompt, jnp.zeros((b, POOL_HIST, TOK_WIDTH), F32), 0, mem_k_p, mem_v_p, None, wts,
        tm=512, tm_proj=1024, pool_bb=1, pool_chunk=256, mem_bb=1, mem_ts=512, act=BF16)
    n_pp = s_len // PAGE_SIZE
    head_shape = (FOX_HEADS, FOX_HEAD_DIM)

    past = (page_table.T.reshape(-1),
            cache_k.transpose(0, 2, 3, 1), cache_v.transpose(0, 2, 3, 1),
            cache_logf.transpose(0, 2, 1))
    prev = jnp.pad(state_pool[0], ((0, 0), (POOL_HIST - POOL_STATE, 0), (0, 0)))
    y_s, ztok_s, k_s, v_s, logf_s = _decoder(
        x_sample, prev, POOL_STATE, cache_mem_k.reshape((depth, db) + mem_rows),
        cache_mem_v.reshape((depth, db) + mem_rows), past, wts,
        tm=512, tm_proj=512, pool_bb=32, pool_chunk=ds, mem_bb=8, mem_ts=ds, act=F32)

    pool_state_p = ztok_p[:, s_len - POOL_STATE:][None]
    pool_state_s = jnp.concatenate([state_pool[0], ztok_s], axis=1)[:, -POOL_STATE:][None]
    paged = lambda a: a.reshape((b, n_pp) + head_shape + (PAGE_SIZE,)).transpose(0, 1, 4, 2, 3)
    return (y_p, y_s, paged(k_p), paged(v_p),
            logf_p.reshape(b, n_pp, FOX_HEADS, PAGE_SIZE).transpose(0, 1, 3, 2),
            mem_k_p.reshape(depth, b, N_MEM, MEM_HEADS, MEM_HEAD_DIM),
            mem_v_p.reshape(depth, b, N_MEM, MEM_HEADS, MEM_HEAD_DIM),
            pool_state_p,
            k_s.reshape((db, ds) + head_shape),
            v_s.reshape((db, ds) + head_shape),
            logf_s.reshape(db, ds, FOX_HEADS),
            pool_state_s)
```

```python
import functools

import jax
import jax.numpy as jnp
from jax import lax
from jax.experimental import pallas as pl
from jax.experimental.pallas import tpu as pltpu

D_MODEL = 1024
TOK_WIDTH = 512
MEM_WIDTH = 512
POOL_WINDOWS = (2, 4, 8, 16)
POOL_GROUP = 128
POOL_STATE = 15
POOL_HIST = 16
FOX_HEADS = 8
FOX_HEAD_DIM = 64
MEM_HEADS = 4
MEM_HEAD_DIM = 128
N_MEM = 256
D_FF = 2816
PAGE_SIZE = 128
EPS = 1e-6
LOG2E = 1.4426950408889634
LANES = 128
BF16_ROWS = 16
VMEM_LIMIT = 56 * 1024 * 1024
FOX_CHUNK_ROWS = 16
PAGE_SLOTS = 4

F32 = jnp.float32
BF16 = jnp.bfloat16
NT_DIMS = (((1,), (1,)), ((), ()))


def _rmsnorm(x, g):
    return x * lax.rsqrt(jnp.mean(x * x, axis=-1, keepdims=True) + EPS) * g


def _headnorm128(z, g):
    outs = []
    for h in range(z.shape[-1] // LANES):
        zh = z[:, h * LANES:(h + 1) * LANES]
        r = lax.rsqrt(jnp.mean(zh * zh, axis=-1, keepdims=True) + EPS)
        outs.append(zh * r * g[:, h * LANES:(h + 1) * LANES])
    return jnp.concatenate(outs, axis=-1)


def _headnorm64(z, g):
    outs = []
    for p in range(z.shape[-1] // LANES):
        zp = z[:, p * LANES:(p + 1) * LANES]
        sq = zp * zp
        lane = lax.broadcasted_iota(jnp.int32, zp.shape, 1)
        lo = lane < FOX_HEAD_DIM
        s_lo = jnp.sum(jnp.where(lo, sq, 0.0), axis=-1, keepdims=True)
        s_hi = jnp.sum(jnp.where(lo, 0.0, sq), axis=-1, keepdims=True)
        r = jnp.where(lo, lax.rsqrt(s_lo / FOX_HEAD_DIM + EPS), lax.rsqrt(s_hi / FOX_HEAD_DIM + EPS))
        outs.append(zp * r * g[:, p * LANES:(p + 1) * LANES])
    return jnp.concatenate(outs, axis=-1)


def _mem_attend(q, k_ref, v_ref):
    outs = []
    for h in range(MEM_HEADS):
        qh = q[:, h * MEM_HEAD_DIM:(h + 1) * MEM_HEAD_DIM].astype(BF16)
        k = k_ref[pl.ds(h, N_MEM, stride=MEM_HEADS), :].astype(BF16)
        v = v_ref[pl.ds(h, N_MEM, stride=MEM_HEADS), :].astype(BF16)
        s = lax.dot_general(qh, k, NT_DIMS, preferred_element_type=F32) * (MEM_HEAD_DIM ** -0.5)
        p = jnp.exp(s - jnp.max(s, axis=-1, keepdims=True))
        o = jnp.dot(p.astype(BF16), v, preferred_element_type=F32)
        outs.append(o / jnp.sum(p, axis=-1, keepdims=True))
    return jnp.concatenate(outs, axis=-1)


def _proj_in_kernel(x_ref, g_ref, w_ref, gtok_ref, gmem_ref, *refs, fox, fuse_mem):
    xn = _rmsnorm(x_ref[...], g_ref[...]).astype(BF16)
    z = jnp.dot(xn, w_ref[...], preferred_element_type=F32)
    z_tok = z[:, :TOK_WIDTH]
    qm = _headnorm128(z[:, TOK_WIDTH:], gmem_ref[...])
    if fuse_mem:
        mk_ref, mv_ref, tok_ref, mem_ref = refs
        mem_ref[...] = _mem_attend(qm, mk_ref, mv_ref).astype(mem_ref.dtype)
    else:
        tok_ref, mem_ref = refs
        mem_ref[...] = qm.astype(mem_ref.dtype)
    if fox:
        tok_ref[...] = (_headnorm64(z_tok, gtok_ref[...]) * (FOX_HEAD_DIM ** -0.5 * LOG2E)).astype(tok_ref.dtype)
    else:
        tok_ref[...] = z_tok


def _proj_in(x, g, w, g_tok, g_mem, mem=None, *, fox, tm, act):
    t = x.shape[0]
    row = lambda i: (i, 0)
    fixed = lambda i: (0, 0)
    operands = [x, g, w, g_tok, g_mem]
    in_specs = [pl.BlockSpec((tm, D_MODEL), row),
                pl.BlockSpec((1, D_MODEL), fixed),
                pl.BlockSpec((D_MODEL, D_MODEL), fixed),
                pl.BlockSpec((1, TOK_WIDTH), fixed),
                pl.BlockSpec((1, MEM_WIDTH), fixed)]
    if mem is not None:
        mk, mv, layer, rows_per_batch = mem
        tiles = rows_per_batch // tm
        mem_block = pl.BlockSpec((None, None, N_MEM * MEM_HEADS, MEM_HEAD_DIM),
                                 lambda i: (layer, i // tiles, 0, 0))
        operands += [mk, mv]
        in_specs += [mem_block, mem_block]
    return pl.pallas_call(
        functools.partial(_proj_in_kernel, fox=fox, fuse_mem=mem is not None),
        out_shape=(jax.ShapeDtypeStruct((t, TOK_WIDTH), act if fox else F32),
                   jax.ShapeDtypeStruct((t, MEM_WIDTH), act)),
        grid=(t // tm,),
        in_specs=in_specs,
        out_specs=(pl.BlockSpec((tm, TOK_WIDTH), row), pl.BlockSpec((tm, MEM_WIDTH), row)),
        compiler_params=pltpu.CompilerParams(dimension_semantics=("arbitrary",),
                                             vmem_limit_bytes=VMEM_LIMIT),
        name="proj_in_fox" if fox else "proj_in_pool",
    )(*operands)


def _log_sigmoid(x):
    return jnp.minimum(x, 0.0) - jnp.log(1.0 + jnp.exp(-jnp.abs(x)))


def _proj_kv_kernel(x_ref, g_ref, w_ref, gk_ref, wfg_ref, bfg_ref, k_ref, v_ref, logf_ref):
    xn = _rmsnorm(x_ref[...], g_ref[...]).astype(BF16)
    z = jnp.dot(xn, w_ref[...], preferred_element_type=F32)
    k_ref[...] = _headnorm64(z[:, :TOK_WIDTH], gk_ref[...])
    v_ref[...] = z[:, TOK_WIDTH:]
    gate = lax.dot_general(wfg_ref[...], xn, NT_DIMS, preferred_element_type=F32) + bfg_ref[...]
    logf_ref[...] = _log_sigmoid(gate)


def _proj_kv_paged_kernel(x_ref, g_ref, w_ref, gk_ref, wfg_ref, bfg_ref,
                          kp_ref, vp_ref, kb_ref, vb_ref, logf_ref, logfp_ref):
    xn = _rmsnorm(x_ref[...], g_ref[...]).astype(BF16)
    z = jnp.dot(xn, w_ref[...], preferred_element_type=F32)
    v = z[:, TOK_WIDTH:]
    vb_ref[...] = v.astype(BF16)
    gate = lax.dot_general(wfg_ref[...], xn, NT_DIMS, preferred_element_type=F32) + bfg_ref[...]
    logf = _log_sigmoid(gate)
    logf_ref[...] = logf
    zk_t = z[:, :TOK_WIDTH].T
    heads = []
    for h in range(FOX_HEADS):
        zh = zk_t[h * FOX_HEAD_DIM:(h + 1) * FOX_HEAD_DIM, :]
        heads.append(zh * lax.rsqrt(jnp.mean(zh * zh, axis=0, keepdims=True) + EPS))
    k_t = jnp.concatenate(heads, axis=0) * gk_ref[...]
    v_t = v.T
    kb_ref[...] = k_t.astype(BF16)
    for pg in range(kp_ref.shape[0]):
        rows = slice(pg * PAGE_SIZE, (pg + 1) * PAGE_SIZE)
        kp_ref[pg] = k_t[:, rows]
        vp_ref[pg] = v_t[:, rows]
        logfp_ref[pg] = logf[:, rows]


def _proj_kv_paged(x, g, w, g_k, wfg_t, bfg, *, tm):
    t = x.shape[0]
    row = lambda i: (i, 0)
    fixed = lambda i: (0, 0)
    ppt = tm // PAGE_SIZE
    page_rows = lambda i: (i, 0, 0)
    return pl.pallas_call(
        _proj_kv_paged_kernel,
        out_shape=(jax.ShapeDtypeStruct((t // PAGE_SIZE, TOK_WIDTH, PAGE_SIZE), F32),
                   jax.ShapeDtypeStruct((t // PAGE_SIZE, TOK_WIDTH, PAGE_SIZE), F32),
                   jax.ShapeDtypeStruct((TOK_WIDTH, t), BF16),
                   jax.ShapeDtypeStruct((t, TOK_WIDTH), BF16),
                   jax.ShapeDtypeStruct((FOX_HEADS, t), F32),
                   jax.ShapeDtypeStruct((t // PAGE_SIZE, FOX_HEADS, PAGE_SIZE), F32)),
        grid=(t // tm,),
        in_specs=[pl.BlockSpec((tm, D_MODEL), row),
                  pl.BlockSpec((1, D_MODEL), fixed),
                  pl.BlockSpec((D_MODEL, D_MODEL), fixed),
                  pl.BlockSpec((TOK_WIDTH, 1), fixed),
                  pl.BlockSpec((FOX_HEADS, D_MODEL), fixed),
                  pl.BlockSpec((FOX_HEADS, 1), fixed)],
        out_specs=(pl.BlockSpec((ppt, TOK_WIDTH, PAGE_SIZE), page_rows),
                   pl.BlockSpec((ppt, TOK_WIDTH, PAGE_SIZE), page_rows),
                   pl.BlockSpec((TOK_WIDTH, tm), lambda i: (0, i)), pl.BlockSpec((tm, TOK_WIDTH), row),
                   pl.BlockSpec((FOX_HEADS, tm), lambda i: (0, i)),
                   pl.BlockSpec((ppt, FOX_HEADS, PAGE_SIZE), page_rows)),
        compiler_params=pltpu.CompilerParams(dimension_semantics=("arbitrary",),
                                             vmem_limit_bytes=VMEM_LIMIT),
        name="proj_kv_paged",
    )(x, g, w, g_k, wfg_t, bfg)


def _proj_kv(x, g, w, g_k, wfg_t, bfg, *, tm):
    t = x.shape[0]
    row = lambda i: (i, 0)
    fixed = lambda i: (0, 0)
    return pl.pallas_call(
        _proj_kv_kernel,
        out_shape=(jax.ShapeDtypeStruct((t, TOK_WIDTH), F32),
                   jax.ShapeDtypeStruct((t, TOK_WIDTH), F32),
                   jax.ShapeDtypeStruct((FOX_HEADS, t), F32)),
        grid=(t // tm,),
        in_specs=[pl.BlockSpec((tm, D_MODEL), row),
                  pl.BlockSpec((1, D_MODEL), fixed),
                  pl.BlockSpec((D_MODEL, D_MODEL), fixed),
                  pl.BlockSpec((1, TOK_WIDTH), fixed),
                  pl.BlockSpec((FOX_HEADS, D_MODEL), fixed),
                  pl.BlockSpec((FOX_HEADS, 1), fixed)],
        out_specs=(pl.BlockSpec((tm, TOK_WIDTH), row), pl.BlockSpec((tm, TOK_WIDTH), row),
                   pl.BlockSpec((FOX_HEADS, tm), lambda i: (0, i))),
        compiler_params=pltpu.CompilerParams(dimension_semantics=("arbitrary",),
                                             vmem_limit_bytes=VMEM_LIMIT),
        name="proj_kv",
    )(x, g, w, g_k, wfg_t, bfg)


def _proj_memkv_kernel(x_ref, g_ref, w_ref, gk_ref, k_ref, v_ref):
    xn = _rmsnorm(x_ref[...], g_ref[0]).astype(BF16)
    z = jnp.dot(xn, w_ref[0], preferred_element_type=F32)
    k = _headnorm128(z[:, :MEM_WIDTH], gk_ref[0])
    tm = x_ref.shape[0]
    for h in range(MEM_HEADS):
        rows = pl.ds(h, tm, stride=MEM_HEADS)
        k_ref[0, rows, :] = k[:, h * MEM_HEAD_DIM:(h + 1) * MEM_HEAD_DIM]
        v_ref[0, rows, :] = z[:, MEM_WIDTH + h * MEM_HEAD_DIM:MEM_WIDTH + (h + 1) * MEM_HEAD_DIM]


def _proj_memkv(x, g, w, g_k, *, tm):
    t = x.shape[0]
    n_layers = w.shape[0]
    out_block = pl.BlockSpec((1, tm * MEM_HEADS, MEM_HEAD_DIM), lambda l, i: (l, i, 0))
    return pl.pallas_call(
        _proj_memkv_kernel,
        out_shape=(jax.ShapeDtypeStruct((n_layers, t * MEM_HEADS, MEM_HEAD_DIM), F32),
                   jax.ShapeDtypeStruct((n_layers, t * MEM_HEADS, MEM_HEAD_DIM), F32)),
        grid=(n_layers, t // tm),
        in_specs=[pl.BlockSpec((tm, D_MODEL), lambda l, i: (i, 0)),
                  pl.BlockSpec((1, 1, D_MODEL), lambda l, i: (l, 0, 0)),
                  pl.BlockSpec((1, D_MODEL, D_MODEL), lambda l, i: (l, 0, 0)),
                  pl.BlockSpec((1, 1, MEM_WIDTH), lambda l, i: (l, 0, 0))],
        out_specs=(out_block, out_block),
        compiler_params=pltpu.CompilerParams(dimension_semantics=("arbitrary", "arbitrary"),
                                             vmem_limit_bytes=VMEM_LIMIT),
        name="proj_memkv",
    )(x, g, w, g_k)


def _pool_kernel(u_ref, prev_ref, w_ref, scale_ref, o_ref, ext_ref, *, n_prev, chunk):
    bb, s_len, _ = u_ref.shape
    ext_ref[:, 0:POOL_HIST, :] = prev_ref[...]
    ext_ref[:, POOL_HIST:, :] = u_ref[...]
    for c in range(s_len // chunk):
        r0 = POOL_HIST + c * chunk
        pos = c * chunk + lax.broadcasted_iota(jnp.int32, (1, chunk, 1), 1)
        for gi, win in enumerate(POOL_WINDOWS):
            lanes = slice(gi * POOL_GROUP, (gi + 1) * POOL_GROUP)
            u_new = ext_ref[:, r0:r0 + chunk, lanes]
            acc = u_new
            for k in range(1, win):
                acc = acc + ext_ref[:, r0 - k:r0 - k + chunk, lanes]
            count = jnp.minimum(win, n_prev + pos + 1).astype(F32)
            y = acc / count - u_new
            z = jnp.dot(y.reshape(bb * chunk, POOL_GROUP).astype(BF16), w_ref[gi],
                        preferred_element_type=F32)
            z = z.reshape(bb, chunk, POOL_GROUP) * scale_ref[:, lanes]
            o_ref[:, c * chunk:(c + 1) * chunk, lanes] = z.astype(o_ref.dtype)


def _pool_mix(u, prev, w_pool, scale, *, n_prev, bb, chunk, act):
    b, s_len, _ = u.shape
    return pl.pallas_call(
        functools.partial(_pool_kernel, n_prev=n_prev, chunk=chunk),
        out_shape=jax.ShapeDtypeStruct((b, s_len, TOK_WIDTH), act),
        grid=(b // bb,),
        in_specs=[pl.BlockSpec((bb, s_len, TOK_WIDTH), lambda i: (i, 0, 0)),
                  pl.BlockSpec((bb, POOL_HIST, TOK_WIDTH), lambda i: (i, 0, 0)),
                  pl.BlockSpec((len(POOL_WINDOWS), POOL_GROUP, POOL_GROUP), lambda i: (0, 0, 0)),
                  pl.BlockSpec((1, TOK_WIDTH), lambda i: (0, 0))],
        out_specs=pl.BlockSpec((bb, s_len, TOK_WIDTH), lambda i: (i, 0, 0)),
        scratch_shapes=[pltpu.VMEM((bb, POOL_HIST + s_len, TOK_WIDTH), F32)],
        compiler_params=pltpu.CompilerParams(dimension_semantics=("arbitrary",),
                                             vmem_limit_bytes=VMEM_LIMIT),
        name="pool_mix",
    )(u, prev, w_pool, scale)


def _mem_attn_kernel(q_ref, k_ref, v_ref, o_ref):
    for h in range(MEM_HEADS):
        lanes = slice(h * MEM_HEAD_DIM, (h + 1) * MEM_HEAD_DIM)
        q = q_ref[:, :, lanes].astype(BF16)
        k = k_ref[:, pl.ds(h, N_MEM, stride=MEM_HEADS), :].astype(BF16)
        v = v_ref[:, pl.ds(h, N_MEM, stride=MEM_HEADS), :].astype(BF16)
        s = jnp.einsum("bqd,bkd->bqk", q, k, preferred_element_type=F32) * (MEM_HEAD_DIM ** -0.5)
        m = jnp.max(s, axis=-1, keepdims=True)
        p = jnp.exp(s - m)
        l = jnp.sum(p, axis=-1, keepdims=True)
        o = jnp.einsum("bqk,bkd->bqd", p.astype(BF16), v, preferred_element_type=F32)
        o_ref[:, :, lanes] = (o / l).astype(o_ref.dtype)


def _mem_attn(q, mk, mv, layer, *, bb, ts):
    b, s_len, _ = q.shape
    mem_block = pl.BlockSpec((None, bb, N_MEM * MEM_HEADS, MEM_HEAD_DIM),
                             lambda i, j: (layer, i, 0, 0))
    return pl.pallas_call(
        _mem_attn_kernel,
        out_shape=jax.ShapeDtypeStruct((b, s_len, MEM_WIDTH), q.dtype),
        grid=(b // bb, s_len // ts),
        in_specs=[pl.BlockSpec((bb, ts, MEM_WIDTH), lambda i, j: (i, j, 0)), mem_block, mem_block],
        out_specs=pl.BlockSpec((bb, ts, MEM_WIDTH), lambda i, j: (i, j, 0)),
        compiler_params=pltpu.CompilerParams(dimension_semantics=("arbitrary", "arbitrary"),
                                             vmem_limit_bytes=VMEM_LIMIT),
        name="mem_attn",
    )(q, mk, mv)


def _out_ffn_kernel(h_ref, tok_ref, mo_ref, wout_ref, g_ref, wg_ref, wu_ref, wd_ref, o_ref):
    h1 = (h_ref[...]
          + jnp.dot(tok_ref[...].astype(BF16), wout_ref[:TOK_WIDTH, :], preferred_element_type=F32)
          + jnp.dot(mo_ref[...].astype(BF16), wout_ref[TOK_WIDTH:, :], preferred_element_type=F32))
    x = _rmsnorm(h1, g_ref[...]).astype(BF16)
    gate = jnp.dot(x, wg_ref[...], preferred_element_type=F32)
    up = jnp.dot(x, wu_ref[...], preferred_element_type=F32)
    act = (gate * (1.0 / (1.0 + jnp.exp(-gate))) * up).astype(BF16)
    o_ref[...] = h1 + jnp.dot(act, wd_ref[...], preferred_element_type=F32)


def _out_ffn(h, tok, mo, w_out, g, w_gu, w_down, layer, *, tm):
    t = h.shape[0]
    row = lambda i: (i, 0)
    once = pl.Buffered(1)
    return pl.pallas_call(
        _out_ffn_kernel,
        out_shape=jax.ShapeDtypeStruct((t, D_MODEL), F32),
        grid=(t // tm,),
        in_specs=[pl.BlockSpec((tm, D_MODEL), row),
                  pl.BlockSpec((tm, TOK_WIDTH), row),
                  pl.BlockSpec((tm, MEM_WIDTH), row),
                  pl.BlockSpec((None, D_MODEL, D_MODEL), lambda i: (layer, 0, 0), pipeline_mode=once),
                  pl.BlockSpec((1, D_MODEL), lambda i: (0, 0)),
                  pl.BlockSpec((None, D_MODEL, D_FF), lambda i: (layer, 0, 0), pipeline_mode=once),
                  pl.BlockSpec((None, D_MODEL, D_FF), lambda i: (layer, 0, 1), pipeline_mode=once),
                  pl.BlockSpec((None, D_FF, D_MODEL), lambda i: (layer, 0, 0), pipeline_mode=once)],
        out_specs=pl.BlockSpec((tm, D_MODEL), row),
        compiler_params=pltpu.CompilerParams(dimension_semantics=("arbitrary",),
                                             vmem_limit_bytes=VMEM_LIMIT),
        name="out_ffn",
    )(h, tok, mo, w_out, g, w_gu, w_gu, w_down)


def _lane_cumsum(x):
    n = x.shape[-1]
    lane = lax.broadcasted_iota(jnp.int32, x.shape, 1)
    k = 1
    while k < n:
        x = x + jnp.where(lane >= k, pltpu.roll(x, k, axis=1), 0.0)
        k *= 2
    return x


def _cumsum_kernel(x_ref, o_ref):
    o_ref[...] = _lane_cumsum(x_ref[...])


def _cumsum_rows(x, *, seg):
    rows, t = x.shape
    return pl.pallas_call(
        _cumsum_kernel,
        out_shape=jax.ShapeDtypeStruct((rows, t), F32),
        grid=(t // seg,),
        in_specs=[pl.BlockSpec((rows, seg), lambda i: (0, i))],
        out_specs=pl.BlockSpec((rows, seg), lambda i: (0, i)),
        compiler_params=pltpu.CompilerParams(dimension_semantics=("arbitrary",)),
        name="logf_cumsum",
    )(x)


def _round_to_bf16(x):
    return x.astype(BF16).astype(F32)


def _fox_prompt_kernel(q_ref, k_ref, v_ref, frow_ref, o_ref, m_ref, l_ref, acc_ref, alpha_ref,
                       s_ref, p_all_ref, *, blk, n_q_blocks):
    qi = pl.program_id(2)
    q = q_ref[0]
    lane = lax.broadcasted_iota(jnp.int32, q.shape, 1)
    sub = FOX_CHUNK_ROWS
    n_rep = blk // LANES
    bias_lane = [FOX_HEAD_DIM, 0]
    q_f32 = q.astype(F32)
    q_head = []
    for j in range(2):
        own = (lane < FOX_HEAD_DIM) if j == 0 else (lane >= FOX_HEAD_DIM)
        ones_from = jnp.where(lane >= bias_lane[j], 1.0, 0.0)
        bias_ones = jnp.where(lane < bias_lane[j] + 3, ones_from, 0.0)
        q_head.append(jnp.where(own, q_f32, bias_ones).astype(BF16))
    m_ref[...] = jnp.full(m_ref.shape, -jnp.inf, F32)
    l_ref[...] = jnp.zeros_like(l_ref)
    acc_ref[...] = jnp.zeros_like(acc_ref)

    def scores(qc, ki, s_buf):
        keys = slice(ki * blk, (ki + 1) * blk)
        kt = k_ref[:, keys]
        tile_row = lax.broadcasted_iota(jnp.int32, (BF16_ROWS, blk), 0)
        for j in range(2):
            f_end = frow_ref[0, j:j + 1, (qc + 1) * blk - 1:(qc + 1) * blk]
            bias = (f_end - frow_ref[0, j:j + 1, keys]) * LOG2E
            hi = _round_to_bf16(bias)
            mid = _round_to_bf16(bias - hi)
            lo = _round_to_bf16(bias - hi - mid)
            bias_rows = jnp.where(tile_row == 0, hi, jnp.where(tile_row == 1, mid,
                                                               jnp.where(tile_row == 2, lo, 0.0)))
            b0 = bias_lane[j]
            pieces = [kt[:b0], bias_rows.astype(BF16), kt[b0 + BF16_ROWS:]]
            kt_j = jnp.concatenate([x for x in pieces if x.shape[0]], axis=0)
            s_buf[j] = jnp.dot(q_head[j], kt_j, preferred_element_type=F32)

    def update(ki, s_buf, masked):
        p_ref = p_all_ref.at[ki]
        vt = v_ref[0, ki * blk:(ki + 1) * blk, :]
        for j in range(2):
            def chunk(r):
                n_use = -(-(r + 1) * sub // LANES) if masked else n_rep
                rows = slice(r * sub, (r + 1) * sub)
                s = s_buf[j, rows, 0:n_use * LANES]
                if masked:
                    row = lax.broadcasted_iota(jnp.int32, s.shape, 0) + r * sub
                    col = lax.broadcasted_iota(jnp.int32, s.shape, 1)
                    s = jnp.where(col <= row, s, -jnp.inf)
                return rows, n_use, s

            for r in range(blk // sub):
                rows, n_use, s = chunk(r)
                m_prev = m_ref[j, rows]
                m_new = jnp.maximum(m_prev, jnp.max(s, axis=-1, keepdims=True))
                alpha_ref[j, rows] = jnp.exp2(m_prev - m_new)
                m_ref[j, rows] = m_new
            for r in range(blk // sub):
                rows, n_use, s = chunk(r)
                p = jnp.exp2(s - jnp.concatenate([m_ref[j, rows]] * n_use, axis=1))
                p_lanes = p[:, :LANES]
                for c in range(1, n_use):
                    p_lanes = p_lanes + p[:, c * LANES:(c + 1) * LANES]
                l_ref[j, rows] = alpha_ref[j, rows] * l_ref[j, rows] + p_lanes
                p_ref[j, rows, 0:n_use * LANES] = p.astype(BF16)
                if n_use < n_rep:
                    p_ref[j, rows, n_use * LANES:] = jnp.zeros((sub, blk - n_use * LANES), BF16)
            acc_ref[j] = alpha_ref[j] * acc_ref[j] + jnp.dot(p_ref[j], vt, preferred_element_type=F32)

    for qc in range(n_q_blocks):
        @pl.when(qi == qc)
        def _(qc=qc):
            scores(qc, 0, s_ref.at[0])
            for ki in range(qc):
                scores(qc, ki + 1, s_ref.at[ki + 1])
                update(ki, s_ref.at[ki], False)
            update(qc, s_ref.at[qc], True)

    l0 = jnp.sum(l_ref[0], axis=-1, keepdims=True)
    l1 = jnp.sum(l_ref[1], axis=-1, keepdims=True)
    o = jnp.where(lane < FOX_HEAD_DIM, acc_ref[0] / l0, acc_ref[1] / l1)
    o_ref[0] = o.astype(o_ref.dtype)


def _fox_prompt(q, k, v, frow, *, blk):
    b, s_len, _ = q.shape
    n_pairs = TOK_WIDTH // LANES
    nq = s_len // blk
    return pl.pallas_call(
        functools.partial(_fox_prompt_kernel, blk=blk, n_q_blocks=nq),
        out_shape=jax.ShapeDtypeStruct((b, s_len, TOK_WIDTH), BF16),
        grid=(b, n_pairs, nq),
        in_specs=[pl.BlockSpec((1, blk, LANES), lambda bi, hp, qi: (bi, qi, hp)),
                  pl.BlockSpec((LANES, s_len), lambda bi, hp, qi: (hp, bi)),
                  pl.BlockSpec((1, s_len, LANES), lambda bi, hp, qi: (bi, 0, hp)),
                  pl.BlockSpec((1, 2, s_len), lambda bi, hp, qi: (hp, 0, bi))],
        out_specs=pl.BlockSpec((1, blk, LANES), lambda bi, hp, qi: (bi, qi, hp)),
        scratch_shapes=[pltpu.VMEM((2, blk, LANES), F32), pltpu.VMEM((2, blk, LANES), F32),
                        pltpu.VMEM((2, blk, LANES), F32), pltpu.VMEM((2, blk, LANES), F32),
                        pltpu.VMEM((nq, 2, blk, blk), F32),
                        pltpu.VMEM((nq, 2, blk, blk), BF16)],
        compiler_params=pltpu.CompilerParams(
            dimension_semantics=("arbitrary", "arbitrary", "arbitrary"),
            vmem_limit_bytes=VMEM_LIMIT),
        name="fox_prompt",
    )(q, k, v, frow)


def _lane_suffix_sum(x):
    n = x.shape[-1]
    lane = lax.broadcasted_iota(jnp.int32, x.shape, 1)
    k = 1
    while k < n:
        x = x + jnp.where(lane + k < n, pltpu.roll(x, n - k, axis=1), 0.0)
        k *= 2
    return x


def _fox_paged_kernel(pt_ref, q_ref, kn_ref, vn_ref, lfn_ref, logf_ref, k_hbm, v_hbm, o_ref,
                      kbuf, vbuf, sem, pad_k_ref, pad_v_ref, *, n_pages, n_batch):
    b = pl.program_id(0)
    n_steps = pl.num_programs(0)
    slot = lax.rem(b, PAGE_SLOTS)
    n_q = q_ref.shape[1]
    n_rows = n_q * FOX_HEADS

    def page_copies(batch, slot_):
        copies = []
        for j in range(n_pages):
            page = pt_ref[j * n_batch + batch]
            copies.append(pltpu.make_async_copy(k_hbm.at[page], kbuf.at[slot_, j], sem.at[0, slot_]))
            copies.append(pltpu.make_async_copy(v_hbm.at[page], vbuf.at[slot_, j], sem.at[1, slot_]))
        return copies

    ahead = PAGE_SLOTS - 1
    for r in range(ahead):
        @pl.when(jnp.logical_and(b == 0, r < n_steps))
        def _(r=r):
            for c in page_copies(r, r):
                c.start()

    @pl.when(b + ahead < n_steps)
    def _():
        for c in page_copies(b + ahead, lax.rem(b + ahead, PAGE_SLOTS)):
            c.start()

    head = lax.broadcasted_iota(jnp.int32, (FOX_HEADS, TOK_WIDTH), 0)
    lane_head = jnp.right_shift(lax.broadcasted_iota(jnp.int32, (FOX_HEADS, TOK_WIDTH), 1), 6)
    own_head = head == lane_head
    tile_q = lambda x: jnp.concatenate([x] * n_q, axis=0)

    q = q_ref[0]
    qblk = jnp.concatenate(
        [jnp.where(own_head, jnp.broadcast_to(q[t:t + 1, :], (FOX_HEADS, TOK_WIDTH)), 0.0)
         for t in range(n_q)], axis=0).astype(BF16)

    pad_k_ref[...] = jnp.zeros_like(pad_k_ref)
    pad_v_ref[...] = jnp.zeros_like(pad_v_ref)
    pad_k_ref[0:n_q, :] = kn_ref[0]
    pad_v_ref[0:n_q, :] = vn_ref[0]
    c_new = _lane_cumsum(lfn_ref[0])
    cq = jnp.concatenate([c_new[:, t:t + 1] for t in range(n_q)], axis=0)
    s_own = lax.dot_general(qblk, pad_k_ref[...].astype(BF16), NT_DIMS, preferred_element_type=F32)
    t_of_row = jnp.right_shift(lax.broadcasted_iota(jnp.int32, (n_rows, PAGE_SIZE), 0), 3)
    key = lax.broadcasted_iota(jnp.int32, (n_rows, PAGE_SIZE), 1)
    s_own = jnp.where(key <= t_of_row, s_own + (tile_q(-c_new) + cq) * LOG2E, -jnp.inf)

    logf = [logf_ref[pt_ref[j * n_batch + b]] for j in range(n_pages)]
    incl = _lane_suffix_sum(jnp.concatenate(logf, axis=0))
    later_pages = jnp.zeros((FOX_HEADS, 1), F32)
    bias = [None] * n_pages
    for j in reversed(range(n_pages)):
        rows = slice(j * FOX_HEADS, (j + 1) * FOX_HEADS)
        bias[j] = (tile_q(later_pages + (incl[rows] - logf[j])) + cq) * LOG2E
        later_pages = later_pages + incl[rows, 0:1]

    for c in page_copies(b, slot):
        c.wait()

    scores = [s_own]
    for j in range(n_pages):
        k_t = kbuf[slot, j].reshape(TOK_WIDTH, PAGE_SIZE).astype(BF16)
        scores.append(jnp.dot(qblk, k_t, preferred_element_type=F32) + bias[j])
    s_max = scores[0]
    for s in scores[1:]:
        s_max = jnp.maximum(s_max, s)
    m = jnp.max(s_max, axis=-1, keepdims=True)
    p = jnp.exp2(scores[0] - m)
    p_sum = p
    acc = jnp.dot(p.astype(BF16), pad_v_ref[...].astype(BF16), preferred_element_type=F32)
    for j in range(n_pages):
        p = jnp.exp2(scores[j + 1] - m)
        p_sum = p_sum + p
        v_t = vbuf[slot, j].reshape(TOK_WIDTH, PAGE_SIZE).astype(BF16)
        acc = acc + lax.dot_general(p.astype(BF16), v_t, NT_DIMS, preferred_element_type=F32)
    o = acc / jnp.sum(p_sum, axis=-1, keepdims=True)
    for t in range(n_q):
        o_t = jnp.where(own_head, o[t * FOX_HEADS:(t + 1) * FOX_HEADS, :], 0.0)
        o_ref[0, t:t + 1, :] = jnp.sum(o_t, axis=0, keepdims=True)


def _fox_paged(page_table, q, k_new, v_new, lfn, cache_logf_t, cache_k_t, cache_v_t):
    b, n_q, _ = q.shape
    n_pages = page_table.shape[0] // b
    cur = lambda bi, pt: (bi, 0, 0)
    page_shape = cache_k_t.shape[1:]
    grid_spec = pltpu.PrefetchScalarGridSpec(
        num_scalar_prefetch=1,
        grid=(b,),
        in_specs=[pl.BlockSpec((1, n_q, TOK_WIDTH), cur),
                  pl.BlockSpec((1, n_q, TOK_WIDTH), cur),
                  pl.BlockSpec((1, n_q, TOK_WIDTH), cur),
                  pl.BlockSpec((1, FOX_HEADS, LANES), cur),
                  pl.BlockSpec(cache_logf_t.shape, lambda bi, pt: (0, 0, 0), pipeline_mode=pl.Buffered(1)),
                  pl.BlockSpec(memory_space=pl.ANY),
                  pl.BlockSpec(memory_space=pl.ANY)],
        out_specs=pl.BlockSpec((1, n_q, TOK_WIDTH), cur),
        scratch_shapes=[pltpu.VMEM((PAGE_SLOTS, n_pages) + page_shape, F32),
                        pltpu.VMEM((PAGE_SLOTS, n_pages) + page_shape, F32),
                        pltpu.SemaphoreType.DMA((2, PAGE_SLOTS)),
                        pltpu.VMEM((PAGE_SIZE, TOK_WIDTH), F32),
                        pltpu.VMEM((PAGE_SIZE, TOK_WIDTH), F32)])
    return pl.pallas_call(
        functools.partial(_fox_paged_kernel, n_pages=n_pages, n_batch=b),
        out_shape=jax.ShapeDtypeStruct((b, n_q, TOK_WIDTH), F32),
        grid_spec=grid_spec,
        compiler_params=pltpu.CompilerParams(dimension_semantics=("arbitrary",),
                                             vmem_limit_bytes=VMEM_LIMIT),
        name="fox_paged",
    )(page_table, q, k_new, v_new, lfn, cache_logf_t, cache_k_t, cache_v_t)


def _decoder(x, pool_prev, n_prev, mem_k, mem_v, past, wts, *, tm, tm_proj, pool_bb, pool_chunk,
             mem_bb, mem_ts, act):
    b, s_len, _ = x.shape
    t = b * s_len
    h = x.reshape(t, D_MODEL)

    fuse_mem = s_len % tm_proj == 0

    def project_and_attend(layer, fox):
        mem = (mem_k, mem_v, layer, s_len) if fuse_mem else None
        tok_part, mem_part = _proj_in(h, wts["g_mix"][layer], wts["w_in"][layer], wts["g_fox_q_t"],
                                      wts["g_mem_q_t"][layer], mem, fox=fox, tm=tm_proj, act=act)
        if not fuse_mem:
            mem_part = _mem_attn(mem_part.reshape(b, s_len, MEM_WIDTH), mem_k, mem_v, layer,
                                 bb=mem_bb, ts=mem_ts)
        return tok_part, mem_part.reshape(t, MEM_WIDTH)

    z_tok, mo = project_and_attend(0, fox=False)
    z_tok = z_tok.reshape(b, s_len, TOK_WIDTH)
    tok = _pool_mix(z_tok, pool_prev, wts["w_pool"], wts["pool_scale"],
                    n_prev=n_prev, bb=pool_bb, chunk=pool_chunk, act=act)
    h = _out_ffn(h, tok.reshape(t, TOK_WIDTH), mo, wts["w_out"],
                 wts["g_ffn"][0], wts["w_gu"], wts["w_down"], 0, tm=tm)

    if past is None:
        k_new, v_new, k_bf, v_bf, logf_t, logf_new = _proj_kv_paged(
            h, wts["g_kv"], wts["w_kv"], wts["g_fox_k_t"].reshape(TOK_WIDTH, 1), wts["w_fg_t"],
            wts["b_fg"], tm=tm_proj)
    else:
        k_new, v_new, logf_t = _proj_kv(h, wts["g_kv"], wts["w_kv"], wts["g_fox_k_t"], wts["w_fg_t"],
                                        wts["b_fg"], tm=tm_proj)
        logf_new = logf_t.T

    q, mo = project_and_attend(1, fox=True)
    if past is None:
        f_row = _cumsum_rows(logf_t, seg=s_len)
        n_pairs = FOX_HEADS // 2
        f_row = f_row.reshape(n_pairs, 2, t)
        tok = _fox_prompt(q.reshape(b, s_len, TOK_WIDTH), k_bf,
                          v_bf.reshape(b, s_len, TOK_WIDTH), f_row, blk=512)
    else:
        page_table, cache_k_t, cache_v_t, cache_logf_t = past
        lfn = logf_t.reshape(FOX_HEADS, b, s_len).transpose(1, 0, 2)
        lfn = jnp.pad(lfn, ((0, 0), (0, 0), (0, LANES - s_len)))
        per_row = lambda a: a.reshape(b, s_len, TOK_WIDTH)
        tok = _fox_paged(page_table, per_row(q), per_row(k_new), per_row(v_new), lfn,
                         cache_logf_t, cache_k_t, cache_v_t)
    h = _out_ffn(h, tok.reshape(t, TOK_WIDTH), mo, wts["w_out"],
                 wts["g_ffn"][1], wts["w_gu"], wts["w_down"], 1, tm=tm)
    return h.reshape(b, s_len, D_MODEL), z_tok, k_new, v_new, logf_new


def kernel(x_prompt, x_sample, cache_mem_k, cache_mem_v, state_pool, cache_k, cache_v, cache_logf,
           page_table, mem_prompt, g_mix, w_in, w_out, g_ffn, w_gu, w_down, g_mem, w_mem_kv,
           g_mem_q, g_mem_k, w_pool, pool_scale, g_kv, w_kv, g_fox_k, w_fg, b_fg, g_fox_q):
    depth = w_in.shape[0]
    b, s_len, _ = x_prompt.shape
    db, ds, _ = x_sample.shape
    wts = {
        "g_mix": g_mix.reshape(depth, 1, D_MODEL),
        "w_in": w_in.astype(BF16),
        "w_out": w_out.astype(BF16),
        "g_ffn": g_ffn.reshape(depth, 1, D_MODEL),
        "w_gu": w_gu.astype(BF16),
        "w_down": w_down.astype(BF16),
        "g_mem_q_t": jnp.tile(g_mem_q, (1, MEM_HEADS)).reshape(depth, 1, MEM_WIDTH),
        "w_pool": w_pool[0].astype(BF16),
        "pool_scale": pool_scale[0].reshape(1, TOK_WIDTH),
        "g_kv": g_kv.reshape(1, D_MODEL),
        "w_kv": w_kv.astype(BF16),
        "g_fox_k_t": jnp.tile(g_fox_k, FOX_HEADS).reshape(1, TOK_WIDTH),
        "w_fg_t": w_fg.T.astype(BF16),
        "b_fg": b_fg.reshape(FOX_HEADS, 1),
        "g_fox_q_t": jnp.tile(g_fox_q[0], FOX_HEADS).reshape(1, TOK_WIDTH),
    }

    mem_k_p, mem_v_p = _proj_memkv(mem_prompt.reshape(b * N_MEM, D_MODEL),
                                   g_mem.reshape(depth, 1, D_MODEL), w_mem_kv.astype(BF16),
                                   jnp.tile(g_mem_k, (1, MEM_HEADS)).reshape(depth, 1, MEM_WIDTH), tm=512)
    mem_rows = (N_MEM * MEM_HEADS, MEM_HEAD_DIM)
    mem_k_p = mem_k_p.reshape((depth, b) + mem_rows)
    mem_v_p = mem_v_p.reshape((depth, b) + mem_rows)
    y_p, ztok_p, k_p, v_p, logf_p = _decoder(
        x_prompt, jnp.zeros((b, POOL_HIST, TOK_WIDTH), F32), 0, mem_k_p, mem_v_p, None, wts,
        tm=512, tm_proj=1024, pool_bb=1, pool_chunk=256, mem_bb=1, mem_ts=512, act=BF16)
    n_pp = s_len // PAGE_SIZE
    head_shape = (FOX_HEADS, FOX_HEAD_DIM)

    past = (page_table.T.reshape(-1),
            cache_k.transpose(0, 2, 3, 1), cache_v.transpose(0, 2, 3, 1),
            cache_logf.transpose(0, 2, 1))
    prev = jnp.pad(state_pool[0], ((0, 0), (POOL_HIST - POOL_STATE, 0), (0, 0)))
    y_s, ztok_s, k_s, v_s, logf_s = _decoder(
        x_sample, prev, POOL_STATE, cache_mem_k.reshape((depth, db) + mem_rows),
        cache_mem_v.reshape((depth, db) + mem_rows), past, wts,
        tm=512, tm_proj=512, pool_bb=32, pool_chunk=ds, mem_bb=8, mem_ts=ds, act=F32)

    pool_state_p = ztok_p[:, s_len - POOL_STATE:][None]
    pool_state_s = jnp.concatenate([state_pool[0], ztok_s], axis=1)[:, -POOL_STATE:][None]
    paged = lambda a: a.reshape((b, n_pp) + head_shape + (PAGE_SIZE,)).transpose(0, 1, 4, 2, 3)
    return (y_p, y_s, paged(k_p), paged(v_p),
            logf_p.reshape(b, n_pp, FOX_HEADS, PAGE_SIZE).transpose(0, 1, 3, 2),
            mem_k_p.reshape(depth, b, N_MEM, MEM_HEADS, MEM_HEAD_DIM),
            mem_v_p.reshape(depth, b, N_MEM, MEM_HEADS, MEM_HEAD_DIM),
            pool_state_p,
            k_s.reshape((db, ds) + head_shape),
            v_s.reshape((db, ds) + head_shape),
            logf_s.reshape(db, ds, FOX_HEADS),
            pool_state_s)
```

```python
import functools

import jax
import jax.numpy as jnp
from jax import lax
from jax.experimental import pallas as pl
from jax.experimental.pallas import tpu as pltpu

D_MODEL = 1024
TOK_WIDTH = 512
MEM_WIDTH = 512
POOL_WINDOWS = (2, 4, 8, 16)
POOL_GROUP = 128
POOL_STATE = 15
POOL_HIST = 16
FOX_HEADS = 8
FOX_HEAD_DIM = 64
MEM_HEADS = 4
MEM_HEAD_DIM = 128
N_MEM = 256
D_FF = 2816
PAGE_SIZE = 128
EPS = 1e-6
LOG2E = 1.4426950408889634
LANES = 128
SUBLANES = 8
BF16_ROWS = 16
VMEM_LIMIT = 56 * 1024 * 1024
FOX_CHUNK_ROWS = 16
PAGE_SLOTS = 4

F32 = jnp.float32
BF16 = jnp.bfloat16
NT_DIMS = (((1,), (1,)), ((), ()))


def _rmsnorm(x, g):
    return x * lax.rsqrt(jnp.mean(x * x, axis=-1, keepdims=True) + EPS) * g


def _headnorm128(z, g):
    outs = []
    for h in range(z.shape[-1] // LANES):
        zh = z[:, h * LANES:(h + 1) * LANES]
        r = lax.rsqrt(jnp.mean(zh * zh, axis=-1, keepdims=True) + EPS)
        outs.append(zh * r * g[:, h * LANES:(h + 1) * LANES])
    return jnp.concatenate(outs, axis=-1)


def _headnorm64(z, g):
    outs = []
    for p in range(z.shape[-1] // LANES):
        zp = z[:, p * LANES:(p + 1) * LANES]
        sq = zp * zp
        lane = lax.broadcasted_iota(jnp.int32, zp.shape, 1)
        lo = lane < FOX_HEAD_DIM
        s_lo = jnp.sum(jnp.where(lo, sq, 0.0), axis=-1, keepdims=True)
        s_hi = jnp.sum(jnp.where(lo, 0.0, sq), axis=-1, keepdims=True)
        r = jnp.where(lo, lax.rsqrt(s_lo / FOX_HEAD_DIM + EPS), lax.rsqrt(s_hi / FOX_HEAD_DIM + EPS))
        outs.append(zp * r * g[:, p * LANES:(p + 1) * LANES])
    return jnp.concatenate(outs, axis=-1)


def _mem_attend(q, k_ref, v_ref):
    outs = []
    for h in range(MEM_HEADS):
        qh = q[:, h * MEM_HEAD_DIM:(h + 1) * MEM_HEAD_DIM].astype(BF16)
        k = k_ref[pl.ds(h, N_MEM, stride=MEM_HEADS), :].astype(BF16)
        v = v_ref[pl.ds(h, N_MEM, stride=MEM_HEADS), :].astype(BF16)
        s = lax.dot_general(qh, k, NT_DIMS, preferred_element_type=F32) * (MEM_HEAD_DIM ** -0.5 * LOG2E)
        p = jnp.exp2(s - jnp.max(s, axis=-1, keepdims=True))
        o = jnp.dot(p.astype(BF16), v, preferred_element_type=F32)
        outs.append(o / jnp.sum(p, axis=-1, keepdims=True))
    return jnp.concatenate(outs, axis=-1)


def _proj_in_kernel(x_ref, g_ref, w_ref, gtok_ref, gmem_ref, *refs, fox, fuse_mem):
    xn = _rmsnorm(x_ref[...], g_ref[...]).astype(BF16)
    z = jnp.dot(xn, w_ref[...], preferred_element_type=F32)
    z_tok = z[:, :TOK_WIDTH]
    qm = _headnorm128(z[:, TOK_WIDTH:], gmem_ref[...])
    if fuse_mem:
        mk_ref, mv_ref, tok_ref, mem_ref = refs
        mem_ref[...] = _mem_attend(qm, mk_ref, mv_ref).astype(mem_ref.dtype)
    else:
        tok_ref, mem_ref = refs
        mem_ref[...] = qm.astype(mem_ref.dtype)
    if fox:
        tok_ref[...] = (_headnorm64(z_tok, gtok_ref[...]) * (FOX_HEAD_DIM ** -0.5 * LOG2E)).astype(tok_ref.dtype)
    else:
        tok_ref[...] = z_tok


def _proj_in(x, g, w, g_tok, g_mem, mem=None, *, fox, tm, act):
    t = x.shape[0]
    row = lambda i: (i, 0)
    fixed = lambda i: (0, 0)
    operands = [x, g, w, g_tok, g_mem]
    in_specs = [pl.BlockSpec((tm, D_MODEL), row),
                pl.BlockSpec((1, D_MODEL), fixed),
                pl.BlockSpec((D_MODEL, D_MODEL), fixed),
                pl.BlockSpec((1, TOK_WIDTH), fixed),
                pl.BlockSpec((1, MEM_WIDTH), fixed)]
    if mem is not None:
        mk, mv, layer, rows_per_batch = mem
        tiles = rows_per_batch // tm
        mem_block = pl.BlockSpec((None, None, N_MEM * MEM_HEADS, MEM_HEAD_DIM),
                                 lambda i: (layer, i // tiles, 0, 0))
        operands += [mk, mv]
        in_specs += [mem_block, mem_block]
    return pl.pallas_call(
        functools.partial(_proj_in_kernel, fox=fox, fuse_mem=mem is not None),
        out_shape=(jax.ShapeDtypeStruct((t, TOK_WIDTH), act if fox else F32),
                   jax.ShapeDtypeStruct((t, MEM_WIDTH), act)),
        grid=(t // tm,),
        in_specs=in_specs,
        out_specs=(pl.BlockSpec((tm, TOK_WIDTH), row), pl.BlockSpec((tm, MEM_WIDTH), row)),
        compiler_params=pltpu.CompilerParams(dimension_semantics=("arbitrary",),
                                             vmem_limit_bytes=VMEM_LIMIT),
        name="proj_in_fox" if fox else "proj_in_pool",
    )(*operands)


def _log_sigmoid(x):
    return jnp.minimum(x, 0.0) - jnp.log(1.0 + jnp.exp(-jnp.abs(x)))


def _proj_kv_kernel(x_ref, g_ref, w_ref, gk_ref, wfg_ref, bfg_ref, k_ref, v_ref, logf_ref):
    xn = _rmsnorm(x_ref[...], g_ref[...]).astype(BF16)
    z = jnp.dot(xn, w_ref[...], preferred_element_type=F32)
    k_ref[...] = _headnorm64(z[:, :TOK_WIDTH], gk_ref[...])
    v_ref[...] = z[:, TOK_WIDTH:]
    gate = lax.dot_general(wfg_ref[...], xn, NT_DIMS, preferred_element_type=F32) + bfg_ref[...]
    logf_ref[...] = _log_sigmoid(gate)


def _proj_kv_paged_kernel(x_ref, g_ref, w_ref, gk_ref, wfg_ref, bfg_ref,
                          kp_ref, vp_ref, kb_ref, vb_ref, logf_ref, logfp_ref):
    xn = _rmsnorm(x_ref[...], g_ref[...]).astype(BF16)
    z = jnp.dot(xn, w_ref[...], preferred_element_type=F32)
    v = z[:, TOK_WIDTH:]
    vb_ref[...] = v.astype(BF16)
    gate = lax.dot_general(wfg_ref[...], xn, NT_DIMS, preferred_element_type=F32) + bfg_ref[...]
    logf = _log_sigmoid(gate)
    logf_ref[...] = logf
    zk_t = z[:, :TOK_WIDTH].T
    heads = []
    for h in range(FOX_HEADS):
        zh = zk_t[h * FOX_HEAD_DIM:(h + 1) * FOX_HEAD_DIM, :]
        heads.append(zh * lax.rsqrt(jnp.mean(zh * zh, axis=0, keepdims=True) + EPS))
    k_t = jnp.concatenate(heads, axis=0) * gk_ref[...]
    v_t = v.T
    kb_ref[...] = k_t.astype(BF16)
    for pg in range(kp_ref.shape[0]):
        rows = slice(pg * PAGE_SIZE, (pg + 1) * PAGE_SIZE)
        kp_ref[pg] = k_t[:, rows]
        vp_ref[pg] = v_t[:, rows]
        logfp_ref[pg] = logf[:, rows]


def _proj_kv_paged(x, g, w, g_k, wfg_t, bfg, *, tm):
    t = x.shape[0]
    row = lambda i: (i, 0)
    fixed = lambda i: (0, 0)
    ppt = tm // PAGE_SIZE
    page_rows = lambda i: (i, 0, 0)
    return pl.pallas_call(
        _proj_kv_paged_kernel,
        out_shape=(jax.ShapeDtypeStruct((t // PAGE_SIZE, TOK_WIDTH, PAGE_SIZE), F32),
                   jax.ShapeDtypeStruct((t // PAGE_SIZE, TOK_WIDTH, PAGE_SIZE), F32),
                   jax.ShapeDtypeStruct((TOK_WIDTH, t), BF16),
                   jax.ShapeDtypeStruct((t, TOK_WIDTH), BF16),
                   jax.ShapeDtypeStruct((FOX_HEADS, t), F32),
                   jax.ShapeDtypeStruct((t // PAGE_SIZE, FOX_HEADS, PAGE_SIZE), F32)),
        grid=(t // tm,),
        in_specs=[pl.BlockSpec((tm, D_MODEL), row),
                  pl.BlockSpec((1, D_MODEL), fixed),
                  pl.BlockSpec((D_MODEL, D_MODEL), fixed),
                  pl.BlockSpec((TOK_WIDTH, 1), fixed),
                  pl.BlockSpec((FOX_HEADS, D_MODEL), fixed),
                  pl.BlockSpec((FOX_HEADS, 1), fixed)],
        out_specs=(pl.BlockSpec((ppt, TOK_WIDTH, PAGE_SIZE), page_rows),
                   pl.BlockSpec((ppt, TOK_WIDTH, PAGE_SIZE), page_rows),
                   pl.BlockSpec((TOK_WIDTH, tm), lambda i: (0, i)), pl.BlockSpec((tm, TOK_WIDTH), row),
                   pl.BlockSpec((FOX_HEADS, tm), lambda i: (0, i)),
                   pl.BlockSpec((ppt, FOX_HEADS, PAGE_SIZE), page_rows)),
        compiler_params=pltpu.CompilerParams(dimension_semantics=("arbitrary",),
                                             vmem_limit_bytes=VMEM_LIMIT),
        name="proj_kv_paged",
    )(x, g, w, g_k, wfg_t, bfg)


def _proj_kv(x, g, w, g_k, wfg_t, bfg, *, tm):
    t = x.shape[0]
    row = lambda i: (i, 0)
    fixed = lambda i: (0, 0)
    return pl.pallas_call(
        _proj_kv_kernel,
        out_shape=(jax.ShapeDtypeStruct((t, TOK_WIDTH), F32),
                   jax.ShapeDtypeStruct((t, TOK_WIDTH), F32),
                   jax.ShapeDtypeStruct((FOX_HEADS, t), F32)),
        grid=(t // tm,),
        in_specs=[pl.BlockSpec((tm, D_MODEL), row),
                  pl.BlockSpec((1, D_MODEL), fixed),
                  pl.BlockSpec((D_MODEL, D_MODEL), fixed),
                  pl.BlockSpec((1, TOK_WIDTH), fixed),
                  pl.BlockSpec((FOX_HEADS, D_MODEL), fixed),
                  pl.BlockSpec((FOX_HEADS, 1), fixed)],
        out_specs=(pl.BlockSpec((tm, TOK_WIDTH), row), pl.BlockSpec((tm, TOK_WIDTH), row),
                   pl.BlockSpec((FOX_HEADS, tm), lambda i: (0, i))),
        compiler_params=pltpu.CompilerParams(dimension_semantics=("arbitrary",),
                                             vmem_limit_bytes=VMEM_LIMIT),
        name="proj_kv",
    )(x, g, w, g_k, wfg_t, bfg)


def _proj_memkv_kernel(x_ref, g_ref, w_ref, gk_ref, k_ref, v_ref):
    xn = _rmsnorm(x_ref[...], g_ref[0]).astype(BF16)
    z = jnp.dot(xn, w_ref[0], preferred_element_type=F32)
    k = _headnorm128(z[:, :MEM_WIDTH], gk_ref[0])
    tm = x_ref.shape[0]
    for h in range(MEM_HEADS):
        rows = pl.ds(h, tm, stride=MEM_HEADS)
        k_ref[0, rows, :] = k[:, h * MEM_HEAD_DIM:(h + 1) * MEM_HEAD_DIM]
        v_ref[0, rows, :] = z[:, MEM_WIDTH + h * MEM_HEAD_DIM:MEM_WIDTH + (h + 1) * MEM_HEAD_DIM]


def _proj_memkv(x, g, w, g_k, *, tm):
    t = x.shape[0]
    n_layers = w.shape[0]
    out_block = pl.BlockSpec((1, tm * MEM_HEADS, MEM_HEAD_DIM), lambda l, i: (l, i, 0))
    return pl.pallas_call(
        _proj_memkv_kernel,
        out_shape=(jax.ShapeDtypeStruct((n_layers, t * MEM_HEADS, MEM_HEAD_DIM), F32),
                   jax.ShapeDtypeStruct((n_layers, t * MEM_HEADS, MEM_HEAD_DIM), F32)),
        grid=(n_layers, t // tm),
        in_specs=[pl.BlockSpec((tm, D_MODEL), lambda l, i: (i, 0)),
                  pl.BlockSpec((1, 1, D_MODEL), lambda l, i: (l, 0, 0)),
                  pl.BlockSpec((1, D_MODEL, D_MODEL), lambda l, i: (l, 0, 0)),
                  pl.BlockSpec((1, 1, MEM_WIDTH), lambda l, i: (l, 0, 0))],
        out_specs=(out_block, out_block),
        compiler_params=pltpu.CompilerParams(dimension_semantics=("arbitrary", "arbitrary"),
                                             vmem_limit_bytes=VMEM_LIMIT),
        name="proj_memkv",
    )(x, g, w, g_k)


def _pool_kernel(u_ref, prev_ref, w_ref, scale_ref, o_ref, ext_ref, *, n_prev, chunk):
    bb, s_len, _ = u_ref.shape
    ext_ref[:, 0:POOL_HIST, :] = prev_ref[...]
    ext_ref[:, POOL_HIST:, :] = u_ref[...]
    for c in range(s_len // chunk):
        r0 = POOL_HIST + c * chunk
        pos = c * chunk + lax.broadcasted_iota(jnp.int32, (1, chunk, 1), 1)
        for gi, win in enumerate(POOL_WINDOWS):
            lanes = slice(gi * POOL_GROUP, (gi + 1) * POOL_GROUP)
            u_new = ext_ref[:, r0:r0 + chunk, lanes]
            if win == 2 * SUBLANES:
                a8 = ext_ref[:, r0 - SUBLANES:r0 + chunk, lanes]
                for k in range(1, SUBLANES):
                    a8 = a8 + ext_ref[:, r0 - SUBLANES - k:r0 + chunk - k, lanes]
                acc = a8[:, SUBLANES:, :] + a8[:, :chunk, :]
            else:
                acc = u_new
                for k in range(1, win):
                    acc = acc + ext_ref[:, r0 - k:r0 - k + chunk, lanes]
            count = jnp.minimum(win, n_prev + pos + 1).astype(F32)
            y = acc / count - u_new
            z = jnp.dot(y.reshape(bb * chunk, POOL_GROUP).astype(BF16), w_ref[gi],
                        preferred_element_type=F32)
            z = z.reshape(bb, chunk, POOL_GROUP) * scale_ref[:, lanes]
            o_ref[:, c * chunk:(c + 1) * chunk, lanes] = z.astype(o_ref.dtype)


def _pool_mix(u, prev, w_pool, scale, *, n_prev, bb, chunk, act):
    b, s_len, _ = u.shape
    return pl.pallas_call(
        functools.partial(_pool_kernel, n_prev=n_prev, chunk=chunk),
        out_shape=jax.ShapeDtypeStruct((b, s_len, TOK_WIDTH), act),
        grid=(b // bb,),
        in_specs=[pl.BlockSpec((bb, s_len, TOK_WIDTH), lambda i: (i, 0, 0)),
                  pl.BlockSpec((bb, POOL_HIST, TOK_WIDTH), lambda i: (i, 0, 0)),
                  pl.BlockSpec((len(POOL_WINDOWS), POOL_GROUP, POOL_GROUP), lambda i: (0, 0, 0)),
                  pl.BlockSpec((1, TOK_WIDTH), lambda i: (0, 0))],
        out_specs=pl.BlockSpec((bb, s_len, TOK_WIDTH), lambda i: (i, 0, 0)),
        scratch_shapes=[pltpu.VMEM((bb, POOL_HIST + s_len, TOK_WIDTH), F32)],
        compiler_params=pltpu.CompilerParams(dimension_semantics=("arbitrary",),
                                             vmem_limit_bytes=VMEM_LIMIT),
        name="pool_mix",
    )(u, prev, w_pool, scale)


def _mem_attn_kernel(q_ref, k_ref, v_ref, o_ref):
    for h in range(MEM_HEADS):
        lanes = slice(h * MEM_HEAD_DIM, (h + 1) * MEM_HEAD_DIM)
        q = q_ref[:, :, lanes].astype(BF16)
        k = k_ref[:, pl.ds(h, N_MEM, stride=MEM_HEADS), :].astype(BF16)
        v = v_ref[:, pl.ds(h, N_MEM, stride=MEM_HEADS), :].astype(BF16)
        s = jnp.einsum("bqd,bkd->bqk", q, k, preferred_element_type=F32) * (MEM_HEAD_DIM ** -0.5 * LOG2E)
        m = jnp.max(s, axis=-1, keepdims=True)
        p = jnp.exp2(s - m)
        l = jnp.sum(p, axis=-1, keepdims=True)
        o = jnp.einsum("bqk,bkd->bqd", p.astype(BF16), v, preferred_element_type=F32)
        o_ref[:, :, lanes] = (o / l).astype(o_ref.dtype)


def _mem_attn(q, mk, mv, layer, *, bb, ts):
    b, s_len, _ = q.shape
    mem_block = pl.BlockSpec((None, bb, N_MEM * MEM_HEADS, MEM_HEAD_DIM),
                             lambda i, j: (layer, i, 0, 0))
    return pl.pallas_call(
        _mem_attn_kernel,
        out_shape=jax.ShapeDtypeStruct((b, s_len, MEM_WIDTH), q.dtype),
        grid=(b // bb, s_len // ts),
        in_specs=[pl.BlockSpec((bb, ts, MEM_WIDTH), lambda i, j: (i, j, 0)), mem_block, mem_block],
        out_specs=pl.BlockSpec((bb, ts, MEM_WIDTH), lambda i, j: (i, j, 0)),
        compiler_params=pltpu.CompilerParams(dimension_semantics=("arbitrary", "arbitrary"),
                                             vmem_limit_bytes=VMEM_LIMIT),
        name="mem_attn",
    )(q, mk, mv)


def _out_ffn_kernel(h_ref, tok_ref, mo_ref, wout_ref, g_ref, wg_ref, wu_ref, wd_ref, o_ref):
    h1 = (h_ref[...]
          + jnp.dot(tok_ref[...].astype(BF16), wout_ref[:TOK_WIDTH, :], preferred_element_type=F32)
          + jnp.dot(mo_ref[...].astype(BF16), wout_ref[TOK_WIDTH:, :], preferred_element_type=F32))
    x = _rmsnorm(h1, g_ref[...]).astype(BF16)
    gate = jnp.dot(x, wg_ref[...], preferred_element_type=F32)
    up = jnp.dot(x, wu_ref[...], preferred_element_type=F32)
    act = (gate * (1.0 / (1.0 + jnp.exp(-gate))) * up).astype(BF16)
    o_ref[...] = h1 + jnp.dot(act, wd_ref[...], preferred_element_type=F32)


def _out_ffn(h, tok, mo, w_out, g, w_gu, w_down, layer, *, tm):
    t = h.shape[0]
    row = lambda i: (i, 0)
    once = pl.Buffered(1)
    return pl.pallas_call(
        _out_ffn_kernel,
        out_shape=jax.ShapeDtypeStruct((t, D_MODEL), F32),
        grid=(t // tm,),
        in_specs=[pl.BlockSpec((tm, D_MODEL), row),
                  pl.BlockSpec((tm, TOK_WIDTH), row),
                  pl.BlockSpec((tm, MEM_WIDTH), row),
                  pl.BlockSpec((None, D_MODEL, D_MODEL), lambda i: (layer, 0, 0), pipeline_mode=once),
                  pl.BlockSpec((1, D_MODEL), lambda i: (0, 0)),
                  pl.BlockSpec((None, D_MODEL, D_FF), lambda i: (layer, 0, 0), pipeline_mode=once),
                  pl.BlockSpec((None, D_MODEL, D_FF), lambda i: (layer, 0, 1), pipeline_mode=once),
                  pl.BlockSpec((None, D_FF, D_MODEL), lambda i: (layer, 0, 0), pipeline_mode=once)],
        out_specs=pl.BlockSpec((tm, D_MODEL), row),
        compiler_params=pltpu.CompilerParams(dimension_semantics=("arbitrary",),
                                             vmem_limit_bytes=VMEM_LIMIT),
        name="out_ffn",
    )(h, tok, mo, w_out, g, w_gu, w_gu, w_down)


def _lane_cumsum(x):
    n = x.shape[-1]
    lane = lax.broadcasted_iota(jnp.int32, x.shape, 1)
    k = 1
    while k < n:
        x = x + jnp.where(lane >= k, pltpu.roll(x, k, axis=1), 0.0)
        k *= 2
    return x


def _cumsum_kernel(x_ref, o_ref):
    o_ref[...] = _lane_cumsum(x_ref[...])


def _cumsum_rows(x, *, seg):
    rows, t = x.shape
    return pl.pallas_call(
        _cumsum_kernel,
        out_shape=jax.ShapeDtypeStruct((rows, t), F32),
        grid=(t // seg,),
        in_specs=[pl.BlockSpec((rows, seg), lambda i: (0, i))],
        out_specs=pl.BlockSpec((rows, seg), lambda i: (0, i)),
        compiler_params=pltpu.CompilerParams(dimension_semantics=("arbitrary",)),
        name="logf_cumsum",
    )(x)


def _round_to_bf16(x):
    return x.astype(BF16).astype(F32)


def _fox_prompt_kernel(q_ref, k_ref, v_ref, frow_ref, o_ref, m_ref, l_ref, acc_ref, alpha_ref,
                       s_ref, p_all_ref, *, blk, n_q_blocks):
    qi = pl.program_id(2)
    q = q_ref[0]
    lane = lax.broadcasted_iota(jnp.int32, q.shape, 1)
    sub = FOX_CHUNK_ROWS
    n_rep = blk // LANES
    bias_lane = [FOX_HEAD_DIM, 0]
    q_f32 = q.astype(F32)
    q_head = []
    for j in range(2):
        own = (lane < FOX_HEAD_DIM) if j == 0 else (lane >= FOX_HEAD_DIM)
        ones_from = jnp.where(lane >= bias_lane[j], 1.0, 0.0)
        bias_ones = jnp.where(lane < bias_lane[j] + 3, ones_from, 0.0)
        q_head.append(jnp.where(own, q_f32, bias_ones).astype(BF16))
    m_ref[...] = jnp.full(m_ref.shape, -jnp.inf, F32)
    l_ref[...] = jnp.zeros_like(l_ref)
    acc_ref[...] = jnp.zeros_like(acc_ref)

    def scores(qc, ki, s_buf):
        keys = slice(ki * blk, (ki + 1) * blk)
        kt = k_ref[:, keys]
        tile_row = lax.broadcasted_iota(jnp.int32, (BF16_ROWS, blk), 0)
        for j in range(2):
            f_end = frow_ref[0, j:j + 1, (qc + 1) * blk - 1:(qc + 1) * blk]
            bias = (f_end - frow_ref[0, j:j + 1, keys]) * LOG2E
            hi = _round_to_bf16(bias)
            mid = _round_to_bf16(bias - hi)
            lo = _round_to_bf16(bias - hi - mid)
            bias_rows = jnp.where(tile_row == 0, hi, jnp.where(tile_row == 1, mid,
                                                               jnp.where(tile_row == 2, lo, 0.0)))
            b0 = bias_lane[j]
            pieces = [kt[:b0], bias_rows.astype(BF16), kt[b0 + BF16_ROWS:]]
            kt_j = jnp.concatenate([x for x in pieces if x.shape[0]], axis=0)
            s_buf[j] = jnp.dot(q_head[j], kt_j, preferred_element_type=F32)

    def update(ki, s_buf, masked):
        p_ref = p_all_ref.at[ki]
        vt = v_ref[0, ki * blk:(ki + 1) * blk, :]
        for j in range(2):
            def chunk(r):
                n_use = -(-(r + 1) * sub // LANES) if masked else n_rep
                rows = slice(r * sub, (r + 1) * sub)
                s = s_buf[j, rows, 0:n_use * LANES]
                if masked:
                    row = lax.broadcasted_iota(jnp.int32, s.shape, 0) + r * sub
                    col = lax.broadcasted_iota(jnp.int32, s.shape, 1)
                    s = jnp.where(col <= row, s, -jnp.inf)
                return rows, n_use, s

            for r in range(blk // sub):
                rows, n_use, s = chunk(r)
                m_prev = m_ref[j, rows]
                m_new = jnp.maximum(m_prev, jnp.max(s, axis=-1, keepdims=True))
                alpha_ref[j, rows] = jnp.exp2(m_prev - m_new)
                m_ref[j, rows] = m_new
            for r in range(blk // sub):
                rows, n_use, s = chunk(r)
                p = jnp.exp2(s - jnp.concatenate([m_ref[j, rows]] * n_use, axis=1))
                p_lanes = p[:, :LANES]
                for c in range(1, n_use):
                    p_lanes = p_lanes + p[:, c * LANES:(c + 1) * LANES]
                l_ref[j, rows] = alpha_ref[j, rows] * l_ref[j, rows] + p_lanes
                p_ref[j, rows, 0:n_use * LANES] = p.astype(BF16)
                if n_use < n_rep:
                    p_ref[j, rows, n_use * LANES:] = jnp.zeros((sub, blk - n_use * LANES), BF16)
            acc_ref[j] = alpha_ref[j] * acc_ref[j] + jnp.dot(p_ref[j], vt, preferred_element_type=F32)

    for qc in range(n_q_blocks):
        @pl.when(qi == qc)
        def _(qc=qc):
            scores(qc, 0, s_ref.at[0])
            for ki in range(qc):
                scores(qc, ki + 1, s_ref.at[ki + 1])
                update(ki, s_ref.at[ki], False)
            update(qc, s_ref.at[qc], True)

    l0 = jnp.sum(l_ref[0], axis=-1, keepdims=True)
    l1 = jnp.sum(l_ref[1], axis=-1, keepdims=True)
    o = jnp.where(lane < FOX_HEAD_DIM, acc_ref[0] / l0, acc_ref[1] / l1)
    o_ref[0] = o.astype(o_ref.dtype)


def _fox_prompt(q, k, v, frow, *, blk):
    b, s_len, _ = q.shape
    n_pairs = TOK_WIDTH // LANES
    nq = s_len // blk
    return pl.pallas_call(
        functools.partial(_fox_prompt_kernel, blk=blk, n_q_blocks=nq),
        out_shape=jax.ShapeDtypeStruct((b, s_len, TOK_WIDTH), BF16),
        grid=(b, n_pairs, nq),
        in_specs=[pl.BlockSpec((1, blk, LANES), lambda bi, hp, qi: (bi, qi, hp)),
                  pl.BlockSpec((LANES, s_len), lambda bi, hp, qi: (hp, bi)),
                  pl.BlockSpec((1, s_len, LANES), lambda bi, hp, qi: (bi, 0, hp)),
                  pl.BlockSpec((1, 2, s_len), lambda bi, hp, qi: (hp, 0, bi))],
        out_specs=pl.BlockSpec((1, blk, LANES), lambda bi, hp, qi: (bi, qi, hp)),
        scratch_shapes=[pltpu.VMEM((2, blk, LANES), F32), pltpu.VMEM((2, blk, LANES), F32),
                        pltpu.VMEM((2, blk, LANES), F32), pltpu.VMEM((2, blk, LANES), F32),
                        pltpu.VMEM((nq, 2, blk, blk), F32),
                        pltpu.VMEM((nq, 2, blk, blk), BF16)],
        compiler_params=pltpu.CompilerParams(
            dimension_semantics=("arbitrary", "arbitrary", "arbitrary"),
            vmem_limit_bytes=VMEM_LIMIT),
        name="fox_prompt",
    )(q, k, v, frow)


def _lane_suffix_sum(x):
    n = x.shape[-1]
    lane = lax.broadcasted_iota(jnp.int32, x.shape, 1)
    k = 1
    while k < n:
        x = x + jnp.where(lane + k < n, pltpu.roll(x, n - k, axis=1), 0.0)
        k *= 2
    return x


def _fox_paged_kernel(pt_ref, q_ref, kn_ref, vn_ref, lfn_ref, logf_ref, k_hbm, v_hbm, o_ref,
                      kbuf, vbuf, sem, pad_k_ref, pad_v_ref, *, n_pages, n_batch):
    b = pl.program_id(0)
    n_steps = pl.num_programs(0)
    slot = lax.rem(b, PAGE_SLOTS)
    n_q = q_ref.shape[1]
    n_rows = n_q * FOX_HEADS

    def page_copies(batch, slot_):
        copies = []
        for j in range(n_pages):
            page = pt_ref[j * n_batch + batch]
            copies.append(pltpu.make_async_copy(k_hbm.at[page], kbuf.at[slot_, j], sem.at[0, slot_]))
            copies.append(pltpu.make_async_copy(v_hbm.at[page], vbuf.at[slot_, j], sem.at[1, slot_]))
        return copies

    ahead = PAGE_SLOTS - 1
    for r in range(ahead):
        @pl.when(jnp.logical_and(b == 0, r < n_steps))
        def _(r=r):
            for c in page_copies(r, r):
                c.start()

    @pl.when(b + ahead < n_steps)
    def _():
        for c in page_copies(b + ahead, lax.rem(b + ahead, PAGE_SLOTS)):
            c.start()

    head = lax.broadcasted_iota(jnp.int32, (FOX_HEADS, TOK_WIDTH), 0)
    lane_head = jnp.right_shift(lax.broadcasted_iota(jnp.int32, (FOX_HEADS, TOK_WIDTH), 1), 6)
    own_head = head == lane_head
    tile_q = lambda x: jnp.concatenate([x] * n_q, axis=0)

    q = q_ref[0]
    qblk = jnp.concatenate(
        [jnp.where(own_head, jnp.broadcast_to(q[t:t + 1, :], (FOX_HEADS, TOK_WIDTH)), 0.0)
         for t in range(n_q)], axis=0).astype(BF16)

    pad_k_ref[...] = jnp.zeros_like(pad_k_ref)
    pad_v_ref[...] = jnp.zeros_like(pad_v_ref)
    pad_k_ref[0:n_q, :] = kn_ref[0]
    pad_v_ref[0:n_q, :] = vn_ref[0]
    c_new = _lane_cumsum(lfn_ref[0])
    cq = jnp.concatenate([c_new[:, t:t + 1] for t in range(n_q)], axis=0)
    s_own = lax.dot_general(qblk, pad_k_ref[...].astype(BF16), NT_DIMS, preferred_element_type=F32)
    t_of_row = jnp.right_shift(lax.broadcasted_iota(jnp.int32, (n_rows, PAGE_SIZE), 0), 3)
    key = lax.broadcasted_iota(jnp.int32, (n_rows, PAGE_SIZE), 1)
    s_own = jnp.where(key <= t_of_row, s_own + (tile_q(-c_new) + cq) * LOG2E, -jnp.inf)

    logf = [logf_ref[pt_ref[j * n_batch + b]] for j in range(n_pages)]
    incl = _lane_suffix_sum(jnp.concatenate(logf, axis=0))
    later_pages = jnp.zeros((FOX_HEADS, 1), F32)
    bias = [None] * n_pages
    for j in reversed(range(n_pages)):
        rows = slice(j * FOX_HEADS, (j + 1) * FOX_HEADS)
        bias[j] = (tile_q(later_pages + (incl[rows] - logf[j])) + cq) * LOG2E
        later_pages = later_pages + incl[rows, 0:1]

    for c in page_copies(b, slot):
        c.wait()

    scores = [s_own]
    for j in range(n_pages):
        k_t = kbuf[slot, j].reshape(TOK_WIDTH, PAGE_SIZE).astype(BF16)
        scores.append(jnp.dot(qblk, k_t, preferred_element_type=F32) + bias[j])
    s_max = scores[0]
    for s in scores[1:]:
        s_max = jnp.maximum(s_max, s)
    m = jnp.max(s_max, axis=-1, keepdims=True)
    p = jnp.exp2(scores[0] - m)
    p_sum = p
    acc = jnp.dot(p.astype(BF16), pad_v_ref[...].astype(BF16), preferred_element_type=F32)
    for j in range(n_pages):
        p = jnp.exp2(scores[j + 1] - m)
        p_sum = p_sum + p
        v_t = vbuf[slot, j].reshape(TOK_WIDTH, PAGE_SIZE).astype(BF16)
        acc = acc + lax.dot_general(p.astype(BF16), v_t, NT_DIMS, preferred_element_type=F32)
    o = acc / jnp.sum(p_sum, axis=-1, keepdims=True)
    for t in range(n_q):
        o_t = jnp.where(own_head, o[t * FOX_HEADS:(t + 1) * FOX_HEADS, :], 0.0)
        o_ref[0, t:t + 1, :] = jnp.sum(o_t, axis=0, keepdims=True)


def _fox_paged(page_table, q, k_new, v_new, lfn, cache_logf_t, cache_k_t, cache_v_t):
    b, n_q, _ = q.shape
    n_pages = page_table.shape[0] // b
    cur = lambda bi, pt: (bi, 0, 0)
    page_shape = cache_k_t.shape[1:]
    grid_spec = pltpu.PrefetchScalarGridSpec(
        num_scalar_prefetch=1,
        grid=(b,),
        in_specs=[pl.BlockSpec((1, n_q, TOK_WIDTH), cur),
                  pl.BlockSpec((1, n_q, TOK_WIDTH), cur),
                  pl.BlockSpec((1, n_q, TOK_WIDTH), cur),
                  pl.BlockSpec((1, FOX_HEADS, LANES), cur),
                  pl.BlockSpec(cache_logf_t.shape, lambda bi, pt: (0, 0, 0), pipeline_mode=pl.Buffered(1)),
                  pl.BlockSpec(memory_space=pl.ANY),
                  pl.BlockSpec(memory_space=pl.ANY)],
        out_specs=pl.BlockSpec((1, n_q, TOK_WIDTH), cur),
        scratch_shapes=[pltpu.VMEM((PAGE_SLOTS, n_pages) + page_shape, F32),
                        pltpu.VMEM((PAGE_SLOTS, n_pages) + page_shape, F32),
                        pltpu.SemaphoreType.DMA((2, PAGE_SLOTS)),
                        pltpu.VMEM((PAGE_SIZE, TOK_WIDTH), F32),
                        pltpu.VMEM((PAGE_SIZE, TOK_WIDTH), F32)])
    return pl.pallas_call(
        functools.partial(_fox_paged_kernel, n_pages=n_pages, n_batch=b),
        out_shape=jax.ShapeDtypeStruct((b, n_q, TOK_WIDTH), F32),
        grid_spec=grid_spec,
        compiler_params=pltpu.CompilerParams(dimension_semantics=("arbitrary",),
                                             vmem_limit_bytes=VMEM_LIMIT),
        name="fox_paged",
    )(page_table, q, k_new, v_new, lfn, cache_logf_t, cache_k_t, cache_v_t)


def _decoder(x, pool_prev, n_prev, mem_k, mem_v, past, wts, *, tm, tm_proj, pool_bb, pool_chunk,
             mem_bb, mem_ts, act):
    b, s_len, _ = x.shape
    t = b * s_len
    h = x.reshape(t, D_MODEL)

    fuse_mem = s_len % tm_proj == 0

    def project_and_attend(layer, fox):
        mem = (mem_k, mem_v, layer, s_len) if fuse_mem else None
        tok_part, mem_part = _proj_in(h, wts["g_mix"][layer], wts["w_in"][layer], wts["g_fox_q_t"],
                                      wts["g_mem_q_t"][layer], mem, fox=fox, tm=tm_proj, act=act)
        if not fuse_mem:
            mem_part = _mem_attn(mem_part.reshape(b, s_len, MEM_WIDTH), mem_k, mem_v, layer,
                                 bb=mem_bb, ts=mem_ts)
        return tok_part, mem_part.reshape(t, MEM_WIDTH)

    z_tok, mo = project_and_attend(0, fox=False)
    z_tok = z_tok.reshape(b, s_len, TOK_WIDTH)
    tok = _pool_mix(z_tok, pool_prev, wts["w_pool"], wts["pool_scale"],
                    n_prev=n_prev, bb=pool_bb, chunk=pool_chunk, act=act)
    h = _out_ffn(h, tok.reshape(t, TOK_WIDTH), mo, wts["w_out"],
                 wts["g_ffn"][0], wts["w_gu"], wts["w_down"], 0, tm=tm)

    if past is None:
        k_new, v_new, k_bf, v_bf, logf_t, logf_new = _proj_kv_paged(
            h, wts["g_kv"], wts["w_kv"], wts["g_fox_k_t"].reshape(TOK_WIDTH, 1), wts["w_fg_t"],
            wts["b_fg"], tm=tm_proj)
    else:
        k_new, v_new, logf_t = _proj_kv(h, wts["g_kv"], wts["w_kv"], wts["g_fox_k_t"], wts["w_fg_t"],
                                        wts["b_fg"], tm=tm_proj)
        logf_new = logf_t.T

    q, mo = project_and_attend(1, fox=True)
    if past is None:
        f_row = _cumsum_rows(logf_t, seg=s_len)
        n_pairs = FOX_HEADS // 2
        f_row = f_row.reshape(n_pairs, 2, t)
        tok = _fox_prompt(q.reshape(b, s_len, TOK_WIDTH), k_bf,
                          v_bf.reshape(b, s_len, TOK_WIDTH), f_row, blk=512)
    else:
        page_table, cache_k_t, cache_v_t, cache_logf_t = past
        lfn = logf_t.reshape(FOX_HEADS, b, s_len).transpose(1, 0, 2)
        lfn = jnp.pad(lfn, ((0, 0), (0, 0), (0, LANES - s_len)))
        per_row = lambda a: a.reshape(b, s_len, TOK_WIDTH)
        tok = _fox_paged(page_table, per_row(q), per_row(k_new), per_row(v_new), lfn,
                         cache_logf_t, cache_k_t, cache_v_t)
    h = _out_ffn(h, tok.reshape(t, TOK_WIDTH), mo, wts["w_out"],
                 wts["g_ffn"][1], wts["w_gu"], wts["w_down"], 1, tm=tm)
    return h.reshape(b, s_len, D_MODEL), z_tok, k_new, v_new, logf_new


def kernel(x_prompt, x_sample, cache_mem_k, cache_mem_v, state_pool, cache_k, cache_v, cache_logf,
           page_table, mem_prompt, g_mix, w_in, w_out, g_ffn, w_gu, w_down, g_mem, w_mem_kv,
           g_mem_q, g_mem_k, w_pool, pool_scale, g_kv, w_kv, g_fox_k, w_fg, b_fg, g_fox_q):
    depth = w_in.shape[0]
    b, s_len, _ = x_prompt.shape
    db, ds, _ = x_sample.shape
    wts = {
        "g_mix": g_mix.reshape(depth, 1, D_MODEL),
        "w_in": w_in.astype(BF16),
        "w_out": w_out.astype(BF16),
        "g_ffn": g_ffn.reshape(depth, 1, D_MODEL),
        "w_gu": w_gu.astype(BF16),
        "w_down": w_down.astype(BF16),
        "g_mem_q_t": jnp.tile(g_mem_q, (1, MEM_HEADS)).reshape(depth, 1, MEM_WIDTH),
        "w_pool": w_pool[0].astype(BF16),
        "pool_scale": pool_scale[0].reshape(1, TOK_WIDTH),
        "g_kv": g_kv.reshape(1, D_MODEL),
        "w_kv": w_kv.astype(BF16),
        "g_fox_k_t": jnp.tile(g_fox_k, FOX_HEADS).reshape(1, TOK_WIDTH),
        "w_fg_t": w_fg.T.astype(BF16),
        "b_fg": b_fg.reshape(FOX_HEADS, 1),
        "g_fox_q_t": jnp.tile(g_fox_q[0], FOX_HEADS).reshape(1, TOK_WIDTH),
    }

    mem_k_p, mem_v_p = _proj_memkv(mem_prompt.reshape(b * N_MEM, D_MODEL),
                                   g_mem.reshape(depth, 1, D_MODEL), w_mem_kv.astype(BF16),
                                   jnp.tile(g_mem_k, (1, MEM_HEADS)).reshape(depth, 1, MEM_WIDTH), tm=512)
    mem_rows = (N_MEM * MEM_HEADS, MEM_HEAD_DIM)
    mem_k_p = mem_k_p.reshape((depth, b) + mem_rows)
    mem_v_p = mem_v_p.reshape((depth, b) + mem_rows)
    y_p, ztok_p, k_p, v_p, logf_p = _decoder(
        x_prompt, jnp.zeros((b, POOL_HIST, TOK_WIDTH), F32), 0, mem_k_p, mem_v_p, None, wts,
        tm=512, tm_proj=1024, pool_bb=1, pool_chunk=256, mem_bb=1, mem_ts=512, act=BF16)
    n_pp = s_len // PAGE_SIZE
    head_shape = (FOX_HEADS, FOX_HEAD_DIM)

    past = (page_table.T.reshape(-1),
            cache_k.transpose(0, 2, 3, 1), cache_v.transpose(0, 2, 3, 1),
            cache_logf.transpose(0, 2, 1))
    prev = jnp.pad(state_pool[0], ((0, 0), (POOL_HIST - POOL_STATE, 0), (0, 0)))
    y_s, ztok_s, k_s, v_s, logf_s = _decoder(
        x_sample, prev, POOL_STATE, cache_mem_k.reshape((depth, db) + mem_rows),
        cache_mem_v.reshape((depth, db) + mem_rows), past, wts,
        tm=512, tm_proj=512, pool_bb=32, pool_chunk=ds, mem_bb=8, mem_ts=ds, act=F32)

    pool_state_p = ztok_p[:, s_len - POOL_STATE:][None]
    pool_state_s = jnp.concatenate([state_pool[0], ztok_s], axis=1)[:, -POOL_STATE:][None]
    paged = lambda a: a.reshape((b, n_pp) + head_shape + (PAGE_SIZE,)).transpose(0, 1, 4, 2, 3)
    return (y_p, y_s, paged(k_p), paged(v_p),
            logf_p.reshape(b, n_pp, FOX_HEADS, PAGE_SIZE).transpose(0, 1, 3, 2),
            mem_k_p.reshape(depth, b, N_MEM, MEM_HEADS, MEM_HEAD_DIM),
            mem_v_p.reshape(depth, b, N_MEM, MEM_HEADS, MEM_HEAD_DIM),
            pool_state_p,
            k_s.reshape((db, ds) + head_shape),
            v_s.reshape((db, ds) + head_shape),
            logf_s.reshape(db, ds, FOX_HEADS),
            pool_state_s)
```

```python
import functools

import jax
import jax.numpy as jnp
from jax import lax
from jax.experimental import pallas as pl
from jax.experimental.pallas import tpu as pltpu

D_MODEL = 1024
TOK_WIDTH = 512
MEM_WIDTH = 512
POOL_WINDOWS = (2, 4, 8, 16)
POOL_GROUP = 128
POOL_STATE = 15
POOL_HIST = 16
FOX_HEADS = 8
FOX_HEAD_DIM = 64
MEM_HEADS = 4
MEM_HEAD_DIM = 128
N_MEM = 256
D_FF = 2816
PAGE_SIZE = 128
EPS = 1e-6
LOG2E = 1.4426950408889634
LANES = 128
SUBLANES = 8
BF16_ROWS = 16
VMEM_LIMIT = 56 * 1024 * 1024
FOX_CHUNK_ROWS = 16
PAGE_SLOTS = 4

F32 = jnp.float32
BF16 = jnp.bfloat16
NT_DIMS = (((1,), (1,)), ((), ()))


def _rmsnorm(x, g):
    return x * lax.rsqrt(jnp.mean(x * x, axis=-1, keepdims=True) + EPS) * g


def _headnorm128(z, g):
    outs = []
    for h in range(z.shape[-1] // LANES):
        zh = z[:, h * LANES:(h + 1) * LANES]
        r = lax.rsqrt(jnp.mean(zh * zh, axis=-1, keepdims=True) + EPS)
        outs.append(zh * r * g[:, h * LANES:(h + 1) * LANES])
    return jnp.concatenate(outs, axis=-1)


def _headnorm64(z, g):
    outs = []
    for p in range(z.shape[-1] // LANES):
        zp = z[:, p * LANES:(p + 1) * LANES]
        sq = zp * zp
        lane = lax.broadcasted_iota(jnp.int32, zp.shape, 1)
        lo = lane < FOX_HEAD_DIM
        s_lo = jnp.sum(jnp.where(lo, sq, 0.0), axis=-1, keepdims=True)
        s_hi = jnp.sum(jnp.where(lo, 0.0, sq), axis=-1, keepdims=True)
        r = jnp.where(lo, lax.rsqrt(s_lo / FOX_HEAD_DIM + EPS), lax.rsqrt(s_hi / FOX_HEAD_DIM + EPS))
        outs.append(zp * r * g[:, p * LANES:(p + 1) * LANES])
    return jnp.concatenate(outs, axis=-1)


def _mem_attend(q, k_ref, v_ref):
    outs = []
    for h in range(MEM_HEADS):
        qh = q[:, h * MEM_HEAD_DIM:(h + 1) * MEM_HEAD_DIM].astype(BF16)
        k = k_ref[pl.ds(h, N_MEM, stride=MEM_HEADS), :].astype(BF16)
        v = v_ref[pl.ds(h, N_MEM, stride=MEM_HEADS), :].astype(BF16)
        s = lax.dot_general(qh, k, NT_DIMS, preferred_element_type=F32) * (MEM_HEAD_DIM ** -0.5 * LOG2E)
        p = jnp.exp2(s - jnp.max(s, axis=-1, keepdims=True))
        o = jnp.dot(p.astype(BF16), v, preferred_element_type=F32)
        outs.append(o / jnp.sum(p, axis=-1, keepdims=True))
    return jnp.concatenate(outs, axis=-1)


def _proj_in_kernel(x_ref, g_ref, w_ref, gtok_ref, gmem_ref, *refs, fox, fuse_mem):
    xn = _rmsnorm(x_ref[...], g_ref[...]).astype(BF16)
    z = jnp.dot(xn, w_ref[...], preferred_element_type=F32)
    z_tok = z[:, :TOK_WIDTH]
    qm = _headnorm128(z[:, TOK_WIDTH:], gmem_ref[...])
    if fuse_mem:
        mk_ref, mv_ref, tok_ref, mem_ref = refs
        mem_ref[...] = _mem_attend(qm, mk_ref, mv_ref).astype(mem_ref.dtype)
    else:
        tok_ref, mem_ref = refs
        mem_ref[...] = qm.astype(mem_ref.dtype)
    if fox:
        tok_ref[...] = (_headnorm64(z_tok, gtok_ref[...]) * (FOX_HEAD_DIM ** -0.5 * LOG2E)).astype(tok_ref.dtype)
    else:
        tok_ref[...] = z_tok


def _proj_in(x, g, w, g_tok, g_mem, mem=None, *, fox, tm, act):
    t = x.shape[0]
    row = lambda i: (i, 0)
    fixed = lambda i: (0, 0)
    operands = [x, g, w, g_tok, g_mem]
    in_specs = [pl.BlockSpec((tm, D_MODEL), row),
                pl.BlockSpec((1, D_MODEL), fixed),
                pl.BlockSpec((D_MODEL, D_MODEL), fixed),
                pl.BlockSpec((1, TOK_WIDTH), fixed),
                pl.BlockSpec((1, MEM_WIDTH), fixed)]
    if mem is not None:
        mk, mv, layer, rows_per_batch = mem
        tiles = rows_per_batch // tm
        mem_block = pl.BlockSpec((None, None, N_MEM * MEM_HEADS, MEM_HEAD_DIM),
                                 lambda i: (layer, i // tiles, 0, 0))
        operands += [mk, mv]
        in_specs += [mem_block, mem_block]
    return pl.pallas_call(
        functools.partial(_proj_in_kernel, fox=fox, fuse_mem=mem is not None),
        out_shape=(jax.ShapeDtypeStruct((t, TOK_WIDTH), act if fox else F32),
                   jax.ShapeDtypeStruct((t, MEM_WIDTH), act)),
        grid=(t // tm,),
        in_specs=in_specs,
        out_specs=(pl.BlockSpec((tm, TOK_WIDTH), row), pl.BlockSpec((tm, MEM_WIDTH), row)),
        compiler_params=pltpu.CompilerParams(dimension_semantics=("arbitrary",),
                                             vmem_limit_bytes=VMEM_LIMIT),
        name="proj_in_fox" if fox else "proj_in_pool",
    )(*operands)


def _log_sigmoid(x):
    return jnp.minimum(x, 0.0) - jnp.log(1.0 + jnp.exp(-jnp.abs(x)))


def _proj_kv_kernel(x_ref, g_ref, w_ref, gk_ref, wfg_ref, bfg_ref, k_ref, v_ref, logf_ref):
    xn = _rmsnorm(x_ref[...], g_ref[...]).astype(BF16)
    z = jnp.dot(xn, w_ref[...], preferred_element_type=F32)
    k_ref[...] = _headnorm64(z[:, :TOK_WIDTH], gk_ref[...])
    v_ref[...] = z[:, TOK_WIDTH:]
    gate = lax.dot_general(wfg_ref[...], xn, NT_DIMS, preferred_element_type=F32) + bfg_ref[...]
    logf_ref[...] = _log_sigmoid(gate)


def _proj_kv_paged_kernel(x_ref, g_ref, w_ref, gk_ref, wfg_ref, bfg_ref,
                          kp_ref, vp_ref, kb_ref, vb_ref, logf_ref, logfp_ref):
    xn = _rmsnorm(x_ref[...], g_ref[...]).astype(BF16)
    z = jnp.dot(xn, w_ref[...], preferred_element_type=F32)
    v = z[:, TOK_WIDTH:]
    vb_ref[...] = v.astype(BF16)
    gate = lax.dot_general(wfg_ref[...], xn, NT_DIMS, preferred_element_type=F32) + bfg_ref[...]
    logf = _log_sigmoid(gate)
    logf_ref[...] = logf
    zk_t = z[:, :TOK_WIDTH].T
    heads = []
    for h in range(FOX_HEADS):
        zh = zk_t[h * FOX_HEAD_DIM:(h + 1) * FOX_HEAD_DIM, :]
        heads.append(zh * lax.rsqrt(jnp.mean(zh * zh, axis=0, keepdims=True) + EPS))
    k_t = jnp.concatenate(heads, axis=0) * gk_ref[...]
    v_t = v.T
    kb_ref[...] = k_t.astype(BF16)
    for pg in range(kp_ref.shape[0]):
        rows = slice(pg * PAGE_SIZE, (pg + 1) * PAGE_SIZE)
        kp_ref[pg] = k_t[:, rows]
        vp_ref[pg] = v_t[:, rows]
        logfp_ref[pg] = logf[:, rows]


def _proj_kv_paged(x, g, w, g_k, wfg_t, bfg, *, tm):
    t = x.shape[0]
    row = lambda i: (i, 0)
    fixed = lambda i: (0, 0)
    ppt = tm // PAGE_SIZE
    page_rows = lambda i: (i, 0, 0)
    return pl.pallas_call(
        _proj_kv_paged_kernel,
        out_shape=(jax.ShapeDtypeStruct((t // PAGE_SIZE, TOK_WIDTH, PAGE_SIZE), F32),
                   jax.ShapeDtypeStruct((t // PAGE_SIZE, TOK_WIDTH, PAGE_SIZE), F32),
                   jax.ShapeDtypeStruct((TOK_WIDTH, t), BF16),
                   jax.ShapeDtypeStruct((t, TOK_WIDTH), BF16),
                   jax.ShapeDtypeStruct((FOX_HEADS, t), F32),
                   jax.ShapeDtypeStruct((t // PAGE_SIZE, FOX_HEADS, PAGE_SIZE), F32)),
        grid=(t // tm,),
        in_specs=[pl.BlockSpec((tm, D_MODEL), row),
                  pl.BlockSpec((1, D_MODEL), fixed),
                  pl.BlockSpec((D_MODEL, D_MODEL), fixed),
                  pl.BlockSpec((TOK_WIDTH, 1), fixed),
                  pl.BlockSpec((FOX_HEADS, D_MODEL), fixed),
                  pl.BlockSpec((FOX_HEADS, 1), fixed)],
        out_specs=(pl.BlockSpec((ppt, TOK_WIDTH, PAGE_SIZE), page_rows),
                   pl.BlockSpec((ppt, TOK_WIDTH, PAGE_SIZE), page_rows),
                   pl.BlockSpec((TOK_WIDTH, tm), lambda i: (0, i)), pl.BlockSpec((tm, TOK_WIDTH), row),
                   pl.BlockSpec((FOX_HEADS, tm), lambda i: (0, i)),
                   pl.BlockSpec((ppt, FOX_HEADS, PAGE_SIZE), page_rows)),
        compiler_params=pltpu.CompilerParams(dimension_semantics=("arbitrary",),
                                             vmem_limit_bytes=VMEM_LIMIT),
        name="proj_kv_paged",
    )(x, g, w, g_k, wfg_t, bfg)


def _proj_kv(x, g, w, g_k, wfg_t, bfg, *, tm):
    t = x.shape[0]
    row = lambda i: (i, 0)
    fixed = lambda i: (0, 0)
    return pl.pallas_call(
        _proj_kv_kernel,
        out_shape=(jax.ShapeDtypeStruct((t, TOK_WIDTH), F32),
                   jax.ShapeDtypeStruct((t, TOK_WIDTH), F32),
                   jax.ShapeDtypeStruct((FOX_HEADS, t), F32)),
        grid=(t // tm,),
        in_specs=[pl.BlockSpec((tm, D_MODEL), row),
                  pl.BlockSpec((1, D_MODEL), fixed),
                  pl.BlockSpec((D_MODEL, D_MODEL), fixed),
                  pl.BlockSpec((1, TOK_WIDTH), fixed),
                  pl.BlockSpec((FOX_HEADS, D_MODEL), fixed),
                  pl.BlockSpec((FOX_HEADS, 1), fixed)],
        out_specs=(pl.BlockSpec((tm, TOK_WIDTH), row), pl.BlockSpec((tm, TOK_WIDTH), row),
                   pl.BlockSpec((FOX_HEADS, tm), lambda i: (0, i))),
        compiler_params=pltpu.CompilerParams(dimension_semantics=("arbitrary",),
                                             vmem_limit_bytes=VMEM_LIMIT),
        name="proj_kv",
    )(x, g, w, g_k, wfg_t, bfg)


def _proj_memkv_kernel(x_ref, g_ref, w_ref, gk_ref, k_ref, v_ref):
    xn = _rmsnorm(x_ref[...], g_ref[0]).astype(BF16)
    z = jnp.dot(xn, w_ref[0], preferred_element_type=F32)
    k = _headnorm128(z[:, :MEM_WIDTH], gk_ref[0])
    tm = x_ref.shape[0]
    for h in range(MEM_HEADS):
        rows = pl.ds(h, tm, stride=MEM_HEADS)
        k_ref[0, rows, :] = k[:, h * MEM_HEAD_DIM:(h + 1) * MEM_HEAD_DIM]
        v_ref[0, rows, :] = z[:, MEM_WIDTH + h * MEM_HEAD_DIM:MEM_WIDTH + (h + 1) * MEM_HEAD_DIM]


def _proj_memkv(x, g, w, g_k, *, tm):
    t = x.shape[0]
    n_layers = w.shape[0]
    out_block = pl.BlockSpec((1, tm * MEM_HEADS, MEM_HEAD_DIM), lambda l, i: (l, i, 0))
    return pl.pallas_call(
        _proj_memkv_kernel,
        out_shape=(jax.ShapeDtypeStruct((n_layers, t * MEM_HEADS, MEM_HEAD_DIM), F32),
                   jax.ShapeDtypeStruct((n_layers, t * MEM_HEADS, MEM_HEAD_DIM), F32)),
        grid=(n_layers, t // tm),
        in_specs=[pl.BlockSpec((tm, D_MODEL), lambda l, i: (i, 0)),
                  pl.BlockSpec((1, 1, D_MODEL), lambda l, i: (l, 0, 0)),
                  pl.BlockSpec((1, D_MODEL, D_MODEL), lambda l, i: (l, 0, 0)),
                  pl.BlockSpec((1, 1, MEM_WIDTH), lambda l, i: (l, 0, 0))],
        out_specs=(out_block, out_block),
        compiler_params=pltpu.CompilerParams(dimension_semantics=("arbitrary", "arbitrary"),
                                             vmem_limit_bytes=VMEM_LIMIT),
        name="proj_memkv",
    )(x, g, w, g_k)


def _pool_kernel(u_ref, prev_ref, w_ref, scale_ref, o_ref, ext_ref, *, n_prev, chunk):
    bb, s_len, _ = u_ref.shape
    ext_ref[:, 0:POOL_HIST, :] = prev_ref[...]
    ext_ref[:, POOL_HIST:, :] = u_ref[...]
    for c in range(s_len // chunk):
        r0 = POOL_HIST + c * chunk
        pos = c * chunk + lax.broadcasted_iota(jnp.int32, (1, chunk, 1), 1)
        for gi, win in enumerate(POOL_WINDOWS):
            lanes = slice(gi * POOL_GROUP, (gi + 1) * POOL_GROUP)
            u_new = ext_ref[:, r0:r0 + chunk, lanes]
            if win == 2 * SUBLANES:
                a8 = ext_ref[:, r0 - SUBLANES:r0 + chunk, lanes]
                for k in range(1, SUBLANES):
                    a8 = a8 + ext_ref[:, r0 - SUBLANES - k:r0 + chunk - k, lanes]
                acc = a8[:, SUBLANES:, :] + a8[:, :chunk, :]
            else:
                acc = u_new
                for k in range(1, win):
                    acc = acc + ext_ref[:, r0 - k:r0 - k + chunk, lanes]
            count = jnp.minimum(win, n_prev + pos + 1).astype(F32)
            y = acc / count - u_new
            z = jnp.dot(y.reshape(bb * chunk, POOL_GROUP).astype(BF16), w_ref[gi],
                        preferred_element_type=F32)
            z = z.reshape(bb, chunk, POOL_GROUP) * scale_ref[:, lanes]
            o_ref[:, c * chunk:(c + 1) * chunk, lanes] = z.astype(o_ref.dtype)


def _pool_mix(u, prev, w_pool, scale, *, n_prev, bb, chunk, act):
    b, s_len, _ = u.shape
    return pl.pallas_call(
        functools.partial(_pool_kernel, n_prev=n_prev, chunk=chunk),
        out_shape=jax.ShapeDtypeStruct((b, s_len, TOK_WIDTH), act),
        grid=(b // bb,),
        in_specs=[pl.BlockSpec((bb, s_len, TOK_WIDTH), lambda i: (i, 0, 0)),
                  pl.BlockSpec((bb, POOL_HIST, TOK_WIDTH), lambda i: (i, 0, 0)),
                  pl.BlockSpec((len(POOL_WINDOWS), POOL_GROUP, POOL_GROUP), lambda i: (0, 0, 0)),
                  pl.BlockSpec((1, TOK_WIDTH), lambda i: (0, 0))],
        out_specs=pl.BlockSpec((bb, s_len, TOK_WIDTH), lambda i: (i, 0, 0)),
        scratch_shapes=[pltpu.VMEM((bb, POOL_HIST + s_len, TOK_WIDTH), F32)],
        compiler_params=pltpu.CompilerParams(dimension_semantics=("arbitrary",),
                                             vmem_limit_bytes=VMEM_LIMIT),
        name="pool_mix",
    )(u, prev, w_pool, scale)


def _mem_attn_kernel(q_ref, k_ref, v_ref, o_ref):
    for h in range(MEM_HEADS):
        lanes = slice(h * MEM_HEAD_DIM, (h + 1) * MEM_HEAD_DIM)
        q = q_ref[:, :, lanes].astype(BF16)
        k = k_ref[:, pl.ds(h, N_MEM, stride=MEM_HEADS), :].astype(BF16)
        v = v_ref[:, pl.ds(h, N_MEM, stride=MEM_HEADS), :].astype(BF16)
        s = jnp.einsum("bqd,bkd->bqk", q, k, preferred_element_type=F32) * (MEM_HEAD_DIM ** -0.5 * LOG2E)
        m = jnp.max(s, axis=-1, keepdims=True)
        p = jnp.exp2(s - m)
        l = jnp.sum(p, axis=-1, keepdims=True)
        o = jnp.einsum("bqk,bkd->bqd", p.astype(BF16), v, preferred_element_type=F32)
        o_ref[:, :, lanes] = (o / l).astype(o_ref.dtype)


def _mem_attn(q, mk, mv, layer, *, bb, ts):
    b, s_len, _ = q.shape
    mem_block = pl.BlockSpec((None, bb, N_MEM * MEM_HEADS, MEM_HEAD_DIM),
                             lambda i, j: (layer, i, 0, 0))
    return pl.pallas_call(
        _mem_attn_kernel,
        out_shape=jax.ShapeDtypeStruct((b, s_len, MEM_WIDTH), q.dtype),
        grid=(b // bb, s_len // ts),
        in_specs=[pl.BlockSpec((bb, ts, MEM_WIDTH), lambda i, j: (i, j, 0)), mem_block, mem_block],
        out_specs=pl.BlockSpec((bb, ts, MEM_WIDTH), lambda i, j: (i, j, 0)),
        compiler_params=pltpu.CompilerParams(dimension_semantics=("arbitrary", "arbitrary"),
                                             vmem_limit_bytes=VMEM_LIMIT),
        name="mem_attn",
    )(q, mk, mv)


def _out_ffn_kernel(h_ref, tok_ref, mo_ref, wout_ref, g_ref, wg_ref, wu_ref, wd_ref, o_ref):
    h1 = (h_ref[...]
          + jnp.dot(tok_ref[...].astype(BF16), wout_ref[:TOK_WIDTH, :], preferred_element_type=F32)
          + jnp.dot(mo_ref[...].astype(BF16), wout_ref[TOK_WIDTH:, :], preferred_element_type=F32))
    x = _rmsnorm(h1, g_ref[...]).astype(BF16)
    gate = jnp.dot(x, wg_ref[...], preferred_element_type=F32)
    up = jnp.dot(x, wu_ref[...], preferred_element_type=F32)
    act = (gate * (1.0 / (1.0 + jnp.exp(-gate))) * up).astype(BF16)
    o_ref[...] = h1 + jnp.dot(act, wd_ref[...], preferred_element_type=F32)


def _out_ffn(h, tok, mo, w_out, g, w_gu, w_down, layer, *, tm):
    t = h.shape[0]
    row = lambda i: (i, 0)
    once = pl.Buffered(1)
    return pl.pallas_call(
        _out_ffn_kernel,
        out_shape=jax.ShapeDtypeStruct((t, D_MODEL), F32),
        grid=(t // tm,),
        in_specs=[pl.BlockSpec((tm, D_MODEL), row),
                  pl.BlockSpec((tm, TOK_WIDTH), row),
                  pl.BlockSpec((tm, MEM_WIDTH), row),
                  pl.BlockSpec((None, D_MODEL, D_MODEL), lambda i: (layer, 0, 0), pipeline_mode=once),
                  pl.BlockSpec((1, D_MODEL), lambda i: (0, 0)),
                  pl.BlockSpec((None, D_MODEL, D_FF), lambda i: (layer, 0, 0), pipeline_mode=once),
                  pl.BlockSpec((None, D_MODEL, D_FF), lambda i: (layer, 0, 1), pipeline_mode=once),
                  pl.BlockSpec((None, D_FF, D_MODEL), lambda i: (layer, 0, 0), pipeline_mode=once)],
        out_specs=pl.BlockSpec((tm, D_MODEL), row),
        compiler_params=pltpu.CompilerParams(dimension_semantics=("arbitrary",),
                                             vmem_limit_bytes=VMEM_LIMIT),
        name="out_ffn",
    )(h, tok, mo, w_out, g, w_gu, w_gu, w_down)


def _lane_cumsum(x):
    n = x.shape[-1]
    lane = lax.broadcasted_iota(jnp.int32, x.shape, 1)
    k = 1
    while k < n:
        x = x + jnp.where(lane >= k, pltpu.roll(x, k, axis=1), 0.0)
        k *= 2
    return x


def _cumsum_kernel(x_ref, o_ref):
    o_ref[...] = _lane_cumsum(x_ref[...])


def _cumsum_rows(x, *, seg):
    rows, t = x.shape
    return pl.pallas_call(
        _cumsum_kernel,
        out_shape=jax.ShapeDtypeStruct((rows, t), F32),
        grid=(t // seg,),
        in_specs=[pl.BlockSpec((rows, seg), lambda i: (0, i))],
        out_specs=pl.BlockSpec((rows, seg), lambda i: (0, i)),
        compiler_params=pltpu.CompilerParams(dimension_semantics=("arbitrary",)),
        name="logf_cumsum",
    )(x)


def _round_to_bf16(x):
    return x.astype(BF16).astype(F32)


def _fox_prompt_kernel(q_ref, k_ref, v_ref, frow_ref, o_ref, m_ref, l_ref, acc_ref, alpha_ref,
                       s_ref, p_all_ref, *, blk, n_q_blocks):
    qi = pl.program_id(2)
    q = q_ref[0]
    lane = lax.broadcasted_iota(jnp.int32, q.shape, 1)
    sub = FOX_CHUNK_ROWS
    n_rep = blk // LANES
    bias_lane = [FOX_HEAD_DIM, 0]
    q_f32 = q.astype(F32)
    q_head = []
    for j in range(2):
        own = (lane < FOX_HEAD_DIM) if j == 0 else (lane >= FOX_HEAD_DIM)
        ones_from = jnp.where(lane >= bias_lane[j], 1.0, 0.0)
        bias_ones = jnp.where(lane < bias_lane[j] + 3, ones_from, 0.0)
        q_head.append(jnp.where(own, q_f32, bias_ones).astype(BF16))
    m_ref[...] = jnp.full(m_ref.shape, -jnp.inf, F32)
    l_ref[...] = jnp.zeros_like(l_ref)
    acc_ref[...] = jnp.zeros_like(acc_ref)

    def scores(qc, ki, s_buf):
        keys = slice(ki * blk, (ki + 1) * blk)
        kt = k_ref[:, keys]
        tile_row = lax.broadcasted_iota(jnp.int32, (BF16_ROWS, blk), 0)
        for j in range(2):
            f_end = frow_ref[0, j:j + 1, (qc + 1) * blk - 1:(qc + 1) * blk]
            bias = (f_end - frow_ref[0, j:j + 1, keys]) * LOG2E
            hi = _round_to_bf16(bias)
            mid = _round_to_bf16(bias - hi)
            lo = _round_to_bf16(bias - hi - mid)
            bias_rows = jnp.where(tile_row == 0, hi, jnp.where(tile_row == 1, mid,
                                                               jnp.where(tile_row == 2, lo, 0.0)))
            b0 = bias_lane[j]
            pieces = [kt[:b0], bias_rows.astype(BF16), kt[b0 + BF16_ROWS:]]
            kt_j = jnp.concatenate([x for x in pieces if x.shape[0]], axis=0)
            s_buf[j] = jnp.dot(q_head[j], kt_j, preferred_element_type=F32)

    def update(ki, s_buf, masked):
        p_ref = p_all_ref.at[ki]
        vt = v_ref[0, ki * blk:(ki + 1) * blk, :]
        for j in range(2):
            def chunk(r):
                n_use = -(-(r + 1) * sub // LANES) if masked else n_rep
                rows = slice(r * sub, (r + 1) * sub)
                s = s_buf[j, rows, 0:n_use * LANES]
                if masked:
                    row = lax.broadcasted_iota(jnp.int32, s.shape, 0) + r * sub
                    col = lax.broadcasted_iota(jnp.int32, s.shape, 1)
                    s = jnp.where(col <= row, s, -jnp.inf)
                return rows, n_use, s

            for r in range(blk // sub):
                rows, n_use, s = chunk(r)
                m_prev = m_ref[j, rows]
                m_new = jnp.maximum(m_prev, jnp.max(s, axis=-1, keepdims=True))
                alpha_ref[j, rows] = jnp.exp2(m_prev - m_new)
                m_ref[j, rows] = m_new
            for r in range(blk // sub):
                rows, n_use, s = chunk(r)
                p = jnp.exp2(s - jnp.concatenate([m_ref[j, rows]] * n_use, axis=1))
                p_lanes = p[:, :LANES]
                for c in range(1, n_use):
                    p_lanes = p_lanes + p[:, c * LANES:(c + 1) * LANES]
                l_ref[j, rows] = alpha_ref[j, rows] * l_ref[j, rows] + p_lanes
                p_ref[j, rows, 0:n_use * LANES] = p.astype(BF16)
                if n_use < n_rep:
                    p_ref[j, rows, n_use * LANES:] = jnp.zeros((sub, blk - n_use * LANES), BF16)
            acc_ref[j] = alpha_ref[j] * acc_ref[j] + jnp.dot(p_ref[j], vt, preferred_element_type=F32)

    for qc in range(n_q_blocks):
        @pl.when(qi == qc)
        def _(qc=qc):
            scores(qc, 0, s_ref.at[0])
            for ki in range(qc):
                scores(qc, ki + 1, s_ref.at[ki + 1])
                update(ki, s_ref.at[ki], False)
            update(qc, s_ref.at[qc], True)

    l0 = jnp.sum(l_ref[0], axis=-1, keepdims=True)
    l1 = jnp.sum(l_ref[1], axis=-1, keepdims=True)
    o = jnp.where(lane < FOX_HEAD_DIM, acc_ref[0] / l0, acc_ref[1] / l1)
    o_ref[0] = o.astype(o_ref.dtype)


def _fox_prompt(q, k, v, frow, *, blk):
    b, s_len, _ = q.shape
    n_pairs = TOK_WIDTH // LANES
    nq = s_len // blk
    return pl.pallas_call(
        functools.partial(_fox_prompt_kernel, blk=blk, n_q_blocks=nq),
        out_shape=jax.ShapeDtypeStruct((b, s_len, TOK_WIDTH), BF16),
        grid=(b, n_pairs, nq),
        in_specs=[pl.BlockSpec((1, blk, LANES), lambda bi, hp, qi: (bi, qi, hp)),
                  pl.BlockSpec((LANES, s_len), lambda bi, hp, qi: (hp, bi)),
                  pl.BlockSpec((1, s_len, LANES), lambda bi, hp, qi: (bi, 0, hp)),
                  pl.BlockSpec((1, 2, s_len), lambda bi, hp, qi: (hp, 0, bi))],
        out_specs=pl.BlockSpec((1, blk, LANES), lambda bi, hp, qi: (bi, qi, hp)),
        scratch_shapes=[pltpu.VMEM((2, blk, LANES), F32), pltpu.VMEM((2, blk, LANES), F32),
                        pltpu.VMEM((2, blk, LANES), F32), pltpu.VMEM((2, blk, LANES), F32),
                        pltpu.VMEM((nq, 2, blk, blk), F32),
                        pltpu.VMEM((nq, 2, blk, blk), BF16)],
        compiler_params=pltpu.CompilerParams(
            dimension_semantics=("arbitrary", "arbitrary", "arbitrary"),
            vmem_limit_bytes=VMEM_LIMIT),
        name="fox_prompt",
    )(q, k, v, frow)


def _lane_suffix_sum(x):
    n = x.shape[-1]
    lane = lax.broadcasted_iota(jnp.int32, x.shape, 1)
    k = 1
    while k < n:
        x = x + jnp.where(lane + k < n, pltpu.roll(x, n - k, axis=1), 0.0)
        k *= 2
    return x


def _fox_paged_kernel(pt_ref, q_ref, kn_ref, vn_ref, lfn_ref, logf_ref, k_hbm, v_hbm, o_ref,
                      kbuf, vbuf, sem, pad_k_ref, pad_v_ref, *, n_pages, n_batch):
    b = pl.program_id(0)
    n_steps = pl.num_programs(0)
    slot = lax.rem(b, PAGE_SLOTS)
    n_q = q_ref.shape[1]
    n_rows = n_q * FOX_HEADS

    def page_copies(batch, slot_):
        copies = []
        for j in range(n_pages):
            page = pt_ref[j * n_batch + batch]
            copies.append(pltpu.make_async_copy(k_hbm.at[page], kbuf.at[slot_, j], sem.at[0, slot_]))
            copies.append(pltpu.make_async_copy(v_hbm.at[page], vbuf.at[slot_, j], sem.at[1, slot_]))
        return copies

    ahead = PAGE_SLOTS - 1
    for r in range(ahead):
        @pl.when(jnp.logical_and(b == 0, r < n_steps))
        def _(r=r):
            for c in page_copies(r, r):
                c.start()

    @pl.when(b + ahead < n_steps)
    def _():
        for i, c in enumerate(page_copies(b + ahead, lax.rem(b + ahead, PAGE_SLOTS))):
            c.start(priority=i % 2)

    head = lax.broadcasted_iota(jnp.int32, (FOX_HEADS, TOK_WIDTH), 0)
    lane_head = jnp.right_shift(lax.broadcasted_iota(jnp.int32, (FOX_HEADS, TOK_WIDTH), 1), 6)
    own_head = head == lane_head
    tile_q = lambda x: jnp.concatenate([x] * n_q, axis=0)

    q = q_ref[0]
    qblk = jnp.concatenate(
        [jnp.where(own_head, jnp.broadcast_to(q[t:t + 1, :], (FOX_HEADS, TOK_WIDTH)), 0.0)
         for t in range(n_q)], axis=0).astype(BF16)

    pad_k_ref[...] = jnp.zeros_like(pad_k_ref)
    pad_v_ref[...] = jnp.zeros_like(pad_v_ref)
    pad_k_ref[0:n_q, :] = kn_ref[0]
    pad_v_ref[0:n_q, :] = vn_ref[0]
    c_new = _lane_cumsum(lfn_ref[0])
    cq = jnp.concatenate([c_new[:, t:t + 1] for t in range(n_q)], axis=0)
    s_own = lax.dot_general(qblk, pad_k_ref[...].astype(BF16), NT_DIMS, preferred_element_type=F32)
    t_of_row = jnp.right_shift(lax.broadcasted_iota(jnp.int32, (n_rows, PAGE_SIZE), 0), 3)
    key = lax.broadcasted_iota(jnp.int32, (n_rows, PAGE_SIZE), 1)
    s_own = jnp.where(key <= t_of_row, s_own + (tile_q(-c_new) + cq) * LOG2E, -jnp.inf)

    logf = [logf_ref[pt_ref[j * n_batch + b]] for j in range(n_pages)]
    incl = _lane_suffix_sum(jnp.concatenate(logf, axis=0))
    later_pages = jnp.zeros((FOX_HEADS, 1), F32)
    bias = [None] * n_pages
    for j in reversed(range(n_pages)):
        rows = slice(j * FOX_HEADS, (j + 1) * FOX_HEADS)
        bias[j] = (tile_q(later_pages + (incl[rows] - logf[j])) + cq) * LOG2E
        later_pages = later_pages + incl[rows, 0:1]

    for c in page_copies(b, slot):
        c.wait()

    scores = [s_own]
    for j in range(n_pages):
        k_t = kbuf[slot, j].reshape(TOK_WIDTH, PAGE_SIZE).astype(BF16)
        scores.append(jnp.dot(qblk, k_t, preferred_element_type=F32) + bias[j])
    s_max = scores[0]
    for s in scores[1:]:
        s_max = jnp.maximum(s_max, s)
    m = jnp.max(s_max, axis=-1, keepdims=True)
    p = jnp.exp2(scores[0] - m)
    p_sum = p
    acc = jnp.dot(p.astype(BF16), pad_v_ref[...].astype(BF16), preferred_element_type=F32)
    for j in range(n_pages):
        p = jnp.exp2(scores[j + 1] - m)
        p_sum = p_sum + p
        v_t = vbuf[slot, j].reshape(TOK_WIDTH, PAGE_SIZE).astype(BF16)
        acc = acc + lax.dot_general(p.astype(BF16), v_t, NT_DIMS, preferred_element_type=F32)
    o = acc / jnp.sum(p_sum, axis=-1, keepdims=True)
    for t in range(n_q):
        o_t = jnp.where(own_head, o[t * FOX_HEADS:(t + 1) * FOX_HEADS, :], 0.0)
        o_ref[0, t:t + 1, :] = jnp.sum(o_t, axis=0, keepdims=True)


def _fox_paged(page_table, q, k_new, v_new, lfn, cache_logf_t, cache_k_t, cache_v_t):
    b, n_q, _ = q.shape
    n_pages = page_table.shape[0] // b
    cur = lambda bi, pt: (bi, 0, 0)
    page_shape = cache_k_t.shape[1:]
    grid_spec = pltpu.PrefetchScalarGridSpec(
        num_scalar_prefetch=1,
        grid=(b,),
        in_specs=[pl.BlockSpec((1, n_q, TOK_WIDTH), cur),
                  pl.BlockSpec((1, n_q, TOK_WIDTH), cur),
                  pl.BlockSpec((1, n_q, TOK_WIDTH), cur),
                  pl.BlockSpec((1, FOX_HEADS, LANES), cur),
                  pl.BlockSpec(cache_logf_t.shape, lambda bi, pt: (0, 0, 0), pipeline_mode=pl.Buffered(1)),
                  pl.BlockSpec(memory_space=pl.ANY),
                  pl.BlockSpec(memory_space=pl.ANY)],
        out_specs=pl.BlockSpec((1, n_q, TOK_WIDTH), cur),
        scratch_shapes=[pltpu.VMEM((PAGE_SLOTS, n_pages) + page_shape, F32),
                        pltpu.VMEM((PAGE_SLOTS, n_pages) + page_shape, F32),
                        pltpu.SemaphoreType.DMA((2, PAGE_SLOTS)),
                        pltpu.VMEM((PAGE_SIZE, TOK_WIDTH), F32),
                        pltpu.VMEM((PAGE_SIZE, TOK_WIDTH), F32)])
    return pl.pallas_call(
        functools.partial(_fox_paged_kernel, n_pages=n_pages, n_batch=b),
        out_shape=jax.ShapeDtypeStruct((b, n_q, TOK_WIDTH), F32),
        grid_spec=grid_spec,
        compiler_params=pltpu.CompilerParams(dimension_semantics=("arbitrary",),
                                             vmem_limit_bytes=VMEM_LIMIT),
        name="fox_paged",
    )(page_table, q, k_new, v_new, lfn, cache_logf_t, cache_k_t, cache_v_t)


def _decoder(x, pool_prev, n_prev, mem_k, mem_v, past, wts, *, tm, tm_proj, pool_bb, pool_chunk,
             mem_bb, mem_ts, act):
    b, s_len, _ = x.shape
    t = b * s_len
    h = x.reshape(t, D_MODEL)

    fuse_mem = s_len % tm_proj == 0

    def project_and_attend(layer, fox):
        mem = (mem_k, mem_v, layer, s_len) if fuse_mem else None
        tok_part, mem_part = _proj_in(h, wts["g_mix"][layer], wts["w_in"][layer], wts["g_fox_q_t"],
                                      wts["g_mem_q_t"][layer], mem, fox=fox, tm=tm_proj, act=act)
        if not fuse_mem:
            mem_part = _mem_attn(mem_part.reshape(b, s_len, MEM_WIDTH), mem_k, mem_v, layer,
                                 bb=mem_bb, ts=mem_ts)
        return tok_part, mem_part.reshape(t, MEM_WIDTH)

    z_tok, mo = project_and_attend(0, fox=False)
    z_tok = z_tok.reshape(b, s_len, TOK_WIDTH)
    tok = _pool_mix(z_tok, pool_prev, wts["w_pool"], wts["pool_scale"],
                    n_prev=n_prev, bb=pool_bb, chunk=pool_chunk, act=act)
    h = _out_ffn(h, tok.reshape(t, TOK_WIDTH), mo, wts["w_out"],
                 wts["g_ffn"][0], wts["w_gu"], wts["w_down"], 0, tm=tm)

    if past is None:
        k_new, v_new, k_bf, v_bf, logf_t, logf_new = _proj_kv_paged(
            h, wts["g_kv"], wts["w_kv"], wts["g_fox_k_t"].reshape(TOK_WIDTH, 1), wts["w_fg_t"],
            wts["b_fg"], tm=tm_proj)
    else:
        k_new, v_new, logf_t = _proj_kv(h, wts["g_kv"], wts["w_kv"], wts["g_fox_k_t"], wts["w_fg_t"],
                                        wts["b_fg"], tm=tm_proj)
        logf_new = logf_t.T

    q, mo = project_and_attend(1, fox=True)
    if past is None:
        f_row = _cumsum_rows(logf_t, seg=s_len)
        n_pairs = FOX_HEADS // 2
        f_row = f_row.reshape(n_pairs, 2, t)
        tok = _fox_prompt(q.reshape(b, s_len, TOK_WIDTH), k_bf,
                          v_bf.reshape(b, s_len, TOK_WIDTH), f_row, blk=512)
    else:
        page_table, cache_k_t, cache_v_t, cache_logf_t = past
        lfn = logf_t.reshape(FOX_HEADS, b, s_len).transpose(1, 0, 2)
        lfn = jnp.pad(lfn, ((0, 0), (0, 0), (0, LANES - s_len)))
        per_row = lambda a: a.reshape(b, s_len, TOK_WIDTH)
        tok = _fox_paged(page_table, per_row(q), per_row(k_new), per_row(v_new), lfn,
                         cache_logf_t, cache_k_t, cache_v_t)
    h = _out_ffn(h, tok.reshape(t, TOK_WIDTH), mo, wts["w_out"],
                 wts["g_ffn"][1], wts["w_gu"], wts["w_down"], 1, tm=tm)
    return h.reshape(b, s_len, D_MODEL), z_tok, k_new, v_new, logf_new


def kernel(x_prompt, x_sample, cache_mem_k, cache_mem_v, state_pool, cache_k, cache_v, cache_logf,
           page_table, mem_prompt, g_mix, w_in, w_out, g_ffn, w_gu, w_down, g_mem, w_mem_kv,
           g_mem_q, g_mem_k, w_pool, pool_scale, g_kv, w_kv, g_fox_k, w_fg, b_fg, g_fox_q):
    depth = w_in.shape[0]
    b, s_len, _ = x_prompt.shape
    db, ds, _ = x_sample.shape
    wts = {
        "g_mix": g_mix.reshape(depth, 1, D_MODEL),
        "w_in": w_in.astype(BF16),
        "w_out": w_out.astype(BF16),
        "g_ffn": g_ffn.reshape(depth, 1, D_MODEL),
        "w_gu": w_gu.astype(BF16),
        "w_down": w_down.astype(BF16),
        "g_mem_q_t": jnp.tile(g_mem_q, (1, MEM_HEADS)).reshape(depth, 1, MEM_WIDTH),
        "w_pool": w_pool[0].astype(BF16),
        "pool_scale": pool_scale[0].reshape(1, TOK_WIDTH),
        "g_kv": g_kv.reshape(1, D_MODEL),
        "w_kv": w_kv.astype(BF16),
        "g_fox_k_t": jnp.tile(g_fox_k, FOX_HEADS).reshape(1, TOK_WIDTH),
        "w_fg_t": w_fg.T.astype(BF16),
        "b_fg": b_fg.reshape(FOX_HEADS, 1),
        "g_fox_q_t": jnp.tile(g_fox_q[0], FOX_HEADS).reshape(1, TOK_WIDTH),
    }

    mem_k_p, mem_v_p = _proj_memkv(mem_prompt.reshape(b * N_MEM, D_MODEL),
                                   g_mem.reshape(depth, 1, D_MODEL), w_mem_kv.astype(BF16),
                                   jnp.tile(g_mem_k, (1, MEM_HEADS)).reshape(depth, 1, MEM_WIDTH), tm=512)
    mem_rows = (N_MEM * MEM_HEADS, MEM_HEAD_DIM)
    mem_k_p = mem_k_p.reshape((depth, b) + mem_rows)
    mem_v_p = mem_v_p.reshape((depth, b) + mem_rows)
    y_p, ztok_p, k_p, v_p, logf_p = _decoder(
        x_prompt, jnp.zeros((b, POOL_HIST, TOK_WIDTH), F32), 0, mem_k_p, mem_v_p, None, wts,
        tm=512, tm_proj=1024, pool_bb=1, pool_chunk=256, mem_bb=1, mem_ts=512, act=BF16)
    n_pp = s_len // PAGE_SIZE
    head_shape = (FOX_HEADS, FOX_HEAD_DIM)

    past = (page_table.T.reshape(-1),
            cache_k.transpose(0, 2, 3, 1), cache_v.transpose(0, 2, 3, 1),
            cache_logf.transpose(0, 2, 1))
    prev = jnp.pad(state_pool[0], ((0, 0), (POOL_HIST - POOL_STATE, 0), (0, 0)))
    y_s, ztok_s, k_s, v_s, logf_s = _decoder(
        x_sample, prev, POOL_STATE, cache_mem_k.reshape((depth, db) + mem_rows),
        cache_mem_v.reshape((depth, db) + mem_rows), past, wts,
        tm=512, tm_proj=512, pool_bb=32, pool_chunk=ds, mem_bb=8, mem_ts=ds, act=F32)

    pool_state_p = ztok_p[:, s_len - POOL_STATE:][None]
    pool_state_s = jnp.concatenate([state_pool[0], ztok_s], axis=1)[:, -POOL_STATE:][None]
    paged = lambda a: a.reshape((b, n_pp) + head_shape + (PAGE_SIZE,)).transpose(0, 1, 4, 2, 3)
    return (y_p, y_s, paged(k_p), paged(v_p),
            logf_p.reshape(b, n_pp, FOX_HEADS, PAGE_SIZE).transpose(0, 1, 3, 2),
            mem_k_p.reshape(depth, b, N_MEM, MEM_HEADS, MEM_HEAD_DIM),
            mem_v_p.reshape(depth, b, N_MEM, MEM_HEADS, MEM_HEAD_DIM),
            pool_state_p,
            k_s.reshape((db, ds) + head_shape),
            v_s.reshape((db, ds) + head_shape),
            logf_s.reshape(db, ds, FOX_HEADS),
            pool_state_s)
```
